```python
import math
import jax
import jax.numpy as jnp
from jax import lax
import numpy as np

D_MODEL = 1024
BATCH = 8
SEQ = 2048
DEPTH = 2

MEM_LEN = 256
HEAD_DIM = 64
MIX_WIDTH = 512
ROPE_THETA = 10000.0
NORM_EPS = 1e-6
NEG_INF = -1e30

SSD_HEADS = 8
SSD_INNER = SSD_HEADS * HEAD_DIM
SSD_GROUPS = 2
SSD_STATE = 64
SSD_CONV = 4
SSD_CONV_DIM = SSD_INNER + 2 * SSD_GROUPS * SSD_STATE
SSD_CHUNK = 128

RET_HEADS = 8
RET_DIM = RET_HEADS * HEAD_DIM
RET_CHUNK = 128

MOBA_HEADS = 8
MOBA_DIM = MOBA_HEADS * HEAD_DIM
MOBA_BLOCK = 256
MOBA_TOPK = 3
MOBA_QCHUNK = 32

DIL_HEADS = 8
DIL_DIM = DIL_HEADS * HEAD_DIM
DIL_PATTERNS = ((128, 1), (512, 4), (2048, 16))

N_BRANCHES = 4

X_HEADS = 4
X_HEAD_DIM = D_MODEL // X_HEADS

PEER_HEADS = 8
PEER_NKEYS = 128
PEER_EXPERTS = PEER_NKEYS * PEER_NKEYS
PEER_QDIM = 256
PEER_TOPK = 16
PEER_TCHUNK = 128

IN_SIZES = (SSD_INNER, SSD_CONV_DIM, SSD_HEADS,
            RET_DIM, RET_DIM, RET_DIM, RET_DIM,
            MOBA_DIM, MOBA_DIM, MOBA_DIM,
            DIL_DIM, DIL_DIM, DIL_DIM,
            N_BRANCHES * D_MODEL)
IN_DIM = sum(IN_SIZES)

kernel_name = 'hybrid_gated_ssd_ret_moba_dilated_peer'


def _split_points():
    pts, acc = [], 0
    for size in IN_SIZES[:-1]:
        acc += size
        pts.append(acc)
    return pts


def rms_norm(x, gain):
    xf = x.astype(jnp.float32)
    y = xf * lax.rsqrt(jnp.mean(xf * xf, axis=-1, keepdims=True) + NORM_EPS)
    return (y * gain.astype(jnp.float32)).astype(x.dtype)


def rope(t):
    seq, dh = t.shape[1], t.shape[-1]
    half = dh // 2
    inv_freq = ROPE_THETA ** (-jnp.arange(half, dtype=jnp.float32) / half)
    ang = jnp.arange(seq, dtype=jnp.float32)[:, None] * inv_freq[None, :]
    cos = jnp.cos(ang)[None, :, None, :]
    sin = jnp.sin(ang)[None, :, None, :]
    tf = t.astype(jnp.float32)
    t1, t2 = tf[..., :half], tf[..., half:]
    return jnp.concatenate([t1 * cos - t2 * sin, t2 * cos + t1 * sin], axis=-1).astype(t.dtype)


def softmax_stats(s):
    m = jnp.max(s, axis=-1, keepdims=True)
    e = jnp.exp(s - m)
    l = jnp.sum(e, axis=-1, keepdims=True)
    return e / l, (m + jnp.log(l))[..., 0]


def causal_dwconv(x, w, b):
    k, c = w.shape
    xp = jnp.pad(x, ((0, 0), (k - 1, 0), (0, 0)))
    y = lax.conv_general_dilated(xp, w[:, None, :].astype(x.dtype), window_strides=(1,), padding='VALID',
                                 dimension_numbers=('NWC', 'WIO', 'NWC'), feature_group_count=c)
    return y + b.astype(x.dtype)


def ssd_mixer(z, xbc, dt_raw, conv_w, conv_b, dt_bias, a_log, d_skip, norm_gain):
    bsz, seq, _ = z.shape
    nh, p, n, lc = SSD_HEADS, HEAD_DIM, SSD_STATE, SSD_CHUNK
    nc = seq // lc
    xbc = jax.nn.silu(causal_dwconv(xbc, conv_w, conv_b))
    xs, bm, cm = jnp.split(xbc, [SSD_INNER, SSD_INNER + SSD_GROUPS * n], axis=-1)
    rep = nh // SSD_GROUPS
    xs = xs.reshape(bsz, seq, nh, p)
    bm = jnp.repeat(bm.reshape(bsz, seq, SSD_GROUPS, n), rep, axis=2)
    cm = jnp.repeat(cm.reshape(bsz, seq, SSD_GROUPS, n), rep, axis=2)
    dt = jax.nn.softplus(dt_raw.astype(jnp.float32) + dt_bias.astype(jnp.float32))
    log_a = dt * -jnp.exp(a_log.astype(jnp.float32))
    xdt = (xs * dt[..., None]).reshape(bsz, nc, lc, nh, p)
    bc = bm.reshape(bsz, nc, lc, nh, n)
    cc = cm.reshape(bsz, nc, lc, nh, n)
    acum = jnp.cumsum(log_a.reshape(bsz, nc, lc, nh), axis=2)
    tri = jnp.tril(jnp.ones((lc, lc), dtype=bool))[None, None, :, :, None]
    seg = acum[:, :, :, None, :] - acum[:, :, None, :, :]
    decay = jnp.exp(jnp.where(tri, seg, -jnp.inf))
    scores = jnp.einsum('bclhn,bcshn->bclsh', cc, bc) * decay
    y_diag = jnp.einsum('bclsh,bcshp->bclhp', scores, xdt)
    to_end = jnp.exp(acum[:, :, -1:, :] - acum)
    chunk_states = jnp.einsum('bclhn,bclh,bclhp->bchpn', bc, to_end, xdt)
    chunk_decay = jnp.exp(acum[:, :, -1, :])

    def step(state, inp):
        st, dec = inp
        return state * dec[:, :, None, None] + st, state

    init = jnp.zeros((bsz, nh, p, n), chunk_states.dtype)
    _, prev = lax.scan(step, init, (jnp.moveaxis(chunk_states, 1, 0), jnp.moveaxis(chunk_decay, 1, 0)))
    prev = jnp.moveaxis(prev, 0, 1)
    y_off = jnp.einsum('bclhn,bchpn,bclh->bclhp', cc, prev, jnp.exp(acum))
    y = (y_diag + y_off).reshape(bsz, seq, nh, p) + xs * d_skip.astype(jnp.float32)[:, None]
    y = y.reshape(bsz, seq, SSD_INNER)
    return rms_norm(y * jax.nn.silu(z.astype(jnp.float32)), norm_gain).astype(z.dtype)


def retention_mixer(q, k, v, g, norm_gain):
    bsz, seq, _ = q.shape
    nh, dh, lc = RET_HEADS, HEAD_DIM, RET_CHUNK
    nc = seq // lc
    q = rope(q.reshape(bsz, seq, nh, dh))
    k = rope(k.reshape(bsz, seq, nh, dh)) * dh ** -0.5
    v = v.reshape(bsz, seq, nh, dh)
    log_gamma = jnp.log1p(-jnp.exp2(-5.0 - jnp.arange(nh, dtype=jnp.float32)))
    idx = jnp.arange(lc, dtype=jnp.float32)
    rel = idx[:, None] - idx[None, :]
    dmat = jnp.where((rel >= 0)[..., None], jnp.exp(jnp.maximum(rel, 0.0)[..., None] * log_gamma), 0.0)
    qc = q.reshape(bsz, nc, lc, nh, dh)
    kc = k.reshape(bsz, nc, lc, nh, dh)
    vc = v.reshape(bsz, nc, lc, nh, dh)
    inner = jnp.einsum('bclhd,bcshd->bclsh', qc, kc) * dmat
    y_inner = jnp.einsum('bclsh,bcshe->bclhe', inner, vc)
    zeta = jnp.exp((lc - 1.0 - idx)[:, None] * log_gamma)
    chunk_states = jnp.einsum('bcshd,sh,bcshe->bchde', kc, zeta, vc)
    chunk_decay = jnp.exp(lc * log_gamma)

    def step(state, st):
        return state * chunk_decay[None, :, None, None] + st, state

    init = jnp.zeros((bsz, nh, dh, dh), chunk_states.dtype)
    _, prev = lax.scan(step, init, jnp.moveaxis(chunk_states, 1, 0))
    prev = jnp.moveaxis(prev, 0, 1)
    xi = jnp.exp((idx + 1.0)[:, None] * log_gamma)
    y_cross = jnp.einsum('bclhd,bchde,lh->bclhe', qc, prev, xi)
    y = (y_inner + y_cross).astype(jnp.float32).reshape(bsz, seq, nh, dh)
    mu = jnp.mean(y, axis=-1, keepdims=True)
    var = jnp.mean(jnp.square(y - mu), axis=-1, keepdims=True)
    y = ((y - mu) * lax.rsqrt(var + NORM_EPS)).reshape(bsz, seq, RET_DIM) * norm_gain.astype(jnp.float32)
    return (jax.nn.silu(g.astype(jnp.float32)) * y).astype(g.dtype)


def moba_mixer(q, k, v):
    bsz, seq, _ = q.shape
    nh, dh, blk = MOBA_HEADS, HEAD_DIM, MOBA_BLOCK
    nb = -(-seq // blk)
    sp = nb * blk
    scale = dh ** -0.5

    def heads_first(t):
        t = jnp.pad(t, ((0, 0), (0, sp - seq), (0, 0), (0, 0)))
        return t.transpose(0, 2, 1, 3)

    q = heads_first(rope(q.reshape(bsz, seq, nh, dh)))
    k = heads_first(rope(k.reshape(bsz, seq, nh, dh)))
    v = heads_first(v.reshape(bsz, seq, nh, dh))
    qb = q.reshape(bsz, nh, nb, blk, dh)
    kb = k.reshape(bsz, nh, nb, blk, dh)
    vb = v.reshape(bsz, nh, nb, blk, dh)
    tri = jnp.tril(jnp.ones((blk, blk), dtype=bool))
    s_own = jnp.einsum('bhnid,bhnjd->bhnij', qb, kb).astype(jnp.float32) * scale
    p_own, lse_own = softmax_stats(jnp.where(tri, s_own, NEG_INF))
    o_own = jnp.einsum('bhnij,bhnjd->bhnid', p_own.astype(v.dtype), vb).reshape(bsz, nh, sp, dh)
    lse_own = lse_own.reshape(bsz, nh, sp)
    n_sel = min(MOBA_TOPK, nb - 1)
    if n_sel == 0:
        out = o_own
    else:
        k_mean = jnp.mean(kb, axis=3)
        gate = jnp.einsum('bhtd,bhnd->bhtn', q, k_mean).astype(jnp.float32)
        q_block = jnp.arange(sp) // blk
        past = jnp.arange(nb)[None, :] < q_block[:, None]
        _, sel = lax.top_k(jnp.where(past, gate, -jnp.inf), n_sel)
        valid = sel < q_block[:, None]
        nq = sp // MOBA_QCHUNK

        def by_chunk(t):
            return jnp.moveaxis(t.reshape(bsz, nh, nq, MOBA_QCHUNK, *t.shape[3:]), 2, 0)

        gather_blocks = jax.vmap(jax.vmap(lambda blocks, ids: blocks[ids]))

        def attend(args):
            qc, selc, validc = args
            ksel = gather_blocks(kb, selc)
            vsel = gather_blocks(vb, selc)
            s = jnp.einsum('bhqd,bhqnjd->bhqnj', qc, ksel).astype(jnp.float32) * scale
            s = jnp.where(validc[..., None], s, NEG_INF).reshape(bsz, nh, MOBA_QCHUNK, n_sel * blk)
            p, lse = softmax_stats(s)
            o = jnp.einsum('bhqm,bhqmd->bhqd', p.astype(v.dtype),
                           vsel.reshape(bsz, nh, MOBA_QCHUNK, n_sel * blk, dh))
            return o, lse

        o_past, lse_past = lax.map(attend, (by_chunk(q), by_chunk(sel), by_chunk(valid)))
        o_past = jnp.moveaxis(o_past, 0, 2).reshape(bsz, nh, sp, dh)
        lse_past = jnp.moveaxis(lse_past, 0, 2).reshape(bsz, nh, sp)
        w = jax.nn.softmax(jnp.stack([lse_own, lse_past], axis=-1), axis=-1)
        out = w[..., :1] * o_own + w[..., 1:] * o_past
    return out[:, :, :seq].transpose(0, 2, 1, 3).reshape(bsz, seq, MOBA_DIM).astype(v.dtype)


def dilated_group(q, k, v, window, dil):
    bsz, nh, seq, dh = q.shape
    n_off = window // dil
    sd = -(-seq // dil) * dil
    ln = sd // dil
    nblk = -(-ln // n_off)
    lp = nblk * n_off

    def to_sub(t):
        t = jnp.pad(t, ((0, 0), (0, 0), (0, sd - seq), (0, 0)))
        t = t.reshape(bsz, nh, ln, dil, dh).transpose(0, 1, 3, 2, 4)
        return jnp.pad(t, ((0, 0), (0, 0), (0, 0), (0, lp - ln), (0, 0)))

    def to_band(t):
        t = jnp.pad(to_sub(t), ((0, 0), (0, 0), (0, 0), (n_off, 0), (0, 0)))
        t = t.reshape(bsz, nh, dil, nblk + 1, n_off, dh)
        return jnp.concatenate([t[:, :, :, :-1], t[:, :, :, 1:]], axis=4)

    qb = to_sub(q).reshape(bsz, nh, dil, nblk, n_off, dh)
    kb, vb = to_band(k), to_band(v)
    s = jnp.einsum('bhrnid,bhrnjd->bhrnij', qb, kb).astype(jnp.float32) * dh ** -0.5
    i = jnp.arange(n_off)[:, None]
    j = jnp.arange(2 * n_off)[None, :]
    steps = i + n_off - j
    key_pos = (jnp.arange(nblk)[:, None, None] - 1) * n_off + j[None]
    mask = (steps >= 0) & (steps <= n_off) & (key_pos >= 0)
    p, lse = softmax_stats(jnp.where(mask, s, NEG_INF))
    o = jnp.einsum('bhrnij,bhrnjd->bhrnid', p.astype(v.dtype), vb)
    o = o.reshape(bsz, nh, dil, lp, dh)[:, :, :, :ln].transpose(0, 1, 3, 2, 4).reshape(bsz, nh, sd, dh)[:, :, :seq]
    lse = lse.reshape(bsz, nh, dil, lp)[..., :ln].transpose(0, 1, 3, 2).reshape(bsz, nh, sd)[..., :seq]
    return o, lse


def dilated_mixer(q, k, v):
    bsz, seq, _ = q.shape
    q = rope(q.reshape(bsz, seq, DIL_HEADS, HEAD_DIM)).transpose(0, 2, 1, 3)
    k = rope(k.reshape(bsz, seq, DIL_HEADS, HEAD_DIM)).transpose(0, 2, 1, 3)
    v = v.reshape(bsz, seq, DIL_HEADS, HEAD_DIM).transpose(0, 2, 1, 3)
    groups = [dilated_group(q, k, v, w, d) for (w, d) in DIL_PATTERNS]
    wts = jax.nn.softmax(jnp.stack([lse for _, lse in groups], axis=-1), axis=-1)
    out = jnp.einsum('bhsg,gbhsd->bhsd', wts.astype(v.dtype), jnp.stack([o for o, _ in groups]))
    return out.transpose(0, 2, 1, 3).reshape(bsz, seq, DIL_DIM).astype(v.dtype)


def cross_attention(h, mem, w_q, w_kv, w_o):
    bsz, seq, _ = h.shape
    q = (h @ w_q).reshape(bsz, seq, X_HEADS, X_HEAD_DIM)
    k, v = jnp.split(mem @ w_kv, 2, axis=-1)
    k = k.reshape(bsz, -1, X_HEADS, X_HEAD_DIM)
    v = v.reshape(bsz, -1, X_HEADS, X_HEAD_DIM)
    s = jnp.einsum('bshd,bmhd->bhsm', q, k).astype(jnp.float32) * X_HEAD_DIM ** -0.5
    p = jax.nn.softmax(s, axis=-1).astype(v.dtype)
    o = jnp.einsum('bhsm,bmhd->bshd', p, v).reshape(bsz, seq, D_MODEL)
    return o @ w_o


def peer_ffn(h, w_q, sub_keys, expert_u, expert_v):
    bsz, seq, d = h.shape
    t = bsz * seq
    x = h.reshape(t, d)
    q = (x @ w_q).reshape(t, PEER_HEADS, 2, PEER_QDIM // 2)
    s = jnp.einsum('thpd,hpkd->thpk', q, sub_keys).astype(jnp.float32)
    s_top, i_top = lax.top_k(s, PEER_TOPK)
    cand_s = (s_top[:, :, 0, :, None] + s_top[:, :, 1, None, :]).reshape(t, PEER_HEADS, PEER_TOPK * PEER_TOPK)
    cand_i = (i_top[:, :, 0, :, None] * PEER_NKEYS + i_top[:, :, 1, None, :]).reshape(t, PEER_HEADS, PEER_TOPK * PEER_TOPK)
    best_s, pos = lax.top_k(cand_s, PEER_TOPK)
    expert_idx = jnp.take_along_axis(cand_i, pos, axis=-1)
    gate = jax.nn.softmax(best_s, axis=-1)
    nt = t // PEER_TCHUNK

    def apply(args):
        xc, ic, gc = args
        u = expert_u[ic]
        vv = expert_v[ic]
        act = jax.nn.gelu(jnp.einsum('td,thkd->thk', xc, u).astype(jnp.float32), approximate=False)
        return jnp.einsum('thk,thkd->td', (act * gc).astype(vv.dtype), vv)

    y = lax.map(apply, (x.reshape(nt, PEER_TCHUNK, d),
                        expert_idx.reshape(nt, PEER_TCHUNK, PEER_HEADS, PEER_TOPK),
                        gate.reshape(nt, PEER_TCHUNK, PEER_HEADS, PEER_TOPK)))
    return y.reshape(bsz, seq, d).astype(h.dtype)


def setup_inputs(seed: int = 0) -> dict:
    key = jax.random.key(seed)
    ks = jax.random.split(key, 23)
    f32 = jnp.float32

    def nrm(k, shape, scale):
        return jax.random.normal(k, shape, f32) * scale

    def gain(k, shape):
        return 1.0 + 0.02 * jax.random.normal(k, shape, f32)

    dt0 = jnp.exp(jax.random.uniform(ks[6], (DEPTH, SSD_HEADS), f32, math.log(1e-3), math.log(1e-1)))
    dt_bias = dt0 + jnp.log(-jnp.expm1(-dt0))
    return {
        'x': nrm(ks[0], (BATCH, SEQ, D_MODEL), 1.0),
        'mem': nrm(ks[1], (BATCH, MEM_LEN, D_MODEL), 1.0),
        'mix_norm': gain(ks[2], (DEPTH, D_MODEL)),
        'w_in': nrm(ks[3], (DEPTH, D_MODEL, IN_DIM), D_MODEL ** -0.5),
        'ssd_conv_w': nrm(ks[4], (DEPTH, SSD_CONV, SSD_CONV_DIM), SSD_CONV ** -0.5),
        'ssd_conv_b': nrm(ks[5], (DEPTH, SSD_CONV_DIM), 0.01),
        'ssd_dt_bias': dt_bias,
        'ssd_a_log': jnp.log(jax.random.uniform(ks[7], (DEPTH, SSD_HEADS), f32, 1.0, 16.0)),
        'ssd_d': 1.0 + 0.1 * jax.random.normal(ks[8], (DEPTH, SSD_HEADS), f32),
        'ssd_norm': gain(ks[9], (DEPTH, SSD_INNER)),
        'ret_norm': gain(ks[10], (DEPTH, RET_DIM)),
        'w_branch': nrm(ks[11], (DEPTH, N_BRANCHES, MIX_WIDTH, D_MODEL), MIX_WIDTH ** -0.5),
        'w_out': nrm(ks[12], (DEPTH, D_MODEL, D_MODEL), D_MODEL ** -0.5),
        'x_norm': gain(ks[13], (DEPTH, D_MODEL)),
        'w_xq': nrm(ks[14], (DEPTH, D_MODEL, D_MODEL), D_MODEL ** -0.5),
        'w_xkv': nrm(ks[15], (DEPTH, D_MODEL, 2 * D_MODEL), D_MODEL ** -0.5),
        'w_xo': nrm(ks[16], (DEPTH, D_MODEL, D_MODEL), D_MODEL ** -0.5),
        'ffn_norm': gain(ks[17], (DEPTH, D_MODEL)),
        'w_pq': nrm(ks[18], (DEPTH, D_MODEL, PEER_HEADS * PEER_QDIM), D_MODEL ** -0.5),
        'peer_sub_keys': nrm(ks[19], (DEPTH, PEER_HEADS, 2, PEER_NKEYS, PEER_QDIM // 2), (PEER_QDIM // 2) ** -0.5),
        'peer_u': nrm(ks[20], (DEPTH, PEER_EXPERTS, D_MODEL), D_MODEL ** -0.5),
        'peer_v': nrm(ks[21], (DEPTH, PEER_EXPERTS, D_MODEL), (PEER_HEADS * PEER_TOPK) ** -0.5),
        'final_norm': gain(ks[22], (D_MODEL,)),
    }


def reference(x, mem, mix_norm, w_in, ssd_conv_w, ssd_conv_b, ssd_dt_bias, ssd_a_log, ssd_d, ssd_norm,
              ret_norm, w_branch, w_out, x_norm, w_xq, w_xkv, w_xo, ffn_norm, w_pq, peer_sub_keys,
              peer_u, peer_v, final_norm):
    bsz, seq, _ = x.shape
    pts = _split_points()
    h = x
    for layer in range(DEPTH):
        hn = rms_norm(h, mix_norm[layer])
        (z, xbc, dt_raw, rq, rk, rv, rg, mq, mk, mv, dq, dk, dv,
         gate_logits) = jnp.split(hn @ w_in[layer], pts, axis=-1)
        ys = (ssd_mixer(z, xbc, dt_raw, ssd_conv_w[layer], ssd_conv_b[layer], ssd_dt_bias[layer],
                        ssd_a_log[layer], ssd_d[layer], ssd_norm[layer]),
              retention_mixer(rq, rk, rv, rg, ret_norm[layer]),
              moba_mixer(mq, mk, mv),
              dilated_mixer(dq, dk, dv))
        gates = jax.nn.sigmoid(gate_logits.reshape(bsz, seq, N_BRANCHES, D_MODEL))
        merged = sum(gates[:, :, i] * (ys[i] @ w_branch[layer, i]) for i in range(N_BRANCHES))
        h = h + merged @ w_out[layer]
        h = h + cross_attention(rms_norm(h, x_norm[layer]), mem, w_xq[layer], w_xkv[layer], w_xo[layer])
        h = h + peer_ffn(rms_norm(h, ffn_norm[layer]), w_pq[layer], peer_sub_keys[layer],
                         peer_u[layer], peer_v[layer])
    return rms_norm(h, final_norm)
```

```python
import functools
import math

import jax
import jax.numpy as jnp
from jax import lax
from jax.experimental import pallas as pl
from jax.experimental.pallas import tpu as pltpu

F32 = jnp.float32
BF16 = jnp.bfloat16
HI = lax.Precision.HIGHEST

NORM_EPS = 1e-6
NEG_INF = -1e30
ROPE_THETA = 10000.0
HEAD_DIM = 64
N_HEADS = 8
MIX_WIDTH = N_HEADS * HEAD_DIM
SSD_GROUPS = 2
SSD_STATE = 64
SSD_CONV = 4
CHUNK = 128
ATT_BLOCK = 256
MOBA_TOPK = 3
DIL_PATTERNS = ((128, 1), (512, 4), (2048, 16))
N_BRANCHES = 4
X_HEADS = 4
PEER_HEADS = 8
PEER_NKEYS = 128
PEER_TOPK = 16
LANES = 128
VMEM_LIMIT = 56 * 1024 * 1024


def _cparams(*sem):
    return pltpu.CompilerParams(dimension_semantics=sem, vmem_limit_bytes=VMEM_LIMIT)


def _dot(a, b):
    return jnp.dot(a.astype(BF16), b.astype(BF16), preferred_element_type=F32)


def _dot_nt(a, b, precision=None):
    return lax.dot_general(a, b, (((1,), (1,)), ((), ())), precision=precision,
                           preferred_element_type=F32)


def _silu(x):
    return x * (1.0 / (1.0 + jnp.exp(-x)))


def _norm_matmul_body(x_ref, g_ref, w_ref, o_ref, xn_ref, *, normalize):
    @pl.when(pl.program_id(1) == 0)
    def _():
        x = x_ref[...]
        if normalize:
            ms = jnp.mean(x * x, axis=-1, keepdims=True)
            x = x * lax.rsqrt(ms + NORM_EPS) * g_ref[...]
        xn_ref[...] = x.astype(xn_ref.dtype)

    o_ref[...] = jnp.dot(xn_ref[...], w_ref[...], preferred_element_type=F32).astype(o_ref.dtype)


def norm_matmul(x, gain, w, *, normalize=True, tm=512, tn=512, out_dtype=F32):
    t, k = x.shape
    n = w.shape[1]
    tm, tn = min(tm, t), min(tn, n)
    assert t % tm == 0 and n % tn == 0
    return pl.pallas_call(
        functools.partial(_norm_matmul_body, normalize=normalize),
        grid=(t // tm, n // tn),
        in_specs=[pl.BlockSpec((tm, k), lambda i, j: (i, 0)),
                  pl.BlockSpec((1, k), lambda i, j: (0, 0)),
                  pl.BlockSpec((k, tn), lambda i, j: (0, j))],
        out_specs=pl.BlockSpec((tm, tn), lambda i, j: (i, j)),
        out_shape=jax.ShapeDtypeStruct((t, n), out_dtype),
        scratch_shapes=[pltpu.VMEM((tm, k), BF16)],
        compiler_params=_cparams("parallel", "arbitrary"),
        name="norm_matmul",
    )(x, gain.reshape(1, k), w)


def _ssd_body(xs_ref, bt_ref, c_ref, dt_ref, la_ref, lat_ref, z_ref, dskip_ref, gain_ref,
              o_ref, state_ref):
    seq = xs_ref.shape[1]
    nchunks = seq // CHUNK
    rep = N_HEADS // SSD_GROUPS
    state_ref[...] = jnp.zeros_like(state_ref)
    row = lax.broadcasted_iota(jnp.int32, (CHUNK, CHUNK), 0)
    col = lax.broadcasted_iota(jnp.int32, (CHUNK, CHUNK), 1)
    tril = row >= col
    tril_f = tril.astype(F32)
    triu_f = (row <= col).astype(F32)

    def chunk(c, carry):
        r0 = pl.multiple_of(c * CHUNK, CHUNK)
        rows = pl.ds(r0, CHUNK)
        acum = jnp.dot(tril_f, la_ref[0, rows, :], precision=HI, preferred_element_type=F32)
        acum_t = jnp.dot(lat_ref[0, c], triu_f, precision=HI, preferred_element_type=F32)
        dt = dt_ref[0, rows, :]
        ys = []
        for h in range(N_HEADS):
            g = h // rep
            hs = slice(h * HEAD_DIM, (h + 1) * HEAD_DIM)
            gs = slice(g * SSD_STATE, (g + 1) * SSD_STATE)
            a_col = acum[:, h:h + 1]
            a_row = acum_t[h:h + 1, :]
            a_last = acum[CHUNK - 1:CHUNK, h:h + 1]
            decay = jnp.exp(jnp.where(tril, a_col - a_row, -jnp.inf))
            cc = c_ref[0, rows, gs]
            bt = bt_ref[0, c, gs, :]
            x_h = xs_ref[0, rows, hs]
            xdt = x_h * dt[:, h:h + 1]
            scores = _dot(cc, bt) * decay
            prev = state_ref[h]
            y = _dot(scores, xdt) + _dot(cc, prev) * jnp.exp(a_col)
            state_ref[h] = prev * jnp.exp(a_last) + _dot(bt * jnp.exp(a_last - a_row), xdt)
            ys.append(y + x_h * dskip_ref[0:1, hs])
        y = jnp.concatenate(ys, axis=-1)
        y = y * _silu(z_ref[0, rows, :])
        ms = jnp.mean(y * y, axis=-1, keepdims=True)
        o_ref[0, rows, :] = (y * lax.rsqrt(ms + NORM_EPS) * gain_ref[...]).astype(o_ref.dtype)
        return carry

    lax.fori_loop(0, nchunks, chunk, 0)


def ssd_mixer(xs, bm_t, cm, dt, log_a, log_a_t, z, d_skip_lanes, norm_gain):
    bsz, seq, width = xs.shape
    gn = SSD_GROUPS * SSD_STATE
    nc = seq // CHUNK
    full = lambda shp: pl.BlockSpec(shp, lambda b: (b,) + (0,) * (len(shp) - 1))
    const = lambda shp: pl.BlockSpec(shp, lambda b: (0,) * len(shp))
    return pl.pallas_call(
        _ssd_body,
        grid=(bsz,),
        in_specs=[full((1, seq, width)), full((1, nc, gn, CHUNK)), full((1, seq, gn)),
                  full((1, seq, N_HEADS)), full((1, seq, N_HEADS)), full((1, nc, N_HEADS, CHUNK)),
                  full((1, seq, width)), const((1, width)), const((1, width))],
        out_specs=full((1, seq, width)),
        out_shape=jax.ShapeDtypeStruct((bsz, seq, width), F32),
        scratch_shapes=[pltpu.VMEM((N_HEADS, SSD_STATE, HEAD_DIM), F32)],
        compiler_params=_cparams("parallel"),
        name="ssd_mixer",
    )(xs, bm_t, cm, dt, log_a, log_a_t, z, d_skip_lanes, norm_gain.reshape(1, width))


def _ret_body(q_ref, kt_ref, v_ref, g_ref, dmat_ref, zeta_ref, xi_ref, cdec_ref, gain_ref,
              o_ref, state_ref):
    seq = q_ref.shape[1]
    nchunks = seq // CHUNK
    state_ref[...] = jnp.zeros_like(state_ref)

    def chunk(c, carry):
        r0 = pl.multiple_of(c * CHUNK, CHUNK)
        rows = pl.ds(r0, CHUNK)
        ys = []
        for h in range(N_HEADS):
            hs = slice(h * HEAD_DIM, (h + 1) * HEAD_DIM)
            qc = q_ref[0, rows, hs]
            kt = kt_ref[0, c, hs, :]
            vc = v_ref[0, rows, hs]
            inner = _dot(qc, kt) * dmat_ref[h]
            prev = state_ref[h]
            y = _dot(inner, vc) + _dot(qc, prev) * xi_ref[h]
            state_ref[h] = prev * cdec_ref[h] + _dot(kt * zeta_ref[h], vc)
            mu = jnp.mean(y, axis=-1, keepdims=True)
            yc = y - mu
            var = jnp.mean(yc * yc, axis=-1, keepdims=True)
            ys.append(yc * lax.rsqrt(var + NORM_EPS))
        y = jnp.concatenate(ys, axis=-1) * gain_ref[...]
        o_ref[0, rows, :] = (_silu(g_ref[0, rows, :]) * y).astype(o_ref.dtype)
        return carry

    lax.fori_loop(0, nchunks, chunk, 0)


def retention_mixer(q, k_t, v, g, norm_gain):
    bsz, seq, width = q.shape
    nc = seq // CHUNK
    log_gamma = jnp.log1p(-jnp.exp2(-5.0 - jnp.arange(N_HEADS, dtype=F32)))
    idx = jnp.arange(CHUNK, dtype=F32)
    rel = idx[:, None] - idx[None, :]
    dmat = jnp.where(rel >= 0, jnp.exp(jnp.maximum(rel, 0.0)[None] * log_gamma[:, None, None]), 0.0)
    zeta = jnp.exp((CHUNK - 1.0 - idx)[None, None, :] * log_gamma[:, None, None])
    xi = jnp.exp((idx + 1.0)[None, :, None] * log_gamma[:, None, None])
    cdec = jnp.exp(CHUNK * log_gamma)[:, None, None]
    full = lambda shp: pl.BlockSpec(shp, lambda b: (b,) + (0,) * (len(shp) - 1))
    const = lambda shp: pl.BlockSpec(shp, lambda b: (0,) * len(shp))
    return pl.pallas_call(
        _ret_body,
        grid=(bsz,),
        in_specs=[full((1, seq, width)), full((1, nc, width, CHUNK)), full((1, seq, width)),
                  full((1, seq, width)), const((N_HEADS, CHUNK, CHUNK)), const((N_HEADS, 1, CHUNK)),
                  const((N_HEADS, CHUNK, 1)), const((N_HEADS, 1, 1)), const((1, width))],
        out_specs=full((1, seq, width)),
        out_shape=jax.ShapeDtypeStruct((bsz, seq, width), F32),
        scratch_shapes=[pltpu.VMEM((N_HEADS, HEAD_DIM, HEAD_DIM), F32)],
        compiler_params=_cparams("parallel"),
        name="retention_mixer",
    )(q, k_t, v, g, dmat, zeta, xi, cdec, norm_gain.reshape(1, width))


def _online_update(m, l, acc, s, w, v):
    s = jnp.where(w > 0, s, NEG_INF)
    m_new = jnp.maximum(m, jnp.max(s, axis=-1, keepdims=True))
    alpha = jnp.exp(m - m_new)
    p = jnp.exp(s - m_new) * w
    l = alpha * l + jnp.sum(p, axis=-1, keepdims=True)
    acc = alpha * acc + _dot(p, v)
    return m_new, l, acc


def _moba_body(q_ref, kt_ref, k_ref, v_ref, o_ref):
    seq = q_ref.shape[1]
    nb = seq // ATT_BLOCK
    scale = HEAD_DIM ** -0.5
    heads = q_ref.shape[2] // HEAD_DIM
    row = lax.broadcasted_iota(jnp.int32, (ATT_BLOCK, ATT_BLOCK), 0)
    col = lax.broadcasted_iota(jnp.int32, (ATT_BLOCK, ATT_BLOCK), 1)
    tril_f = (row >= col).astype(F32)
    blk_of_key = lax.broadcasted_iota(jnp.int32, (nb, seq), 1) // ATT_BLOCK
    blk_id = lax.broadcasted_iota(jnp.int32, (nb, seq), 0)
    pool = jnp.where(blk_of_key == blk_id, 1.0 / ATT_BLOCK, 0.0).astype(F32)
    lane_blk = lax.broadcasted_iota(jnp.int32, (ATT_BLOCK, nb), 1)

    for h in range(heads):
        hs = slice(h * HEAD_DIM, (h + 1) * HEAD_DIM)
        k_mean = jnp.dot(pool, k_ref[0, :, hs], precision=HI, preferred_element_type=F32)

        def q_tile(i, carry):
            r0 = pl.multiple_of(i * ATT_BLOCK, ATT_BLOCK)
            rows = pl.ds(r0, ATT_BLOCK)
            q = q_ref[0, rows, hs]
            gate = _dot_nt(q, k_mean, precision=HI)
            past = lane_blk < i
            n_sel = min(MOBA_TOPK, nb - 1)
            sel = jnp.zeros((ATT_BLOCK, nb), F32)
            for n in range(nb):
                g_n = gate[:, n:n + 1]
                beats = past & ((gate > g_n) | ((gate == g_n) & (lane_blk < n)))
                cnt = jnp.sum(beats.astype(F32), axis=-1, keepdims=True)
                sel = jnp.where((lane_blk == n) & past & (cnt < n_sel), 1.0, sel)
            qs = q * scale
            s0 = _dot(qs, kt_ref[0, i, hs, :])
            m0 = jnp.full((ATT_BLOCK, 1), NEG_INF, F32)
            l0 = jnp.zeros((ATT_BLOCK, 1), F32)
            a0 = jnp.zeros((ATT_BLOCK, HEAD_DIM), F32)
            state = _online_update(m0, l0, a0, s0, tril_f, v_ref[0, rows, hs])

            def k_tile(j, st):
                c0 = pl.multiple_of(j * ATT_BLOCK, ATT_BLOCK)
                cols = pl.ds(c0, ATT_BLOCK)
                w = jnp.sum(jnp.where(lane_blk == j, sel, 0.0), axis=-1, keepdims=True)
                s = _dot(qs, kt_ref[0, j, hs, :])
                return _online_update(*st, s, jnp.broadcast_to(w, s.shape), v_ref[0, cols, hs])

            m, l, acc = lax.fori_loop(0, i, k_tile, state)
            o_ref[0, rows, hs] = (acc / l).astype(o_ref.dtype)
            return carry

        lax.fori_loop(0, nb, q_tile, 0)


def _dil_body(q_ref, kt_ref, v_ref, mult_ref, o_ref):
    seq = q_ref.shape[1]
    nb = seq // ATT_BLOCK
    scale = HEAD_DIM ** -0.5
    heads = q_ref.shape[2] // HEAD_DIM

    for h in range(heads):
        hs = slice(h * HEAD_DIM, (h + 1) * HEAD_DIM)

        def q_tile(i, carry):
            r0 = pl.multiple_of(i * ATT_BLOCK, ATT_BLOCK)
            rows = pl.ds(r0, ATT_BLOCK)
            qs = q_ref[0, rows, hs] * scale
            m0 = jnp.full((ATT_BLOCK, 1), NEG_INF, F32)
            l0 = jnp.zeros((ATT_BLOCK, 1), F32)
            a0 = jnp.zeros((ATT_BLOCK, HEAD_DIM), F32)

            def k_tile(jj, st):
                j = i - jj
                c0 = pl.multiple_of(j * ATT_BLOCK, ATT_BLOCK)
                cols = pl.ds(c0, ATT_BLOCK)
                s = _dot(qs, kt_ref[0, j, hs, :])
                return _online_update(*st, s, mult_ref[jj], v_ref[0, cols, hs])

            m, l, acc = lax.fori_loop(0, i + 1, k_tile, (m0, l0, a0))
            o_ref[0, rows, hs] = (acc / l).astype(o_ref.dtype)
            return carry

        lax.fori_loop(0, nb, q_tile, 0)


def _dilation_multiplicity(seq):
    nb = seq // ATT_BLOCK
    r = jnp.arange(ATT_BLOCK)
    d = (jnp.arange(nb)[:, None, None] * ATT_BLOCK + r[None, :, None] - r[None, None, :])
    mult = jnp.zeros(d.shape, F32)
    for window, dil in DIL_PATTERNS:
        mult = mult + ((d >= 0) & (d <= window) & (d % dil == 0)).astype(F32)
    return mult


def _head_pair_specs(seq):
    tok = pl.BlockSpec((1, seq, LANES), lambda b, p: (b, 0, p))
    tr = pl.BlockSpec((1, seq // ATT_BLOCK, LANES, ATT_BLOCK), lambda b, p: (b, 0, p, 0))
    return tok, tr


def moba_mixer(q, k_t, k, v):
    bsz, seq, width = q.shape
    tok, tr = _head_pair_specs(seq)
    return pl.pallas_call(
        _moba_body,
        grid=(bsz, width // LANES),
        in_specs=[tok, tr, tok, tok],
        out_specs=tok,
        out_shape=jax.ShapeDtypeStruct((bsz, seq, width), F32),
        compiler_params=_cparams("parallel", "parallel"),
        name="moba_mixer",
    )(q, k_t, k, v)


def dilated_mixer(q, k_t, v):
    bsz, seq, width = q.shape
    nb = seq // ATT_BLOCK
    tok, tr = _head_pair_specs(seq)
    mult = _dilation_multiplicity(seq)
    return pl.pallas_call(
        _dil_body,
        grid=(bsz, width // LANES),
        in_specs=[tok, tr, tok, pl.BlockSpec((nb, ATT_BLOCK, ATT_BLOCK), lambda b, p: (0, 0, 0))],
        out_specs=tok,
        out_shape=jax.ShapeDtypeStruct((bsz, seq, width), F32),
        compiler_params=_cparams("parallel", "parallel"),
        name="dilated_mixer",
    )(q, k_t, v, mult)


def _merge_body(h_ref, gates_ref, y0_ref, y1_ref, y2_ref, y3_ref, wb_ref, wo_ref, o_ref):
    d = h_ref.shape[1]
    merged = jnp.zeros(h_ref.shape, F32)
    for i, y_ref in enumerate((y0_ref, y1_ref, y2_ref, y3_ref)):
        gate = 1.0 / (1.0 + jnp.exp(-gates_ref[:, i * d:(i + 1) * d]))
        merged = merged + gate * _dot(y_ref[...], wb_ref[i])
    o_ref[...] = h_ref[...] + _dot(merged, wo_ref[...])


def gated_merge(h, proj, ys, w_branch, w_out, tm=512):
    t, d = h.shape
    width = ys[0].shape[1]
    tm = min(tm, t)
    rowblk = lambda w: pl.BlockSpec((tm, w), lambda i: (i, 0))
    return pl.pallas_call(
        _merge_body,
        grid=(t // tm,),
        in_specs=[rowblk(d), rowblk(N_BRANCHES * d)] + [rowblk(width)] * N_BRANCHES
                 + [pl.BlockSpec((N_BRANCHES, width, d), lambda i: (0, 0, 0)),
                    pl.BlockSpec((d, d), lambda i: (0, 0))],
        out_specs=rowblk(d),
        out_shape=jax.ShapeDtypeStruct((t, d), F32),
        compiler_params=_cparams("parallel"),
        name="gated_merge",
    )(h, proj, *ys, w_branch, w_out)


def _xattn_body(h_ref, gain_ref, wq_ref, k_ref, v_ref, wo_ref, o_ref):
    h = h_ref[0]
    d = h.shape[1]
    dh = d // X_HEADS
    ms = jnp.mean(h * h, axis=-1, keepdims=True)
    hn = h * lax.rsqrt(ms + NORM_EPS) * gain_ref[...]
    q = _dot(hn, wq_ref[...]) * dh ** -0.5
    outs = []
    for a in range(X_HEADS):
        cs = slice(a * dh, (a + 1) * dh)
        s = _dot_nt(q[:, cs].astype(BF16), k_ref[0, :, cs])
        m = jnp.max(s, axis=-1, keepdims=True)
        e = jnp.exp(s - m)
        p = e / jnp.sum(e, axis=-1, keepdims=True)
        outs.append(_dot(p, v_ref[0, :, cs]))
    o = jnp.concatenate(outs, axis=-1)
    o_ref[0] = h + _dot(o, wo_ref[...])


def cross_attention(h, kv, gain, w_q, w_o, tq=512):
    bsz, seq, d = h.shape
    mlen = kv.shape[1]
    tq = min(tq, seq)
    return pl.pallas_call(
        _xattn_body,
        grid=(bsz, seq // tq),
        in_specs=[pl.BlockSpec((1, tq, d), lambda b, i: (b, i, 0)),
                  pl.BlockSpec((1, d), lambda b, i: (0, 0)),
                  pl.BlockSpec((d, d), lambda b, i: (0, 0)),
                  pl.BlockSpec((1, mlen, d), lambda b, i: (b, 0, 0)),
                  pl.BlockSpec((1, mlen, d), lambda b, i: (b, 0, 1)),
                  pl.BlockSpec((d, d), lambda b, i: (0, 0))],
        out_specs=pl.BlockSpec((1, tq, d), lambda b, i: (b, i, 0)),
        out_shape=jax.ShapeDtypeStruct((bsz, seq, d), F32),
        compiler_params=_cparams("parallel", "parallel"),
        name="cross_attention",
    )(h, gain.reshape(1, d), w_q, kv, kv, w_o)


def _top_rows(vals, count):
    tops = []
    for _ in range(count):
        m = jnp.max(vals, axis=0, keepdims=True)
        tops.append(m)
        vals = jnp.where(vals == m, -jnp.inf, vals)
    return tops


_N_RANKS = PEER_TOPK + 1
_RANK_ROWS = 24


def _pair_candidates(a1, a2_ref):
    blocks = []
    for i in range(_N_RANKS):
        n_j = _N_RANKS // (i + 1)
        rows = -(-n_j // 8) * 8
        blk = a1[i] + a2_ref[0:rows, :]
        if n_j % 8:
            rank = lax.broadcasted_iota(jnp.int32, blk.shape, 0)
            blk = jnp.where(rank < n_j, blk, -jnp.inf)
        blocks.append(blk)
    return jnp.concatenate(blocks, axis=0)


def _router_body(h_ref, gain_ref, wq_ref, keys_ref, s2_ref, e2_ref, thr_ref, e1_ref, a2_ref):
    h = h_ref[...]
    ms = jnp.mean(h * h, axis=-1, keepdims=True)
    hn = h * lax.rsqrt(ms + NORM_EPS) * gain_ref[...]
    q = jnp.dot(hn, wq_ref[...], precision=HI, preferred_element_type=F32)
    dk = keys_ref.shape[-1]
    for hd in range(PEER_HEADS):
        s = []
        for half in range(2):
            c0 = (hd * 2 + half) * dk
            s.append(_dot_nt(keys_ref[hd, half], q[:, c0:c0 + dk], precision=HI))
        s1, s2 = s
        a1 = _top_rows(s1, _N_RANKS)
        a2 = _top_rows(s2, _N_RANKS)
        a2_ref[...] = jnp.full(a2_ref.shape, -jnp.inf, F32)
        for r in range(_N_RANKS):
            a2_ref[r:r + 1, :] = a2[r]
        cand = _pair_candidates(a1, a2_ref)
        top = _top_rows(cand, _N_RANKS)
        tau = 0.5 * (top[PEER_TOPK - 1] + top[PEER_TOPK])
        z = jnp.sum(jnp.where(cand > tau, jnp.exp(cand - (a1[0] + a2[0])), 0.0), axis=0, keepdims=True)
        s2_ref[hd] = s2
        e2_ref[hd] = jnp.exp(s2 - a2[0])
        thr_ref[hd] = tau - s1
        e1_ref[hd] = jnp.exp(s1 - a1[0]) / z


def peer_router(h, gain, w_q, sub_keys, tt=256):
    t, d = h.shape
    tt = min(tt, t)
    nk, dk = sub_keys.shape[2], sub_keys.shape[3]
    out = jax.ShapeDtypeStruct((PEER_HEADS, nk, t), F32)
    ospec = pl.BlockSpec((PEER_HEADS, nk, tt), lambda i: (0, 0, i))
    return pl.pallas_call(
        _router_body,
        grid=(t // tt,),
        in_specs=[pl.BlockSpec((tt, d), lambda i: (i, 0)),
                  pl.BlockSpec((1, d), lambda i: (0, 0)),
                  pl.BlockSpec(w_q.shape, lambda i: (0, 0)),
                  pl.BlockSpec(sub_keys.shape, lambda i: (0, 0, 0, 0))],
        out_specs=[ospec] * 4,
        out_shape=[out] * 4,
        scratch_shapes=[pltpu.VMEM((_RANK_ROWS, tt), F32)],
        compiler_params=_cparams("parallel"),
        name="peer_router",
    )(h, gain.reshape(1, d), w_q, sub_keys)


def _gelu(x):
    return 0.5 * x * (1.0 + lax.erf(x * (2.0 ** -0.5)))


def _experts_body(h_ref, gain_ref, u_ref, vt_ref, s2_ref, e2_ref, thr_ref, e1_ref, o_ref,
                  xn_ref, acc_ref, p_ref, *, rows_per_step, sub):
    step = pl.program_id(1)
    nk = s2_ref.shape[1]
    tt = h_ref.shape[0]

    @pl.when(step == 0)
    def _():
        h = h_ref[...]
        ms = jnp.mean(h * h, axis=-1, keepdims=True)
        xn_ref[...] = (h * lax.rsqrt(ms + NORM_EPS) * gain_ref[...]).astype(BF16)
        acc_ref[...] = jnp.zeros_like(acc_ref)

    act_t = _dot_nt(u_ref[...], xn_ref[...])
    i0 = pl.multiple_of(step * rows_per_step, rows_per_step)
    for c in range(tt // sub):
        ts = slice(c * sub, (c + 1) * sub)
        thr_rows = [thr_ref[hd, pl.ds(i0, rows_per_step), ts] for hd in range(PEER_HEADS)]
        e1_rows = [e1_ref[hd, pl.ds(i0, rows_per_step), ts] for hd in range(PEER_HEADS)]
        for r in range(rows_per_step):
            gate = jnp.zeros((nk, sub), F32)
            for hd in range(PEER_HEADS):
                thr = thr_rows[hd][r:r + 1]
                e1 = e1_rows[hd][r:r + 1]
                gate = gate + jnp.where(s2_ref[hd, :, ts] > thr, e2_ref[hd, :, ts] * e1, 0.0)
            a = act_t[r * nk:(r + 1) * nk, ts]
            p_ref[r * nk:(r + 1) * nk, ts] = (_gelu(a) * gate).astype(BF16)
    acc_ref[...] += jnp.dot(vt_ref[...], p_ref[...], preferred_element_type=F32)

    @pl.when(step == pl.num_programs(1) - 1)
    def _():
        o_ref[...] = h_ref[...] + acc_ref[...].T


def peer_experts(h, gain, u, v_t, s2, e2, thr, e1, tt=512, rows_per_step=8, sub=128):
    t, d = h.shape
    nk = s2.shape[1]
    tt = min(tt, t)
    assert rows_per_step % 8 == 0
    et = rows_per_step * nk
    rspec = pl.BlockSpec((PEER_HEADS, nk, tt), lambda i, s: (0, 0, i))
    return pl.pallas_call(
        functools.partial(_experts_body, rows_per_step=rows_per_step, sub=sub),
        grid=(t // tt, nk // rows_per_step),
        in_specs=[pl.BlockSpec((tt, d), lambda i, s: (i, 0)),
                  pl.BlockSpec((1, d), lambda i, s: (0, 0)),
                  pl.BlockSpec((et, d), lambda i, s: (s, 0)),
                  pl.BlockSpec((d, et), lambda i, s: (0, s)),
                  rspec, rspec, rspec, rspec],
        out_specs=pl.BlockSpec((tt, d), lambda i, s: (i, 0)),
        out_shape=jax.ShapeDtypeStruct((t, d), F32),
        scratch_shapes=[pltpu.VMEM((tt, d), BF16), pltpu.VMEM((d, tt), F32), pltpu.VMEM((et, tt), BF16)],
        compiler_params=_cparams("parallel", "arbitrary"),
        name="peer_experts",
    )(h, gain.reshape(1, d), u, v_t, s2, e2, thr, e1)


def _final_norm_body(x_ref, g_ref, o_ref):
    x = x_ref[...]
    ms = jnp.mean(x * x, axis=-1, keepdims=True)
    o_ref[...] = x * lax.rsqrt(ms + NORM_EPS) * g_ref[...]


def final_norm(x, gain, tm=1024):
    t, d = x.shape
    tm = min(tm, t)
    return pl.pallas_call(
        _final_norm_body,
        grid=(t // tm,),
        in_specs=[pl.BlockSpec((tm, d), lambda i: (i, 0)), pl.BlockSpec((1, d), lambda i: (0, 0))],
        out_specs=pl.BlockSpec((tm, d), lambda i: (i, 0)),
        out_shape=jax.ShapeDtypeStruct((t, d), F32),
        compiler_params=_cparams("parallel"),
        name="final_norm",
    )(x, gain.reshape(1, d))


def _rope(t):
    bsz, seq, width = t.shape
    half = HEAD_DIM // 2
    inv_freq = ROPE_THETA ** (-jnp.arange(half, dtype=F32) / half)
    ang = jnp.arange(seq, dtype=F32)[:, None] * inv_freq[None, :]
    cos = jnp.cos(ang)[None, :, None, :]
    sin = jnp.sin(ang)[None, :, None, :]
    th = t.reshape(bsz, seq, width // HEAD_DIM, HEAD_DIM)
    t1, t2 = th[..., :half], th[..., half:]
    return jnp.concatenate([t1 * cos - t2 * sin, t2 * cos + t1 * sin], axis=-1).reshape(bsz, seq, width)


def _causal_dwconv(x, w, b):
    k = w.shape[0]
    seq = x.shape[1]
    xp = jnp.pad(x, ((0, 0), (k - 1, 0), (0, 0)))
    return sum(xp[:, i:i + seq, :] * w[i] for i in range(k)) + b


def _tr_blocks(t, blk):
    bsz, seq, width = t.shape
    return jnp.swapaxes(t.reshape(bsz, seq // blk, blk, width), 2, 3)


_DT_PAD = 256


def _pack_w_in(w_in_l, d_model):
    ssd_conv_dim = MIX_WIDTH + 2 * SSD_GROUPS * SSD_STATE
    o_xbc = MIX_WIDTH
    o_dt = o_xbc + ssd_conv_dim
    o_rest = o_dt + N_HEADS
    o_gate = o_rest + 10 * MIX_WIDTH
    dt_cols = jnp.pad(w_in_l[:, o_dt:o_rest], ((0, 0), (0, _DT_PAD - N_HEADS)))
    return jnp.concatenate([w_in_l[:, o_gate:], w_in_l[:, :o_dt], w_in_l[:, o_rest:o_gate], dt_cols],
                           axis=1).astype(BF16)


def kernel(x, mem, mix_norm, w_in, ssd_conv_w, ssd_conv_b, ssd_dt_bias, ssd_a_log, ssd_d, ssd_norm, ret_norm, w_branch, w_out, x_norm, w_xq, w_xkv, w_xo, ffn_norm, w_pq, peer_sub_keys, peer_u, peer_v, final_norm_gain):
    bsz, seq, d = x.shape
    depth = w_in.shape[0]
    t = bsz * seq
    w = MIX_WIDTH
    gn = SSD_GROUPS * SSD_STATE
    h = x.reshape(t, d)
    mem2 = mem.reshape(-1, d)
    ones = jnp.ones((d,), F32)

    for layer in range(depth):
        proj = norm_matmul(h, mix_norm[layer], _pack_w_in(w_in[layer], d), tm=1024)
        widths = [w, w + 2 * gn] + [w] * 10 + [_DT_PAD]
        offs = [N_BRANCHES * d]
        for width in widths:
            offs.append(offs[-1] + width)
        (z, xbc, rq, rk, rv, rg, mq, mk, mv, dq, dk, dv, dt_pad) = [
            proj[:, a:b].reshape(bsz, seq, b - a) for a, b in zip(offs[:-1], offs[1:])]
        dt_raw = dt_pad[..., :N_HEADS]

        xbc = _silu(_causal_dwconv(xbc, ssd_conv_w[layer], ssd_conv_b[layer]))
        xs, bm, cm = xbc[..., :w], xbc[..., w:w + gn], xbc[..., w + gn:]
        dt = jax.nn.softplus(dt_raw + ssd_dt_bias[layer])
        log_a = dt * -jnp.exp(ssd_a_log[layer])
        d_lanes = jnp.repeat(ssd_d[layer], HEAD_DIM).reshape(1, w)
        y_a = ssd_mixer(xs, _tr_blocks(bm, CHUNK), cm, dt, log_a, _tr_blocks(log_a, CHUNK), z,
                        d_lanes, ssd_norm[layer])
        y_b = retention_mixer(_rope(rq), _tr_blocks(_rope(rk) * HEAD_DIM ** -0.5, CHUNK), rv, rg,
                              ret_norm[layer])
        mk_r = _rope(mk)
        y_c = moba_mixer(_rope(mq), _tr_blocks(mk_r, ATT_BLOCK).astype(BF16), mk_r, mv.astype(BF16))
        y_d = dilated_mixer(_rope(dq), _tr_blocks(_rope(dk), ATT_BLOCK).astype(BF16), dv.astype(BF16))
        ys = [y.reshape(t, w) for y in (y_a, y_b, y_c, y_d)]
        h = gated_merge(h, proj, ys, w_branch[layer].astype(BF16), w_out[layer].astype(BF16))
        kv = norm_matmul(mem2, ones, w_xkv[layer].astype(BF16), normalize=False, out_dtype=BF16)
        h = cross_attention(h.reshape(bsz, seq, d), kv.reshape(bsz, -1, 2 * d), x_norm[layer],
                            w_xq[layer].astype(BF16), w_xo[layer].astype(BF16)).reshape(t, d)
        s2, e2, thr, e1 = peer_router(h, ffn_norm[layer], w_pq[layer], peer_sub_keys[layer])
        h = peer_experts(h, ffn_norm[layer], peer_u[layer].astype(BF16),
                         peer_v[layer].T.astype(BF16), s2, e2, thr, e1)

    return final_norm(h, final_norm_gain).reshape(bsz, seq, d)
```

```python
import functools
import math

import jax
import jax.numpy as jnp
from jax import lax
from jax.experimental import pallas as pl
from jax.experimental.pallas import tpu as pltpu

F32 = jnp.float32
BF16 = jnp.bfloat16
HI = lax.Precision.HIGHEST

NORM_EPS = 1e-6
NEG_INF = -1e30
ROPE_THETA = 10000.0
HEAD_DIM = 64
N_HEADS = 8
MIX_WIDTH = N_HEADS * HEAD_DIM
SSD_GROUPS = 2
SSD_STATE = 64
SSD_CONV = 4
CHUNK = 128
ATT_BLOCK = 256
MOBA_TOPK = 3
DIL_PATTERNS = ((128, 1), (512, 4), (2048, 16))
N_BRANCHES = 4
X_HEADS = 4
PEER_HEADS = 8
PEER_NKEYS = 128
PEER_TOPK = 16
LANES = 128
VMEM_LIMIT = 56 * 1024 * 1024


def _cparams(*sem):
    return pltpu.CompilerParams(dimension_semantics=sem, vmem_limit_bytes=VMEM_LIMIT)


def _dot(a, b):
    return jnp.dot(a.astype(BF16), b.astype(BF16), preferred_element_type=F32)


def _dot_nt(a, b, precision=None):
    return lax.dot_general(a, b, (((1,), (1,)), ((), ())), precision=precision,
                           preferred_element_type=F32)


def _silu(x):
    return x * (1.0 / (1.0 + jnp.exp(-x)))


def _norm_matmul_body(x_ref, g_ref, w_ref, o_ref, xn_ref, *, normalize):
    @pl.when(pl.program_id(1) == 0)
    def _():
        x = x_ref[...]
        if normalize:
            ms = jnp.mean(x * x, axis=-1, keepdims=True)
            x = x * lax.rsqrt(ms + NORM_EPS) * g_ref[...]
        xn_ref[...] = x.astype(xn_ref.dtype)

    o_ref[...] = jnp.dot(xn_ref[...], w_ref[...], preferred_element_type=F32).astype(o_ref.dtype)


def norm_matmul(x, gain, w, *, normalize=True, tm=512, tn=512, out_dtype=F32):
    t, k = x.shape
    n = w.shape[1]
    tm, tn = min(tm, t), min(tn, n)
    assert t % tm == 0 and n % tn == 0
    return pl.pallas_call(
        functools.partial(_norm_matmul_body, normalize=normalize),
        grid=(t // tm, n // tn),
        in_specs=[pl.BlockSpec((tm, k), lambda i, j: (i, 0)),
                  pl.BlockSpec((1, k), lambda i, j: (0, 0)),
                  pl.BlockSpec((k, tn), lambda i, j: (0, j))],
        out_specs=pl.BlockSpec((tm, tn), lambda i, j: (i, j)),
        out_shape=jax.ShapeDtypeStruct((t, n), out_dtype),
        scratch_shapes=[pltpu.VMEM((tm, k), BF16)],
        compiler_params=_cparams("parallel", "arbitrary"),
        name="norm_matmul",
    )(x, gain.reshape(1, k), w)


def _ssd_body(xs_ref, bt_ref, c_ref, dt_ref, la_ref, lat_ref, z_ref, dskip_ref, gain_ref,
              o_ref, state_ref):
    seq = xs_ref.shape[1]
    nchunks = seq // CHUNK
    rep = N_HEADS // SSD_GROUPS
    state_ref[...] = jnp.zeros_like(state_ref)
    row = lax.broadcasted_iota(jnp.int32, (CHUNK, CHUNK), 0)
    col = lax.broadcasted_iota(jnp.int32, (CHUNK, CHUNK), 1)
    tril = row >= col
    tril_f = tril.astype(F32)
    triu_f = (row <= col).astype(F32)

    def chunk(c, carry):
        r0 = pl.multiple_of(c * CHUNK, CHUNK)
        rows = pl.ds(r0, CHUNK)
        acum = jnp.dot(tril_f, la_ref[0, rows, :], precision=HI, preferred_element_type=F32)
        acum_t = jnp.dot(lat_ref[0, c], triu_f, precision=HI, preferred_element_type=F32)
        dt = dt_ref[0, rows, :]
        ys = []
        for h in range(N_HEADS):
            g = h // rep
            hs = slice(h * HEAD_DIM, (h + 1) * HEAD_DIM)
            gs = slice(g * SSD_STATE, (g + 1) * SSD_STATE)
            a_col = acum[:, h:h + 1]
            a_row = acum_t[h:h + 1, :]
            a_last = acum[CHUNK - 1:CHUNK, h:h + 1]
            decay = jnp.exp(jnp.where(tril, a_col - a_row, -jnp.inf))
            cc = c_ref[0, rows, gs]
            bt = bt_ref[0, c, gs, :]
            x_h = xs_ref[0, rows, hs]
            xdt = x_h * dt[:, h:h + 1]
            scores = _dot(cc, bt) * decay
            prev = state_ref[h]
            y = _dot(scores, xdt) + _dot(cc, prev) * jnp.exp(a_col)
            state_ref[h] = prev * jnp.exp(a_last) + _dot(bt * jnp.exp(a_last - a_row), xdt)
            ys.append(y + x_h * dskip_ref[0:1, hs])
        y = jnp.concatenate(ys, axis=-1)
        y = y * _silu(z_ref[0, rows, :])
        ms = jnp.mean(y * y, axis=-1, keepdims=True)
        o_ref[0, rows, :] = (y * lax.rsqrt(ms + NORM_EPS) * gain_ref[...]).astype(o_ref.dtype)
        return carry

    lax.fori_loop(0, nchunks, chunk, 0)


def ssd_mixer(xs, bm_t, cm, dt, log_a, log_a_t, z, d_skip_lanes, norm_gain):
    bsz, seq, width = xs.shape
    gn = SSD_GROUPS * SSD_STATE
    nc = seq // CHUNK
    full = lambda shp: pl.BlockSpec(shp, lambda b: (b,) + (0,) * (len(shp) - 1))
    const = lambda shp: pl.BlockSpec(shp, lambda b: (0,) * len(shp))
    return pl.pallas_call(
        _ssd_body,
        grid=(bsz,),
        in_specs=[full((1, seq, width)), full((1, nc, gn, CHUNK)), full((1, seq, gn)),
                  full((1, seq, N_HEADS)), full((1, seq, N_HEADS)), full((1, nc, N_HEADS, CHUNK)),
                  full((1, seq, width)), const((1, width)), const((1, width))],
        out_specs=full((1, seq, width)),
        out_shape=jax.ShapeDtypeStruct((bsz, seq, width), F32),
        scratch_shapes=[pltpu.VMEM((N_HEADS, SSD_STATE, HEAD_DIM), F32)],
        compiler_params=_cparams("parallel"),
        name="ssd_mixer",
    )(xs, bm_t, cm, dt, log_a, log_a_t, z, d_skip_lanes, norm_gain.reshape(1, width))


def _ret_body(q_ref, kt_ref, v_ref, g_ref, dmat_ref, zeta_ref, xi_ref, cdec_ref, gain_ref,
              o_ref, state_ref):
    seq = q_ref.shape[1]
    nchunks = seq // CHUNK
    state_ref[...] = jnp.zeros_like(state_ref)

    def chunk(c, carry):
        r0 = pl.multiple_of(c * CHUNK, CHUNK)
        rows = pl.ds(r0, CHUNK)
        ys = []
        for h in range(N_HEADS):
            hs = slice(h * HEAD_DIM, (h + 1) * HEAD_DIM)
            qc = q_ref[0, rows, hs]
            kt = kt_ref[0, c, hs, :]
            vc = v_ref[0, rows, hs]
            inner = _dot(qc, kt) * dmat_ref[h]
            prev = state_ref[h]
            y = _dot(inner, vc) + _dot(qc, prev) * xi_ref[h]
            state_ref[h] = prev * cdec_ref[h] + _dot(kt * zeta_ref[h], vc)
            mu = jnp.mean(y, axis=-1, keepdims=True)
            yc = y - mu
            var = jnp.mean(yc * yc, axis=-1, keepdims=True)
            ys.append(yc * lax.rsqrt(var + NORM_EPS))
        y = jnp.concatenate(ys, axis=-1) * gain_ref[...]
        o_ref[0, rows, :] = (_silu(g_ref[0, rows, :]) * y).astype(o_ref.dtype)
        return carry

    lax.fori_loop(0, nchunks, chunk, 0)


def retention_mixer(q, k_t, v, g, norm_gain):
    bsz, seq, width = q.shape
    nc = seq // CHUNK
    log_gamma = jnp.log1p(-jnp.exp2(-5.0 - jnp.arange(N_HEADS, dtype=F32)))
    idx = jnp.arange(CHUNK, dtype=F32)
    rel = idx[:, None] - idx[None, :]
    dmat = jnp.where(rel >= 0, jnp.exp(jnp.maximum(rel, 0.0)[None] * log_gamma[:, None, None]), 0.0)
    zeta = jnp.exp((CHUNK - 1.0 - idx)[None, None, :] * log_gamma[:, None, None])
    xi = jnp.exp((idx + 1.0)[None, :, None] * log_gamma[:, None, None])
    cdec = jnp.exp(CHUNK * log_gamma)[:, None, None]
    full = lambda shp: pl.BlockSpec(shp, lambda b: (b,) + (0,) * (len(shp) - 1))
    const = lambda shp: pl.BlockSpec(shp, lambda b: (0,) * len(shp))
    return pl.pallas_call(
        _ret_body,
        grid=(bsz,),
        in_specs=[full((1, seq, width)), full((1, nc, width, CHUNK)), full((1, seq, width)),
                  full((1, seq, width)), const((N_HEADS, CHUNK, CHUNK)), const((N_HEADS, 1, CHUNK)),
                  const((N_HEADS, CHUNK, 1)), const((N_HEADS, 1, 1)), const((1, width))],
        out_specs=full((1, seq, width)),
        out_shape=jax.ShapeDtypeStruct((bsz, seq, width), F32),
        scratch_shapes=[pltpu.VMEM((N_HEADS, HEAD_DIM, HEAD_DIM), F32)],
        compiler_params=_cparams("parallel"),
        name="retention_mixer",
    )(q, k_t, v, g, dmat, zeta, xi, cdec, norm_gain.reshape(1, width))


def _softmax_pv(pieces, v):
    m = jnp.max(functools.reduce(jnp.maximum, pieces), axis=-1, keepdims=True)
    ps = [jnp.exp(sp - m) for sp in pieces]
    l = jnp.sum(functools.reduce(jnp.add, ps), axis=-1, keepdims=True)
    p = jnp.concatenate(ps, axis=1).astype(BF16)
    return jnp.dot(p, v, preferred_element_type=F32) / l


def _split_bf16(x):
    hi = x.astype(BF16)
    return hi, (x - hi.astype(F32)).astype(BF16)


def _moba_body(q_ref, kt_ref, v_ref, o_ref):
    seq = q_ref.shape[1]
    nb = seq // ATT_BLOCK
    halves = ATT_BLOCK // LANES
    scale = HEAD_DIM ** -0.5
    heads = q_ref.shape[2] // HEAD_DIM
    n_sel = min(MOBA_TOPK, nb - 1)
    row = lax.broadcasted_iota(jnp.int32, (ATT_BLOCK, LANES), 0)
    col = lax.broadcasted_iota(jnp.int32, (ATT_BLOCK, LANES), 1)
    tril = [row >= col + hf * LANES for hf in range(halves)]

    for h in range(heads):
        hs = slice(h * HEAD_DIM, (h + 1) * HEAD_DIM)
        kt = kt_ref[0, hs, :]
        kt_bf = kt.astype(BF16)
        reps = [jnp.broadcast_to(jnp.mean(kt[:, n * ATT_BLOCK:(n + 1) * ATT_BLOCK], axis=1, keepdims=True),
                                 (HEAD_DIM, LANES)) for n in range(nb - 1)]
        kr_hi, kr_lo = _split_bf16(jnp.concatenate(reps, axis=1))
        kr4 = jnp.concatenate([kr_hi, kr_lo, kr_hi, kr_lo], axis=0)

        for i in range(nb):
            rows = slice(i * ATT_BLOCK, (i + 1) * ATT_BLOCK)
            n_keys = (i + 1) * ATT_BLOCK
            q = q_ref[0, rows, hs]
            sel = None
            if i > n_sel:
                q_hi, q_lo = _split_bf16(q)
                q4 = jnp.concatenate([q_hi, q_hi, q_lo, q_lo], axis=1)
                g = jnp.dot(q4, kr4[:, :i * LANES], preferred_element_type=F32)
                gs = [g[:, n * LANES:(n + 1) * LANES] for n in range(i)]
                cnt = [jnp.full((ATT_BLOCK, LANES), float(i - 1 - a), F32) for a in range(i)]
                for a in range(i):
                    for b in range(a + 1, i):
                        a_wins = jnp.where(gs[a] >= gs[b], 1.0, 0.0)
                        cnt[b] = cnt[b] + a_wins
                        cnt[a] = cnt[a] - a_wins
                sel = [c < n_sel for c in cnt]
            s = jnp.dot((q * scale).astype(BF16), kt_bf[:, :n_keys], preferred_element_type=F32)
            pieces = []
            for k in range(halves * (i + 1)):
                n, hf = divmod(k, halves)
                sp = s[:, k * LANES:(k + 1) * LANES]
                if n == i:
                    sp = jnp.where(tril[hf], sp, NEG_INF)
                elif sel is not None:
                    sp = jnp.where(sel[n], sp, NEG_INF)
                pieces.append(sp)
            o_ref[0, rows, hs] = _softmax_pv(pieces, v_ref[0, :n_keys, hs]).astype(o_ref.dtype)


def _dil_body(q_ref, kt_ref, v_ref, lm_ref, o_ref):
    seq = q_ref.shape[1]
    nb = seq // ATT_BLOCK
    halves = ATT_BLOCK // LANES
    scale = HEAD_DIM ** -0.5
    heads = q_ref.shape[2] // HEAD_DIM

    for h in range(heads):
        hs = slice(h * HEAD_DIM, (h + 1) * HEAD_DIM)
        kt_bf = kt_ref[0, hs, :]
        for i in range(nb):
            rows = slice(i * ATT_BLOCK, (i + 1) * ATT_BLOCK)
            n_keys = (i + 1) * ATT_BLOCK
            qs = (q_ref[0, rows, hs] * scale).astype(BF16)
            s = jnp.dot(qs, kt_bf[:, :n_keys], preferred_element_type=F32)
            pieces = []
            for k in range(halves * (i + 1)):
                n, hf = divmod(k, halves)
                pieces.append(s[:, k * LANES:(k + 1) * LANES] + lm_ref[i - n, :, hf * LANES:(hf + 1) * LANES])
            o_ref[0, rows, hs] = _softmax_pv(pieces, v_ref[0, :n_keys, hs]).astype(o_ref.dtype)


def _dilation_log_multiplicity(seq):
    nb = seq // ATT_BLOCK
    r = jnp.arange(ATT_BLOCK)
    d = (jnp.arange(nb)[:, None, None] * ATT_BLOCK + r[None, :, None] - r[None, None, :])
    mult = jnp.zeros(d.shape, F32)
    for window, dil in DIL_PATTERNS:
        mult = mult + ((d >= 0) & (d <= window) & (d % dil == 0)).astype(F32)
    return jnp.where(mult > 0, jnp.log(jnp.maximum(mult, 1.0)), NEG_INF)


def _head_pair_specs(seq):
    tok = pl.BlockSpec((1, seq, LANES), lambda b, p: (b, 0, p))
    tr = pl.BlockSpec((1, LANES, seq), lambda b, p: (b, p, 0))
    return tok, tr


def moba_mixer(q, k_t, v):
    bsz, seq, width = q.shape
    tok, tr = _head_pair_specs(seq)
    return pl.pallas_call(
        _moba_body,
        grid=(bsz, width // LANES),
        in_specs=[tok, tr, tok],
        out_specs=tok,
        out_shape=jax.ShapeDtypeStruct((bsz, seq, width), F32),
        compiler_params=_cparams("parallel", "parallel"),
        name="moba_mixer",
    )(q, k_t, v)


def dilated_mixer(q, k_t, v):
    bsz, seq, width = q.shape
    nb = seq // ATT_BLOCK
    tok, tr = _head_pair_specs(seq)
    return pl.pallas_call(
        _dil_body,
        grid=(bsz, width // LANES),
        in_specs=[tok, tr, tok, pl.BlockSpec((nb, ATT_BLOCK, ATT_BLOCK), lambda b, p: (0, 0, 0))],
        out_specs=tok,
        out_shape=jax.ShapeDtypeStruct((bsz, seq, width), F32),
        compiler_params=_cparams("parallel", "parallel"),
        name="dilated_mixer",
    )(q, k_t, v, _dilation_log_multiplicity(seq))


def _merge_body(h_ref, gates_ref, y0_ref, y1_ref, y2_ref, y3_ref, wb_ref, wo_ref, o_ref):
    d = h_ref.shape[1]
    merged = jnp.zeros(h_ref.shape, F32)
    for i, y_ref in enumerate((y0_ref, y1_ref, y2_ref, y3_ref)):
        gate = 1.0 / (1.0 + jnp.exp(-gates_ref[:, i * d:(i + 1) * d]))
        merged = merged + gate * _dot(y_ref[...], wb_ref[i])
    o_ref[...] = h_ref[...] + _dot(merged, wo_ref[...])


def gated_merge(h, proj, ys, w_branch, w_out, tm=512):
    t, d = h.shape
    width = ys[0].shape[1]
    tm = min(tm, t)
    rowblk = lambda w: pl.BlockSpec((tm, w), lambda i: (i, 0))
    return pl.pallas_call(
        _merge_body,
        grid=(t // tm,),
        in_specs=[rowblk(d), rowblk(N_BRANCHES * d)] + [rowblk(width)] * N_BRANCHES
                 + [pl.BlockSpec((N_BRANCHES, width, d), lambda i: (0, 0, 0)),
                    pl.BlockSpec((d, d), lambda i: (0, 0))],
        out_specs=rowblk(d),
        out_shape=jax.ShapeDtypeStruct((t, d), F32),
        compiler_params=_cparams("parallel"),
        name="gated_merge",
    )(h, proj, *ys, w_branch, w_out)


def _xattn_body(h_ref, gain_ref, wq_ref, k_ref, v_ref, wo_ref, o_ref):
    h = h_ref[0]
    d = h.shape[1]
    dh = d // X_HEADS
    ms = jnp.mean(h * h, axis=-1, keepdims=True)
    hn = h * lax.rsqrt(ms + NORM_EPS) * gain_ref[...]
    q = _dot(hn, wq_ref[...]) * dh ** -0.5
    outs = []
    for a in range(X_HEADS):
        cs = slice(a * dh, (a + 1) * dh)
        s = _dot_nt(q[:, cs].astype(BF16), k_ref[0, :, cs])
        m = jnp.max(s, axis=-1, keepdims=True)
        e = jnp.exp(s - m)
        p = e / jnp.sum(e, axis=-1, keepdims=True)
        outs.append(_dot(p, v_ref[0, :, cs]))
    o = jnp.concatenate(outs, axis=-1)
    o_ref[0] = h + _dot(o, wo_ref[...])


def cross_attention(h, kv, gain, w_q, w_o, tq=512):
    bsz, seq, d = h.shape
    mlen = kv.shape[1]
    tq = min(tq, seq)
    return pl.pallas_call(
        _xattn_body,
        grid=(bsz, seq // tq),
        in_specs=[pl.BlockSpec((1, tq, d), lambda b, i: (b, i, 0)),
                  pl.BlockSpec((1, d), lambda b, i: (0, 0)),
                  pl.BlockSpec((d, d), lambda b, i: (0, 0)),
                  pl.BlockSpec((1, mlen, d), lambda b, i: (b, 0, 0)),
                  pl.BlockSpec((1, mlen, d), lambda b, i: (b, 0, 1)),
                  pl.BlockSpec((d, d), lambda b, i: (0, 0))],
        out_specs=pl.BlockSpec((1, tq, d), lambda b, i: (b, i, 0)),
        out_shape=jax.ShapeDtypeStruct((bsz, seq, d), F32),
        compiler_params=_cparams("parallel", "parallel"),
        name="cross_attention",
    )(h, gain.reshape(1, d), w_q, kv, kv, w_o)


def _top_rows(vals, count):
    tops = []
    for _ in range(count):
        m = jnp.max(vals, axis=0, keepdims=True)
        tops.append(m)
        vals = jnp.where(vals == m, -jnp.inf, vals)
    return tops


_N_RANKS = PEER_TOPK + 1
_RANK_ROWS = 24


def _pair_candidates(a1, a2_ref):
    blocks = []
    for i in range(_N_RANKS):
        n_j = _N_RANKS // (i + 1)
        rows = -(-n_j // 8) * 8
        blk = a1[i] + a2_ref[0:rows, :]
        if n_j % 8:
            rank = lax.broadcasted_iota(jnp.int32, blk.shape, 0)
            blk = jnp.where(rank < n_j, blk, -jnp.inf)
        blocks.append(blk)
    return jnp.concatenate(blocks, axis=0)


def _router_body(h_ref, gain_ref, wq_hi_ref, wq_lo_ref, keys_ref, s2_ref, e2_ref, thr_ref, e1_ref, a2_ref):
    h = h_ref[...]
    ms = jnp.mean(h * h, axis=-1, keepdims=True)
    x_hi, x_lo = _split_bf16(h * lax.rsqrt(ms + NORM_EPS) * gain_ref[...])
    q = (jnp.dot(x_hi, wq_hi_ref[...], preferred_element_type=F32)
         + jnp.dot(x_hi, wq_lo_ref[...], preferred_element_type=F32)
         + jnp.dot(x_lo, wq_hi_ref[...], preferred_element_type=F32))
    dk = keys_ref.shape[-1] // 2
    for hd in range(PEER_HEADS):
        s = []
        for half in range(2):
            c0 = (hd * 2 + half) * dk
            q_hi, q_lo = _split_bf16(q[:, c0:c0 + dk])
            keys2 = keys_ref[hd, half]
            s.append(_dot_nt(keys2, jnp.concatenate([q_hi, q_hi], axis=1))
                     + _dot_nt(keys2, jnp.concatenate([q_lo, q_lo], axis=1)))
        s1, s2 = s
        a1 = _top_rows(s1, _N_RANKS)
        a2 = _top_rows(s2, _N_RANKS)
        a2_ref[...] = jnp.full(a2_ref.shape, -jnp.inf, F32)
        for r in range(_N_RANKS):
            a2_ref[r:r + 1, :] = a2[r]
        cand = _pair_candidates(a1, a2_ref)
        top = _top_rows(cand, _N_RANKS)
        tau = 0.5 * (top[PEER_TOPK - 1] + top[PEER_TOPK])
        z = jnp.sum(jnp.where(cand > tau, jnp.exp(cand - (a1[0] + a2[0])), 0.0), axis=0, keepdims=True)
        s2_ref[hd] = s2
        e2_ref[hd] = jnp.exp(s2 - a2[0])
        thr_ref[hd] = tau - s1
        e1_ref[hd] = jnp.exp(s1 - a1[0]) / z


def peer_router(h, gain, w_q, sub_keys, tt=256):
    t, d = h.shape
    tt = min(tt, t)
    nk = sub_keys.shape[2]
    wq_hi, wq_lo = _split_bf16(w_q)
    keys2 = jnp.concatenate(_split_bf16(sub_keys), axis=-1)
    out = jax.ShapeDtypeStruct((PEER_HEADS, nk, t), F32)
    ospec = pl.BlockSpec((PEER_HEADS, nk, tt), lambda i: (0, 0, i))
    return pl.pallas_call(
        _router_body,
        grid=(t // tt,),
        in_specs=[pl.BlockSpec((tt, d), lambda i: (i, 0)),
                  pl.BlockSpec((1, d), lambda i: (0, 0)),
                  pl.BlockSpec(w_q.shape, lambda i: (0, 0)),
                  pl.BlockSpec(w_q.shape, lambda i: (0, 0)),
                  pl.BlockSpec(keys2.shape, lambda i: (0, 0, 0, 0))],
        out_specs=[ospec] * 4,
        out_shape=[out] * 4,
        scratch_shapes=[pltpu.VMEM((_RANK_ROWS, tt), F32)],
        compiler_params=_cparams("parallel"),
        name="peer_router",
    )(h, gain.reshape(1, d), wq_hi, wq_lo, keys2)


def _gelu(x):
    return 0.5 * x * (1.0 + lax.erf(x * (2.0 ** -0.5)))


def _experts_body(h_ref, gain_ref, u_ref, vt_ref, s2_ref, e2_ref, thr_ref, e1_ref, o_ref,
                  xn_ref, acc_ref, p_ref, *, rows_per_step, sub):
    step = pl.program_id(1)
    nk = s2_ref.shape[1]
    tt = h_ref.shape[0]

    @pl.when(step == 0)
    def _():
        h = h_ref[...]
        ms = jnp.mean(h * h, axis=-1, keepdims=True)
        xn_ref[...] = (h * lax.rsqrt(ms + NORM_EPS) * gain_ref[...]).astype(BF16)
        acc_ref[...] = jnp.zeros_like(acc_ref)

    act_t = _dot_nt(u_ref[...], xn_ref[...])
    i0 = pl.multiple_of(step * rows_per_step, rows_per_step)
    for c in range(tt // sub):
        ts = slice(c * sub, (c + 1) * sub)
        thr_rows = [thr_ref[hd, pl.ds(i0, rows_per_step), ts] for hd in range(PEER_HEADS)]
        e1_rows = [e1_ref[hd, pl.ds(i0, rows_per_step), ts] for hd in range(PEER_HEADS)]
        for r in range(rows_per_step):
            gate = jnp.zeros((nk, sub), F32)
            for hd in range(PEER_HEADS):
                thr = thr_rows[hd][r:r + 1]
                e1 = e1_rows[hd][r:r + 1]
                gate = gate + jnp.where(s2_ref[hd, :, ts] > thr, e2_ref[hd, :, ts] * e1, 0.0)
            a = act_t[r * nk:(r + 1) * nk, ts]
            p_ref[r * nk:(r + 1) * nk, ts] = (_gelu(a) * gate).astype(BF16)
    acc_ref[...] += jnp.dot(vt_ref[...], p_ref[...], preferred_element_type=F32)

    @pl.when(step == pl.num_programs(1) - 1)
    def _():
        o_ref[...] = h_ref[...] + acc_ref[...].T


def peer_experts(h, gain, u, v_t, s2, e2, thr, e1, tt=512, rows_per_step=8, sub=128):
    t, d = h.shape
    nk = s2.shape[1]
    tt = min(tt, t)
    assert rows_per_step % 8 == 0
    et = rows_per_step * nk
    rspec = pl.BlockSpec((PEER_HEADS, nk, tt), lambda i, s: (0, 0, i))
    return pl.pallas_call(
        functools.partial(_experts_body, rows_per_step=rows_per_step, sub=sub),
        grid=(t // tt, nk // rows_per_step),
        in_specs=[pl.BlockSpec((tt, d), lambda i, s: (i, 0)),
                  pl.BlockSpec((1, d), lambda i, s: (0, 0)),
                  pl.BlockSpec((et, d), lambda i, s: (s, 0)),
                  pl.BlockSpec((d, et), lambda i, s: (0, s)),
                  rspec, rspec, rspec, rspec],
        out_specs=pl.BlockSpec((tt, d), lambda i, s: (i, 0)),
        out_shape=jax.ShapeDtypeStruct((t, d), F32),
        scratch_shapes=[pltpu.VMEM((tt, d), BF16), pltpu.VMEM((d, tt), F32), pltpu.VMEM((et, tt), BF16)],
        compiler_params=_cparams("parallel", "arbitrary"),
        name="peer_experts",
    )(h, gain.reshape(1, d), u, v_t, s2, e2, thr, e1)


def _transpose_cast_body(x_ref, o_ref):
    o_ref[...] = x_ref[...].T.astype(o_ref.dtype)


def transpose_cast(x3, index, dtype, blk=512):
    _, r, c = x3.shape
    return pl.pallas_call(
        _transpose_cast_body,
        grid=(r // blk, c // blk),
        in_specs=[pl.BlockSpec((None, blk, blk), lambda i, j: (index, i, j))],
        out_specs=pl.BlockSpec((blk, blk), lambda i, j: (j, i)),
        out_shape=jax.ShapeDtypeStruct((c, r), dtype),
        compiler_params=_cparams("parallel", "parallel"),
        name="transpose_cast",
    )(x3)


def _final_norm_body(x_ref, g_ref, o_ref):
    x = x_ref[...]
    ms = jnp.mean(x * x, axis=-1, keepdims=True)
    o_ref[...] = x * lax.rsqrt(ms + NORM_EPS) * g_ref[...]


def final_norm(x, gain, tm=1024):
    t, d = x.shape
    tm = min(tm, t)
    return pl.pallas_call(
        _final_norm_body,
        grid=(t // tm,),
        in_specs=[pl.BlockSpec((tm, d), lambda i: (i, 0)), pl.BlockSpec((1, d), lambda i: (0, 0))],
        out_specs=pl.BlockSpec((tm, d), lambda i: (i, 0)),
        out_shape=jax.ShapeDtypeStruct((t, d), F32),
        compiler_params=_cparams("parallel"),
        name="final_norm",
    )(x, gain.reshape(1, d))


def _rope(t):
    bsz, seq, width = t.shape
    half = HEAD_DIM // 2
    inv_freq = ROPE_THETA ** (-jnp.arange(half, dtype=F32) / half)
    ang = jnp.arange(seq, dtype=F32)[:, None] * inv_freq[None, :]
    cos = jnp.cos(ang)[None, :, None, :]
    sin = jnp.sin(ang)[None, :, None, :]
    th = t.reshape(bsz, seq, width // HEAD_DIM, HEAD_DIM)
    t1, t2 = th[..., :half], th[..., half:]
    return jnp.concatenate([t1 * cos - t2 * sin, t2 * cos + t1 * sin], axis=-1).reshape(bsz, seq, width)


def _causal_dwconv(x, w, b):
    k = w.shape[0]
    seq = x.shape[1]
    xp = jnp.pad(x, ((0, 0), (k - 1, 0), (0, 0)))
    return sum(xp[:, i:i + seq, :] * w[i] for i in range(k)) + b


def _tr_blocks(t, blk):
    bsz, seq, width = t.shape
    return jnp.swapaxes(t.reshape(bsz, seq // blk, blk, width), 2, 3)


_DT_PAD = 256


def _pack_w_in(w_in_l, d_model):
    ssd_conv_dim = MIX_WIDTH + 2 * SSD_GROUPS * SSD_STATE
    o_xbc = MIX_WIDTH
    o_dt = o_xbc + ssd_conv_dim
    o_rest = o_dt + N_HEADS
    o_gate = o_rest + 10 * MIX_WIDTH
    dt_cols = jnp.pad(w_in_l[:, o_dt:o_rest], ((0, 0), (0, _DT_PAD - N_HEADS)))
    return jnp.concatenate([w_in_l[:, o_gate:], w_in_l[:, :o_dt], w_in_l[:, o_rest:o_gate], dt_cols],
                           axis=1).astype(BF16)


def kernel(x, mem, mix_norm, w_in, ssd_conv_w, ssd_conv_b, ssd_dt_bias, ssd_a_log, ssd_d, ssd_norm, ret_norm, w_branch, w_out, x_norm, w_xq, w_xkv, w_xo, ffn_norm, w_pq, peer_sub_keys, peer_u, peer_v, final_norm_gain):
    bsz, seq, d = x.shape
    depth = w_in.shape[0]
    t = bsz * seq
    w = MIX_WIDTH
    gn = SSD_GROUPS * SSD_STATE
    h = x.reshape(t, d)
    mem2 = mem.reshape(-1, d)
    ones = jnp.ones((d,), F32)

    for layer in range(depth):
        proj = norm_matmul(h, mix_norm[layer], _pack_w_in(w_in[layer], d), tm=1024)
        widths = [w, w + 2 * gn] + [w] * 10 + [_DT_PAD]
        offs = [N_BRANCHES * d]
        for width in widths:
            offs.append(offs[-1] + width)
        (z, xbc, rq, rk, rv, rg, mq, mk, mv, dq, dk, dv, dt_pad) = [
            proj[:, a:b].reshape(bsz, seq, b - a) for a, b in zip(offs[:-1], offs[1:])]
        dt_raw = dt_pad[..., :N_HEADS]

        xbc = _silu(_causal_dwconv(xbc, ssd_conv_w[layer], ssd_conv_b[layer]))
        xs, bm, cm = xbc[..., :w], xbc[..., w:w + gn], xbc[..., w + gn:]
        dt = jax.nn.softplus(dt_raw + ssd_dt_bias[layer])
        log_a = dt * -jnp.exp(ssd_a_log[layer])
        d_lanes = jnp.repeat(ssd_d[layer], HEAD_DIM).reshape(1, w)
        y_a = ssd_mixer(xs, _tr_blocks(bm, CHUNK), cm, dt, log_a, _tr_blocks(log_a, CHUNK), z,
                        d_lanes, ssd_norm[layer])
        y_b = retention_mixer(_rope(rq), _tr_blocks(_rope(rk) * HEAD_DIM ** -0.5, CHUNK), rv, rg,
                              ret_norm[layer])
        y_c = moba_mixer(_rope(mq), jnp.swapaxes(_rope(mk), 1, 2), mv.astype(BF16))
        y_d = dilated_mixer(_rope(dq), jnp.swapaxes(_rope(dk), 1, 2).astype(BF16), dv.astype(BF16))
        ys = [y.reshape(t, w) for y in (y_a, y_b, y_c, y_d)]
        h = gated_merge(h, proj, ys, w_branch[layer].astype(BF16), w_out[layer].astype(BF16))
        kv = norm_matmul(mem2, ones, w_xkv[layer].astype(BF16), normalize=False, out_dtype=BF16)
        h = cross_attention(h.reshape(bsz, seq, d), kv.reshape(bsz, -1, 2 * d), x_norm[layer],
                            w_xq[layer].astype(BF16), w_xo[layer].astype(BF16)).reshape(t, d)
        s2, e2, thr, e1 = peer_router(h, ffn_norm[layer], w_pq[layer], peer_sub_keys[layer])
        h = peer_experts(h, ffn_norm[layer], peer_u[layer].astype(BF16),
                         transpose_cast(peer_v, layer, BF16), s2, e2, thr, e1)

    return final_norm(h, final_norm_gain).reshape(bsz, seq, d)
```

```python
import functools
import math

import jax
import jax.numpy as jnp
from jax import lax
from jax.experimental import pallas as pl
from jax.experimental.pallas import tpu as pltpu

F32 = jnp.float32
BF16 = jnp.bfloat16
HI = lax.Precision.HIGHEST

NORM_EPS = 1e-6
NEG_INF = -1e30
ROPE_THETA = 10000.0
HEAD_DIM = 64
N_HEADS = 8
MIX_WIDTH = N_HEADS * HEAD_DIM
SSD_GROUPS = 2
SSD_STATE = 64
SSD_CONV = 4
CHUNK = 128
ATT_BLOCK = 256
MOBA_TOPK = 3
DIL_PATTERNS = ((128, 1), (512, 4), (2048, 16))
N_BRANCHES = 4
X_HEADS = 4
PEER_HEADS = 8
PEER_NKEYS = 128
PEER_TOPK = 16
LANES = 128
VMEM_LIMIT = 56 * 1024 * 1024


def _cparams(*sem, flags=None):
    return pltpu.CompilerParams(dimension_semantics=sem, vmem_limit_bytes=VMEM_LIMIT, flags=flags)


def _dot(a, b):
    return jnp.dot(a.astype(BF16), b.astype(BF16), preferred_element_type=F32)


def _dot_nt(a, b, precision=None):
    return lax.dot_general(a, b, (((1,), (1,)), ((), ())), precision=precision,
                           preferred_element_type=F32)


def _silu(x):
    return x * (1.0 / (1.0 + jnp.exp(-x)))


def _norm_matmul_body(x_ref, g_ref, w_ref, o_ref, xn_ref, *, normalize):
    @pl.when(pl.program_id(1) == 0)
    def _():
        x = x_ref[...]
        if normalize:
            ms = jnp.mean(x * x, axis=-1, keepdims=True)
            x = x * lax.rsqrt(ms + NORM_EPS) * g_ref[...]
        xn_ref[...] = x.astype(xn_ref.dtype)

    o_ref[...] = jnp.dot(xn_ref[...], w_ref[...], preferred_element_type=F32).astype(o_ref.dtype)


def norm_matmul(x, gain, w, *, normalize=True, tm=512, tn=512, out_dtype=F32):
    t, k = x.shape
    n = w.shape[1]
    tm, tn = min(tm, t), min(tn, n)
    assert t % tm == 0 and n % tn == 0
    return pl.pallas_call(
        functools.partial(_norm_matmul_body, normalize=normalize),
        grid=(t // tm, n // tn),
        in_specs=[pl.BlockSpec((tm, k), lambda i, j: (i, 0)),
                  pl.BlockSpec((1, k), lambda i, j: (0, 0)),
                  pl.BlockSpec((k, tn), lambda i, j: (0, j))],
        out_specs=pl.BlockSpec((tm, tn), lambda i, j: (i, j)),
        out_shape=jax.ShapeDtypeStruct((t, n), out_dtype),
        scratch_shapes=[pltpu.VMEM((tm, k), BF16)],
        compiler_params=_cparams("parallel", "arbitrary"),
        name="norm_matmul",
    )(x, gain.reshape(1, k), w)


def _ssd_body(xs_ref, bt_ref, c_ref, dt_ref, la_ref, lat_ref, z_ref, dskip_ref, gain_ref,
              o_ref, state_ref):
    seq = xs_ref.shape[1]
    nchunks = seq // CHUNK
    rep = N_HEADS // SSD_GROUPS
    state_ref[...] = jnp.zeros_like(state_ref)
    row = lax.broadcasted_iota(jnp.int32, (CHUNK, CHUNK), 0)
    col = lax.broadcasted_iota(jnp.int32, (CHUNK, CHUNK), 1)
    tril = row >= col
    tril_f = tril.astype(F32)
    triu_f = (row <= col).astype(F32)

    def chunk(c, carry):
        r0 = pl.multiple_of(c * CHUNK, CHUNK)
        rows = pl.ds(r0, CHUNK)
        acum = jnp.dot(tril_f, la_ref[0, rows, :], precision=HI, preferred_element_type=F32)
        acum_t = jnp.dot(lat_ref[0, c], triu_f, precision=HI, preferred_element_type=F32)
        dt = dt_ref[0, rows, :]
        ys = []
        for h in range(N_HEADS):
            g = h // rep
            hs = slice(h * HEAD_DIM, (h + 1) * HEAD_DIM)
            gs = slice(g * SSD_STATE, (g + 1) * SSD_STATE)
            a_col = acum[:, h:h + 1]
            a_row = acum_t[h:h + 1, :]
            a_last = acum[CHUNK - 1:CHUNK, h:h + 1]
            decay = jnp.exp(jnp.where(tril, a_col - a_row, -jnp.inf))
            cc = c_ref[0, rows, gs]
            bt = bt_ref[0, c, gs, :]
            x_h = xs_ref[0, rows, hs]
            xdt = x_h * dt[:, h:h + 1]
            scores = _dot(cc, bt) * decay
            prev = state_ref[h]
            y = _dot(scores, xdt) + _dot(cc, prev) * jnp.exp(a_col)
            state_ref[h] = prev * jnp.exp(a_last) + _dot(bt * jnp.exp(a_last - a_row), xdt)
            ys.append(y + x_h * dskip_ref[0:1, hs])
        y = jnp.concatenate(ys, axis=-1)
        y = y * _silu(z_ref[0, rows, :])
        ms = jnp.mean(y * y, axis=-1, keepdims=True)
        o_ref[0, rows, :] = (y * lax.rsqrt(ms + NORM_EPS) * gain_ref[...]).astype(o_ref.dtype)
        return carry

    lax.fori_loop(0, nchunks, chunk, 0)


def ssd_mixer(xs, bm_t, cm, dt, log_a, log_a_t, z, d_skip_lanes, norm_gain):
    bsz, seq, width = xs.shape
    gn = SSD_GROUPS * SSD_STATE
    nc = seq // CHUNK
    full = lambda shp: pl.BlockSpec(shp, lambda b: (b,) + (0,) * (len(shp) - 1))
    const = lambda shp: pl.BlockSpec(shp, lambda b: (0,) * len(shp))
    return pl.pallas_call(
        _ssd_body,
        grid=(bsz,),
        in_specs=[full((1, seq, width)), full((1, nc, gn, CHUNK)), full((1, seq, gn)),
                  full((1, seq, N_HEADS)), full((1, seq, N_HEADS)), full((1, nc, N_HEADS, CHUNK)),
                  full((1, seq, width)), const((1, width)), const((1, width))],
        out_specs=full((1, seq, width)),
        out_shape=jax.ShapeDtypeStruct((bsz, seq, width), F32),
        scratch_shapes=[pltpu.VMEM((N_HEADS, SSD_STATE, HEAD_DIM), F32)],
        compiler_params=_cparams("parallel"),
        name="ssd_mixer",
    )(xs, bm_t, cm, dt, log_a, log_a_t, z, d_skip_lanes, norm_gain.reshape(1, width))


def _ret_body(q_ref, kt_ref, v_ref, g_ref, dmat_ref, zeta_ref, xi_ref, cdec_ref, gain_ref,
              o_ref, state_ref):
    seq = q_ref.shape[1]
    nchunks = seq // CHUNK
    state_ref[...] = jnp.zeros_like(state_ref)

    def chunk(c, carry):
        r0 = pl.multiple_of(c * CHUNK, CHUNK)
        rows = pl.ds(r0, CHUNK)
        ys = []
        for h in range(N_HEADS):
            hs = slice(h * HEAD_DIM, (h + 1) * HEAD_DIM)
            qc = q_ref[0, rows, hs]
            kt = kt_ref[0, c, hs, :]
            vc = v_ref[0, rows, hs]
            inner = _dot(qc, kt) * dmat_ref[h]
            prev = state_ref[h]
            y = _dot(inner, vc) + _dot(qc, prev) * xi_ref[h]
            state_ref[h] = prev * cdec_ref[h] + _dot(kt * zeta_ref[h], vc)
            mu = jnp.mean(y, axis=-1, keepdims=True)
            yc = y - mu
            var = jnp.mean(yc * yc, axis=-1, keepdims=True)
            ys.append(yc * lax.rsqrt(var + NORM_EPS))
        y = jnp.concatenate(ys, axis=-1) * gain_ref[...]
        o_ref[0, rows, :] = (_silu(g_ref[0, rows, :]) * y).astype(o_ref.dtype)
        return carry

    lax.fori_loop(0, nchunks, chunk, 0)


def retention_mixer(q, k_t, v, g, norm_gain):
    bsz, seq, width = q.shape
    nc = seq // CHUNK
    log_gamma = jnp.log1p(-jnp.exp2(-5.0 - jnp.arange(N_HEADS, dtype=F32)))
    idx = jnp.arange(CHUNK, dtype=F32)
    rel = idx[:, None] - idx[None, :]
    dmat = jnp.where(rel >= 0, jnp.exp(jnp.maximum(rel, 0.0)[None] * log_gamma[:, None, None]), 0.0)
    zeta = jnp.exp((CHUNK - 1.0 - idx)[None, None, :] * log_gamma[:, None, None])
    xi = jnp.exp((idx + 1.0)[None, :, None] * log_gamma[:, None, None])
    cdec = jnp.exp(CHUNK * log_gamma)[:, None, None]
    full = lambda shp: pl.BlockSpec(shp, lambda b: (b,) + (0,) * (len(shp) - 1))
    const = lambda shp: pl.BlockSpec(shp, lambda b: (0,) * len(shp))
    return pl.pallas_call(
        _ret_body,
        grid=(bsz,),
        in_specs=[full((1, seq, width)), full((1, nc, width, CHUNK)), full((1, seq, width)),
                  full((1, seq, width)), const((N_HEADS, CHUNK, CHUNK)), const((N_HEADS, 1, CHUNK)),
                  const((N_HEADS, CHUNK, 1)), const((N_HEADS, 1, 1)), const((1, width))],
        out_specs=full((1, seq, width)),
        out_shape=jax.ShapeDtypeStruct((bsz, seq, width), F32),
        scratch_shapes=[pltpu.VMEM((N_HEADS, HEAD_DIM, HEAD_DIM), F32)],
        compiler_params=_cparams("parallel"),
        name="retention_mixer",
    )(q, k_t, v, g, dmat, zeta, xi, cdec, norm_gain.reshape(1, width))


def _softmax_pv(pieces, v):
    m = jnp.max(functools.reduce(jnp.maximum, pieces), axis=-1, keepdims=True)
    ps = [jnp.exp(sp - m) for sp in pieces]
    l = jnp.sum(functools.reduce(jnp.add, ps), axis=-1, keepdims=True)
    p = jnp.concatenate(ps, axis=1).astype(BF16)
    return jnp.dot(p, v, preferred_element_type=F32) / l


def _split_bf16(x):
    hi = x.astype(BF16)
    return hi, (x - hi.astype(F32)).astype(BF16)


def _moba_body(q_ref, kt_ref, v_ref, o_ref):
    seq = q_ref.shape[1]
    nb = seq // ATT_BLOCK
    halves = ATT_BLOCK // LANES
    scale = HEAD_DIM ** -0.5
    heads = q_ref.shape[2] // HEAD_DIM
    n_sel = min(MOBA_TOPK, nb - 1)
    row = lax.broadcasted_iota(jnp.int32, (ATT_BLOCK, LANES), 0)
    col = lax.broadcasted_iota(jnp.int32, (ATT_BLOCK, LANES), 1)
    tril = [row >= col + hf * LANES for hf in range(halves)]

    for h in range(heads):
        hs = slice(h * HEAD_DIM, (h + 1) * HEAD_DIM)
        kt = kt_ref[0, hs, :]
        kt_bf = kt.astype(BF16)
        reps = [jnp.broadcast_to(jnp.mean(kt[:, n * ATT_BLOCK:(n + 1) * ATT_BLOCK], axis=1, keepdims=True),
                                 (HEAD_DIM, LANES)) for n in range(nb - 1)]
        kr_hi, kr_lo = _split_bf16(jnp.concatenate(reps, axis=1))
        kr4 = jnp.concatenate([kr_hi, kr_lo, kr_hi, kr_lo], axis=0)

        for i in range(nb):
            rows = slice(i * ATT_BLOCK, (i + 1) * ATT_BLOCK)
            n_keys = (i + 1) * ATT_BLOCK
            q = q_ref[0, rows, hs]
            sel = None
            if i > n_sel:
                q_hi, q_lo = _split_bf16(q)
                q4 = jnp.concatenate([q_hi, q_hi, q_lo, q_lo], axis=1)
                g = jnp.dot(q4, kr4[:, :i * LANES], preferred_element_type=F32)
                gs = [g[:, n * LANES:(n + 1) * LANES] for n in range(i)]
                cnt = [jnp.full((ATT_BLOCK, LANES), float(i - 1 - a), F32) for a in range(i)]
                for a in range(i):
                    for b in range(a + 1, i):
                        a_wins = jnp.where(gs[a] >= gs[b], 1.0, 0.0)
                        cnt[b] = cnt[b] + a_wins
                        cnt[a] = cnt[a] - a_wins
                sel = [c < n_sel for c in cnt]
            s = jnp.dot((q * scale).astype(BF16), kt_bf[:, :n_keys], preferred_element_type=F32)
            pieces = []
            for k in range(halves * (i + 1)):
                n, hf = divmod(k, halves)
                sp = s[:, k * LANES:(k + 1) * LANES]
                if n == i:
                    sp = jnp.where(tril[hf], sp, NEG_INF)
                elif sel is not None:
                    sp = jnp.where(sel[n], sp, NEG_INF)
                pieces.append(sp)
            o_ref[0, rows, hs] = _softmax_pv(pieces, v_ref[0, :n_keys, hs]).astype(o_ref.dtype)


def _dil_body(q_ref, kt_ref, v_ref, lm_ref, o_ref):
    seq = q_ref.shape[1]
    nb = seq // ATT_BLOCK
    halves = ATT_BLOCK // LANES
    scale = HEAD_DIM ** -0.5
    heads = q_ref.shape[2] // HEAD_DIM

    for h in range(heads):
        hs = slice(h * HEAD_DIM, (h + 1) * HEAD_DIM)
        kt_bf = kt_ref[0, hs, :]
        for i in range(nb):
            rows = slice(i * ATT_BLOCK, (i + 1) * ATT_BLOCK)
            n_keys = (i + 1) * ATT_BLOCK
            qs = (q_ref[0, rows, hs] * scale).astype(BF16)
            s = jnp.dot(qs, kt_bf[:, :n_keys], preferred_element_type=F32)
            pieces = []
            for k in range(halves * (i + 1)):
                n, hf = divmod(k, halves)
                pieces.append(s[:, k * LANES:(k + 1) * LANES] + lm_ref[i - n, :, hf * LANES:(hf + 1) * LANES])
            o_ref[0, rows, hs] = _softmax_pv(pieces, v_ref[0, :n_keys, hs]).astype(o_ref.dtype)


def _dilation_log_multiplicity(seq):
    nb = seq // ATT_BLOCK
    r = jnp.arange(ATT_BLOCK)
    d = (jnp.arange(nb)[:, None, None] * ATT_BLOCK + r[None, :, None] - r[None, None, :])
    mult = jnp.zeros(d.shape, F32)
    for window, dil in DIL_PATTERNS:
        mult = mult + ((d >= 0) & (d <= window) & (d % dil == 0)).astype(F32)
    return jnp.where(mult > 0, jnp.log(jnp.maximum(mult, 1.0)), NEG_INF)


def _head_pair_specs(seq):
    tok = pl.BlockSpec((1, seq, LANES), lambda b, p: (b, 0, p))
    tr = pl.BlockSpec((1, LANES, seq), lambda b, p: (b, p, 0))
    return tok, tr


def moba_mixer(q, k_t, v):
    bsz, seq, width = q.shape
    tok, tr = _head_pair_specs(seq)
    return pl.pallas_call(
        _moba_body,
        grid=(bsz, width // LANES),
        in_specs=[tok, tr, tok],
        out_specs=tok,
        out_shape=jax.ShapeDtypeStruct((bsz, seq, width), F32),
        compiler_params=_cparams("parallel", "parallel"),
        name="moba_mixer",
    )(q, k_t, v)


def dilated_mixer(q, k_t, v):
    bsz, seq, width = q.shape
    nb = seq // ATT_BLOCK
    tok, tr = _head_pair_specs(seq)
    return pl.pallas_call(
        _dil_body,
        grid=(bsz, width // LANES),
        in_specs=[tok, tr, tok, pl.BlockSpec((nb, ATT_BLOCK, ATT_BLOCK), lambda b, p: (0, 0, 0))],
        out_specs=tok,
        out_shape=jax.ShapeDtypeStruct((bsz, seq, width), F32),
        compiler_params=_cparams("parallel", "parallel"),
        name="dilated_mixer",
    )(q, k_t, v, _dilation_log_multiplicity(seq))


def _merge_body(h_ref, gates_ref, y0_ref, y1_ref, y2_ref, y3_ref, wb_ref, wo_ref, o_ref):
    d = h_ref.shape[1]
    merged = jnp.zeros(h_ref.shape, F32)
    for i, y_ref in enumerate((y0_ref, y1_ref, y2_ref, y3_ref)):
        gate = 1.0 / (1.0 + jnp.exp(-gates_ref[:, i * d:(i + 1) * d]))
        merged = merged + gate * _dot(y_ref[...], wb_ref[i])
    o_ref[...] = h_ref[...] + _dot(merged, wo_ref[...])


def gated_merge(h, proj, ys, w_branch, w_out, tm=512):
    t, d = h.shape
    width = ys[0].shape[1]
    tm = min(tm, t)
    rowblk = lambda w: pl.BlockSpec((tm, w), lambda i: (i, 0))
    return pl.pallas_call(
        _merge_body,
        grid=(t // tm,),
        in_specs=[rowblk(d), rowblk(N_BRANCHES * d)] + [rowblk(width)] * N_BRANCHES
                 + [pl.BlockSpec((N_BRANCHES, width, d), lambda i: (0, 0, 0)),
                    pl.BlockSpec((d, d), lambda i: (0, 0))],
        out_specs=rowblk(d),
        out_shape=jax.ShapeDtypeStruct((t, d), F32),
        compiler_params=_cparams("parallel"),
        name="gated_merge",
    )(h, proj, *ys, w_branch, w_out)


def _xattn_body(h_ref, gain_ref, wq_ref, k_ref, v_ref, wo_ref, o_ref):
    h = h_ref[0]
    d = h.shape[1]
    dh = d // X_HEADS
    ms = jnp.mean(h * h, axis=-1, keepdims=True)
    hn = h * lax.rsqrt(ms + NORM_EPS) * gain_ref[...]
    q = _dot(hn, wq_ref[...]) * dh ** -0.5
    outs = []
    for a in range(X_HEADS):
        cs = slice(a * dh, (a + 1) * dh)
        s = _dot_nt(q[:, cs].astype(BF16), k_ref[0, :, cs])
        m = jnp.max(s, axis=-1, keepdims=True)
        e = jnp.exp(s - m)
        p = e / jnp.sum(e, axis=-1, keepdims=True)
        outs.append(_dot(p, v_ref[0, :, cs]))
    o = jnp.concatenate(outs, axis=-1)
    o_ref[0] = h + _dot(o, wo_ref[...])


def cross_attention(h, kv, gain, w_q, w_o, tq=512):
    bsz, seq, d = h.shape
    mlen = kv.shape[1]
    tq = min(tq, seq)
    return pl.pallas_call(
        _xattn_body,
        grid=(bsz, seq // tq),
        in_specs=[pl.BlockSpec((1, tq, d), lambda b, i: (b, i, 0)),
                  pl.BlockSpec((1, d), lambda b, i: (0, 0)),
                  pl.BlockSpec((d, d), lambda b, i: (0, 0)),
                  pl.BlockSpec((1, mlen, d), lambda b, i: (b, 0, 0)),
                  pl.BlockSpec((1, mlen, d), lambda b, i: (b, 0, 1)),
                  pl.BlockSpec((d, d), lambda b, i: (0, 0))],
        out_specs=pl.BlockSpec((1, tq, d), lambda b, i: (b, i, 0)),
        out_shape=jax.ShapeDtypeStruct((bsz, seq, d), F32),
        compiler_params=_cparams("parallel", "parallel"),
        name="cross_attention",
    )(h, gain.reshape(1, d), w_q, kv, kv, w_o)


_NO_RANK = 127.0


def _top_rows(vals, count, with_ranks=False):
    tops = []
    rank = jnp.full(vals.shape, _NO_RANK, F32)
    for r in range(count):
        m = jnp.max(vals, axis=0, keepdims=True)
        tops.append(m)
        hit = vals == m
        if with_ranks:
            rank = jnp.where(hit, float(r), rank)
        vals = jnp.where(hit, -jnp.inf, vals)
    return (tops, rank) if with_ranks else tops


_N_RANKS = PEER_TOPK + 1
_RANK_ROWS = 24


def _pair_candidates(a1, a2_ref):
    blocks = []
    for i in range(_N_RANKS):
        n_j = _N_RANKS // (i + 1)
        rows = -(-n_j // 8) * 8
        blk = a1[i] + a2_ref[0:rows, :]
        if n_j % 8:
            rank = lax.broadcasted_iota(jnp.int32, blk.shape, 0)
            blk = jnp.where(rank < n_j, blk, -jnp.inf)
        blocks.append(blk)
    return jnp.concatenate(blocks, axis=0)


def _router_body(h_ref, gain_ref, wq_hi_ref, wq_lo_ref, keys_ref, rank2_ref, e2_ref, cut_ref, e1_ref, a2_ref):
    h = h_ref[...]
    ms = jnp.mean(h * h, axis=-1, keepdims=True)
    x_hi, x_lo = _split_bf16(h * lax.rsqrt(ms + NORM_EPS) * gain_ref[...])
    q = (jnp.dot(x_hi, wq_hi_ref[...], preferred_element_type=F32)
         + jnp.dot(x_hi, wq_lo_ref[...], preferred_element_type=F32)
         + jnp.dot(x_lo, wq_hi_ref[...], preferred_element_type=F32))
    dk = keys_ref.shape[-1] // 2
    for hd in range(PEER_HEADS):
        s = []
        for half in range(2):
            c0 = (hd * 2 + half) * dk
            q_hi, q_lo = _split_bf16(q[:, c0:c0 + dk])
            keys2 = keys_ref[hd, half]
            s.append(_dot_nt(keys2, jnp.concatenate([q_hi, q_hi], axis=1))
                     + _dot_nt(keys2, jnp.concatenate([q_lo, q_lo], axis=1)))
        s1, s2 = s
        a1 = _top_rows(s1, _N_RANKS)
        a2, rank2 = _top_rows(s2, _N_RANKS, with_ranks=True)
        a2_ref[...] = jnp.full(a2_ref.shape, -jnp.inf, F32)
        for r in range(_N_RANKS):
            a2_ref[r:r + 1, :] = a2[r]
        cand = _pair_candidates(a1, a2_ref)
        top = _top_rows(cand, _N_RANKS)
        tau = 0.5 * (top[PEER_TOPK - 1] + top[PEER_TOPK])
        z = jnp.sum(jnp.where(cand > tau, jnp.exp(cand - (a1[0] + a2[0])), 0.0), axis=0, keepdims=True)
        thr = tau - s1
        cut = jnp.zeros(s1.shape, F32)
        for r in range(PEER_TOPK):
            cut = cut + jnp.where(a2[r] > thr, 1.0, 0.0)
        rank2_ref[hd] = rank2.astype(rank2_ref.dtype)
        e2_ref[hd] = jnp.exp(s2 - a2[0]).astype(e2_ref.dtype)
        cut_ref[hd] = cut
        e1_ref[hd] = jnp.exp(s1 - a1[0]) / z


def peer_router(h, gain, w_q, sub_keys, tt=256):
    t, d = h.shape
    tt = min(tt, t)
    nk = sub_keys.shape[2]
    wq_hi, wq_lo = _split_bf16(w_q)
    keys2 = jnp.concatenate(_split_bf16(sub_keys), axis=-1)
    out = [jax.ShapeDtypeStruct((PEER_HEADS, nk, t), dt) for dt in (BF16, BF16, F32, F32)]
    ospec = pl.BlockSpec((PEER_HEADS, nk, tt), lambda i: (0, 0, i))
    return pl.pallas_call(
        _router_body,
        grid=(t // tt,),
        in_specs=[pl.BlockSpec((tt, d), lambda i: (i, 0)),
                  pl.BlockSpec((1, d), lambda i: (0, 0)),
                  pl.BlockSpec(w_q.shape, lambda i: (0, 0)),
                  pl.BlockSpec(w_q.shape, lambda i: (0, 0)),
                  pl.BlockSpec(keys2.shape, lambda i: (0, 0, 0, 0))],
        out_specs=[ospec] * 4,
        out_shape=out,
        scratch_shapes=[pltpu.VMEM((_RANK_ROWS, tt), F32)],
        compiler_params=_cparams("parallel"),
        name="peer_router",
    )(h, gain.reshape(1, d), wq_hi, wq_lo, keys2)


def _gelu(x):
    return 0.5 * x * (1.0 + lax.erf(x * (2.0 ** -0.5)))


_ROWS_PER_STEP = 8
_J_CHUNK = 32
_ROW_GROUP = 4
_TOK_COLS = 256
_MM_SPLIT = 4
_BF16_ROWS = 16


def _experts_body(h_ref, gain_ref, u0_ref, un_ref, vt_ref, rank2_ref, e2_ref, cut_ref, e1_ref, o_ref,
                  xn_ref, acc_ref, act_a_ref, act_b_ref, p_a_ref, p_b_ref, cutb_ref, e1b_ref):
    step = pl.program_id(1)
    n_tiles = pl.num_programs(1) - 1
    nk = rank2_ref.shape[1]
    tt = h_ref.shape[0]
    n_cols = tt // _TOK_COLS
    act_refs = (act_a_ref, act_b_ref)
    p_refs = (p_a_ref, p_b_ref)

    @pl.when(step == 0)
    def _():
        h = h_ref[...]
        ms = jnp.mean(h * h, axis=-1, keepdims=True)
        xn_ref[...] = (h * lax.rsqrt(ms + NORM_EPS) * gain_ref[...]).astype(BF16)
        acc_ref[...] = jnp.zeros_like(acc_ref)
        p_refs[1][...] = jnp.zeros(p_refs[1].shape, BF16)
        act_refs[0][...] = _dot_nt(u0_ref[...], xn_ref[...])

    tile = jnp.minimum(step, n_tiles - 1)
    i0 = pl.multiple_of(tile * _ROWS_PER_STEP, _ROWS_PER_STEP)

    def main_block(act_cur, act_nxt, p_cur, p_prev):
        d_rows = acc_ref.shape[0] // _MM_SPLIT
        e_rows = un_ref.shape[0] // _MM_SPLIT

        def v_piece(k, c):
            rs, cs = slice(k * d_rows, (k + 1) * d_rows), slice(c * _TOK_COLS, (c + 1) * _TOK_COLS)
            acc_ref[rs, cs] += jnp.dot(vt_ref[rs, :], p_prev[:, cs], preferred_element_type=F32)

        def act_piece(k, c):
            rs, cs = slice(k * e_rows, (k + 1) * e_rows), slice(c * _TOK_COLS, (c + 1) * _TOK_COLS)
            act_nxt[rs, cs] = _dot_nt(un_ref[rs, :], xn_ref[cs, :])

        pieces = [(f, k, c) for k in range(_MM_SPLIT) for c in range(n_cols) for f in (v_piece, act_piece)]

        for hd in range(PEER_HEADS):
            cut8 = cut_ref[hd, pl.ds(i0, _ROWS_PER_STEP), :]
            e18 = e1_ref[hd, pl.ds(i0, _ROWS_PER_STEP), :]
            for r in range(_ROWS_PER_STEP):
                cutb_ref[hd, r] = jnp.broadcast_to(cut8[r:r + 1], (_BF16_ROWS, tt)).astype(BF16)
                e1b_ref[hd, r] = jnp.broadcast_to(e18[r:r + 1], (_BF16_ROWS, tt)).astype(BF16)

        reps = _J_CHUNK // _BF16_ROWS
        chunks = [(c, jc, rg) for c in range(n_cols) for jc in range(nk // _J_CHUNK)
                  for rg in range(_ROWS_PER_STEP // _ROW_GROUP)]
        per_chunk = -(-len(pieces) // len(chunks))
        for n, (c, jc, rg) in enumerate(chunks):
            for f, k, cc in pieces[n * per_chunk:(n + 1) * per_chunk]:
                f(k, cc)
            ts = slice(c * _TOK_COLS, (c + 1) * _TOK_COLS)
            js = slice(jc * _J_CHUNK, (jc + 1) * _J_CHUNK)
            gates = [jnp.zeros((_J_CHUNK, _TOK_COLS), BF16) for _ in range(_ROW_GROUP)]
            for hd in range(PEER_HEADS):
                rank2 = rank2_ref[hd, js, ts]
                e2 = e2_ref[hd, js, ts]
                for k in range(_ROW_GROUP):
                    r = rg * _ROW_GROUP + k
                    cut = jnp.concatenate([cutb_ref[hd, r, :, ts]] * reps, axis=0)
                    e1 = jnp.concatenate([e1b_ref[hd, r, :, ts]] * reps, axis=0)
                    gates[k] = gates[k] + jnp.where(rank2 < cut, e2 * e1, jnp.zeros_like(e2))
            for k in range(_ROW_GROUP):
                rows = slice((rg * _ROW_GROUP + k) * nk + jc * _J_CHUNK,
                             (rg * _ROW_GROUP + k) * nk + (jc + 1) * _J_CHUNK)
                p_cur[rows, ts] = _gelu(act_cur[rows, ts]).astype(BF16) * gates[k]

    for parity in range(2):
        @pl.when(step % 2 == parity)
        def _():
            main_block(act_refs[parity], act_refs[1 - parity], p_refs[parity], p_refs[1 - parity])

    @pl.when(step == n_tiles)
    def _():
        o_ref[...] = h_ref[...] + acc_ref[...].T


def peer_experts(h, gain, u, v_t, rank2, e2, cut, e1, tt=512):
    t, d = h.shape
    nk = rank2.shape[1]
    tt = min(tt, t)
    et = _ROWS_PER_STEP * nk
    n_tiles = nk // _ROWS_PER_STEP
    rspec = pl.BlockSpec((PEER_HEADS, nk, tt), lambda i, s: (0, 0, i))
    return pl.pallas_call(
        _experts_body,
        grid=(t // tt, n_tiles + 1),
        in_specs=[pl.BlockSpec((tt, d), lambda i, s: (i, 0)),
                  pl.BlockSpec((1, d), lambda i, s: (0, 0)),
                  pl.BlockSpec((et, d), lambda i, s: (0, 0)),
                  pl.BlockSpec((et, d), lambda i, s: (jnp.minimum(s + 1, n_tiles - 1), 0)),
                  pl.BlockSpec((d, et), lambda i, s: (0, jnp.maximum(s - 1, 0))),
                  rspec, rspec, rspec, rspec],
        out_specs=pl.BlockSpec((tt, d), lambda i, s: (i, 0)),
        out_shape=jax.ShapeDtypeStruct((t, d), F32),
        scratch_shapes=[pltpu.VMEM((tt, d), BF16), pltpu.VMEM((d, tt), F32),
                        pltpu.VMEM((et, tt), F32), pltpu.VMEM((et, tt), F32),
                        pltpu.VMEM((et, tt), BF16), pltpu.VMEM((et, tt), BF16),
                        pltpu.VMEM((PEER_HEADS, _ROWS_PER_STEP, _BF16_ROWS, tt), BF16),
                        pltpu.VMEM((PEER_HEADS, _ROWS_PER_STEP, _BF16_ROWS, tt), BF16)],
        compiler_params=_cparams("parallel", "arbitrary"),
        name="peer_experts",
    )(h, gain.reshape(1, d), u, u, v_t, rank2, e2, cut, e1)


def _transpose_cast_body(x_ref, o_ref):
    o_ref[...] = x_ref[...].T.astype(o_ref.dtype)


def transpose_cast(x3, index, dtype, blk=512):
    _, r, c = x3.shape
    return pl.pallas_call(
        _transpose_cast_body,
        grid=(r // blk, c // blk),
        in_specs=[pl.BlockSpec((None, blk, blk), lambda i, j: (index, i, j))],
        out_specs=pl.BlockSpec((blk, blk), lambda i, j: (j, i)),
        out_shape=jax.ShapeDtypeStruct((c, r), dtype),
        compiler_params=_cparams("parallel", "parallel"),
        name="transpose_cast",
    )(x3)


def _final_norm_body(x_ref, g_ref, o_ref):
    x = x_ref[...]
    ms = jnp.mean(x * x, axis=-1, keepdims=True)
    o_ref[...] = x * lax.rsqrt(ms + NORM_EPS) * g_ref[...]


def final_norm(x, gain, tm=1024):
    t, d = x.shape
    tm = min(tm, t)
    return pl.pallas_call(
        _final_norm_body,
        grid=(t // tm,),
        in_specs=[pl.BlockSpec((tm, d), lambda i: (i, 0)), pl.BlockSpec((1, d), lambda i: (0, 0))],
        out_specs=pl.BlockSpec((tm, d), lambda i: (i, 0)),
        out_shape=jax.ShapeDtypeStruct((t, d), F32),
        compiler_params=_cparams("parallel"),
        name="final_norm",
    )(x, gain.reshape(1, d))


def _rope(t):
    bsz, seq, width = t.shape
    half = HEAD_DIM // 2
    inv_freq = ROPE_THETA ** (-jnp.arange(half, dtype=F32) / half)
    ang = jnp.arange(seq, dtype=F32)[:, None] * inv_freq[None, :]
    cos = jnp.cos(ang)[None, :, None, :]
    sin = jnp.sin(ang)[None, :, None, :]
    th = t.reshape(bsz, seq, width // HEAD_DIM, HEAD_DIM)
    t1, t2 = th[..., :half], th[..., half:]
    return jnp.concatenate([t1 * cos - t2 * sin, t2 * cos + t1 * sin], axis=-1).reshape(bsz, seq, width)


def _causal_dwconv(x, w, b):
    k = w.shape[0]
    seq = x.shape[1]
    xp = jnp.pad(x, ((0, 0), (k - 1, 0), (0, 0)))
    return sum(xp[:, i:i + seq, :] * w[i] for i in range(k)) + b


def _tr_blocks(t, blk):
    bsz, seq, width = t.shape
    return jnp.swapaxes(t.reshape(bsz, seq // blk, blk, width), 2, 3)


_DT_PAD = 256


def _pack_w_in(w_in_l, d_model):
    ssd_conv_dim = MIX_WIDTH + 2 * SSD_GROUPS * SSD_STATE
    o_xbc = MIX_WIDTH
    o_dt = o_xbc + ssd_conv_dim
    o_rest = o_dt + N_HEADS
    o_gate = o_rest + 10 * MIX_WIDTH
    dt_cols = jnp.pad(w_in_l[:, o_dt:o_rest], ((0, 0), (0, _DT_PAD - N_HEADS)))
    return jnp.concatenate([w_in_l[:, o_gate:], w_in_l[:, :o_dt], w_in_l[:, o_rest:o_gate], dt_cols],
                           axis=1).astype(BF16)


def kernel(x, mem, mix_norm, w_in, ssd_conv_w, ssd_conv_b, ssd_dt_bias, ssd_a_log, ssd_d, ssd_norm, ret_norm, w_branch, w_out, x_norm, w_xq, w_xkv, w_xo, ffn_norm, w_pq, peer_sub_keys, peer_u, peer_v, final_norm_gain):
    bsz, seq, d = x.shape
    depth = w_in.shape[0]
    t = bsz * seq
    w = MIX_WIDTH
    gn = SSD_GROUPS * SSD_STATE
    h = x.reshape(t, d)
    mem2 = mem.reshape(-1, d)
    ones = jnp.ones((d,), F32)

    for layer in range(depth):
        proj = norm_matmul(h, mix_norm[layer], _pack_w_in(w_in[layer], d), tm=1024)
        widths = [w, w + 2 * gn] + [w] * 10 + [_DT_PAD]
        offs = [N_BRANCHES * d]
        for width in widths:
            offs.append(offs[-1] + width)
        (z, xbc, rq, rk, rv, rg, mq, mk, mv, dq, dk, dv, dt_pad) = [
            proj[:, a:b].reshape(bsz, seq, b - a) for a, b in zip(offs[:-1], offs[1:])]
        dt_raw = dt_pad[..., :N_HEADS]

        xbc = _silu(_causal_dwconv(xbc, ssd_conv_w[layer], ssd_conv_b[layer]))
        xs, bm, cm = xbc[..., :w], xbc[..., w:w + gn], xbc[..., w + gn:]
        dt = jax.nn.softplus(dt_raw + ssd_dt_bias[layer])
        log_a = dt * -jnp.exp(ssd_a_log[layer])
        d_lanes = jnp.repeat(ssd_d[layer], HEAD_DIM).reshape(1, w)
        y_a = ssd_mixer(xs, _tr_blocks(bm, CHUNK), cm, dt, log_a, _tr_blocks(log_a, CHUNK), z,
                        d_lanes, ssd_norm[layer])
        y_b = retention_mixer(_rope(rq), _tr_blocks(_rope(rk) * HEAD_DIM ** -0.5, CHUNK), rv, rg,
                              ret_norm[layer])
        y_c = moba_mixer(_rope(mq), jnp.swapaxes(_rope(mk), 1, 2), mv.astype(BF16))
        y_d = dilated_mixer(_rope(dq), jnp.swapaxes(_rope(dk), 1, 2).astype(BF16), dv.astype(BF16))
        ys = [y.reshape(t, w) for y in (y_a, y_b, y_c, y_d)]
        h = gated_merge(h, proj, ys, w_branch[layer].astype(BF16), w_out[layer].astype(BF16))
        kv = norm_matmul(mem2, ones, w_xkv[layer].astype(BF16), normalize=False, out_dtype=BF16)
        h = cross_attention(h.reshape(bsz, seq, d), kv.reshape(bsz, -1, 2 * d), x_norm[layer],
                            w_xq[layer].astype(BF16), w_xo[layer].astype(BF16)).reshape(t, d)
        s2, e2, thr, e1 = peer_router(h, ffn_norm[layer], w_pq[layer], peer_sub_keys[layer])
        h = peer_experts(h, ffn_norm[layer], peer_u[layer].astype(BF16),
                         transpose_cast(peer_v, layer, BF16), s2, e2, thr, e1)

    return final_norm(h, final_norm_gain).reshape(bsz, seq, d)
```

```python
import functools
import math

import jax
import jax.numpy as jnp
from jax import lax
from jax.experimental import pallas as pl
from jax.experimental.pallas import tpu as pltpu

F32 = jnp.float32
BF16 = jnp.bfloat16
HI = lax.Precision.HIGHEST

NORM_EPS = 1e-6
NEG_INF = -1e30
ROPE_THETA = 10000.0
HEAD_DIM = 64
N_HEADS = 8
MIX_WIDTH = N_HEADS * HEAD_DIM
SSD_GROUPS = 2
SSD_STATE = 64
SSD_CONV = 4
CHUNK = 128
ATT_BLOCK = 256
MOBA_TOPK = 3
DIL_PATTERNS = ((128, 1), (512, 4), (2048, 16))
N_BRANCHES = 4
X_HEADS = 4
PEER_HEADS = 8
PEER_NKEYS = 128
PEER_TOPK = 16
LANES = 128
VMEM_LIMIT = 56 * 1024 * 1024


def _cparams(*sem, flags=None):
    return pltpu.CompilerParams(dimension_semantics=sem, vmem_limit_bytes=VMEM_LIMIT, flags=flags)


def _dot(a, b):
    return jnp.dot(a.astype(BF16), b.astype(BF16), preferred_element_type=F32)


def _dot_nt(a, b, precision=None):
    return lax.dot_general(a, b, (((1,), (1,)), ((), ())), precision=precision,
                           preferred_element_type=F32)


def _silu(x):
    return x * (1.0 / (1.0 + jnp.exp(-x)))


def _norm_matmul_body(x_ref, g_ref, w_ref, o_ref, xn_ref, *, normalize):
    @pl.when(pl.program_id(1) == 0)
    def _():
        x = x_ref[...]
        if normalize:
            ms = jnp.mean(x * x, axis=-1, keepdims=True)
            x = x * lax.rsqrt(ms + NORM_EPS) * g_ref[...]
        xn_ref[...] = x.astype(xn_ref.dtype)

    o_ref[...] = jnp.dot(xn_ref[...], w_ref[...], preferred_element_type=F32).astype(o_ref.dtype)


def norm_matmul(x, gain, w, *, normalize=True, tm=512, tn=512, out_dtype=F32):
    t, k = x.shape
    n = w.shape[1]
    tm, tn = min(tm, t), min(tn, n)
    assert t % tm == 0 and n % tn == 0
    return pl.pallas_call(
        functools.partial(_norm_matmul_body, normalize=normalize),
        grid=(t // tm, n // tn),
        in_specs=[pl.BlockSpec((tm, k), lambda i, j: (i, 0)),
                  pl.BlockSpec((1, k), lambda i, j: (0, 0)),
                  pl.BlockSpec((k, tn), lambda i, j: (0, j))],
        out_specs=pl.BlockSpec((tm, tn), lambda i, j: (i, j)),
        out_shape=jax.ShapeDtypeStruct((t, n), out_dtype),
        scratch_shapes=[pltpu.VMEM((tm, k), BF16)],
        compiler_params=_cparams("parallel", "arbitrary"),
        name="norm_matmul",
    )(x, gain.reshape(1, k), w)


def _ssd_body(xs_ref, bt_ref, c_ref, dt_ref, la_ref, lat_ref, z_ref, dskip_ref, gain_ref,
              o_ref, state_ref):
    seq = xs_ref.shape[1]
    nchunks = seq // CHUNK
    rep = N_HEADS // SSD_GROUPS
    state_ref[...] = jnp.zeros_like(state_ref)
    row = lax.broadcasted_iota(jnp.int32, (CHUNK, CHUNK), 0)
    col = lax.broadcasted_iota(jnp.int32, (CHUNK, CHUNK), 1)
    tril = row >= col
    tril_f = tril.astype(F32)
    triu_f = (row <= col).astype(F32)

    def chunk(c, carry):
        r0 = pl.multiple_of(c * CHUNK, CHUNK)
        rows = pl.ds(r0, CHUNK)
        acum = jnp.dot(tril_f, la_ref[0, rows, :], precision=HI, preferred_element_type=F32)
        acum_t = jnp.dot(lat_ref[0, c], triu_f, precision=HI, preferred_element_type=F32)
        dt = dt_ref[0, rows, :]
        ys = []
        for h in range(N_HEADS):
            g = h // rep
            hs = slice(h * HEAD_DIM, (h + 1) * HEAD_DIM)
            gs = slice(g * SSD_STATE, (g + 1) * SSD_STATE)
            a_col = acum[:, h:h + 1]
            a_row = acum_t[h:h + 1, :]
            a_last = acum[CHUNK - 1:CHUNK, h:h + 1]
            decay = jnp.exp(jnp.where(tril, a_col - a_row, -jnp.inf))
            cc = c_ref[0, rows, gs]
            bt = bt_ref[0, c, gs, :]
            x_h = xs_ref[0, rows, hs]
            xdt = x_h * dt[:, h:h + 1]
            scores = _dot(cc, bt) * decay
            prev = state_ref[h]
            y = _dot(scores, xdt) + _dot(cc, prev) * jnp.exp(a_col)
            state_ref[h] = prev * jnp.exp(a_last) + _dot(bt * jnp.exp(a_last - a_row), xdt)
            ys.append(y + x_h * dskip_ref[0:1, hs])
        y = jnp.concatenate(ys, axis=-1)
        y = y * _silu(z_ref[0, rows, :].astype(F32))
        ms = jnp.mean(y * y, axis=-1, keepdims=True)
        o_ref[0, rows, :] = (y * lax.rsqrt(ms + NORM_EPS) * gain_ref[...]).astype(o_ref.dtype)
        return carry

    lax.fori_loop(0, nchunks, chunk, 0)


def ssd_mixer(xs, bm_t, cm, dt, log_a, log_a_t, proj, z_col, d_skip_lanes, norm_gain):
    bsz, seq, width = xs.shape
    gn = SSD_GROUPS * SSD_STATE
    nc = seq // CHUNK
    full = lambda shp: pl.BlockSpec(shp, lambda b: (b,) + (0,) * (len(shp) - 1))
    const = lambda shp: pl.BlockSpec(shp, lambda b: (0,) * len(shp))
    return pl.pallas_call(
        _ssd_body,
        grid=(bsz,),
        in_specs=[full((1, seq, width)), full((1, nc, gn, CHUNK)), full((1, seq, gn)),
                  full((1, seq, N_HEADS)), full((1, seq, N_HEADS)), full((1, nc, N_HEADS, CHUNK)),
                  pl.BlockSpec((1, seq, width), lambda b: (b, 0, z_col)), const((1, width)), const((1, width))],
        out_specs=full((1, seq, width)),
        out_shape=jax.ShapeDtypeStruct((bsz, seq, width), BF16),
        scratch_shapes=[pltpu.VMEM((N_HEADS, SSD_STATE, HEAD_DIM), F32)],
        compiler_params=_cparams("parallel"),
        name="ssd_mixer",
    )(xs, bm_t, cm, dt, log_a, log_a_t, proj, d_skip_lanes, norm_gain.reshape(1, width))


def _rope_tables(seq):
    half = HEAD_DIM // 2
    inv_freq = ROPE_THETA ** (-jnp.arange(half, dtype=F32) / half)
    ang = jnp.arange(seq, dtype=F32)[:, None] * inv_freq[None, :]
    sin = jnp.sin(ang)
    cos = jnp.tile(jnp.cos(ang), (1, LANES // half))
    sin_signed = jnp.tile(jnp.concatenate([-sin, sin], axis=1), (1, LANES // HEAD_DIM))
    return cos, sin_signed


def _rope_lanes(x, cos, sin_signed):
    width = x.shape[1]
    half = HEAD_DIM // 2
    lane = lax.broadcasted_iota(jnp.int32, x.shape, 1)
    partner = jnp.where(lane % HEAD_DIM < half, pltpu.roll(x, width - half, axis=1), pltpu.roll(x, half, axis=1))
    reps = width // LANES
    if reps > 1:
        cos = jnp.concatenate([cos] * reps, axis=1)
        sin_signed = jnp.concatenate([sin_signed] * reps, axis=1)
    return x * cos + partner * sin_signed


def _ret_body(q_ref, k_ref, v_ref, g_ref, cos_ref, sin_ref, dmat_ref, zeta_ref, xi_ref, cdec_ref, gain_ref,
              o_ref, state_ref):
    seq = q_ref.shape[1]
    nchunks = seq // CHUNK
    state_ref[...] = jnp.zeros_like(state_ref)

    def chunk(c, carry):
        r0 = pl.multiple_of(c * CHUNK, CHUNK)
        rows = pl.ds(r0, CHUNK)
        cos, sin = cos_ref[rows, :], sin_ref[rows, :]
        q = _rope_lanes(q_ref[0, rows, :].astype(F32), cos, sin).astype(BF16)
        k = _rope_lanes(k_ref[0, rows, :].astype(F32), cos, sin) * HEAD_DIM ** -0.5
        k_t = k.T
        ys = []
        for h in range(N_HEADS):
            hs = slice(h * HEAD_DIM, (h + 1) * HEAD_DIM)
            qc = q[:, hs]
            kt = k_t[hs, :]
            vc = v_ref[0, rows, hs]
            inner = _dot(qc, kt) * dmat_ref[h]
            prev = state_ref[h]
            y = _dot(inner, vc) + _dot(qc, prev) * xi_ref[h]
            state_ref[h] = prev * cdec_ref[h] + _dot(kt * zeta_ref[h], vc)
            mu = jnp.mean(y, axis=-1, keepdims=True)
            yc = y - mu
            var = jnp.mean(yc * yc, axis=-1, keepdims=True)
            ys.append(yc * lax.rsqrt(var + NORM_EPS))
        y = jnp.concatenate(ys, axis=-1) * gain_ref[...]
        o_ref[0, rows, :] = (_silu(g_ref[0, rows, :].astype(F32)) * y).astype(o_ref.dtype)
        return carry

    lax.fori_loop(0, nchunks, chunk, 0)


def retention_mixer(proj, cols, rope_tables, norm_gain):
    bsz, seq, _ = proj.shape
    width = MIX_WIDTH
    log_gamma = jnp.log1p(-jnp.exp2(-5.0 - jnp.arange(N_HEADS, dtype=F32)))
    idx = jnp.arange(CHUNK, dtype=F32)
    rel = idx[:, None] - idx[None, :]
    dmat = jnp.where(rel >= 0, jnp.exp(jnp.maximum(rel, 0.0)[None] * log_gamma[:, None, None]), 0.0)
    zeta = jnp.exp((CHUNK - 1.0 - idx)[None, None, :] * log_gamma[:, None, None])
    xi = jnp.exp((idx + 1.0)[None, :, None] * log_gamma[:, None, None])
    cdec = jnp.exp(CHUNK * log_gamma)[:, None, None]
    col = lambda blk: pl.BlockSpec((1, seq, width), lambda b: (b, 0, blk))
    const = lambda shp: pl.BlockSpec(shp, lambda b: (0,) * len(shp))
    return pl.pallas_call(
        _ret_body,
        grid=(bsz,),
        in_specs=[col(c) for c in cols] + [const((seq, LANES)), const((seq, LANES)),
                  const((N_HEADS, CHUNK, CHUNK)), const((N_HEADS, 1, CHUNK)),
                  const((N_HEADS, CHUNK, 1)), const((N_HEADS, 1, 1)), const((1, width))],
        out_specs=pl.BlockSpec((1, seq, width), lambda b: (b, 0, 0)),
        out_shape=jax.ShapeDtypeStruct((bsz, seq, width), BF16),
        scratch_shapes=[pltpu.VMEM((N_HEADS, HEAD_DIM, HEAD_DIM), F32)],
        compiler_params=_cparams("parallel"),
        name="retention_mixer",
    )(proj, proj, proj, proj, *rope_tables, dmat, zeta, xi, cdec, norm_gain.reshape(1, width))


def _softmax_pv(pieces, v):
    m = jnp.max(functools.reduce(jnp.maximum, pieces), axis=-1, keepdims=True)
    ps = [jnp.exp(sp - m) for sp in pieces]
    l = jnp.sum(functools.reduce(jnp.add, ps), axis=-1, keepdims=True)
    p = jnp.concatenate(ps, axis=1).astype(BF16)
    return jnp.dot(p, v, preferred_element_type=F32) / l


def _split_bf16(x):
    hi = x.astype(BF16)
    return hi, (x - hi.astype(F32)).astype(BF16)


def _rope_qk(q_ref, k_ref, cos_ref, sin_ref, qs_ref, kt_ref):
    cos, sin = cos_ref[...], sin_ref[...]
    qs_ref[...] = _rope_lanes(q_ref[0].astype(F32), cos, sin)
    kt_ref[...] = _rope_lanes(k_ref[0].astype(F32), cos, sin).T.astype(kt_ref.dtype)


def _moba_body(q_ref, k_ref, v_ref, cos_ref, sin_ref, o_ref, qs_ref, kt_ref):
    seq = q_ref.shape[1]
    nb = seq // ATT_BLOCK
    halves = ATT_BLOCK // LANES
    scale = HEAD_DIM ** -0.5
    heads = q_ref.shape[2] // HEAD_DIM
    n_sel = min(MOBA_TOPK, nb - 1)
    row = lax.broadcasted_iota(jnp.int32, (ATT_BLOCK, LANES), 0)
    col = lax.broadcasted_iota(jnp.int32, (ATT_BLOCK, LANES), 1)
    tril = [row >= col + hf * LANES for hf in range(halves)]
    _rope_qk(q_ref, k_ref, cos_ref, sin_ref, qs_ref, kt_ref)

    for h in range(heads):
        hs = slice(h * HEAD_DIM, (h + 1) * HEAD_DIM)
        kt = kt_ref[hs, :]
        kt_bf = kt.astype(BF16)
        reps = [jnp.broadcast_to(jnp.mean(kt[:, n * ATT_BLOCK:(n + 1) * ATT_BLOCK], axis=1, keepdims=True),
                                 (HEAD_DIM, LANES)) for n in range(nb - 1)]
        kr_hi, kr_lo = _split_bf16(jnp.concatenate(reps, axis=1))
        kr4 = jnp.concatenate([kr_hi, kr_lo, kr_hi, kr_lo], axis=0)

        for i in range(nb):
            rows = slice(i * ATT_BLOCK, (i + 1) * ATT_BLOCK)
            n_keys = (i + 1) * ATT_BLOCK
            q = qs_ref[rows, hs]
            sel = None
            if i > n_sel:
                q_hi, q_lo = _split_bf16(q)
                q4 = jnp.concatenate([q_hi, q_hi, q_lo, q_lo], axis=1)
                g = jnp.dot(q4, kr4[:, :i * LANES], preferred_element_type=F32)
                gs = [g[:, n * LANES:(n + 1) * LANES] for n in range(i)]
                cnt = [jnp.full((ATT_BLOCK, LANES), float(i - 1 - a), F32) for a in range(i)]
                for a in range(i):
                    for b in range(a + 1, i):
                        a_wins = jnp.where(gs[a] >= gs[b], 1.0, 0.0)
                        cnt[b] = cnt[b] + a_wins
                        cnt[a] = cnt[a] - a_wins
                sel = [c < n_sel for c in cnt]
            s = jnp.dot((q * scale).astype(BF16), kt_bf[:, :n_keys], preferred_element_type=F32)
            pieces = []
            for k in range(halves * (i + 1)):
                n, hf = divmod(k, halves)
                sp = s[:, k * LANES:(k + 1) * LANES]
                if n == i:
                    sp = jnp.where(tril[hf], sp, NEG_INF)
                elif sel is not None:
                    sp = jnp.where(sel[n], sp, NEG_INF)
                pieces.append(sp)
            o_ref[0, rows, hs] = _softmax_pv(pieces, v_ref[0, :n_keys, hs]).astype(o_ref.dtype)


def _dil_body(q_ref, k_ref, v_ref, cos_ref, sin_ref, lm_ref, o_ref, qs_ref, kt_ref):
    seq = q_ref.shape[1]
    nb = seq // ATT_BLOCK
    halves = ATT_BLOCK // LANES
    scale = HEAD_DIM ** -0.5
    heads = q_ref.shape[2] // HEAD_DIM
    _rope_qk(q_ref, k_ref, cos_ref, sin_ref, qs_ref, kt_ref)

    for h in range(heads):
        hs = slice(h * HEAD_DIM, (h + 1) * HEAD_DIM)
        kt_bf = kt_ref[hs, :]
        for i in range(nb):
            rows = slice(i * ATT_BLOCK, (i + 1) * ATT_BLOCK)
            n_keys = (i + 1) * ATT_BLOCK
            qs = (qs_ref[rows, hs] * scale).astype(BF16)
            s = jnp.dot(qs, kt_bf[:, :n_keys], preferred_element_type=F32)
            pieces = []
            for k in range(halves * (i + 1)):
                n, hf = divmod(k, halves)
                pieces.append(s[:, k * LANES:(k + 1) * LANES] + lm_ref[i - n, :, hf * LANES:(hf + 1) * LANES])
            o_ref[0, rows, hs] = _softmax_pv(pieces, v_ref[0, :n_keys, hs]).astype(o_ref.dtype)


def _dilation_log_multiplicity(seq):
    nb = seq // ATT_BLOCK
    r = jnp.arange(ATT_BLOCK)
    d = (jnp.arange(nb)[:, None, None] * ATT_BLOCK + r[None, :, None] - r[None, None, :])
    mult = jnp.zeros(d.shape, F32)
    for window, dil in DIL_PATTERNS:
        mult = mult + ((d >= 0) & (d <= window) & (d % dil == 0)).astype(F32)
    return jnp.where(mult > 0, jnp.log(jnp.maximum(mult, 1.0)), NEG_INF)


def _attention_mixer(body, name, kt_dtype, proj, cols, rope_tables, extra=()):
    bsz, seq, _ = proj.shape
    pair = lambda lane_blk: pl.BlockSpec((1, seq, LANES), lambda b, p: (b, 0, lane_blk + p))
    const = lambda shp: pl.BlockSpec(shp, lambda b, p: (0,) * len(shp))
    return pl.pallas_call(
        body,
        grid=(bsz, MIX_WIDTH // LANES),
        in_specs=[pair(c) for c in cols] + [const((seq, LANES)), const((seq, LANES))]
                 + [const(e.shape) for e in extra],
        out_specs=pl.BlockSpec((1, seq, LANES), lambda b, p: (b, 0, p)),
        out_shape=jax.ShapeDtypeStruct((bsz, seq, MIX_WIDTH), BF16),
        scratch_shapes=[pltpu.VMEM((seq, LANES), F32), pltpu.VMEM((LANES, seq), kt_dtype)],
        compiler_params=_cparams("parallel", "parallel"),
        name=name,
    )(proj, proj, proj, *rope_tables, *extra)


def moba_mixer(proj, cols, rope_tables):
    return _attention_mixer(_moba_body, "moba_mixer", F32, proj, cols, rope_tables)


def dilated_mixer(proj, cols, rope_tables):
    seq = proj.shape[1]
    return _attention_mixer(_dil_body, "dilated_mixer", BF16, proj, cols, rope_tables,
                            extra=(_dilation_log_multiplicity(seq),))


def _merge_body(h_ref, gates_ref, y0_ref, y1_ref, y2_ref, y3_ref, wb_ref, wo_ref, o_ref):
    d = h_ref.shape[1]
    merged = jnp.zeros(h_ref.shape, F32)
    for i, y_ref in enumerate((y0_ref, y1_ref, y2_ref, y3_ref)):
        gate = 1.0 / (1.0 + jnp.exp(-gates_ref[:, i * d:(i + 1) * d].astype(F32)))
        merged = merged + gate * _dot(y_ref[...], wb_ref[i])
    o_ref[...] = h_ref[...] + _dot(merged, wo_ref[...])


def gated_merge(h, proj, ys, w_branch, w_out, tm=512):
    t, d = h.shape
    width = ys[0].shape[1]
    tm = min(tm, t)
    rowblk = lambda w: pl.BlockSpec((tm, w), lambda i: (i, 0))
    return pl.pallas_call(
        _merge_body,
        grid=(t // tm,),
        in_specs=[rowblk(d), rowblk(N_BRANCHES * d)] + [rowblk(width)] * N_BRANCHES
                 + [pl.BlockSpec((N_BRANCHES, width, d), lambda i: (0, 0, 0)),
                    pl.BlockSpec((d, d), lambda i: (0, 0))],
        out_specs=rowblk(d),
        out_shape=jax.ShapeDtypeStruct((t, d), F32),
        compiler_params=_cparams("parallel"),
        name="gated_merge",
    )(h, proj, *ys, w_branch, w_out)


def _xattn_body(h_ref, gain_ref, wq_ref, k_ref, v_ref, wo_ref, o_ref):
    h = h_ref[0]
    d = h.shape[1]
    dh = d // X_HEADS
    ms = jnp.mean(h * h, axis=-1, keepdims=True)
    hn = h * lax.rsqrt(ms + NORM_EPS) * gain_ref[...]
    q = _dot(hn, wq_ref[...]) * dh ** -0.5
    outs = []
    for a in range(X_HEADS):
        cs = slice(a * dh, (a + 1) * dh)
        s = _dot_nt(q[:, cs].astype(BF16), k_ref[0, :, cs])
        m = jnp.max(s, axis=-1, keepdims=True)
        e = jnp.exp(s - m)
        p = e / jnp.sum(e, axis=-1, keepdims=True)
        outs.append(_dot(p, v_ref[0, :, cs]))
    o = jnp.concatenate(outs, axis=-1)
    o_ref[0] = h + _dot(o, wo_ref[...])


def cross_attention(h, kv, gain, w_q, w_o, tq=512):
    bsz, seq, d = h.shape
    mlen = kv.shape[1]
    tq = min(tq, seq)
    return pl.pallas_call(
        _xattn_body,
        grid=(bsz, seq // tq),
        in_specs=[pl.BlockSpec((1, tq, d), lambda b, i: (b, i, 0)),
                  pl.BlockSpec((1, d), lambda b, i: (0, 0)),
                  pl.BlockSpec((d, d), lambda b, i: (0, 0)),
                  pl.BlockSpec((1, mlen, d), lambda b, i: (b, 0, 0)),
                  pl.BlockSpec((1, mlen, d), lambda b, i: (b, 0, 1)),
                  pl.BlockSpec((d, d), lambda b, i: (0, 0))],
        out_specs=pl.BlockSpec((1, tq, d), lambda b, i: (b, i, 0)),
        out_shape=jax.ShapeDtypeStruct((bsz, seq, d), F32),
        compiler_params=_cparams("parallel", "parallel"),
        name="cross_attention",
    )(h, gain.reshape(1, d), w_q, kv, kv, w_o)


_NO_RANK = 127.0


def _top_rows(vals, count, with_ranks=False):
    tops = []
    rank = jnp.full(vals.shape, _NO_RANK, F32)
    for r in range(count):
        m = jnp.max(vals, axis=0, keepdims=True)
        tops.append(m)
        hit = vals == m
        if with_ranks:
            rank = jnp.where(hit, float(r), rank)
        vals = jnp.where(hit, -jnp.inf, vals)
    return (tops, rank) if with_ranks else tops


_N_RANKS = PEER_TOPK + 1
_RANK_ROWS = 24


def _pair_candidates(a1, a2_ref):
    blocks = []
    for i in range(_N_RANKS):
        n_j = _N_RANKS // (i + 1)
        rows = -(-n_j // 8) * 8
        blk = a1[i] + a2_ref[0:rows, :]
        if n_j % 8:
            rank = lax.broadcasted_iota(jnp.int32, blk.shape, 0)
            blk = jnp.where(rank < n_j, blk, -jnp.inf)
        blocks.append(blk)
    return jnp.concatenate(blocks, axis=0)


def _router_body(h_ref, gain_ref, wq_hi_ref, wq_lo_ref, keys_ref, rank2_ref, e2_ref, cut_ref, e1_ref, a2_ref):
    h = h_ref[...]
    ms = jnp.mean(h * h, axis=-1, keepdims=True)
    x_hi, x_lo = _split_bf16(h * lax.rsqrt(ms + NORM_EPS) * gain_ref[...])
    q = (jnp.dot(x_hi, wq_hi_ref[...], preferred_element_type=F32)
         + jnp.dot(x_hi, wq_lo_ref[...], preferred_element_type=F32)
         + jnp.dot(x_lo, wq_hi_ref[...], preferred_element_type=F32))
    dk = keys_ref.shape[-1] // 2
    for hd in range(PEER_HEADS):
        s = []
        for half in range(2):
            c0 = (hd * 2 + half) * dk
            q_hi, q_lo = _split_bf16(q[:, c0:c0 + dk])
            keys2 = keys_ref[hd, half]
            s.append(_dot_nt(keys2, jnp.concatenate([q_hi, q_hi], axis=1))
                     + _dot_nt(keys2, jnp.concatenate([q_lo, q_lo], axis=1)))
        s1, s2 = s
        a1 = _top_rows(s1, _N_RANKS)
        a2, rank2 = _top_rows(s2, _N_RANKS, with_ranks=True)
        a2_ref[...] = jnp.full(a2_ref.shape, -jnp.inf, F32)
        for r in range(_N_RANKS):
            a2_ref[r:r + 1, :] = a2[r]
        cand = _pair_candidates(a1, a2_ref)
        top = _top_rows(cand, _N_RANKS)
        tau = 0.5 * (top[PEER_TOPK - 1] + top[PEER_TOPK])
        z = jnp.sum(jnp.where(cand > tau, jnp.exp(cand - (a1[0] + a2[0])), 0.0), axis=0, keepdims=True)
        thr = tau - s1
        cut = jnp.zeros(s1.shape, F32)
        for r in range(PEER_TOPK):
            cut = jnp.where(a2[r] > thr, float(r + 1), cut)
        rank2_ref[hd] = rank2.astype(rank2_ref.dtype)
        e2_ref[hd] = jnp.exp(s2 - a2[0]).astype(e2_ref.dtype)
        cut_ref[hd] = cut
        e1_ref[hd] = jnp.exp(s1 - a1[0]) / z


def peer_router(h, gain, w_q, sub_keys, tt=256):
    t, d = h.shape
    tt = min(tt, t)
    nk = sub_keys.shape[2]
    wq_hi, wq_lo = _split_bf16(w_q)
    keys2 = jnp.concatenate(_split_bf16(sub_keys), axis=-1)
    out = [jax.ShapeDtypeStruct((PEER_HEADS, nk, t), dt) for dt in (BF16, BF16, F32, F32)]
    ospec = pl.BlockSpec((PEER_HEADS, nk, tt), lambda i: (0, 0, i))
    return pl.pallas_call(
        _router_body,
        grid=(t // tt,),
        in_specs=[pl.BlockSpec((tt, d), lambda i: (i, 0)),
                  pl.BlockSpec((1, d), lambda i: (0, 0)),
                  pl.BlockSpec(w_q.shape, lambda i: (0, 0)),
                  pl.BlockSpec(w_q.shape, lambda i: (0, 0)),
                  pl.BlockSpec(keys2.shape, lambda i: (0, 0, 0, 0))],
        out_specs=[ospec] * 4,
        out_shape=out,
        scratch_shapes=[pltpu.VMEM((_RANK_ROWS, tt), F32)],
        compiler_params=_cparams("parallel"),
        name="peer_router",
    )(h, gain.reshape(1, d), wq_hi, wq_lo, keys2)


def _gelu(x):
    return 0.5 * x * (1.0 + lax.erf(x * (2.0 ** -0.5)))


_ROWS_PER_STEP = 8
_J_CHUNK = 32
_ROW_GROUP = 4
_TOK_COLS = 256
_MM_SPLIT = 4
_BF16_ROWS = 16


def _experts_body(h_ref, gain_ref, u0_ref, un_ref, vt_ref, rank2_ref, e2_ref, cut_ref, e1_ref, o_ref,
                  xn_ref, acc_ref, act_a_ref, act_b_ref, p_a_ref, p_b_ref, cutb_ref, e1b_ref):
    step = pl.program_id(1)
    n_tiles = pl.num_programs(1) - 1
    nk = rank2_ref.shape[1]
    tt = h_ref.shape[0]
    n_cols = tt // _TOK_COLS
    act_refs = (act_a_ref, act_b_ref)
    p_refs = (p_a_ref, p_b_ref)

    @pl.when(step == 0)
    def _():
        h = h_ref[...]
        ms = jnp.mean(h * h, axis=-1, keepdims=True)
        xn_ref[...] = (h * lax.rsqrt(ms + NORM_EPS) * gain_ref[...]).astype(BF16)
        acc_ref[...] = jnp.zeros_like(acc_ref)
        p_refs[1][...] = jnp.zeros(p_refs[1].shape, BF16)
        act_refs[0][...] = _dot_nt(u0_ref[...], xn_ref[...])

    tile = jnp.minimum(step, n_tiles - 1)
    i0 = pl.multiple_of(tile * _ROWS_PER_STEP, _ROWS_PER_STEP)

    def main_block(act_cur, act_nxt, p_cur, p_prev):
        d_rows = acc_ref.shape[0] // _MM_SPLIT
        e_rows = un_ref.shape[0] // _MM_SPLIT

        def v_piece(k, c):
            rs, cs = slice(k * d_rows, (k + 1) * d_rows), slice(c * _TOK_COLS, (c + 1) * _TOK_COLS)
            acc_ref[rs, cs] += jnp.dot(vt_ref[rs, :], p_prev[:, cs], preferred_element_type=F32)

        def act_piece(k, c):
            rs, cs = slice(k * e_rows, (k + 1) * e_rows), slice(c * _TOK_COLS, (c + 1) * _TOK_COLS)
            act_nxt[rs, cs] = _dot_nt(un_ref[rs, :], xn_ref[cs, :])

        pieces = [(f, k, c) for k in range(_MM_SPLIT) for c in range(n_cols) for f in (v_piece, act_piece)]

        for hd in range(PEER_HEADS):
            cut8 = cut_ref[hd, pl.ds(i0, _ROWS_PER_STEP), :]
            e18 = e1_ref[hd, pl.ds(i0, _ROWS_PER_STEP), :]
            for r in range(_ROWS_PER_STEP):
                cutb_ref[hd, r] = jnp.broadcast_to(cut8[r:r + 1], (_BF16_ROWS, tt)).astype(BF16)
                e1b_ref[hd, r] = jnp.broadcast_to(e18[r:r + 1], (_BF16_ROWS, tt)).astype(BF16)

        reps = _J_CHUNK // _BF16_ROWS
        chunks = [(c, jc, rg) for c in range(n_cols) for jc in range(nk // _J_CHUNK)
                  for rg in range(_ROWS_PER_STEP // _ROW_GROUP)]
        per_chunk = -(-len(pieces) // len(chunks))
        for n, (c, jc, rg) in enumerate(chunks):
            for f, k, cc in pieces[n * per_chunk:(n + 1) * per_chunk]:
                f(k, cc)
            ts = slice(c * _TOK_COLS, (c + 1) * _TOK_COLS)
            js = slice(jc * _J_CHUNK, (jc + 1) * _J_CHUNK)
            gates = [jnp.zeros((_J_CHUNK, _TOK_COLS), BF16) for _ in range(_ROW_GROUP)]
            for hd in range(PEER_HEADS):
                rank2 = rank2_ref[hd, js, ts]
                e2 = e2_ref[hd, js, ts]
                for k in range(_ROW_GROUP):
                    r = rg * _ROW_GROUP + k
                    cut = jnp.concatenate([cutb_ref[hd, r, :, ts]] * reps, axis=0)
                    e1 = jnp.concatenate([e1b_ref[hd, r, :, ts]] * reps, axis=0)
                    gates[k] = gates[k] + jnp.where(rank2 < cut, e2 * e1, jnp.zeros_like(e2))
            for k in range(_ROW_GROUP):
                rows = slice((rg * _ROW_GROUP + k) * nk + jc * _J_CHUNK,
                             (rg * _ROW_GROUP + k) * nk + (jc + 1) * _J_CHUNK)
                p_cur[rows, ts] = _gelu(act_cur[rows, ts]).astype(BF16) * gates[k]

    for parity in range(2):
        @pl.when(step % 2 == parity)
        def _():
            main_block(act_refs[parity], act_refs[1 - parity], p_refs[parity], p_refs[1 - parity])

    @pl.when(step == n_tiles)
    def _():
        o_ref[...] = h_ref[...] + acc_ref[...].T


def peer_experts(h, gain, u, v_t, rank2, e2, cut, e1, tt=512):
    t, d = h.shape
    nk = rank2.shape[1]
    tt = min(tt, t)
    et = _ROWS_PER_STEP * nk
    n_tiles = nk // _ROWS_PER_STEP
    rspec = pl.BlockSpec((PEER_HEADS, nk, tt), lambda i, s: (0, 0, i))
    return pl.pallas_call(
        _experts_body,
        grid=(t // tt, n_tiles + 1),
        in_specs=[pl.BlockSpec((tt, d), lambda i, s: (i, 0)),
                  pl.BlockSpec((1, d), lambda i, s: (0, 0)),
                  pl.BlockSpec((et, d), lambda i, s: (0, 0)),
                  pl.BlockSpec((et, d), lambda i, s: (jnp.minimum(s + 1, n_tiles - 1), 0)),
                  pl.BlockSpec((None, d, et), lambda i, s: (jnp.maximum(s - 1, 0), 0, 0)),
                  rspec, rspec, rspec, rspec],
        out_specs=pl.BlockSpec((tt, d), lambda i, s: (i, 0)),
        out_shape=jax.ShapeDtypeStruct((t, d), F32),
        scratch_shapes=[pltpu.VMEM((tt, d), BF16), pltpu.VMEM((d, tt), F32),
                        pltpu.VMEM((et, tt), F32), pltpu.VMEM((et, tt), F32),
                        pltpu.VMEM((et, tt), BF16), pltpu.VMEM((et, tt), BF16),
                        pltpu.VMEM((PEER_HEADS, _ROWS_PER_STEP, _BF16_ROWS, tt), BF16),
                        pltpu.VMEM((PEER_HEADS, _ROWS_PER_STEP, _BF16_ROWS, tt), BF16)],
        compiler_params=_cparams("parallel", "arbitrary"),
        name="peer_experts",
    )(h, gain.reshape(1, d), u, u, v_t, rank2, e2, cut, e1)


def _transpose_cast_body(x_ref, o_ref):
    o_ref[...] = x_ref[...].T.astype(o_ref.dtype)


def transpose_cast(x3, index, dtype, tile, blk=512):
    _, r, c = x3.shape
    per = tile // blk
    return pl.pallas_call(
        _transpose_cast_body,
        grid=(r // blk, c // blk),
        in_specs=[pl.BlockSpec((None, blk, blk), lambda i, j: (index, i, j))],
        out_specs=pl.BlockSpec((None, blk, blk), lambda i, j: (i // per, j, i % per)),
        out_shape=jax.ShapeDtypeStruct((r // tile, c, tile), dtype),
        compiler_params=_cparams("parallel", "parallel"),
        name="transpose_cast",
    )(x3)


def _final_norm_body(x_ref, g_ref, o_ref):
    x = x_ref[...]
    ms = jnp.mean(x * x, axis=-1, keepdims=True)
    o_ref[...] = x * lax.rsqrt(ms + NORM_EPS) * g_ref[...]


def final_norm(x, gain, tm=1024):
    t, d = x.shape
    tm = min(tm, t)
    return pl.pallas_call(
        _final_norm_body,
        grid=(t // tm,),
        in_specs=[pl.BlockSpec((tm, d), lambda i: (i, 0)), pl.BlockSpec((1, d), lambda i: (0, 0))],
        out_specs=pl.BlockSpec((tm, d), lambda i: (i, 0)),
        out_shape=jax.ShapeDtypeStruct((t, d), F32),
        compiler_params=_cparams("parallel"),
        name="final_norm",
    )(x, gain.reshape(1, d))


def _causal_dwconv(x, w, b):
    k = w.shape[0]
    seq = x.shape[1]
    xp = jnp.pad(x, ((0, 0), (k - 1, 0), (0, 0)))
    return sum(xp[:, i:i + seq, :] * w[i] for i in range(k)) + b


def _tr_blocks(t, blk):
    bsz, seq, width = t.shape
    return jnp.swapaxes(t.reshape(bsz, seq // blk, blk, width), 2, 3)


_GATE_BLOCKS = N_BRANCHES * 8
_COL_Z = _GATE_BLOCKS
_COL_RET = _COL_Z + 4
_COL_MOBA = _COL_RET + 16
_COL_DIL = _COL_MOBA + 12
_COL_XBC = _COL_DIL + 12
_PROJ_BLOCKS = 84


def _pack_w_in(w_in_l):
    ssd_conv_dim = MIX_WIDTH + 2 * SSD_GROUPS * SSD_STATE
    o_xbc = MIX_WIDTH
    o_dt = o_xbc + ssd_conv_dim
    o_rest = o_dt + N_HEADS
    o_gate = o_rest + 10 * MIX_WIDTH
    d_model = w_in_l.shape[0]
    used = _COL_XBC * LANES + ssd_conv_dim
    pad = jnp.zeros((d_model, _PROJ_BLOCKS * LANES - used), w_in_l.dtype)
    main = jnp.concatenate([w_in_l[:, o_gate:], w_in_l[:, :o_xbc], w_in_l[:, o_rest:o_gate],
                            w_in_l[:, o_xbc:o_dt], pad], axis=1).astype(BF16)
    w_dt = jnp.pad(w_in_l[:, o_dt:o_rest], ((0, 0), (0, LANES - N_HEADS))).astype(BF16)
    return main, w_dt


def kernel(x, mem, mix_norm, w_in, ssd_conv_w, ssd_conv_b, ssd_dt_bias, ssd_a_log, ssd_d, ssd_norm, ret_norm, w_branch, w_out, x_norm, w_xq, w_xkv, w_xo, ffn_norm, w_pq, peer_sub_keys, peer_u, peer_v, final_norm_gain):
    bsz, seq, d = x.shape
    depth = w_in.shape[0]
    t = bsz * seq
    w = MIX_WIDTH
    gn = SSD_GROUPS * SSD_STATE
    h = x.reshape(t, d)
    mem2 = mem.reshape(-1, d)
    ones = jnp.ones((d,), F32)

    rope_tables = _rope_tables(seq)

    for layer in range(depth):
        w_main, w_dt = _pack_w_in(w_in[layer])
        proj2 = norm_matmul(h, mix_norm[layer], w_main, tm=1024, out_dtype=BF16)
        dt_raw = norm_matmul(h, mix_norm[layer], w_dt, tm=1024)[:, :N_HEADS].reshape(bsz, seq, N_HEADS)
        proj = proj2.reshape(bsz, seq, -1)

        xbc = proj[..., _COL_XBC * LANES:_COL_XBC * LANES + w + 2 * gn].astype(F32)
        xbc = _silu(_causal_dwconv(xbc, ssd_conv_w[layer], ssd_conv_b[layer]))
        xs, bm, cm = xbc[..., :w], xbc[..., w:w + gn], xbc[..., w + gn:]
        dt = jax.nn.softplus(dt_raw + ssd_dt_bias[layer])
        log_a = dt * -jnp.exp(ssd_a_log[layer])
        d_lanes = jnp.repeat(ssd_d[layer], HEAD_DIM).reshape(1, w)
        y_a = ssd_mixer(xs, _tr_blocks(bm, CHUNK), cm, dt, log_a, _tr_blocks(log_a, CHUNK), proj,
                        _COL_Z // 4, d_lanes, ssd_norm[layer])
        y_b = retention_mixer(proj, [_COL_RET // 4 + i for i in range(4)], rope_tables, ret_norm[layer])
        y_c = moba_mixer(proj, [_COL_MOBA + 4 * i for i in range(3)], rope_tables)
        y_d = dilated_mixer(proj, [_COL_DIL + 4 * i for i in range(3)], rope_tables)
        ys = [y.reshape(t, w) for y in (y_a, y_b, y_c, y_d)]
        h = gated_merge(h, proj2, ys, w_branch[layer].astype(BF16), w_out[layer].astype(BF16))
        kv = norm_matmul(mem2, ones, w_xkv[layer].astype(BF16), normalize=False, out_dtype=BF16)
        h = cross_attention(h.reshape(bsz, seq, d), kv.reshape(bsz, -1, 2 * d), x_norm[layer],
                            w_xq[layer].astype(BF16), w_xo[layer].astype(BF16)).reshape(t, d)
        s2, e2, thr, e1 = peer_router(h, ffn_norm[layer], w_pq[layer], peer_sub_keys[layer])
        h = peer_experts(h, ffn_norm[layer], peer_u[layer].astype(BF16),
                         transpose_cast(peer_v, layer, BF16, _ROWS_PER_STEP * PEER_NKEYS), s2, e2, thr, e1)

    return final_norm(h, final_norm_gain).reshape(bsz, seq, d)
```

```python
import functools
import math

import jax
import jax.numpy as jnp
from jax import lax
from jax.experimental import pallas as pl
from jax.experimental.pallas import tpu as pltpu

F32 = jnp.float32
BF16 = jnp.bfloat16
HI = lax.Precision.HIGHEST

NORM_EPS = 1e-6
NEG_INF = -1e30
ROPE_THETA = 10000.0
HEAD_DIM = 64
N_HEADS = 8
MIX_WIDTH = N_HEADS * HEAD_DIM
SSD_GROUPS = 2
SSD_STATE = 64
SSD_CONV = 4
CHUNK = 128
ATT_BLOCK = 256
MOBA_TOPK = 3
DIL_PATTERNS = ((128, 1), (512, 4), (2048, 16))
N_BRANCHES = 4
X_HEADS = 4
PEER_HEADS = 8
PEER_NKEYS = 128
PEER_TOPK = 16
LANES = 128
VMEM_LIMIT = 56 * 1024 * 1024


def _cparams(*sem, flags=None):
    return pltpu.CompilerParams(dimension_semantics=sem, vmem_limit_bytes=VMEM_LIMIT, flags=flags)


def _dot(a, b):
    return jnp.dot(a.astype(BF16), b.astype(BF16), preferred_element_type=F32)


def _dot_nt(a, b, precision=None):
    return lax.dot_general(a, b, (((1,), (1,)), ((), ())), precision=precision,
                           preferred_element_type=F32)


def _silu(x):
    return x * (1.0 / (1.0 + jnp.exp(-x)))


def _norm_matmul_body(x_ref, g_ref, w_ref, o_ref, xn_ref, *, normalize):
    @pl.when(pl.program_id(1) == 0)
    def _():
        x = x_ref[...]
        if normalize:
            ms = jnp.mean(x * x, axis=-1, keepdims=True)
            x = x * lax.rsqrt(ms + NORM_EPS) * g_ref[...]
        xn_ref[...] = x.astype(xn_ref.dtype)

    o_ref[...] = jnp.dot(xn_ref[...], w_ref[...], preferred_element_type=F32).astype(o_ref.dtype)


def norm_matmul(x, gain, w, *, normalize=True, tm=512, tn=512, out_dtype=F32):
    t, k = x.shape
    n = w.shape[1]
    tm, tn = min(tm, t), min(tn, n)
    assert t % tm == 0 and n % tn == 0
    return pl.pallas_call(
        functools.partial(_norm_matmul_body, normalize=normalize),
        grid=(t // tm, n // tn),
        in_specs=[pl.BlockSpec((tm, k), lambda i, j: (i, 0)),
                  pl.BlockSpec((1, k), lambda i, j: (0, 0)),
                  pl.BlockSpec((k, tn), lambda i, j: (0, j))],
        out_specs=pl.BlockSpec((tm, tn), lambda i, j: (i, j)),
        out_shape=jax.ShapeDtypeStruct((t, n), out_dtype),
        scratch_shapes=[pltpu.VMEM((tm, k), BF16)],
        compiler_params=_cparams("parallel", "arbitrary"),
        name="norm_matmul",
    )(x, gain.reshape(1, k), w)


def _ssd_body(xs_ref, bt_ref, c_ref, dt_ref, la_ref, lat_ref, z_ref, dskip_ref, gain_ref,
              o_ref, state_ref):
    seq = xs_ref.shape[1]
    nchunks = seq // CHUNK
    rep = N_HEADS // SSD_GROUPS
    state_ref[...] = jnp.zeros_like(state_ref)
    row = lax.broadcasted_iota(jnp.int32, (CHUNK, CHUNK), 0)
    col = lax.broadcasted_iota(jnp.int32, (CHUNK, CHUNK), 1)
    tril = row >= col
    tril_f = tril.astype(F32)
    triu_f = (row <= col).astype(F32)

    def chunk(c, carry):
        r0 = pl.multiple_of(c * CHUNK, CHUNK)
        rows = pl.ds(r0, CHUNK)
        acum = jnp.dot(tril_f, la_ref[0, rows, :], precision=HI, preferred_element_type=F32)
        acum_t = jnp.dot(lat_ref[0, c], triu_f, precision=HI, preferred_element_type=F32)
        dt = dt_ref[0, rows, :]
        ys = []
        for h in range(N_HEADS):
            g = h // rep
            hs = slice(h * HEAD_DIM, (h + 1) * HEAD_DIM)
            gs = slice(g * SSD_STATE, (g + 1) * SSD_STATE)
            a_col = acum[:, h:h + 1]
            a_row = acum_t[h:h + 1, :]
            a_last = acum[CHUNK - 1:CHUNK, h:h + 1]
            decay = jnp.exp(jnp.where(tril, a_col - a_row, -jnp.inf))
            cc = c_ref[0, rows, gs]
            bt = bt_ref[0, c, gs, :]
            x_h = xs_ref[0, rows, hs]
            xdt = x_h * dt[:, h:h + 1]
            scores = _dot(cc, bt) * decay
            prev = state_ref[h]
            y = _dot(scores, xdt) + _dot(cc, prev) * jnp.exp(a_col)
            state_ref[h] = prev * jnp.exp(a_last) + _dot(bt * jnp.exp(a_last - a_row), xdt)
            ys.append(y + x_h * dskip_ref[0:1, hs])
        y = jnp.concatenate(ys, axis=-1)
        y = y * _silu(z_ref[0, rows, :].astype(F32))
        ms = jnp.mean(y * y, axis=-1, keepdims=True)
        o_ref[0, rows, :] = (y * lax.rsqrt(ms + NORM_EPS) * gain_ref[...]).astype(o_ref.dtype)
        return carry

    lax.fori_loop(0, nchunks, chunk, 0)


def ssd_mixer(xs, bm_t, cm, dt, log_a, log_a_t, proj, z_col, d_skip_lanes, norm_gain):
    bsz, seq, width = xs.shape
    gn = SSD_GROUPS * SSD_STATE
    nc = seq // CHUNK
    full = lambda shp: pl.BlockSpec(shp, lambda b: (b,) + (0,) * (len(shp) - 1))
    const = lambda shp: pl.BlockSpec(shp, lambda b: (0,) * len(shp))
    return pl.pallas_call(
        _ssd_body,
        grid=(bsz,),
        in_specs=[full((1, seq, width)), full((1, nc, gn, CHUNK)), full((1, seq, gn)),
                  full((1, seq, N_HEADS)), full((1, seq, N_HEADS)), full((1, nc, N_HEADS, CHUNK)),
                  pl.BlockSpec((1, seq, width), lambda b: (b, 0, z_col)), const((1, width)), const((1, width))],
        out_specs=full((1, seq, width)),
        out_shape=jax.ShapeDtypeStruct((bsz, seq, width), BF16),
        scratch_shapes=[pltpu.VMEM((N_HEADS, SSD_STATE, HEAD_DIM), F32)],
        compiler_params=_cparams("parallel"),
        name="ssd_mixer",
    )(xs, bm_t, cm, dt, log_a, log_a_t, proj, d_skip_lanes, norm_gain.reshape(1, width))


def _rope_tables(seq):
    half = HEAD_DIM // 2
    inv_freq = ROPE_THETA ** (-jnp.arange(half, dtype=F32) / half)
    ang = jnp.arange(seq, dtype=F32)[:, None] * inv_freq[None, :]
    sin = jnp.sin(ang)
    cos = jnp.tile(jnp.cos(ang), (1, LANES // half))
    sin_signed = jnp.tile(jnp.concatenate([-sin, sin], axis=1), (1, LANES // HEAD_DIM))
    return cos, sin_signed


def _rope_lanes(x, cos, sin_signed):
    width = x.shape[1]
    half = HEAD_DIM // 2
    lane = lax.broadcasted_iota(jnp.int32, x.shape, 1)
    partner = jnp.where(lane % HEAD_DIM < half, pltpu.roll(x, width - half, axis=1), pltpu.roll(x, half, axis=1))
    reps = width // LANES
    if reps > 1:
        cos = jnp.concatenate([cos] * reps, axis=1)
        sin_signed = jnp.concatenate([sin_signed] * reps, axis=1)
    return x * cos + partner * sin_signed


def _ret_body(q_ref, k_ref, v_ref, g_ref, cos_ref, sin_ref, dmat_ref, zeta_ref, xi_ref, cdec_ref, gain_ref,
              o_ref, state_ref):
    seq = q_ref.shape[1]
    nchunks = seq // CHUNK
    state_ref[...] = jnp.zeros_like(state_ref)

    def chunk(c, carry):
        r0 = pl.multiple_of(c * CHUNK, CHUNK)
        rows = pl.ds(r0, CHUNK)
        cos, sin = cos_ref[rows, :], sin_ref[rows, :]
        q = _rope_lanes(q_ref[0, rows, :].astype(F32), cos, sin).astype(BF16)
        k = _rope_lanes(k_ref[0, rows, :].astype(F32), cos, sin) * HEAD_DIM ** -0.5
        k_t = k.T
        ys = []
        for h in range(N_HEADS):
            hs = slice(h * HEAD_DIM, (h + 1) * HEAD_DIM)
            qc = q[:, hs]
            kt = k_t[hs, :]
            vc = v_ref[0, rows, hs]
            inner = _dot(qc, kt) * dmat_ref[h]
            prev = state_ref[h]
            y = _dot(inner, vc) + _dot(qc, prev) * xi_ref[h]
            state_ref[h] = prev * cdec_ref[h] + _dot(kt * zeta_ref[h], vc)
            mu = jnp.mean(y, axis=-1, keepdims=True)
            yc = y - mu
            var = jnp.mean(yc * yc, axis=-1, keepdims=True)
            ys.append(yc * lax.rsqrt(var + NORM_EPS))
        y = jnp.concatenate(ys, axis=-1) * gain_ref[...]
        o_ref[0, rows, :] = (_silu(g_ref[0, rows, :].astype(F32)) * y).astype(o_ref.dtype)
        return carry

    lax.fori_loop(0, nchunks, chunk, 0)


def retention_mixer(proj, cols, rope_tables, norm_gain):
    bsz, seq, _ = proj.shape
    width = MIX_WIDTH
    log_gamma = jnp.log1p(-jnp.exp2(-5.0 - jnp.arange(N_HEADS, dtype=F32)))
    idx = jnp.arange(CHUNK, dtype=F32)
    rel = idx[:, None] - idx[None, :]
    dmat = jnp.where(rel >= 0, jnp.exp(jnp.maximum(rel, 0.0)[None] * log_gamma[:, None, None]), 0.0)
    zeta = jnp.exp((CHUNK - 1.0 - idx)[None, None, :] * log_gamma[:, None, None])
    xi = jnp.exp((idx + 1.0)[None, :, None] * log_gamma[:, None, None])
    cdec = jnp.exp(CHUNK * log_gamma)[:, None, None]
    col = lambda blk: pl.BlockSpec((1, seq, width), lambda b: (b, 0, blk))
    const = lambda shp: pl.BlockSpec(shp, lambda b: (0,) * len(shp))
    return pl.pallas_call(
        _ret_body,
        grid=(bsz,),
        in_specs=[col(c) for c in cols] + [const((seq, LANES)), const((seq, LANES)),
                  const((N_HEADS, CHUNK, CHUNK)), const((N_HEADS, 1, CHUNK)),
                  const((N_HEADS, CHUNK, 1)), const((N_HEADS, 1, 1)), const((1, width))],
        out_specs=pl.BlockSpec((1, seq, width), lambda b: (b, 0, 0)),
        out_shape=jax.ShapeDtypeStruct((bsz, seq, width), BF16),
        scratch_shapes=[pltpu.VMEM((N_HEADS, HEAD_DIM, HEAD_DIM), F32)],
        compiler_params=_cparams("parallel"),
        name="retention_mixer",
    )(proj, proj, proj, proj, *rope_tables, dmat, zeta, xi, cdec, norm_gain.reshape(1, width))


def _softmax_pv(pieces, v):
    m = jnp.max(functools.reduce(jnp.maximum, pieces), axis=-1, keepdims=True)
    ps = [jnp.exp(sp - m) for sp in pieces]
    l = jnp.sum(functools.reduce(jnp.add, ps), axis=-1, keepdims=True)
    p = jnp.concatenate(ps, axis=1).astype(BF16)
    return jnp.dot(p, v, preferred_element_type=F32) / l


def _split_bf16(x):
    hi = x.astype(BF16)
    return hi, (x - hi.astype(F32)).astype(BF16)


def _rope_qk(q_ref, k_ref, cos_ref, sin_ref, qs_ref, kt_ref):
    cos, sin = cos_ref[...], sin_ref[...]
    qs_ref[...] = _rope_lanes(q_ref[0].astype(F32), cos, sin)
    kt_ref[...] = _rope_lanes(k_ref[0].astype(F32), cos, sin).T.astype(kt_ref.dtype)


def _moba_body(q_ref, k_ref, v_ref, cos_ref, sin_ref, o_ref, qs_ref, kt_ref):
    seq = q_ref.shape[1]
    nb = seq // ATT_BLOCK
    halves = ATT_BLOCK // LANES
    scale = HEAD_DIM ** -0.5
    heads = q_ref.shape[2] // HEAD_DIM
    n_sel = min(MOBA_TOPK, nb - 1)
    row = lax.broadcasted_iota(jnp.int32, (ATT_BLOCK, LANES), 0)
    col = lax.broadcasted_iota(jnp.int32, (ATT_BLOCK, LANES), 1)
    tril = [row >= col + hf * LANES for hf in range(halves)]
    _rope_qk(q_ref, k_ref, cos_ref, sin_ref, qs_ref, kt_ref)

    for h in range(heads):
        hs = slice(h * HEAD_DIM, (h + 1) * HEAD_DIM)
        kt = kt_ref[hs, :]
        kt_bf = kt.astype(BF16)
        reps = [jnp.broadcast_to(jnp.mean(kt[:, n * ATT_BLOCK:(n + 1) * ATT_BLOCK], axis=1, keepdims=True),
                                 (HEAD_DIM, LANES)) for n in range(nb - 1)]
        kr_hi, kr_lo = _split_bf16(jnp.concatenate(reps, axis=1))
        kr4 = jnp.concatenate([kr_hi, kr_lo, kr_hi, kr_lo], axis=0)

        for i in range(nb):
            rows = slice(i * ATT_BLOCK, (i + 1) * ATT_BLOCK)
            n_keys = (i + 1) * ATT_BLOCK
            q = qs_ref[rows, hs]
            sel = None
            if i > n_sel:
                q_hi, q_lo = _split_bf16(q)
                q4 = jnp.concatenate([q_hi, q_hi, q_lo, q_lo], axis=1)
                g = jnp.dot(q4, kr4[:, :i * LANES], preferred_element_type=F32)
                gs = [g[:, n * LANES:(n + 1) * LANES] for n in range(i)]
                cnt = [jnp.full((ATT_BLOCK, LANES), float(i - 1 - a), F32) for a in range(i)]
                for a in range(i):
                    for b in range(a + 1, i):
                        a_wins = jnp.where(gs[a] >= gs[b], 1.0, 0.0)
                        cnt[b] = cnt[b] + a_wins
                        cnt[a] = cnt[a] - a_wins
                sel = [c < n_sel for c in cnt]
            s = jnp.dot((q * scale).astype(BF16), kt_bf[:, :n_keys], preferred_element_type=F32)
            pieces = []
            for k in range(halves * (i + 1)):
                n, hf = divmod(k, halves)
                sp = s[:, k * LANES:(k + 1) * LANES]
                if n == i:
                    sp = jnp.where(tril[hf], sp, NEG_INF)
                elif sel is not None:
                    sp = jnp.where(sel[n], sp, NEG_INF)
                pieces.append(sp)
            o_ref[0, rows, hs] = _softmax_pv(pieces, v_ref[0, :n_keys, hs]).astype(o_ref.dtype)


def _dil_body(q_ref, k_ref, v_ref, cos_ref, sin_ref, lm_ref, o_ref, qs_ref, kt_ref):
    seq = q_ref.shape[1]
    nb = seq // ATT_BLOCK
    halves = ATT_BLOCK // LANES
    scale = HEAD_DIM ** -0.5
    heads = q_ref.shape[2] // HEAD_DIM
    _rope_qk(q_ref, k_ref, cos_ref, sin_ref, qs_ref, kt_ref)

    for h in range(heads):
        hs = slice(h * HEAD_DIM, (h + 1) * HEAD_DIM)
        kt_bf = kt_ref[hs, :]
        for i in range(nb):
            rows = slice(i * ATT_BLOCK, (i + 1) * ATT_BLOCK)
            n_keys = (i + 1) * ATT_BLOCK
            qs = (qs_ref[rows, hs] * scale).astype(BF16)
            s = jnp.dot(qs, kt_bf[:, :n_keys], preferred_element_type=F32)
            pieces = []
            for k in range(halves * (i + 1)):
                n, hf = divmod(k, halves)
                pieces.append(s[:, k * LANES:(k + 1) * LANES] + lm_ref[i - n, :, hf * LANES:(hf + 1) * LANES])
            o_ref[0, rows, hs] = _softmax_pv(pieces, v_ref[0, :n_keys, hs]).astype(o_ref.dtype)


def _dilation_log_multiplicity(seq):
    nb = seq // ATT_BLOCK
    r = jnp.arange(ATT_BLOCK)
    d = (jnp.arange(nb)[:, None, None] * ATT_BLOCK + r[None, :, None] - r[None, None, :])
    mult = jnp.zeros(d.shape, F32)
    for window, dil in DIL_PATTERNS:
        mult = mult + ((d >= 0) & (d <= window) & (d % dil == 0)).astype(F32)
    return jnp.where(mult > 0, jnp.log(jnp.maximum(mult, 1.0)), NEG_INF)


def _attention_mixer(body, name, kt_dtype, proj, cols, rope_tables, extra=()):
    bsz, seq, _ = proj.shape
    pair = lambda lane_blk: pl.BlockSpec((1, seq, LANES), lambda b, p: (b, 0, lane_blk + p))
    const = lambda shp: pl.BlockSpec(shp, lambda b, p: (0,) * len(shp))
    return pl.pallas_call(
        body,
        grid=(bsz, MIX_WIDTH // LANES),
        in_specs=[pair(c) for c in cols] + [const((seq, LANES)), const((seq, LANES))]
                 + [const(e.shape) for e in extra],
        out_specs=pl.BlockSpec((1, seq, LANES), lambda b, p: (b, 0, p)),
        out_shape=jax.ShapeDtypeStruct((bsz, seq, MIX_WIDTH), BF16),
        scratch_shapes=[pltpu.VMEM((seq, LANES), F32), pltpu.VMEM((LANES, seq), kt_dtype)],
        compiler_params=_cparams("parallel", "parallel"),
        name=name,
    )(proj, proj, proj, *rope_tables, *extra)


def moba_mixer(proj, cols, rope_tables):
    return _attention_mixer(_moba_body, "moba_mixer", F32, proj, cols, rope_tables)


def dilated_mixer(proj, cols, rope_tables):
    seq = proj.shape[1]
    return _attention_mixer(_dil_body, "dilated_mixer", BF16, proj, cols, rope_tables,
                            extra=(_dilation_log_multiplicity(seq),))


def _merge_body(h_ref, gates_ref, y0_ref, y1_ref, y2_ref, y3_ref, wb_ref, wo_ref, o_ref):
    d = h_ref.shape[1]
    merged = jnp.zeros(h_ref.shape, F32)
    for i, y_ref in enumerate((y0_ref, y1_ref, y2_ref, y3_ref)):
        gate = 1.0 / (1.0 + jnp.exp(-gates_ref[:, i * d:(i + 1) * d].astype(F32)))
        merged = merged + gate * _dot(y_ref[...], wb_ref[i])
    o_ref[...] = h_ref[...] + _dot(merged, wo_ref[...])


def gated_merge(h, proj, ys, w_branch, w_out, tm=512):
    t, d = h.shape
    width = ys[0].shape[1]
    tm = min(tm, t)
    rowblk = lambda w: pl.BlockSpec((tm, w), lambda i: (i, 0))
    return pl.pallas_call(
        _merge_body,
        grid=(t // tm,),
        in_specs=[rowblk(d), rowblk(N_BRANCHES * d)] + [rowblk(width)] * N_BRANCHES
                 + [pl.BlockSpec((N_BRANCHES, width, d), lambda i: (0, 0, 0)),
                    pl.BlockSpec((d, d), lambda i: (0, 0))],
        out_specs=rowblk(d),
        out_shape=jax.ShapeDtypeStruct((t, d), F32),
        compiler_params=_cparams("parallel"),
        name="gated_merge",
    )(h, proj, *ys, w_branch, w_out)


def _xattn_body(h_ref, gain_ref, wq_ref, k_ref, v_ref, wo_ref, o_ref):
    h = h_ref[0]
    d = h.shape[1]
    dh = d // X_HEADS
    ms = jnp.mean(h * h, axis=-1, keepdims=True)
    hn = h * lax.rsqrt(ms + NORM_EPS) * gain_ref[...]
    q = _dot(hn, wq_ref[...]) * dh ** -0.5
    outs = []
    for a in range(X_HEADS):
        cs = slice(a * dh, (a + 1) * dh)
        s = _dot_nt(q[:, cs].astype(BF16), k_ref[0, :, cs])
        m = jnp.max(s, axis=-1, keepdims=True)
        e = jnp.exp(s - m)
        p = e / jnp.sum(e, axis=-1, keepdims=True)
        outs.append(_dot(p, v_ref[0, :, cs]))
    o = jnp.concatenate(outs, axis=-1)
    o_ref[0] = h + _dot(o, wo_ref[...])


def cross_attention(h, kv, gain, w_q, w_o, tq=512):
    bsz, seq, d = h.shape
    mlen = kv.shape[1]
    tq = min(tq, seq)
    return pl.pallas_call(
        _xattn_body,
        grid=(bsz, seq // tq),
        in_specs=[pl.BlockSpec((1, tq, d), lambda b, i: (b, i, 0)),
                  pl.BlockSpec((1, d), lambda b, i: (0, 0)),
                  pl.BlockSpec((d, d), lambda b, i: (0, 0)),
                  pl.BlockSpec((1, mlen, d), lambda b, i: (b, 0, 0)),
                  pl.BlockSpec((1, mlen, d), lambda b, i: (b, 0, 1)),
                  pl.BlockSpec((d, d), lambda b, i: (0, 0))],
        out_specs=pl.BlockSpec((1, tq, d), lambda b, i: (b, i, 0)),
        out_shape=jax.ShapeDtypeStruct((bsz, seq, d), F32),
        compiler_params=_cparams("parallel", "parallel"),
        name="cross_attention",
    )(h, gain.reshape(1, d), w_q, kv, kv, w_o)


_NO_RANK = 127.0


def _top_rows(vals, count, with_ranks=False):
    tops = []
    rank = jnp.full(vals.shape, _NO_RANK, F32)
    for r in range(count):
        m = jnp.max(vals, axis=0, keepdims=True)
        tops.append(m)
        hit = vals == m
        if with_ranks:
            rank = jnp.where(hit, float(r), rank)
        vals = jnp.where(hit, -jnp.inf, vals)
    return (tops, rank) if with_ranks else tops


_N_RANKS = PEER_TOPK + 1
_RANK_ROWS = 24


_N_CAND = sum(_N_RANKS // (i + 1) for i in range(_N_RANKS))
_CAND_ROWS = -(-_N_CAND // 8) * 8


def _pair_candidates(a1, a2_ref, cand_ref):
    cand_ref[_N_CAND // 8 * 8:, :] = jnp.full((_CAND_ROWS - _N_CAND // 8 * 8, cand_ref.shape[1]), -jnp.inf, F32)
    row = 0
    for i in range(_N_RANKS):
        n_j = _N_RANKS // (i + 1)
        cand_ref[row:row + n_j, :] = a1[i] + a2_ref[0:n_j, :]
        row += n_j
    return cand_ref[...]


def _router_body(h_ref, gain_ref, wq_hi_ref, wq_lo_ref, keys_ref, rank2_ref, e2_ref, cut_ref, e1_ref,
                 a2_ref, cand_ref):
    h = h_ref[...]
    ms = jnp.mean(h * h, axis=-1, keepdims=True)
    x_hi, x_lo = _split_bf16(h * lax.rsqrt(ms + NORM_EPS) * gain_ref[...])
    q = (jnp.dot(x_hi, wq_hi_ref[...], preferred_element_type=F32)
         + jnp.dot(x_hi, wq_lo_ref[...], preferred_element_type=F32)
         + jnp.dot(x_lo, wq_hi_ref[...], preferred_element_type=F32))
    dk = keys_ref.shape[-1] // 2
    for hd in range(PEER_HEADS):
        s = []
        for half in range(2):
            c0 = (hd * 2 + half) * dk
            q_hi, q_lo = _split_bf16(q[:, c0:c0 + dk])
            keys2 = keys_ref[hd, half]
            s.append(_dot_nt(keys2, jnp.concatenate([q_hi, q_hi], axis=1))
                     + _dot_nt(keys2, jnp.concatenate([q_lo, q_lo], axis=1)))
        s1, s2 = s
        a1 = _top_rows(s1, _N_RANKS)
        a2, rank2 = _top_rows(s2, _N_RANKS, with_ranks=True)
        for r in range(_N_RANKS):
            a2_ref[r:r + 1, :] = a2[r]
        cand = _pair_candidates(a1, a2_ref, cand_ref)
        top = _top_rows(cand, _N_RANKS)
        tau = 0.5 * (top[PEER_TOPK - 1] + top[PEER_TOPK])
        z = jnp.sum(jnp.where(cand > tau, jnp.exp(cand - (a1[0] + a2[0])), 0.0), axis=0, keepdims=True)
        thr = tau - s1
        cut = jnp.zeros(s1.shape, F32)
        for r in range(PEER_TOPK):
            cut = jnp.where(a2[r] > thr, float(r + 1), cut)
        rank2_ref[hd] = rank2.astype(rank2_ref.dtype)
        e2_ref[hd] = jnp.exp(s2 - a2[0]).astype(e2_ref.dtype)
        cut_ref[hd] = cut
        e1_ref[hd] = jnp.exp(s1 - a1[0]) / z


def peer_router(h, gain, w_q, sub_keys, tt=256):
    t, d = h.shape
    tt = min(tt, t)
    nk = sub_keys.shape[2]
    wq_hi, wq_lo = _split_bf16(w_q)
    keys2 = jnp.concatenate(_split_bf16(sub_keys), axis=-1)
    out = [jax.ShapeDtypeStruct((PEER_HEADS, nk, t), dt) for dt in (BF16, BF16, F32, F32)]
    ospec = pl.BlockSpec((PEER_HEADS, nk, tt), lambda i: (0, 0, i))
    return pl.pallas_call(
        _router_body,
        grid=(t // tt,),
        in_specs=[pl.BlockSpec((tt, d), lambda i: (i, 0)),
                  pl.BlockSpec((1, d), lambda i: (0, 0)),
                  pl.BlockSpec(w_q.shape, lambda i: (0, 0)),
                  pl.BlockSpec(w_q.shape, lambda i: (0, 0)),
                  pl.BlockSpec(keys2.shape, lambda i: (0, 0, 0, 0))],
        out_specs=[ospec] * 4,
        out_shape=out,
        scratch_shapes=[pltpu.VMEM((_RANK_ROWS, tt), F32), pltpu.VMEM((_CAND_ROWS, tt), F32)],
        compiler_params=_cparams("parallel"),
        name="peer_router",
    )(h, gain.reshape(1, d), wq_hi, wq_lo, keys2)


def _gelu(x):
    return 0.5 * x * (1.0 + lax.erf(x * (2.0 ** -0.5)))


_ROWS_PER_STEP = 8
_J_CHUNK = 32
_ROW_GROUP = 4
_TOK_COLS = 256
_MM_SPLIT = 4
_BF16_ROWS = 16


def _experts_body(h_ref, gain_ref, u0_ref, un_ref, vt_ref, rank2_ref, e2_ref, cut_ref, e1_ref, o_ref,
                  xn_ref, acc_ref, act_a_ref, act_b_ref, p_a_ref, p_b_ref, cutb_ref, e1b_ref):
    step = pl.program_id(1)
    n_tiles = pl.num_programs(1) - 1
    nk = rank2_ref.shape[1]
    tt = h_ref.shape[0]
    n_cols = tt // _TOK_COLS
    act_refs = (act_a_ref, act_b_ref)
    p_refs = (p_a_ref, p_b_ref)

    @pl.when(step == 0)
    def _():
        h = h_ref[...]
        ms = jnp.mean(h * h, axis=-1, keepdims=True)
        xn_ref[...] = (h * lax.rsqrt(ms + NORM_EPS) * gain_ref[...]).astype(BF16)
        acc_ref[...] = jnp.zeros_like(acc_ref)
        p_refs[1][...] = jnp.zeros(p_refs[1].shape, BF16)
        act_refs[0][...] = _dot_nt(u0_ref[...], xn_ref[...])

    tile = jnp.minimum(step, n_tiles - 1)
    i0 = pl.multiple_of(tile * _ROWS_PER_STEP, _ROWS_PER_STEP)

    def main_block(act_cur, act_nxt, p_cur, p_prev):
        d_rows = acc_ref.shape[0] // _MM_SPLIT
        e_rows = un_ref.shape[0] // _MM_SPLIT

        def v_piece(k, c):
            rs, cs = slice(k * d_rows, (k + 1) * d_rows), slice(c * _TOK_COLS, (c + 1) * _TOK_COLS)
            acc_ref[rs, cs] += jnp.dot(vt_ref[rs, :], p_prev[:, cs], preferred_element_type=F32)

        def act_piece(k, c):
            rs, cs = slice(k * e_rows, (k + 1) * e_rows), slice(c * _TOK_COLS, (c + 1) * _TOK_COLS)
            act_nxt[rs, cs] = _dot_nt(un_ref[rs, :], xn_ref[cs, :])

        pieces = [(f, k, c) for k in range(_MM_SPLIT) for c in range(n_cols) for f in (v_piece, act_piece)]

        for hd in range(PEER_HEADS):
            cut8 = cut_ref[hd, pl.ds(i0, _ROWS_PER_STEP), :]
            e18 = e1_ref[hd, pl.ds(i0, _ROWS_PER_STEP), :]
            for r in range(_ROWS_PER_STEP):
                cutb_ref[hd, r] = jnp.broadcast_to(cut8[r:r + 1], (_BF16_ROWS, tt)).astype(BF16)
                e1b_ref[hd, r] = jnp.broadcast_to(e18[r:r + 1], (_BF16_ROWS, tt)).astype(BF16)

        reps = _J_CHUNK // _BF16_ROWS
        chunks = [(c, jc, rg) for c in range(n_cols) for jc in range(nk // _J_CHUNK)
                  for rg in range(_ROWS_PER_STEP // _ROW_GROUP)]
        per_chunk = -(-len(pieces) // len(chunks))
        for n, (c, jc, rg) in enumerate(chunks):
            for f, k, cc in pieces[n * per_chunk:(n + 1) * per_chunk]:
                f(k, cc)
            ts = slice(c * _TOK_COLS, (c + 1) * _TOK_COLS)
            js = slice(jc * _J_CHUNK, (jc + 1) * _J_CHUNK)
            gates = [jnp.zeros((_J_CHUNK, _TOK_COLS), BF16) for _ in range(_ROW_GROUP)]
            for hd in range(PEER_HEADS):
                rank2 = rank2_ref[hd, js, ts]
                e2 = e2_ref[hd, js, ts]
                for k in range(_ROW_GROUP):
                    r = rg * _ROW_GROUP + k
                    cut = jnp.concatenate([cutb_ref[hd, r, :, ts]] * reps, axis=0)
                    e1 = jnp.concatenate([e1b_ref[hd, r, :, ts]] * reps, axis=0)
                    gates[k] = gates[k] + jnp.where(rank2 < cut, e2 * e1, jnp.zeros_like(e2))
            for k in range(_ROW_GROUP):
                rows = slice((rg * _ROW_GROUP + k) * nk + jc * _J_CHUNK,
                             (rg * _ROW_GROUP + k) * nk + (jc + 1) * _J_CHUNK)
                p_cur[rows, ts] = _gelu(act_cur[rows, ts]).astype(BF16) * gates[k]

    for parity in range(2):
        @pl.when((step % 2 == parity) & (step < n_tiles))
        def _():
            main_block(act_refs[parity], act_refs[1 - parity], p_refs[parity], p_refs[1 - parity])

    last_p = p_refs[(rank2_ref.shape[1] // _ROWS_PER_STEP - 1) % 2]

    @pl.when(step == n_tiles)
    def _():
        acc = acc_ref[...] + jnp.dot(vt_ref[...], last_p[...], preferred_element_type=F32)
        o_ref[...] = h_ref[...] + acc.T


def peer_experts(h, gain, u, v_t, rank2, e2, cut, e1, tt=512):
    t, d = h.shape
    nk = rank2.shape[1]
    tt = min(tt, t)
    et = _ROWS_PER_STEP * nk
    n_tiles = nk // _ROWS_PER_STEP
    rspec = pl.BlockSpec((PEER_HEADS, nk, tt), lambda i, s: (0, 0, i))
    return pl.pallas_call(
        _experts_body,
        grid=(t // tt, n_tiles + 1),
        in_specs=[pl.BlockSpec((tt, d), lambda i, s: (i, 0)),
                  pl.BlockSpec((1, d), lambda i, s: (0, 0)),
                  pl.BlockSpec((et, d), lambda i, s: (0, 0)),
                  pl.BlockSpec((et, d), lambda i, s: (jnp.minimum(s + 1, n_tiles - 1), 0)),
                  pl.BlockSpec((None, d, et), lambda i, s: (jnp.maximum(s - 1, 0), 0, 0)),
                  rspec, rspec, rspec, rspec],
        out_specs=pl.BlockSpec((tt, d), lambda i, s: (i, 0)),
        out_shape=jax.ShapeDtypeStruct((t, d), F32),
        scratch_shapes=[pltpu.VMEM((tt, d), BF16), pltpu.VMEM((d, tt), F32),
                        pltpu.VMEM((et, tt), F32), pltpu.VMEM((et, tt), F32),
                        pltpu.VMEM((et, tt), BF16), pltpu.VMEM((et, tt), BF16),
                        pltpu.VMEM((PEER_HEADS, _ROWS_PER_STEP, _BF16_ROWS, tt), BF16),
                        pltpu.VMEM((PEER_HEADS, _ROWS_PER_STEP, _BF16_ROWS, tt), BF16)],
        compiler_params=_cparams("parallel", "arbitrary"),
        name="peer_experts",
    )(h, gain.reshape(1, d), u, u, v_t, rank2, e2, cut, e1)


def _transpose_cast_body(x_ref, o_ref):
    o_ref[...] = x_ref[...].T.astype(o_ref.dtype)


def transpose_cast(x3, index, dtype, tile, blk=512):
    _, r, c = x3.shape
    per = tile // blk
    return pl.pallas_call(
        _transpose_cast_body,
        grid=(r // blk, c // blk),
        in_specs=[pl.BlockSpec((None, blk, blk), lambda i, j: (index, i, j))],
        out_specs=pl.BlockSpec((None, blk, blk), lambda i, j: (i // per, j, i % per)),
        out_shape=jax.ShapeDtypeStruct((r // tile, c, tile), dtype),
        compiler_params=_cparams("parallel", "parallel"),
        name="transpose_cast",
    )(x3)


def _final_norm_body(x_ref, g_ref, o_ref):
    x = x_ref[...]
    ms = jnp.mean(x * x, axis=-1, keepdims=True)
    o_ref[...] = x * lax.rsqrt(ms + NORM_EPS) * g_ref[...]


def final_norm(x, gain, tm=1024):
    t, d = x.shape
    tm = min(tm, t)
    return pl.pallas_call(
        _final_norm_body,
        grid=(t // tm,),
        in_specs=[pl.BlockSpec((tm, d), lambda i: (i, 0)), pl.BlockSpec((1, d), lambda i: (0, 0))],
        out_specs=pl.BlockSpec((tm, d), lambda i: (i, 0)),
        out_shape=jax.ShapeDtypeStruct((t, d), F32),
        compiler_params=_cparams("parallel"),
        name="final_norm",
    )(x, gain.reshape(1, d))


def _causal_dwconv(x, w, b):
    k = w.shape[0]
    seq = x.shape[1]
    xp = jnp.pad(x, ((0, 0), (k - 1, 0), (0, 0)))
    return sum(xp[:, i:i + seq, :] * w[i] for i in range(k)) + b


def _tr_blocks(t, blk):
    bsz, seq, width = t.shape
    return jnp.swapaxes(t.reshape(bsz, seq // blk, blk, width), 2, 3)


_GATE_BLOCKS = N_BRANCHES * 8
_COL_Z = _GATE_BLOCKS
_COL_RET = _COL_Z + 4
_COL_MOBA = _COL_RET + 16
_COL_DIL = _COL_MOBA + 12
_COL_XBC = _COL_DIL + 12
_PROJ_BLOCKS = 84


def _pack_w_in(w_in_l):
    ssd_conv_dim = MIX_WIDTH + 2 * SSD_GROUPS * SSD_STATE
    o_xbc = MIX_WIDTH
    o_dt = o_xbc + ssd_conv_dim
    o_rest = o_dt + N_HEADS
    o_gate = o_rest + 10 * MIX_WIDTH
    d_model = w_in_l.shape[0]
    used = _COL_XBC * LANES + ssd_conv_dim
    pad = jnp.zeros((d_model, _PROJ_BLOCKS * LANES - used), w_in_l.dtype)
    main = jnp.concatenate([w_in_l[:, o_gate:], w_in_l[:, :o_xbc], w_in_l[:, o_rest:o_gate],
                            w_in_l[:, o_xbc:o_dt], pad], axis=1).astype(BF16)
    w_dt = jnp.pad(w_in_l[:, o_dt:o_rest], ((0, 0), (0, LANES - N_HEADS))).astype(BF16)
    return main, w_dt


def kernel(x, mem, mix_norm, w_in, ssd_conv_w, ssd_conv_b, ssd_dt_bias, ssd_a_log, ssd_d, ssd_norm, ret_norm, w_branch, w_out, x_norm, w_xq, w_xkv, w_xo, ffn_norm, w_pq, peer_sub_keys, peer_u, peer_v, final_norm_gain):
    bsz, seq, d = x.shape
    depth = w_in.shape[0]
    t = bsz * seq
    w = MIX_WIDTH
    gn = SSD_GROUPS * SSD_STATE
    h = x.reshape(t, d)
    mem2 = mem.reshape(-1, d)
    ones = jnp.ones((d,), F32)

    rope_tables = _rope_tables(seq)

    for layer in range(depth):
        w_main, w_dt = _pack_w_in(w_in[layer])
        proj2 = norm_matmul(h, mix_norm[layer], w_main, tm=1024, tn=1536, out_dtype=BF16)
        dt_raw = norm_matmul(h, mix_norm[layer], w_dt, tm=1024)[:, :N_HEADS].reshape(bsz, seq, N_HEADS)
        proj = proj2.reshape(bsz, seq, -1)

        xbc = proj[..., _COL_XBC * LANES:_COL_XBC * LANES + w + 2 * gn].astype(F32)
        xbc = _silu(_causal_dwconv(xbc, ssd_conv_w[layer], ssd_conv_b[layer]))
        xs, bm, cm = xbc[..., :w], xbc[..., w:w + gn], xbc[..., w + gn:]
        dt = jax.nn.softplus(dt_raw + ssd_dt_bias[layer])
        log_a = dt * -jnp.exp(ssd_a_log[layer])
        d_lanes = jnp.repeat(ssd_d[layer], HEAD_DIM).reshape(1, w)
        y_a = ssd_mixer(xs, _tr_blocks(bm, CHUNK), cm, dt, log_a, _tr_blocks(log_a, CHUNK), proj,
                        _COL_Z // 4, d_lanes, ssd_norm[layer])
        y_b = retention_mixer(proj, [_COL_RET // 4 + i for i in range(4)], rope_tables, ret_norm[layer])
        y_c = moba_mixer(proj, [_COL_MOBA + 4 * i for i in range(3)], rope_tables)
        y_d = dilated_mixer(proj, [_COL_DIL + 4 * i for i in range(3)], rope_tables)
        ys = [y.reshape(t, w) for y in (y_a, y_b, y_c, y_d)]
        h = gated_merge(h, proj2, ys, w_branch[layer].astype(BF16), w_out[layer].astype(BF16))
        kv = norm_matmul(mem2, ones, w_xkv[layer].astype(BF16), normalize=False, out_dtype=BF16)
        h = cross_attention(h.reshape(bsz, seq, d), kv.reshape(bsz, -1, 2 * d), x_norm[layer],
                            w_xq[layer].astype(BF16), w_xo[layer].astype(BF16)).reshape(t, d)
        s2, e2, thr, e1 = peer_router(h, ffn_norm[layer], w_pq[layer], peer_sub_keys[layer])
        h = peer_experts(h, ffn_norm[layer], peer_u[layer].astype(BF16),
                         transpose_cast(peer_v, layer, BF16, _ROWS_PER_STEP * PEER_NKEYS), s2, e2, thr, e1)

    return final_norm(h, final_norm_gain).reshape(bsz, seq, d)
```

```python
import functools
import math

import jax
import jax.numpy as jnp
from jax import lax
from jax.experimental import pallas as pl
from jax.experimental.pallas import tpu as pltpu

F32 = jnp.float32
BF16 = jnp.bfloat16
HI = lax.Precision.HIGHEST

NORM_EPS = 1e-6
NEG_INF = -1e30
ROPE_THETA = 10000.0
HEAD_DIM = 64
N_HEADS = 8
MIX_WIDTH = N_HEADS * HEAD_DIM
SSD_GROUPS = 2
SSD_STATE = 64
SSD_CONV = 4
CHUNK = 128
ATT_BLOCK = 256
MOBA_TOPK = 3
DIL_PATTERNS = ((128, 1), (512, 4), (2048, 16))
N_BRANCHES = 4
X_HEADS = 4
PEER_HEADS = 8
PEER_NKEYS = 128
PEER_TOPK = 16
LANES = 128
VMEM_LIMIT = 56 * 1024 * 1024


def _cparams(*sem, flags=None):
    return pltpu.CompilerParams(dimension_semantics=sem, vmem_limit_bytes=VMEM_LIMIT, flags=flags)


def _dot(a, b):
    return jnp.dot(a.astype(BF16), b.astype(BF16), preferred_element_type=F32)


def _dot_nt(a, b, precision=None):
    return lax.dot_general(a, b, (((1,), (1,)), ((), ())), precision=precision,
                           preferred_element_type=F32)


def _silu(x):
    return x * (1.0 / (1.0 + jnp.exp(-x)))


def _norm_matmul_body(x_ref, g_ref, w_ref, o_ref, xn_ref, *, normalize):
    @pl.when(pl.program_id(1) == 0)
    def _():
        x = x_ref[...]
        if normalize:
            ms = jnp.mean(x * x, axis=-1, keepdims=True)
            x = x * lax.rsqrt(ms + NORM_EPS) * g_ref[...]
        xn_ref[...] = x.astype(xn_ref.dtype)

    o_ref[...] = jnp.dot(xn_ref[...], w_ref[...], preferred_element_type=F32).astype(o_ref.dtype)


def norm_matmul(x, gain, w, *, normalize=True, tm=512, tn=512, out_dtype=F32):
    t, k = x.shape
    n = w.shape[1]
    tm, tn = min(tm, t), min(tn, n)
    assert t % tm == 0 and n % tn == 0
    return pl.pallas_call(
        functools.partial(_norm_matmul_body, normalize=normalize),
        grid=(t // tm, n // tn),
        in_specs=[pl.BlockSpec((tm, k), lambda i, j: (i, 0)),
                  pl.BlockSpec((1, k), lambda i, j: (0, 0)),
                  pl.BlockSpec((k, tn), lambda i, j: (0, j))],
        out_specs=pl.BlockSpec((tm, tn), lambda i, j: (i, j)),
        out_shape=jax.ShapeDtypeStruct((t, n), out_dtype),
        scratch_shapes=[pltpu.VMEM((tm, k), BF16)],
        compiler_params=_cparams("parallel", "arbitrary"),
        name="norm_matmul",
    )(x, gain.reshape(1, k), w)


def _ssd_body(xs_ref, bt_ref, c_ref, dt_ref, la_ref, lat_ref, z_ref, dskip_ref, gain_ref,
              o_ref, state_ref):
    seq = xs_ref.shape[1]
    nchunks = seq // CHUNK
    rep = N_HEADS // SSD_GROUPS
    state_ref[...] = jnp.zeros_like(state_ref)
    row = lax.broadcasted_iota(jnp.int32, (CHUNK, CHUNK), 0)
    col = lax.broadcasted_iota(jnp.int32, (CHUNK, CHUNK), 1)
    tril = row >= col
    tril_f = tril.astype(F32)
    triu_f = (row <= col).astype(F32)

    def chunk(c, carry):
        r0 = pl.multiple_of(c * CHUNK, CHUNK)
        rows = pl.ds(r0, CHUNK)
        acum = jnp.dot(tril_f, la_ref[0, rows, :], precision=HI, preferred_element_type=F32)
        acum_t = jnp.dot(lat_ref[0, c], triu_f, precision=HI, preferred_element_type=F32)
        dt = dt_ref[0, rows, :]
        ys = []
        for h in range(N_HEADS):
            g = h // rep
            hs = slice(h * HEAD_DIM, (h + 1) * HEAD_DIM)
            gs = slice(g * SSD_STATE, (g + 1) * SSD_STATE)
            a_col = acum[:, h:h + 1]
            a_row = acum_t[h:h + 1, :]
            a_last = acum[CHUNK - 1:CHUNK, h:h + 1]
            decay = jnp.exp(jnp.where(tril, a_col - a_row, -jnp.inf))
            cc = c_ref[0, rows, gs]
            bt = bt_ref[0, c, gs, :]
            x_h = xs_ref[0, rows, hs]
            xdt = x_h * dt[:, h:h + 1]
            scores = _dot(cc, bt) * decay
            prev = state_ref[h]
            y = _dot(scores, xdt) + _dot(cc, prev) * jnp.exp(a_col)
            state_ref[h] = prev * jnp.exp(a_last) + _dot(bt * jnp.exp(a_last - a_row), xdt)
            ys.append(y + x_h * dskip_ref[0:1, hs])
        y = jnp.concatenate(ys, axis=-1)
        y = y * _silu(z_ref[0, rows, :].astype(F32))
        ms = jnp.mean(y * y, axis=-1, keepdims=True)
        o_ref[0, rows, :] = (y * lax.rsqrt(ms + NORM_EPS) * gain_ref[...]).astype(o_ref.dtype)
        return carry

    lax.fori_loop(0, nchunks, chunk, 0)


def ssd_mixer(xs, bm_t, cm, dt, log_a, log_a_t, proj, z_col, d_skip_lanes, norm_gain):
    bsz, seq, width = xs.shape
    gn = SSD_GROUPS * SSD_STATE
    nc = seq // CHUNK
    full = lambda shp: pl.BlockSpec(shp, lambda b: (b,) + (0,) * (len(shp) - 1))
    const = lambda shp: pl.BlockSpec(shp, lambda b: (0,) * len(shp))
    return pl.pallas_call(
        _ssd_body,
        grid=(bsz,),
        in_specs=[full((1, seq, width)), full((1, nc, gn, CHUNK)), full((1, seq, gn)),
                  full((1, seq, N_HEADS)), full((1, seq, N_HEADS)), full((1, nc, N_HEADS, CHUNK)),
                  pl.BlockSpec((1, seq, width), lambda b: (b, 0, z_col)), const((1, width)), const((1, width))],
        out_specs=full((1, seq, width)),
        out_shape=jax.ShapeDtypeStruct((bsz, seq, width), BF16),
        scratch_shapes=[pltpu.VMEM((N_HEADS, SSD_STATE, HEAD_DIM), F32)],
        compiler_params=_cparams("parallel"),
        name="ssd_mixer",
    )(xs, bm_t, cm, dt, log_a, log_a_t, proj, d_skip_lanes, norm_gain.reshape(1, width))


def _rope_tables(seq):
    half = HEAD_DIM // 2
    inv_freq = ROPE_THETA ** (-jnp.arange(half, dtype=F32) / half)
    ang = jnp.arange(seq, dtype=F32)[:, None] * inv_freq[None, :]
    sin = jnp.sin(ang)
    cos = jnp.tile(jnp.cos(ang), (1, LANES // half))
    sin_signed = jnp.tile(jnp.concatenate([-sin, sin], axis=1), (1, LANES // HEAD_DIM))
    return cos, sin_signed


def _rope_lanes(x, cos, sin_signed):
    width = x.shape[1]
    half = HEAD_DIM // 2
    lane = lax.broadcasted_iota(jnp.int32, x.shape, 1)
    partner = jnp.where(lane % HEAD_DIM < half, pltpu.roll(x, width - half, axis=1), pltpu.roll(x, half, axis=1))
    reps = width // LANES
    if reps > 1:
        cos = jnp.concatenate([cos] * reps, axis=1)
        sin_signed = jnp.concatenate([sin_signed] * reps, axis=1)
    return x * cos + partner * sin_signed


def _ret_body(q_ref, k_ref, v_ref, g_ref, cos_ref, sin_ref, dmat_ref, zeta_ref, xi_ref, cdec_ref, gain_ref,
              o_ref, state_ref):
    seq = q_ref.shape[1]
    nchunks = seq // CHUNK
    state_ref[...] = jnp.zeros_like(state_ref)

    def chunk(c, carry):
        r0 = pl.multiple_of(c * CHUNK, CHUNK)
        rows = pl.ds(r0, CHUNK)
        cos, sin = cos_ref[rows, :], sin_ref[rows, :]
        q = _rope_lanes(q_ref[0, rows, :].astype(F32), cos, sin).astype(BF16)
        k = _rope_lanes(k_ref[0, rows, :].astype(F32), cos, sin) * HEAD_DIM ** -0.5
        k_t = k.T
        ys = []
        for h in range(N_HEADS):
            hs = slice(h * HEAD_DIM, (h + 1) * HEAD_DIM)
            qc = q[:, hs]
            kt = k_t[hs, :]
            vc = v_ref[0, rows, hs]
            inner = _dot(qc, kt) * dmat_ref[h]
            prev = state_ref[h]
            y = _dot(inner, vc) + _dot(qc, prev) * xi_ref[h]
            state_ref[h] = prev * cdec_ref[h] + _dot(kt * zeta_ref[h], vc)
            mu = jnp.mean(y, axis=-1, keepdims=True)
            yc = y - mu
            var = jnp.mean(yc * yc, axis=-1, keepdims=True)
            ys.append(yc * lax.rsqrt(var + NORM_EPS))
        y = jnp.concatenate(ys, axis=-1) * gain_ref[...]
        o_ref[0, rows, :] = (_silu(g_ref[0, rows, :].astype(F32)) * y).astype(o_ref.dtype)
        return carry

    lax.fori_loop(0, nchunks, chunk, 0)


def retention_mixer(proj, cols, rope_tables, norm_gain):
    bsz, seq, _ = proj.shape
    width = MIX_WIDTH
    log_gamma = jnp.log1p(-jnp.exp2(-5.0 - jnp.arange(N_HEADS, dtype=F32)))
    idx = jnp.arange(CHUNK, dtype=F32)
    rel = idx[:, None] - idx[None, :]
    dmat = jnp.where(rel >= 0, jnp.exp(jnp.maximum(rel, 0.0)[None] * log_gamma[:, None, None]), 0.0)
    zeta = jnp.exp((CHUNK - 1.0 - idx)[None, None, :] * log_gamma[:, None, None])
    xi = jnp.exp((idx + 1.0)[None, :, None] * log_gamma[:, None, None])
    cdec = jnp.exp(CHUNK * log_gamma)[:, None, None]
    col = lambda blk: pl.BlockSpec((1, seq, width), lambda b: (b, 0, blk))
    const = lambda shp: pl.BlockSpec(shp, lambda b: (0,) * len(shp))
    return pl.pallas_call(
        _ret_body,
        grid=(bsz,),
        in_specs=[col(c) for c in cols] + [const((seq, LANES)), const((seq, LANES)),
                  const((N_HEADS, CHUNK, CHUNK)), const((N_HEADS, 1, CHUNK)),
                  const((N_HEADS, CHUNK, 1)), const((N_HEADS, 1, 1)), const((1, width))],
        out_specs=pl.BlockSpec((1, seq, width), lambda b: (b, 0, 0)),
        out_shape=jax.ShapeDtypeStruct((bsz, seq, width), BF16),
        scratch_shapes=[pltpu.VMEM((N_HEADS, HEAD_DIM, HEAD_DIM), F32)],
        compiler_params=_cparams("parallel"),
        name="retention_mixer",
    )(proj, proj, proj, proj, *rope_tables, dmat, zeta, xi, cdec, norm_gain.reshape(1, width))


def _softmax_pv(pieces, v_ones, hs):
    m = jnp.max(functools.reduce(jnp.maximum, pieces), axis=-1, keepdims=True)
    p = jnp.concatenate([jnp.exp(sp - m) for sp in pieces], axis=1).astype(BF16)
    o = jnp.dot(p, v_ones, preferred_element_type=F32)
    return (o / pltpu.roll(o, HEAD_DIM, axis=1))[:, hs]


def _values_with_ones(v_ref, hs, vo_ref):
    v = v_ref[0]
    lane = lax.broadcasted_iota(jnp.int32, v.shape, 1)
    vo_ref[...] = jnp.where((lane >= hs.start) & (lane < hs.stop), v, jnp.ones_like(v))


def _split_bf16(x):
    hi = x.astype(BF16)
    return hi, (x - hi.astype(F32)).astype(BF16)


def _rope_qk(q_ref, k_ref, cos_ref, sin_ref, qs_ref, kt_ref):
    cos, sin = cos_ref[...], sin_ref[...]
    qs_ref[...] = _rope_lanes(q_ref[0].astype(F32), cos, sin)
    kt_ref[...] = _rope_lanes(k_ref[0].astype(F32), cos, sin).T.astype(kt_ref.dtype)


def _moba_body(q_ref, k_ref, v_ref, cos_ref, sin_ref, o_ref, qs_ref, kt_ref, vo_ref):
    seq = q_ref.shape[1]
    nb = seq // ATT_BLOCK
    halves = ATT_BLOCK // LANES
    scale = HEAD_DIM ** -0.5
    heads = q_ref.shape[2] // HEAD_DIM
    n_sel = min(MOBA_TOPK, nb - 1)
    row = lax.broadcasted_iota(jnp.int32, (ATT_BLOCK, LANES), 0)
    col = lax.broadcasted_iota(jnp.int32, (ATT_BLOCK, LANES), 1)
    tril = [row >= col + hf * LANES for hf in range(halves)]
    _rope_qk(q_ref, k_ref, cos_ref, sin_ref, qs_ref, kt_ref)

    for h in range(heads):
        hs = slice(h * HEAD_DIM, (h + 1) * HEAD_DIM)
        kt = kt_ref[hs, :]
        kt_bf = kt.astype(BF16)
        _values_with_ones(v_ref, hs, vo_ref)
        reps = [jnp.broadcast_to(jnp.mean(kt[:, n * ATT_BLOCK:(n + 1) * ATT_BLOCK], axis=1, keepdims=True),
                                 (HEAD_DIM, LANES)) for n in range(nb - 1)]
        kr_hi, kr_lo = _split_bf16(jnp.concatenate(reps, axis=1))
        kr4 = jnp.concatenate([kr_hi, kr_lo, kr_hi, kr_lo], axis=0)

        for i in range(nb):
            rows = slice(i * ATT_BLOCK, (i + 1) * ATT_BLOCK)
            n_keys = (i + 1) * ATT_BLOCK
            q = qs_ref[rows, hs]
            sel = None
            if i > n_sel:
                q_hi, q_lo = _split_bf16(q)
                q4 = jnp.concatenate([q_hi, q_hi, q_lo, q_lo], axis=1)
                g = jnp.dot(q4, kr4[:, :i * LANES], preferred_element_type=F32)
                gs = [g[:, n * LANES:(n + 1) * LANES] for n in range(i)]
                cnt = [jnp.full((ATT_BLOCK, LANES), float(i - 1 - a), F32) for a in range(i)]
                for a in range(i):
                    for b in range(a + 1, i):
                        a_wins = jnp.where(gs[a] >= gs[b], 1.0, 0.0)
                        cnt[b] = cnt[b] + a_wins
                        cnt[a] = cnt[a] - a_wins
                sel = [c < n_sel for c in cnt]
            s = jnp.dot((q * scale).astype(BF16), kt_bf[:, :n_keys], preferred_element_type=F32)
            pieces = []
            for k in range(halves * (i + 1)):
                n, hf = divmod(k, halves)
                sp = s[:, k * LANES:(k + 1) * LANES]
                if n == i:
                    sp = jnp.where(tril[hf], sp, NEG_INF)
                elif sel is not None:
                    sp = jnp.where(sel[n], sp, NEG_INF)
                pieces.append(sp)
            o_ref[0, rows, hs] = _softmax_pv(pieces, vo_ref[:n_keys, :], hs).astype(o_ref.dtype)


def _dil_body(q_ref, k_ref, v_ref, cos_ref, sin_ref, lm_ref, o_ref, qs_ref, kt_ref, vo_ref):
    seq = q_ref.shape[1]
    nb = seq // ATT_BLOCK
    halves = ATT_BLOCK // LANES
    scale = HEAD_DIM ** -0.5
    heads = q_ref.shape[2] // HEAD_DIM
    _rope_qk(q_ref, k_ref, cos_ref, sin_ref, qs_ref, kt_ref)

    for h in range(heads):
        hs = slice(h * HEAD_DIM, (h + 1) * HEAD_DIM)
        kt_bf = kt_ref[hs, :]
        _values_with_ones(v_ref, hs, vo_ref)
        for i in range(nb):
            rows = slice(i * ATT_BLOCK, (i + 1) * ATT_BLOCK)
            n_keys = (i + 1) * ATT_BLOCK
            qs = (qs_ref[rows, hs] * scale).astype(BF16)
            s = jnp.dot(qs, kt_bf[:, :n_keys], preferred_element_type=F32)
            pieces = []
            for k in range(halves * (i + 1)):
                n, hf = divmod(k, halves)
                pieces.append(s[:, k * LANES:(k + 1) * LANES] + lm_ref[i - n, :, hf * LANES:(hf + 1) * LANES])
            o_ref[0, rows, hs] = _softmax_pv(pieces, vo_ref[:n_keys, :], hs).astype(o_ref.dtype)


def _dilation_log_multiplicity(seq):
    nb = seq // ATT_BLOCK
    r = jnp.arange(ATT_BLOCK)
    d = (jnp.arange(nb)[:, None, None] * ATT_BLOCK + r[None, :, None] - r[None, None, :])
    mult = jnp.zeros(d.shape, F32)
    for window, dil in DIL_PATTERNS:
        mult = mult + ((d >= 0) & (d <= window) & (d % dil == 0)).astype(F32)
    return jnp.where(mult > 0, jnp.log(jnp.maximum(mult, 1.0)), NEG_INF)


def _attention_mixer(body, name, kt_dtype, proj, cols, rope_tables, extra=()):
    bsz, seq, _ = proj.shape
    pair = lambda lane_blk: pl.BlockSpec((1, seq, LANES), lambda b, p: (b, 0, lane_blk + p))
    const = lambda shp: pl.BlockSpec(shp, lambda b, p: (0,) * len(shp))
    return pl.pallas_call(
        body,
        grid=(bsz, MIX_WIDTH // LANES),
        in_specs=[pair(c) for c in cols] + [const((seq, LANES)), const((seq, LANES))]
                 + [const(e.shape) for e in extra],
        out_specs=pl.BlockSpec((1, seq, LANES), lambda b, p: (b, 0, p)),
        out_shape=jax.ShapeDtypeStruct((bsz, seq, MIX_WIDTH), BF16),
        scratch_shapes=[pltpu.VMEM((seq, LANES), F32), pltpu.VMEM((LANES, seq), kt_dtype),
                        pltpu.VMEM((seq, LANES), BF16)],
        compiler_params=_cparams("parallel", "parallel"),
        name=name,
    )(proj, proj, proj, *rope_tables, *extra)


def moba_mixer(proj, cols, rope_tables):
    return _attention_mixer(_moba_body, "moba_mixer", F32, proj, cols, rope_tables)


def dilated_mixer(proj, cols, rope_tables):
    seq = proj.shape[1]
    return _attention_mixer(_dil_body, "dilated_mixer", BF16, proj, cols, rope_tables,
                            extra=(_dilation_log_multiplicity(seq),))


def _merge_body(h_ref, gates_ref, y0_ref, y1_ref, y2_ref, y3_ref, wb_ref, wo_ref, o_ref):
    d = h_ref.shape[1]
    merged = jnp.zeros(h_ref.shape, F32)
    for i, y_ref in enumerate((y0_ref, y1_ref, y2_ref, y3_ref)):
        gate = 1.0 / (1.0 + jnp.exp(-gates_ref[:, i * d:(i + 1) * d].astype(F32)))
        merged = merged + gate * _dot(y_ref[...], wb_ref[i])
    o_ref[...] = h_ref[...] + _dot(merged, wo_ref[...])


def gated_merge(h, proj, ys, w_branch, w_out, tm=512):
    t, d = h.shape
    width = ys[0].shape[1]
    tm = min(tm, t)
    rowblk = lambda w: pl.BlockSpec((tm, w), lambda i: (i, 0))
    return pl.pallas_call(
        _merge_body,
        grid=(t // tm,),
        in_specs=[rowblk(d), rowblk(N_BRANCHES * d)] + [rowblk(width)] * N_BRANCHES
                 + [pl.BlockSpec((N_BRANCHES, width, d), lambda i: (0, 0, 0)),
                    pl.BlockSpec((d, d), lambda i: (0, 0))],
        out_specs=rowblk(d),
        out_shape=jax.ShapeDtypeStruct((t, d), F32),
        compiler_params=_cparams("parallel"),
        name="gated_merge",
    )(h, proj, *ys, w_branch, w_out)


def _xattn_body(h_ref, gain_ref, wq_ref, k_ref, v_ref, wo_ref, o_ref):
    h = h_ref[0]
    d = h.shape[1]
    dh = d // X_HEADS
    ms = jnp.mean(h * h, axis=-1, keepdims=True)
    hn = h * lax.rsqrt(ms + NORM_EPS) * gain_ref[...]
    q = _dot(hn, wq_ref[...]) * dh ** -0.5
    outs = []
    for a in range(X_HEADS):
        cs = slice(a * dh, (a + 1) * dh)
        s = _dot_nt(q[:, cs].astype(BF16), k_ref[0, :, cs])
        m = jnp.max(s, axis=-1, keepdims=True)
        e = jnp.exp(s - m)
        p = e / jnp.sum(e, axis=-1, keepdims=True)
        outs.append(_dot(p, v_ref[0, :, cs]))
    o = jnp.concatenate(outs, axis=-1)
    o_ref[0] = h + _dot(o, wo_ref[...])


def cross_attention(h, kv, gain, w_q, w_o, tq=512):
    bsz, seq, d = h.shape
    mlen = kv.shape[1]
    tq = min(tq, seq)
    return pl.pallas_call(
        _xattn_body,
        grid=(bsz, seq // tq),
        in_specs=[pl.BlockSpec((1, tq, d), lambda b, i: (b, i, 0)),
                  pl.BlockSpec((1, d), lambda b, i: (0, 0)),
                  pl.BlockSpec((d, d), lambda b, i: (0, 0)),
                  pl.BlockSpec((1, mlen, d), lambda b, i: (b, 0, 0)),
                  pl.BlockSpec((1, mlen, d), lambda b, i: (b, 0, 1)),
                  pl.BlockSpec((d, d), lambda b, i: (0, 0))],
        out_specs=pl.BlockSpec((1, tq, d), lambda b, i: (b, i, 0)),
        out_shape=jax.ShapeDtypeStruct((bsz, seq, d), F32),
        compiler_params=_cparams("parallel", "parallel"),
        name="cross_attention",
    )(h, gain.reshape(1, d), w_q, kv, kv, w_o)


_NO_RANK = 127.0


def _top_rows(vals, count, with_ranks=False):
    tops = []
    rank = jnp.full(vals.shape, _NO_RANK, F32)
    for r in range(count):
        m = jnp.max(vals, axis=0, keepdims=True)
        tops.append(m)
        hit = vals == m
        if with_ranks:
            rank = jnp.where(hit, float(r), rank)
        vals = jnp.where(hit, -jnp.inf, vals)
    return (tops, rank) if with_ranks else tops


_N_RANKS = PEER_TOPK + 1
_RANK_ROWS = 24


_N_CAND = sum(_N_RANKS // (i + 1) for i in range(_N_RANKS))
_CAND_ROWS = -(-_N_CAND // 8) * 8


def _pair_candidates(a1, a2_ref, cand_ref):
    cand_ref[_N_CAND // 8 * 8:, :] = jnp.full((_CAND_ROWS - _N_CAND // 8 * 8, cand_ref.shape[1]), -jnp.inf, F32)
    row = 0
    for i in range(_N_RANKS):
        n_j = _N_RANKS // (i + 1)
        cand_ref[row:row + n_j, :] = a1[i] + a2_ref[0:n_j, :]
        row += n_j
    return cand_ref[...]


def _router_body(h_ref, gain_ref, wq_hi_ref, wq_lo_ref, keys_ref, rank2_ref, e2_ref, cut_ref, e1_ref,
                 a2_ref, cand_ref):
    h = h_ref[...]
    ms = jnp.mean(h * h, axis=-1, keepdims=True)
    x_hi, x_lo = _split_bf16(h * lax.rsqrt(ms + NORM_EPS) * gain_ref[...])
    q = (jnp.dot(x_hi, wq_hi_ref[...], preferred_element_type=F32)
         + jnp.dot(x_hi, wq_lo_ref[...], preferred_element_type=F32)
         + jnp.dot(x_lo, wq_hi_ref[...], preferred_element_type=F32))
    dk = keys_ref.shape[-1] // 2
    for hd in range(PEER_HEADS):
        s = []
        for half in range(2):
            c0 = (hd * 2 + half) * dk
            q_hi, q_lo = _split_bf16(q[:, c0:c0 + dk])
            keys2 = keys_ref[hd, half]
            s.append(_dot_nt(keys2, jnp.concatenate([q_hi, q_hi], axis=1))
                     + _dot_nt(keys2, jnp.concatenate([q_lo, q_lo], axis=1)))
        s1, s2 = s
        a1 = _top_rows(s1, _N_RANKS)
        a2, rank2 = _top_rows(s2, _N_RANKS, with_ranks=True)
        for r in range(_N_RANKS):
            a2_ref[r:r + 1, :] = a2[r]
        cand = _pair_candidates(a1, a2_ref, cand_ref)
        top = _top_rows(cand, _N_RANKS)
        tau = 0.5 * (top[PEER_TOPK - 1] + top[PEER_TOPK])
        z = jnp.sum(jnp.where(cand > tau, jnp.exp(cand - (a1[0] + a2[0])), 0.0), axis=0, keepdims=True)
        thr = tau - s1
        cut = jnp.zeros(s1.shape, F32)
        for r in range(PEER_TOPK):
            cut = jnp.where(a2[r] > thr, float(r + 1), cut)
        rank2_ref[hd] = rank2.astype(rank2_ref.dtype)
        e2_ref[hd] = jnp.exp(s2 - a2[0]).astype(e2_ref.dtype)
        cut_ref[hd] = cut
        e1_ref[hd] = jnp.exp(s1 - a1[0]) / z


def peer_router(h, gain, w_q, sub_keys, tt=256):
    t, d = h.shape
    tt = min(tt, t)
    nk = sub_keys.shape[2]
    wq_hi, wq_lo = _split_bf16(w_q)
    keys2 = jnp.concatenate(_split_bf16(sub_keys), axis=-1)
    out = [jax.ShapeDtypeStruct((PEER_HEADS, nk, t), dt) for dt in (BF16, BF16, F32, F32)]
    ospec = pl.BlockSpec((PEER_HEADS, nk, tt), lambda i: (0, 0, i))
    return pl.pallas_call(
        _router_body,
        grid=(t // tt,),
        in_specs=[pl.BlockSpec((tt, d), lambda i: (i, 0)),
                  pl.BlockSpec((1, d), lambda i: (0, 0)),
                  pl.BlockSpec(w_q.shape, lambda i: (0, 0)),
                  pl.BlockSpec(w_q.shape, lambda i: (0, 0)),
                  pl.BlockSpec(keys2.shape, lambda i: (0, 0, 0, 0))],
        out_specs=[ospec] * 4,
        out_shape=out,
        scratch_shapes=[pltpu.VMEM((_RANK_ROWS, tt), F32), pltpu.VMEM((_CAND_ROWS, tt), F32)],
        compiler_params=_cparams("parallel"),
        name="peer_router",
    )(h, gain.reshape(1, d), wq_hi, wq_lo, keys2)


def _gelu(x):
    return 0.5 * x * (1.0 + lax.erf(x * (2.0 ** -0.5)))


_ROWS_PER_STEP = 8
_J_CHUNK = 32
_ROW_GROUP = 4
_TOK_COLS = 256
_MM_SPLIT = 4
_BF16_ROWS = 16


def _experts_body(h_ref, gain_ref, u0_ref, un_ref, vt_ref, rank2_ref, e2_ref, cut_ref, e1_ref, o_ref,
                  xn_ref, acc_ref, act_a_ref, act_b_ref, p_a_ref, p_b_ref, cutb_ref, e1b_ref):
    step = pl.program_id(1)
    n_tiles = pl.num_programs(1) - 1
    nk = rank2_ref.shape[1]
    tt = h_ref.shape[0]
    n_cols = tt // _TOK_COLS
    act_refs = (act_a_ref, act_b_ref)
    p_refs = (p_a_ref, p_b_ref)

    @pl.when(step == 0)
    def _():
        h = h_ref[...]
        ms = jnp.mean(h * h, axis=-1, keepdims=True)
        xn_ref[...] = (h * lax.rsqrt(ms + NORM_EPS) * gain_ref[...]).astype(BF16)
        acc_ref[...] = jnp.zeros_like(acc_ref)
        p_refs[1][...] = jnp.zeros(p_refs[1].shape, BF16)
        act_refs[0][...] = _dot_nt(u0_ref[...], xn_ref[...])

    def main_block(act_cur, act_nxt, p_cur, p_prev):
        d_rows = acc_ref.shape[0] // _MM_SPLIT
        e_rows = un_ref.shape[0] // _MM_SPLIT

        def v_piece(k, c):
            rs, cs = slice(k * d_rows, (k + 1) * d_rows), slice(c * _TOK_COLS, (c + 1) * _TOK_COLS)
            acc_ref[rs, cs] += jnp.dot(vt_ref[rs, :], p_prev[:, cs], preferred_element_type=F32)

        def act_piece(k, c):
            rs, cs = slice(k * e_rows, (k + 1) * e_rows), slice(c * _TOK_COLS, (c + 1) * _TOK_COLS)
            act_nxt[rs, cs] = _dot_nt(un_ref[rs, :], xn_ref[cs, :])

        pieces = [(f, k, c) for k in range(_MM_SPLIT) for c in range(n_cols) for f in (v_piece, act_piece)]

        for hd in range(PEER_HEADS):
            cut8 = cut_ref[hd]
            e18 = e1_ref[hd]
            for r in range(_ROWS_PER_STEP):
                cutb_ref[hd, r] = jnp.broadcast_to(cut8[r:r + 1], (_BF16_ROWS, tt)).astype(BF16)
                e1b_ref[hd, r] = jnp.broadcast_to(e18[r:r + 1], (_BF16_ROWS, tt)).astype(BF16)

        reps = _J_CHUNK // _BF16_ROWS
        chunks = [(c, jc, rg) for c in range(n_cols) for jc in range(nk // _J_CHUNK)
                  for rg in range(_ROWS_PER_STEP // _ROW_GROUP)]
        per_chunk = -(-len(pieces) // len(chunks))
        for n, (c, jc, rg) in enumerate(chunks):
            for f, k, cc in pieces[n * per_chunk:(n + 1) * per_chunk]:
                f(k, cc)
            ts = slice(c * _TOK_COLS, (c + 1) * _TOK_COLS)
            js = slice(jc * _J_CHUNK, (jc + 1) * _J_CHUNK)
            gates = [jnp.zeros((_J_CHUNK, _TOK_COLS), BF16) for _ in range(_ROW_GROUP)]
            for hd in range(PEER_HEADS):
                rank2 = rank2_ref[hd, js, ts]
                e2 = e2_ref[hd, js, ts]
                for k in range(_ROW_GROUP):
                    r = rg * _ROW_GROUP + k
                    cut = jnp.concatenate([cutb_ref[hd, r, :, ts]] * reps, axis=0)
                    e1 = jnp.concatenate([e1b_ref[hd, r, :, ts]] * reps, axis=0)
                    gates[k] = gates[k] + jnp.where(rank2 < cut, e2 * e1, jnp.zeros_like(e2))
            for k in range(_ROW_GROUP):
                rows = slice((rg * _ROW_GROUP + k) * nk + jc * _J_CHUNK,
                             (rg * _ROW_GROUP + k) * nk + (jc + 1) * _J_CHUNK)
                p_cur[rows, ts] = _gelu(act_cur[rows, ts]).astype(BF16) * gates[k]

    for parity in range(2):
        @pl.when((step % 2 == parity) & (step < n_tiles))
        def _():
            main_block(act_refs[parity], act_refs[1 - parity], p_refs[parity], p_refs[1 - parity])

    last_p = p_refs[(rank2_ref.shape[1] // _ROWS_PER_STEP - 1) % 2]

    @pl.when(step == n_tiles)
    def _():
        acc = acc_ref[...] + jnp.dot(vt_ref[...], last_p[...], preferred_element_type=F32)
        o_ref[...] = h_ref[...] + acc.T


def peer_experts(h, gain, u, v_t, rank2, e2, cut, e1, tt=512):
    t, d = h.shape
    nk = rank2.shape[1]
    tt = min(tt, t)
    et = _ROWS_PER_STEP * nk
    n_tiles = nk // _ROWS_PER_STEP
    rspec = pl.BlockSpec((PEER_HEADS, nk, tt), lambda i, s: (0, 0, i))
    row_spec = pl.BlockSpec((PEER_HEADS, _ROWS_PER_STEP, tt), lambda i, s: (0, jnp.minimum(s, n_tiles - 1), i))
    return pl.pallas_call(
        _experts_body,
        grid=(t // tt, n_tiles + 1),
        in_specs=[pl.BlockSpec((tt, d), lambda i, s: (i, 0)),
                  pl.BlockSpec((1, d), lambda i, s: (0, 0)),
                  pl.BlockSpec((et, d), lambda i, s: (0, 0)),
                  pl.BlockSpec((et, d), lambda i, s: (jnp.minimum(s + 1, n_tiles - 1), 0)),
                  pl.BlockSpec((None, d, et), lambda i, s: (jnp.maximum(s - 1, 0), 0, 0)),
                  rspec, rspec, row_spec, row_spec],
        out_specs=pl.BlockSpec((tt, d), lambda i, s: (i, 0)),
        out_shape=jax.ShapeDtypeStruct((t, d), F32),
        scratch_shapes=[pltpu.VMEM((tt, d), BF16), pltpu.VMEM((d, tt), F32),
                        pltpu.VMEM((et, tt), F32), pltpu.VMEM((et, tt), F32),
                        pltpu.VMEM((et, tt), BF16), pltpu.VMEM((et, tt), BF16),
                        pltpu.VMEM((PEER_HEADS, _ROWS_PER_STEP, _BF16_ROWS, tt), BF16),
                        pltpu.VMEM((PEER_HEADS, _ROWS_PER_STEP, _BF16_ROWS, tt), BF16)],
        compiler_params=_cparams("parallel", "arbitrary"),
        name="peer_experts",
    )(h, gain.reshape(1, d), u, u, v_t, rank2, e2, cut, e1)


def _transpose_cast_body(x_ref, o_ref):
    o_ref[...] = x_ref[...].T.astype(o_ref.dtype)


def transpose_cast(x3, index, dtype, tile, blk=512):
    _, r, c = x3.shape
    per = tile // blk
    return pl.pallas_call(
        _transpose_cast_body,
        grid=(r // blk, c // blk),
        in_specs=[pl.BlockSpec((None, blk, blk), lambda i, j: (index, i, j))],
        out_specs=pl.BlockSpec((None, blk, blk), lambda i, j: (i // per, j, i % per)),
        out_shape=jax.ShapeDtypeStruct((r // tile, c, tile), dtype),
        compiler_params=_cparams("parallel", "parallel"),
        name="transpose_cast",
    )(x3)


def _final_norm_body(x_ref, g_ref, o_ref):
    x = x_ref[...]
    ms = jnp.mean(x * x, axis=-1, keepdims=True)
    o_ref[...] = x * lax.rsqrt(ms + NORM_EPS) * g_ref[...]


def final_norm(x, gain, tm=1024):
    t, d = x.shape
    tm = min(tm, t)
    return pl.pallas_call(
        _final_norm_body,
        grid=(t // tm,),
        in_specs=[pl.BlockSpec((tm, d), lambda i: (i, 0)), pl.BlockSpec((1, d), lambda i: (0, 0))],
        out_specs=pl.BlockSpec((tm, d), lambda i: (i, 0)),
        out_shape=jax.ShapeDtypeStruct((t, d), F32),
        compiler_params=_cparams("parallel"),
        name="final_norm",
    )(x, gain.reshape(1, d))


def _causal_dwconv(x, w, b):
    k = w.shape[0]
    seq = x.shape[1]
    xp = jnp.pad(x, ((0, 0), (k - 1, 0), (0, 0)))
    return sum(xp[:, i:i + seq, :] * w[i] for i in range(k)) + b


def _tr_blocks(t, blk):
    bsz, seq, width = t.shape
    return jnp.swapaxes(t.reshape(bsz, seq // blk, blk, width), 2, 3)


_GATE_BLOCKS = N_BRANCHES * 8
_COL_Z = _GATE_BLOCKS
_COL_RET = _COL_Z + 4
_COL_MOBA = _COL_RET + 16
_COL_DIL = _COL_MOBA + 12
_COL_XBC = _COL_DIL + 12
_PROJ_BLOCKS = 84


def _pack_w_in(w_in_l):
    ssd_conv_dim = MIX_WIDTH + 2 * SSD_GROUPS * SSD_STATE
    o_xbc = MIX_WIDTH
    o_dt = o_xbc + ssd_conv_dim
    o_rest = o_dt + N_HEADS
    o_gate = o_rest + 10 * MIX_WIDTH
    d_model = w_in_l.shape[0]
    used = _COL_XBC * LANES + ssd_conv_dim
    pad = jnp.zeros((d_model, _PROJ_BLOCKS * LANES - used), w_in_l.dtype)
    main = jnp.concatenate([w_in_l[:, o_gate:], w_in_l[:, :o_xbc], w_in_l[:, o_rest:o_gate],
                            w_in_l[:, o_xbc:o_dt], pad], axis=1).astype(BF16)
    w_dt = jnp.pad(w_in_l[:, o_dt:o_rest], ((0, 0), (0, LANES - N_HEADS))).astype(BF16)
    return main, w_dt


def kernel(x, mem, mix_norm, w_in, ssd_conv_w, ssd_conv_b, ssd_dt_bias, ssd_a_log, ssd_d, ssd_norm, ret_norm, w_branch, w_out, x_norm, w_xq, w_xkv, w_xo, ffn_norm, w_pq, peer_sub_keys, peer_u, peer_v, final_norm_gain):
    bsz, seq, d = x.shape
    depth = w_in.shape[0]
    t = bsz * seq
    w = MIX_WIDTH
    gn = SSD_GROUPS * SSD_STATE
    h = x.reshape(t, d)
    mem2 = mem.reshape(-1, d)
    ones = jnp.ones((d,), F32)

    rope_tables = _rope_tables(seq)

    for layer in range(depth):
        w_main, w_dt = _pack_w_in(w_in[layer])
        proj2 = norm_matmul(h, mix_norm[layer], w_main, tm=1024, tn=1536, out_dtype=BF16)
        dt_raw = norm_matmul(h, mix_norm[layer], w_dt, tm=1024)[:, :N_HEADS].reshape(bsz, seq, N_HEADS)
        proj = proj2.reshape(bsz, seq, -1)

        xbc = proj[..., _COL_XBC * LANES:_COL_XBC * LANES + w + 2 * gn].astype(F32)
        xbc = _silu(_causal_dwconv(xbc, ssd_conv_w[layer], ssd_conv_b[layer]))
        xs, bm, cm = xbc[..., :w], xbc[..., w:w + gn], xbc[..., w + gn:]
        dt = jax.nn.softplus(dt_raw + ssd_dt_bias[layer])
        log_a = dt * -jnp.exp(ssd_a_log[layer])
        d_lanes = jnp.repeat(ssd_d[layer], HEAD_DIM).reshape(1, w)
        y_a = ssd_mixer(xs, _tr_blocks(bm, CHUNK), cm, dt, log_a, _tr_blocks(log_a, CHUNK), proj,
                        _COL_Z // 4, d_lanes, ssd_norm[layer])
        y_b = retention_mixer(proj, [_COL_RET // 4 + i for i in range(4)], rope_tables, ret_norm[layer])
        y_c = moba_mixer(proj, [_COL_MOBA + 4 * i for i in range(3)], rope_tables)
        y_d = dilated_mixer(proj, [_COL_DIL + 4 * i for i in range(3)], rope_tables)
        ys = [y.reshape(t, w) for y in (y_a, y_b, y_c, y_d)]
        h = gated_merge(h, proj2, ys, w_branch[layer].astype(BF16), w_out[layer].astype(BF16))
        kv = norm_matmul(mem2, ones, w_xkv[layer].astype(BF16), normalize=False, out_dtype=BF16)
        h = cross_attention(h.reshape(bsz, seq, d), kv.reshape(bsz, -1, 2 * d), x_norm[layer],
                            w_xq[layer].astype(BF16), w_xo[layer].astype(BF16)).reshape(t, d)
        s2, e2, thr, e1 = peer_router(h, ffn_norm[layer], w_pq[layer], peer_sub_keys[layer])
        h = peer_experts(h, ffn_norm[layer], peer_u[layer].astype(BF16),
                         transpose_cast(peer_v, layer, BF16, _ROWS_PER_STEP * PEER_NKEYS), s2, e2, thr, e1)

    return final_norm(h, final_norm_gain).reshape(bsz, seq, d)
```

```python
import functools
import math

import jax
import jax.numpy as jnp
from jax import lax
from jax.experimental import pallas as pl
from jax.experimental.pallas import tpu as pltpu

F32 = jnp.float32
BF16 = jnp.bfloat16
HI = lax.Precision.HIGHEST

NORM_EPS = 1e-6
NEG_INF = -1e30
ROPE_THETA = 10000.0
HEAD_DIM = 64
N_HEADS = 8
MIX_WIDTH = N_HEADS * HEAD_DIM
SSD_GROUPS = 2
SSD_STATE = 64
SSD_CONV = 4
CHUNK = 128
ATT_BLOCK = 256
MOBA_TOPK = 3
DIL_PATTERNS = ((128, 1), (512, 4), (2048, 16))
N_BRANCHES = 4
X_HEADS = 4
PEER_HEADS = 8
PEER_NKEYS = 128
PEER_TOPK = 16
LANES = 128
VMEM_LIMIT = 56 * 1024 * 1024


def _cparams(*sem, flags=None):
    return pltpu.CompilerParams(dimension_semantics=sem, vmem_limit_bytes=VMEM_LIMIT, flags=flags)


def _dot(a, b):
    return jnp.dot(a.astype(BF16), b.astype(BF16), preferred_element_type=F32)


def _dot_nt(a, b, precision=None):
    return lax.dot_general(a, b, (((1,), (1,)), ((), ())), precision=precision,
                           preferred_element_type=F32)


def _silu(x):
    return x * (1.0 / (1.0 + jnp.exp(-x)))


def _norm_matmul_body(x_ref, g_ref, w_ref, o_ref, xn_ref, *, normalize):
    @pl.when(pl.program_id(1) == 0)
    def _():
        x = x_ref[...]
        if normalize:
            ms = jnp.mean(x * x, axis=-1, keepdims=True)
            x = x * lax.rsqrt(ms + NORM_EPS) * g_ref[...]
        xn_ref[...] = x.astype(xn_ref.dtype)

    o_ref[...] = jnp.dot(xn_ref[...], w_ref[...], preferred_element_type=F32).astype(o_ref.dtype)


def norm_matmul(x, gain, w, *, normalize=True, tm=512, tn=512, out_dtype=F32):
    t, k = x.shape
    n = w.shape[1]
    tm, tn = min(tm, t), min(tn, n)
    assert t % tm == 0 and n % tn == 0
    return pl.pallas_call(
        functools.partial(_norm_matmul_body, normalize=normalize),
        grid=(t // tm, n // tn),
        in_specs=[pl.BlockSpec((tm, k), lambda i, j: (i, 0)),
                  pl.BlockSpec((1, k), lambda i, j: (0, 0)),
                  pl.BlockSpec((k, tn), lambda i, j: (0, j))],
        out_specs=pl.BlockSpec((tm, tn), lambda i, j: (i, j)),
        out_shape=jax.ShapeDtypeStruct((t, n), out_dtype),
        scratch_shapes=[pltpu.VMEM((tm, k), BF16)],
        compiler_params=_cparams("parallel", "arbitrary"),
        name="norm_matmul",
    )(x, gain.reshape(1, k), w)


def _ssd_body(x_ref, b_ref, c_ref, z_ref, dtraw_ref, convw_ref, convb_ref, dtbias_ref, alog_ref,
              dskip_ref, gain_ref, o_ref, state_ref, prev_ref):
    seq = x_ref.shape[1]
    nchunks = seq // CHUNK
    rep = N_HEADS // SSD_GROUPS
    gn = SSD_GROUPS * SSD_STATE
    state_ref[...] = jnp.zeros_like(state_ref)
    prev_ref[...] = jnp.zeros_like(prev_ref)
    row = lax.broadcasted_iota(jnp.int32, (CHUNK, CHUNK), 0)
    col = lax.broadcasted_iota(jnp.int32, (CHUNK, CHUNK), 1)
    tril = row >= col
    tril_f = tril.astype(F32)
    conv_row = lax.broadcasted_iota(jnp.int32, (CHUNK, MIX_WIDTH + 2 * gn), 0)
    neg_a = -jnp.exp(alog_ref[...])

    def chunk(c, carry):
        r0 = pl.multiple_of(c * CHUNK, CHUNK)
        rows = pl.ds(r0, CHUNK)
        raw = jnp.concatenate([x_ref[0, rows, :], b_ref[0, rows, :], c_ref[0, rows, :]], axis=1).astype(F32)
        prev = prev_ref[...]
        conv = raw * convw_ref[SSD_CONV - 1:SSD_CONV, :] + convb_ref[...]
        for k in range(1, SSD_CONV):
            shifted = jnp.where(conv_row < k, pltpu.roll(prev, k, axis=0), pltpu.roll(raw, k, axis=0))
            conv = conv + shifted * convw_ref[SSD_CONV - 1 - k:SSD_CONV - k, :]
        prev_ref[...] = raw
        xbc = _silu(conv)
        xs, cm = xbc[:, :MIX_WIDTH], xbc[:, MIX_WIDTH + gn:]
        bm_t = xbc[:, MIX_WIDTH:MIX_WIDTH + gn].T
        pre = dtraw_ref[0, rows, :] + dtbias_ref[...]
        dt = jnp.maximum(pre, 0.0) + jnp.log1p(jnp.exp(-jnp.abs(pre)))
        acum = jnp.dot(tril_f, dt * neg_a, precision=HI, preferred_element_type=F32)
        acum_t = acum.T
        ys = []
        for h in range(N_HEADS):
            g = h // rep
            hs = slice(h * HEAD_DIM, (h + 1) * HEAD_DIM)
            gs = slice(g * SSD_STATE, (g + 1) * SSD_STATE)
            a_col = acum[:, h:h + 1]
            a_row = acum_t[h:h + 1, :]
            a_last = acum[CHUNK - 1:CHUNK, h:h + 1]
            decay = jnp.exp(jnp.where(tril, a_col - a_row, -jnp.inf))
            cc = cm[:, gs]
            bt = bm_t[gs, :]
            x_h = xs[:, hs]
            xdt = x_h * dt[:, h:h + 1]
            scores = _dot(cc, bt) * decay
            prev = state_ref[h]
            y = _dot(scores, xdt) + _dot(cc, prev) * jnp.exp(a_col)
            state_ref[h] = prev * jnp.exp(a_last) + _dot(bt * jnp.exp(a_last - a_row), xdt)
            ys.append(y + x_h * dskip_ref[0:1, hs])
        y = jnp.concatenate(ys, axis=-1)
        y = y * _silu(z_ref[0, rows, :].astype(F32))
        ms = jnp.mean(y * y, axis=-1, keepdims=True)
        o_ref[0, rows, :] = (y * lax.rsqrt(ms + NORM_EPS) * gain_ref[...]).astype(o_ref.dtype)
        return carry

    lax.fori_loop(0, nchunks, chunk, 0)


def ssd_mixer(proj, x_col, z_col, dt_raw, conv_w, conv_b, dt_bias, a_log, d_skip, norm_gain):
    bsz, seq, _ = proj.shape
    width = MIX_WIDTH
    gn = SSD_GROUPS * SSD_STATE
    conv_dim = width + 2 * gn
    lanes8 = lambda v: jnp.pad(v, (0, LANES - N_HEADS)).reshape(1, LANES)
    wide = lambda blk: pl.BlockSpec((1, seq, width), lambda b: (b, 0, blk))
    narrow = lambda blk: pl.BlockSpec((1, seq, LANES), lambda b: (b, 0, blk))
    const = lambda shp: pl.BlockSpec(shp, lambda b: (0,) * len(shp))
    return pl.pallas_call(
        _ssd_body,
        grid=(bsz,),
        in_specs=[wide(x_col // 4), narrow(x_col + 4), narrow(x_col + 5), wide(z_col // 4), narrow(0),
                  const((SSD_CONV, conv_dim)), const((1, conv_dim)), const((1, LANES)), const((1, LANES)),
                  const((1, width)), const((1, width))],
        out_specs=pl.BlockSpec((1, seq, width), lambda b: (b, 0, 0)),
        out_shape=jax.ShapeDtypeStruct((bsz, seq, width), BF16),
        scratch_shapes=[pltpu.VMEM((N_HEADS, SSD_STATE, HEAD_DIM), F32), pltpu.VMEM((CHUNK, conv_dim), F32)],
        compiler_params=_cparams("parallel"),
        name="ssd_mixer",
    )(proj, proj, proj, proj, dt_raw, conv_w, conv_b.reshape(1, conv_dim), lanes8(dt_bias), lanes8(a_log),
      jnp.repeat(d_skip, HEAD_DIM).reshape(1, width), norm_gain.reshape(1, width))


def _rope_tables(seq):
    half = HEAD_DIM // 2
    inv_freq = ROPE_THETA ** (-jnp.arange(half, dtype=F32) / half)
    ang = jnp.arange(seq, dtype=F32)[:, None] * inv_freq[None, :]
    sin = jnp.sin(ang)
    cos = jnp.tile(jnp.cos(ang), (1, LANES // half))
    sin_signed = jnp.tile(jnp.concatenate([-sin, sin], axis=1), (1, LANES // HEAD_DIM))
    return cos, sin_signed


def _rope_lanes(x, cos, sin_signed):
    width = x.shape[1]
    half = HEAD_DIM // 2
    lane = lax.broadcasted_iota(jnp.int32, x.shape, 1)
    partner = jnp.where(lane % HEAD_DIM < half, pltpu.roll(x, width - half, axis=1), pltpu.roll(x, half, axis=1))
    reps = width // LANES
    if reps > 1:
        cos = jnp.concatenate([cos] * reps, axis=1)
        sin_signed = jnp.concatenate([sin_signed] * reps, axis=1)
    return x * cos + partner * sin_signed


def _ret_body(q_ref, k_ref, v_ref, g_ref, cos_ref, sin_ref, dmat_ref, zeta_ref, xi_ref, cdec_ref, gain_ref,
              o_ref, state_ref):
    seq = q_ref.shape[1]
    nchunks = seq // CHUNK
    state_ref[...] = jnp.zeros_like(state_ref)

    def chunk(c, carry):
        r0 = pl.multiple_of(c * CHUNK, CHUNK)
        rows = pl.ds(r0, CHUNK)
        cos, sin = cos_ref[rows, :], sin_ref[rows, :]
        q = _rope_lanes(q_ref[0, rows, :].astype(F32), cos, sin).astype(BF16)
        k = _rope_lanes(k_ref[0, rows, :].astype(F32), cos, sin) * HEAD_DIM ** -0.5
        k_t = k.T
        ys = []
        for h in range(N_HEADS):
            hs = slice(h * HEAD_DIM, (h + 1) * HEAD_DIM)
            qc = q[:, hs]
            kt = k_t[hs, :]
            vc = v_ref[0, rows, hs]
            inner = _dot(qc, kt) * dmat_ref[h]
            prev = state_ref[h]
            y = _dot(inner, vc) + _dot(qc, prev) * xi_ref[h]
            state_ref[h] = prev * cdec_ref[h] + _dot(kt * zeta_ref[h], vc)
            mu = jnp.mean(y, axis=-1, keepdims=True)
            yc = y - mu
            var = jnp.mean(yc * yc, axis=-1, keepdims=True)
            ys.append(yc * lax.rsqrt(var + NORM_EPS))
        y = jnp.concatenate(ys, axis=-1) * gain_ref[...]
        o_ref[0, rows, :] = (_silu(g_ref[0, rows, :].astype(F32)) * y).astype(o_ref.dtype)
        return carry

    lax.fori_loop(0, nchunks, chunk, 0)


def retention_mixer(proj, cols, rope_tables, norm_gain):
    bsz, seq, _ = proj.shape
    width = MIX_WIDTH
    log_gamma = jnp.log1p(-jnp.exp2(-5.0 - jnp.arange(N_HEADS, dtype=F32)))
    idx = jnp.arange(CHUNK, dtype=F32)
    rel = idx[:, None] - idx[None, :]
    dmat = jnp.where(rel >= 0, jnp.exp(jnp.maximum(rel, 0.0)[None] * log_gamma[:, None, None]), 0.0)
    zeta = jnp.exp((CHUNK - 1.0 - idx)[None, None, :] * log_gamma[:, None, None])
    xi = jnp.exp((idx + 1.0)[None, :, None] * log_gamma[:, None, None])
    cdec = jnp.exp(CHUNK * log_gamma)[:, None, None]
    col = lambda blk: pl.BlockSpec((1, seq, width), lambda b: (b, 0, blk))
    const = lambda shp: pl.BlockSpec(shp, lambda b: (0,) * len(shp))
    return pl.pallas_call(
        _ret_body,
        grid=(bsz,),
        in_specs=[col(c) for c in cols] + [const((seq, LANES)), const((seq, LANES)),
                  const((N_HEADS, CHUNK, CHUNK)), const((N_HEADS, 1, CHUNK)),
                  const((N_HEADS, CHUNK, 1)), const((N_HEADS, 1, 1)), const((1, width))],
        out_specs=pl.BlockSpec((1, seq, width), lambda b: (b, 0, 0)),
        out_shape=jax.ShapeDtypeStruct((bsz, seq, width), BF16),
        scratch_shapes=[pltpu.VMEM((N_HEADS, HEAD_DIM, HEAD_DIM), F32)],
        compiler_params=_cparams("parallel"),
        name="retention_mixer",
    )(proj, proj, proj, proj, *rope_tables, dmat, zeta, xi, cdec, norm_gain.reshape(1, width))


def _softmax_pv(pieces, v_ones, hs):
    m = jnp.max(functools.reduce(jnp.maximum, pieces), axis=-1, keepdims=True)
    p = jnp.concatenate([jnp.exp(sp - m) for sp in pieces], axis=1).astype(BF16)
    o = jnp.dot(p, v_ones, preferred_element_type=F32)
    return (o / pltpu.roll(o, HEAD_DIM, axis=1))[:, hs]


def _values_with_ones(v_ref, hs, vo_ref):
    v = v_ref[0]
    lane = lax.broadcasted_iota(jnp.int32, v.shape, 1)
    vo_ref[...] = jnp.where((lane >= hs.start) & (lane < hs.stop), v, jnp.ones_like(v))


def _split_bf16(x):
    hi = x.astype(BF16)
    return hi, (x - hi.astype(F32)).astype(BF16)


def _rope_qk(q_ref, k_ref, cos_ref, sin_ref, qs_ref, kt_ref):
    cos, sin = cos_ref[...], sin_ref[...]
    qs_ref[...] = _rope_lanes(q_ref[0].astype(F32), cos, sin)
    kt_ref[...] = _rope_lanes(k_ref[0].astype(F32), cos, sin).T.astype(kt_ref.dtype)


def _moba_body(q_ref, k_ref, v_ref, cos_ref, sin_ref, o_ref, qs_ref, kt_ref, vo_ref):
    seq = q_ref.shape[1]
    nb = seq // ATT_BLOCK
    halves = ATT_BLOCK // LANES
    scale = HEAD_DIM ** -0.5
    heads = q_ref.shape[2] // HEAD_DIM
    n_sel = min(MOBA_TOPK, nb - 1)
    row = lax.broadcasted_iota(jnp.int32, (ATT_BLOCK, LANES), 0)
    col = lax.broadcasted_iota(jnp.int32, (ATT_BLOCK, LANES), 1)
    tril = [row >= col + hf * LANES for hf in range(halves)]
    _rope_qk(q_ref, k_ref, cos_ref, sin_ref, qs_ref, kt_ref)

    for h in range(heads):
        hs = slice(h * HEAD_DIM, (h + 1) * HEAD_DIM)
        kt = kt_ref[hs, :]
        kt_bf = kt.astype(BF16)
        _values_with_ones(v_ref, hs, vo_ref)
        reps = [jnp.broadcast_to(jnp.mean(kt[:, n * ATT_BLOCK:(n + 1) * ATT_BLOCK], axis=1, keepdims=True),
                                 (HEAD_DIM, LANES)) for n in range(nb - 1)]
        kr_hi, kr_lo = _split_bf16(jnp.concatenate(reps, axis=1))
        kr4 = jnp.concatenate([kr_hi, kr_lo, kr_hi, kr_lo], axis=0)

        for i in range(nb):
            rows = slice(i * ATT_BLOCK, (i + 1) * ATT_BLOCK)
            n_keys = (i + 1) * ATT_BLOCK
            q = qs_ref[rows, hs]
            sel = None
            if i > n_sel:
                q_hi, q_lo = _split_bf16(q)
                q4 = jnp.concatenate([q_hi, q_hi, q_lo, q_lo], axis=1)
                g = jnp.dot(q4, kr4[:, :i * LANES], preferred_element_type=F32)
                gs = [g[:, n * LANES:(n + 1) * LANES] for n in range(i)]
                cnt = [jnp.full((ATT_BLOCK, LANES), float(i - 1 - a), F32) for a in range(i)]
                for a in range(i):
                    for b in range(a + 1, i):
                        a_wins = jnp.where(gs[a] >= gs[b], 1.0, 0.0)
                        cnt[b] = cnt[b] + a_wins
                        cnt[a] = cnt[a] - a_wins
                sel = [c < n_sel for c in cnt]
            s = jnp.dot((q * scale).astype(BF16), kt_bf[:, :n_keys], preferred_element_type=F32)
            pieces = []
            for k in range(halves * (i + 1)):
                n, hf = divmod(k, halves)
                sp = s[:, k * LANES:(k + 1) * LANES]
                if n == i:
                    sp = jnp.where(tril[hf], sp, NEG_INF)
                elif sel is not None:
                    sp = jnp.where(sel[n], sp, NEG_INF)
                pieces.append(sp)
            o_ref[0, rows, hs] = _softmax_pv(pieces, vo_ref[:n_keys, :], hs).astype(o_ref.dtype)


def _dil_body(q_ref, k_ref, v_ref, cos_ref, sin_ref, lm_ref, o_ref, qs_ref, kt_ref, vo_ref):
    seq = q_ref.shape[1]
    nb = seq // ATT_BLOCK
    halves = ATT_BLOCK // LANES
    scale = HEAD_DIM ** -0.5
    heads = q_ref.shape[2] // HEAD_DIM
    _rope_qk(q_ref, k_ref, cos_ref, sin_ref, qs_ref, kt_ref)

    for h in range(heads):
        hs = slice(h * HEAD_DIM, (h + 1) * HEAD_DIM)
        kt_bf = kt_ref[hs, :]
        _values_with_ones(v_ref, hs, vo_ref)
        for i in range(nb):
            rows = slice(i * ATT_BLOCK, (i + 1) * ATT_BLOCK)
            n_keys = (i + 1) * ATT_BLOCK
            qs = (qs_ref[rows, hs] * scale).astype(BF16)
            s = jnp.dot(qs, kt_bf[:, :n_keys], preferred_element_type=F32)
            pieces = []
            for k in range(halves * (i + 1)):
                n, hf = divmod(k, halves)
                pieces.append(s[:, k * LANES:(k + 1) * LANES] + lm_ref[i - n, :, hf * LANES:(hf + 1) * LANES])
            o_ref[0, rows, hs] = _softmax_pv(pieces, vo_ref[:n_keys, :], hs).astype(o_ref.dtype)


def _dilation_log_multiplicity(seq):
    nb = seq // ATT_BLOCK
    r = jnp.arange(ATT_BLOCK)
    d = (jnp.arange(nb)[:, None, None] * ATT_BLOCK + r[None, :, None] - r[None, None, :])
    mult = jnp.zeros(d.shape, F32)
    for window, dil in DIL_PATTERNS:
        mult = mult + ((d >= 0) & (d <= window) & (d % dil == 0)).astype(F32)
    return jnp.where(mult > 0, jnp.log(jnp.maximum(mult, 1.0)), NEG_INF)


def _attention_mixer(body, name, kt_dtype, proj, cols, rope_tables, extra=()):
    bsz, seq, _ = proj.shape
    pair = lambda lane_blk: pl.BlockSpec((1, seq, LANES), lambda b, p: (b, 0, lane_blk + p))
    const = lambda shp: pl.BlockSpec(shp, lambda b, p: (0,) * len(shp))
    return pl.pallas_call(
        body,
        grid=(bsz, MIX_WIDTH // LANES),
        in_specs=[pair(c) for c in cols] + [const((seq, LANES)), const((seq, LANES))]
                 + [const(e.shape) for e in extra],
        out_specs=pl.BlockSpec((1, seq, LANES), lambda b, p: (b, 0, p)),
        out_shape=jax.ShapeDtypeStruct((bsz, seq, MIX_WIDTH), BF16),
        scratch_shapes=[pltpu.VMEM((seq, LANES), F32), pltpu.VMEM((LANES, seq), kt_dtype),
                        pltpu.VMEM((seq, LANES), BF16)],
        compiler_params=_cparams("parallel", "parallel"),
        name=name,
    )(proj, proj, proj, *rope_tables, *extra)


def moba_mixer(proj, cols, rope_tables):
    return _attention_mixer(_moba_body, "moba_mixer", F32, proj, cols, rope_tables)


def dilated_mixer(proj, cols, rope_tables):
    seq = proj.shape[1]
    return _attention_mixer(_dil_body, "dilated_mixer", BF16, proj, cols, rope_tables,
                            extra=(_dilation_log_multiplicity(seq),))


def _merge_body(h_ref, gates_ref, y0_ref, y1_ref, y2_ref, y3_ref, wb_ref, wo_ref, o_ref):
    d = h_ref.shape[1]
    merged = jnp.zeros(h_ref.shape, F32)
    for i, y_ref in enumerate((y0_ref, y1_ref, y2_ref, y3_ref)):
        gate = 1.0 / (1.0 + jnp.exp(-gates_ref[:, i * d:(i + 1) * d].astype(F32)))
        merged = merged + gate * _dot(y_ref[...], wb_ref[i])
    o_ref[...] = h_ref[...] + _dot(merged, wo_ref[...])


def gated_merge(h, proj, ys, w_branch, w_out, tm=512):
    t, d = h.shape
    width = ys[0].shape[1]
    tm = min(tm, t)
    rowblk = lambda w: pl.BlockSpec((tm, w), lambda i: (i, 0))
    return pl.pallas_call(
        _merge_body,
        grid=(t // tm,),
        in_specs=[rowblk(d), rowblk(N_BRANCHES * d)] + [rowblk(width)] * N_BRANCHES
                 + [pl.BlockSpec((N_BRANCHES, width, d), lambda i: (0, 0, 0)),
                    pl.BlockSpec((d, d), lambda i: (0, 0))],
        out_specs=rowblk(d),
        out_shape=jax.ShapeDtypeStruct((t, d), F32),
        compiler_params=_cparams("parallel"),
        name="gated_merge",
    )(h, proj, *ys, w_branch, w_out)


def _xattn_body(h_ref, gain_ref, wq_ref, k_ref, v_ref, wo_ref, o_ref):
    h = h_ref[0]
    d = h.shape[1]
    dh = d // X_HEADS
    ms = jnp.mean(h * h, axis=-1, keepdims=True)
    hn = h * lax.rsqrt(ms + NORM_EPS) * gain_ref[...]
    q = _dot(hn, wq_ref[...]) * dh ** -0.5
    outs = []
    for a in range(X_HEADS):
        cs = slice(a * dh, (a + 1) * dh)
        s = _dot_nt(q[:, cs].astype(BF16), k_ref[0, :, cs])
        m = jnp.max(s, axis=-1, keepdims=True)
        e = jnp.exp(s - m)
        p = e / jnp.sum(e, axis=-1, keepdims=True)
        outs.append(_dot(p, v_ref[0, :, cs]))
    o = jnp.concatenate(outs, axis=-1)
    o_ref[0] = h + _dot(o, wo_ref[...])


def cross_attention(h, kv, gain, w_q, w_o, tq=512):
    bsz, seq, d = h.shape
    mlen = kv.shape[1]
    tq = min(tq, seq)
    return pl.pallas_call(
        _xattn_body,
        grid=(bsz, seq // tq),
        in_specs=[pl.BlockSpec((1, tq, d), lambda b, i: (b, i, 0)),
                  pl.BlockSpec((1, d), lambda b, i: (0, 0)),
                  pl.BlockSpec((d, d), lambda b, i: (0, 0)),
                  pl.BlockSpec((1, mlen, d), lambda b, i: (b, 0, 0)),
                  pl.BlockSpec((1, mlen, d), lambda b, i: (b, 0, 1)),
                  pl.BlockSpec((d, d), lambda b, i: (0, 0))],
        out_specs=pl.BlockSpec((1, tq, d), lambda b, i: (b, i, 0)),
        out_shape=jax.ShapeDtypeStruct((bsz, seq, d), F32),
        compiler_params=_cparams("parallel", "parallel"),
        name="cross_attention",
    )(h, gain.reshape(1, d), w_q, kv, kv, w_o)


_NO_RANK = 127.0


def _top_rows(vals, count, with_ranks=False):
    tops = []
    rank = jnp.full(vals.shape, _NO_RANK, F32)
    for r in range(count):
        m = jnp.max(vals, axis=0, keepdims=True)
        tops.append(m)
        hit = vals == m
        if with_ranks:
            rank = jnp.where(hit, float(r), rank)
        vals = jnp.where(hit, -jnp.inf, vals)
    return (tops, rank) if with_ranks else tops


_N_RANKS = PEER_TOPK + 1
_RANK_ROWS = 24


_N_CAND = sum(_N_RANKS // (i + 1) for i in range(_N_RANKS))
_CAND_ROWS = -(-_N_CAND // 8) * 8


def _pair_candidates(a1, a2_ref, cand_ref):
    cand_ref[_N_CAND // 8 * 8:, :] = jnp.full((_CAND_ROWS - _N_CAND // 8 * 8, cand_ref.shape[1]), -jnp.inf, F32)
    row = 0
    for i in range(_N_RANKS):
        n_j = _N_RANKS // (i + 1)
        cand_ref[row:row + n_j, :] = a1[i] + a2_ref[0:n_j, :]
        row += n_j
    return cand_ref[...]


def _router_body(h_ref, gain_ref, wq_hi_ref, wq_lo_ref, keys_ref, rank2_ref, e2_ref, cut_ref, e1_ref,
                 a2_ref, cand_ref):
    h = h_ref[...]
    ms = jnp.mean(h * h, axis=-1, keepdims=True)
    x_hi, x_lo = _split_bf16(h * lax.rsqrt(ms + NORM_EPS) * gain_ref[...])
    q = (jnp.dot(x_hi, wq_hi_ref[...], preferred_element_type=F32)
         + jnp.dot(x_hi, wq_lo_ref[...], preferred_element_type=F32)
         + jnp.dot(x_lo, wq_hi_ref[...], preferred_element_type=F32))
    dk = keys_ref.shape[-1] // 2
    for hd in range(PEER_HEADS):
        s = []
        for half in range(2):
            c0 = (hd * 2 + half) * dk
            q_hi, q_lo = _split_bf16(q[:, c0:c0 + dk])
            keys2 = keys_ref[hd, half]
            s.append(_dot_nt(keys2, jnp.concatenate([q_hi, q_hi], axis=1))
                     + _dot_nt(keys2, jnp.concatenate([q_lo, q_lo], axis=1)))
        s1, s2 = s
        a1 = _top_rows(s1, _N_RANKS)
        a2, rank2 = _top_rows(s2, _N_RANKS, with_ranks=True)
        for r in range(_N_RANKS):
            a2_ref[r:r + 1, :] = a2[r]
        cand = _pair_candidates(a1, a2_ref, cand_ref)
        top = _top_rows(cand, _N_RANKS)
        tau = 0.5 * (top[PEER_TOPK - 1] + top[PEER_TOPK])
        z = jnp.sum(jnp.where(cand > tau, jnp.exp(cand - (a1[0] + a2[0])), 0.0), axis=0, keepdims=True)
        thr = tau - s1
        cut = jnp.zeros(s1.shape, F32)
        for r in range(PEER_TOPK):
            cut = jnp.where(a2[r] > thr, float(r + 1), cut)
        rank2_ref[hd] = rank2.astype(rank2_ref.dtype)
        e2_ref[hd] = jnp.exp(s2 - a2[0]).astype(e2_ref.dtype)
        cut_ref[hd] = cut
        e1_ref[hd] = jnp.exp(s1 - a1[0]) / z


def peer_router(h, gain, w_q, sub_keys, tt=256):
    t, d = h.shape
    tt = min(tt, t)
    nk = sub_keys.shape[2]
    wq_hi, wq_lo = _split_bf16(w_q)
    keys2 = jnp.concatenate(_split_bf16(sub_keys), axis=-1)
    out = [jax.ShapeDtypeStruct((PEER_HEADS, nk, t), dt) for dt in (BF16, BF16, F32, F32)]
    ospec = pl.BlockSpec((PEER_HEADS, nk, tt), lambda i: (0, 0, i))
    return pl.pallas_call(
        _router_body,
        grid=(t // tt,),
        in_specs=[pl.BlockSpec((tt, d), lambda i: (i, 0)),
                  pl.BlockSpec((1, d), lambda i: (0, 0)),
                  pl.BlockSpec(w_q.shape, lambda i: (0, 0)),
                  pl.BlockSpec(w_q.shape, lambda i: (0, 0)),
                  pl.BlockSpec(keys2.shape, lambda i: (0, 0, 0, 0))],
        out_specs=[ospec] * 4,
        out_shape=out,
        scratch_shapes=[pltpu.VMEM((_RANK_ROWS, tt), F32), pltpu.VMEM((_CAND_ROWS, tt), F32)],
        compiler_params=_cparams("parallel"),
        name="peer_router",
    )(h, gain.reshape(1, d), wq_hi, wq_lo, keys2)


def _gelu(x):
    return 0.5 * x * (1.0 + lax.erf(x * (2.0 ** -0.5)))


_ROWS_PER_STEP = 8
_J_CHUNK = 32
_ROW_GROUP = 4
_TOK_COLS = 256
_MM_SPLIT = 4
_BF16_ROWS = 16


def _experts_body(h_ref, gain_ref, u0_ref, un_ref, vt_ref, rank2_ref, e2_ref, cut_ref, e1_ref, o_ref,
                  xn_ref, acc_ref, act_a_ref, act_b_ref, p_a_ref, p_b_ref, cutb_ref, e1b_ref):
    step = pl.program_id(1)
    n_tiles = pl.num_programs(1) - 1
    nk = rank2_ref.shape[1]
    tt = h_ref.shape[0]
    n_cols = tt // _TOK_COLS
    act_refs = (act_a_ref, act_b_ref)
    p_refs = (p_a_ref, p_b_ref)

    @pl.when(step == 0)
    def _():
        h = h_ref[...]
        ms = jnp.mean(h * h, axis=-1, keepdims=True)
        xn_ref[...] = (h * lax.rsqrt(ms + NORM_EPS) * gain_ref[...]).astype(BF16)
        acc_ref[...] = jnp.zeros_like(acc_ref)
        p_refs[1][...] = jnp.zeros(p_refs[1].shape, BF16)
        act_refs[0][...] = _dot_nt(u0_ref[...], xn_ref[...])

    def main_block(act_cur, act_nxt, p_cur, p_prev):
        d_rows = acc_ref.shape[0] // _MM_SPLIT
        e_rows = un_ref.shape[0] // _MM_SPLIT

        def v_piece(k, c):
            rs, cs = slice(k * d_rows, (k + 1) * d_rows), slice(c * _TOK_COLS, (c + 1) * _TOK_COLS)
            acc_ref[rs, cs] += jnp.dot(vt_ref[rs, :], p_prev[:, cs], preferred_element_type=F32)

        def act_piece(k, c):
            rs, cs = slice(k * e_rows, (k + 1) * e_rows), slice(c * _TOK_COLS, (c + 1) * _TOK_COLS)
            act_nxt[rs, cs] = _dot_nt(un_ref[rs, :], xn_ref[cs, :])

        pieces = [(f, k, c) for k in range(_MM_SPLIT) for c in range(n_cols) for f in (v_piece, act_piece)]

        for hd in range(PEER_HEADS):
            cut8 = cut_ref[hd]
            e18 = e1_ref[hd]
            for r in range(_ROWS_PER_STEP):
                cutb_ref[hd, r] = jnp.broadcast_to(cut8[r:r + 1], (_BF16_ROWS, tt)).astype(BF16)
                e1b_ref[hd, r] = jnp.broadcast_to(e18[r:r + 1], (_BF16_ROWS, tt)).astype(BF16)

        reps = _J_CHUNK // _BF16_ROWS
        chunks = [(c, jc, rg) for c in range(n_cols) for jc in range(nk // _J_CHUNK)
                  for rg in range(_ROWS_PER_STEP // _ROW_GROUP)]
        per_chunk = -(-len(pieces) // len(chunks))
        for n, (c, jc, rg) in enumerate(chunks):
            for f, k, cc in pieces[n * per_chunk:(n + 1) * per_chunk]:
                f(k, cc)
            ts = slice(c * _TOK_COLS, (c + 1) * _TOK_COLS)
            js = slice(jc * _J_CHUNK, (jc + 1) * _J_CHUNK)
            gates = [jnp.zeros((_J_CHUNK, _TOK_COLS), BF16) for _ in range(_ROW_GROUP)]
            for hd in range(PEER_HEADS):
                rank2 = rank2_ref[hd, js, ts]
                e2 = e2_ref[hd, js, ts]
                for k in range(_ROW_GROUP):
                    r = rg * _ROW_GROUP + k
                    cut = jnp.concatenate([cutb_ref[hd, r, :, ts]] * reps, axis=0)
                    e1 = jnp.concatenate([e1b_ref[hd, r, :, ts]] * reps, axis=0)
                    gates[k] = gates[k] + jnp.where(rank2 < cut, e2 * e1, jnp.zeros_like(e2))
            for k in range(_ROW_GROUP):
                rows = slice((rg * _ROW_GROUP + k) * nk + jc * _J_CHUNK,
                             (rg * _ROW_GROUP + k) * nk + (jc + 1) * _J_CHUNK)
                p_cur[rows, ts] = _gelu(act_cur[rows, ts]).astype(BF16) * gates[k]

    for parity in range(2):
        @pl.when((step % 2 == parity) & (step < n_tiles))
        def _():
            main_block(act_refs[parity], act_refs[1 - parity], p_refs[parity], p_refs[1 - parity])

    last_p = p_refs[(rank2_ref.shape[1] // _ROWS_PER_STEP - 1) % 2]

    @pl.when(step == n_tiles)
    def _():
        acc = acc_ref[...] + jnp.dot(vt_ref[...], last_p[...], preferred_element_type=F32)
        o_ref[...] = h_ref[...] + acc.T


def peer_experts(h, gain, u, v_t, rank2, e2, cut, e1, tt=512):
    t, d = h.shape
    nk = rank2.shape[1]
    tt = min(tt, t)
    et = _ROWS_PER_STEP * nk
    n_tiles = nk // _ROWS_PER_STEP
    rspec = pl.BlockSpec((PEER_HEADS, nk, tt), lambda i, s: (0, 0, i))
    row_spec = pl.BlockSpec((PEER_HEADS, _ROWS_PER_STEP, tt), lambda i, s: (0, jnp.minimum(s, n_tiles - 1), i))
    return pl.pallas_call(
        _experts_body,
        grid=(t // tt, n_tiles + 1),
        in_specs=[pl.BlockSpec((tt, d), lambda i, s: (i, 0)),
                  pl.BlockSpec((1, d), lambda i, s: (0, 0)),
                  pl.BlockSpec((et, d), lambda i, s: (0, 0)),
                  pl.BlockSpec((et, d), lambda i, s: (jnp.minimum(s + 1, n_tiles - 1), 0)),
                  pl.BlockSpec((None, d, et), lambda i, s: (jnp.maximum(s - 1, 0), 0, 0)),
                  rspec, rspec, row_spec, row_spec],
        out_specs=pl.BlockSpec((tt, d), lambda i, s: (i, 0)),
        out_shape=jax.ShapeDtypeStruct((t, d), F32),
        scratch_shapes=[pltpu.VMEM((tt, d), BF16), pltpu.VMEM((d, tt), F32),
                        pltpu.VMEM((et, tt), F32), pltpu.VMEM((et, tt), F32),
                        pltpu.VMEM((et, tt), BF16), pltpu.VMEM((et, tt), BF16),
                        pltpu.VMEM((PEER_HEADS, _ROWS_PER_STEP, _BF16_ROWS, tt), BF16),
                        pltpu.VMEM((PEER_HEADS, _ROWS_PER_STEP, _BF16_ROWS, tt), BF16)],
        compiler_params=_cparams("parallel", "arbitrary"),
        name="peer_experts",
    )(h, gain.reshape(1, d), u, u, v_t, rank2, e2, cut, e1)


def _transpose_cast_body(x_ref, o_ref):
    o_ref[...] = x_ref[...].T.astype(o_ref.dtype)


def transpose_cast(x3, index, dtype, tile, blk=512):
    _, r, c = x3.shape
    per = tile // blk
    return pl.pallas_call(
        _transpose_cast_body,
        grid=(r // blk, c // blk),
        in_specs=[pl.BlockSpec((None, blk, blk), lambda i, j: (index, i, j))],
        out_specs=pl.BlockSpec((None, blk, blk), lambda i, j: (i // per, j, i % per)),
        out_shape=jax.ShapeDtypeStruct((r // tile, c, tile), dtype),
        compiler_params=_cparams("parallel", "parallel"),
        name="transpose_cast",
    )(x3)


def _final_norm_body(x_ref, g_ref, o_ref):
    x = x_ref[...]
    ms = jnp.mean(x * x, axis=-1, keepdims=True)
    o_ref[...] = x * lax.rsqrt(ms + NORM_EPS) * g_ref[...]


def final_norm(x, gain, tm=1024):
    t, d = x.shape
    tm = min(tm, t)
    return pl.pallas_call(
        _final_norm_body,
        grid=(t // tm,),
        in_specs=[pl.BlockSpec((tm, d), lambda i: (i, 0)), pl.BlockSpec((1, d), lambda i: (0, 0))],
        out_specs=pl.BlockSpec((tm, d), lambda i: (i, 0)),
        out_shape=jax.ShapeDtypeStruct((t, d), F32),
        compiler_params=_cparams("parallel"),
        name="final_norm",
    )(x, gain.reshape(1, d))


_GATE_BLOCKS = N_BRANCHES * 8
_COL_Z = _GATE_BLOCKS
_COL_RET = _COL_Z + 4
_COL_MOBA = _COL_RET + 16
_COL_DIL = _COL_MOBA + 12
_COL_XBC = _COL_DIL + 12
_PROJ_BLOCKS = 84


def _pack_w_in(w_in_l):
    ssd_conv_dim = MIX_WIDTH + 2 * SSD_GROUPS * SSD_STATE
    o_xbc = MIX_WIDTH
    o_dt = o_xbc + ssd_conv_dim
    o_rest = o_dt + N_HEADS
    o_gate = o_rest + 10 * MIX_WIDTH
    d_model = w_in_l.shape[0]
    used = _COL_XBC * LANES + ssd_conv_dim
    pad = jnp.zeros((d_model, _PROJ_BLOCKS * LANES - used), w_in_l.dtype)
    main = jnp.concatenate([w_in_l[:, o_gate:], w_in_l[:, :o_xbc], w_in_l[:, o_rest:o_gate],
                            w_in_l[:, o_xbc:o_dt], pad], axis=1).astype(BF16)
    w_dt = jnp.pad(w_in_l[:, o_dt:o_rest], ((0, 0), (0, LANES - N_HEADS))).astype(BF16)
    return main, w_dt


def kernel(x, mem, mix_norm, w_in, ssd_conv_w, ssd_conv_b, ssd_dt_bias, ssd_a_log, ssd_d, ssd_norm, ret_norm, w_branch, w_out, x_norm, w_xq, w_xkv, w_xo, ffn_norm, w_pq, peer_sub_keys, peer_u, peer_v, final_norm_gain):
    bsz, seq, d = x.shape
    depth = w_in.shape[0]
    t = bsz * seq
    w = MIX_WIDTH
    h = x.reshape(t, d)
    mem2 = mem.reshape(-1, d)
    ones = jnp.ones((d,), F32)

    rope_tables = _rope_tables(seq)

    for layer in range(depth):
        w_main, w_dt = _pack_w_in(w_in[layer])
        proj2 = norm_matmul(h, mix_norm[layer], w_main, tm=1024, tn=1536, out_dtype=BF16)
        dt_raw = norm_matmul(h, mix_norm[layer], w_dt, tm=1024).reshape(bsz, seq, LANES)
        proj = proj2.reshape(bsz, seq, -1)

        y_a = ssd_mixer(proj, _COL_XBC, _COL_Z, dt_raw, ssd_conv_w[layer], ssd_conv_b[layer],
                        ssd_dt_bias[layer], ssd_a_log[layer], ssd_d[layer], ssd_norm[layer])
        y_b = retention_mixer(proj, [_COL_RET // 4 + i for i in range(4)], rope_tables, ret_norm[layer])
        y_c = moba_mixer(proj, [_COL_MOBA + 4 * i for i in range(3)], rope_tables)
        y_d = dilated_mixer(proj, [_COL_DIL + 4 * i for i in range(3)], rope_tables)
        ys = [y.reshape(t, w) for y in (y_a, y_b, y_c, y_d)]
        h = gated_merge(h, proj2, ys, w_branch[layer].astype(BF16), w_out[layer].astype(BF16))
        kv = norm_matmul(mem2, ones, w_xkv[layer].astype(BF16), normalize=False, out_dtype=BF16)
        h = cross_attention(h.reshape(bsz, seq, d), kv.reshape(bsz, -1, 2 * d), x_norm[layer],
                            w_xq[layer].astype(BF16), w_xo[layer].astype(BF16)).reshape(t, d)
        s2, e2, thr, e1 = peer_router(h, ffn_norm[layer], w_pq[layer], peer_sub_keys[layer])
        h = peer_experts(h, ffn_norm[layer], peer_u[layer].astype(BF16),
                         transpose_cast(peer_v, layer, BF16, _ROWS_PER_STEP * PEER_NKEYS), s2, e2, thr, e1)

    return final_norm(h, final_norm_gain).reshape(bsz, seq, d)
```

```python
import functools
import math

import jax
import jax.numpy as jnp
from jax import lax
from jax.experimental import pallas as pl
from jax.experimental.pallas import tpu as pltpu

F32 = jnp.float32
BF16 = jnp.bfloat16
HI = lax.Precision.HIGHEST

NORM_EPS = 1e-6
NEG_INF = -1e30
ROPE_THETA = 10000.0
HEAD_DIM = 64
N_HEADS = 8
MIX_WIDTH = N_HEADS * HEAD_DIM
SSD_GROUPS = 2
SSD_STATE = 64
SSD_CONV = 4
CHUNK = 128
_SCAN_BATCH = 1
ATT_BLOCK = 256
MOBA_TOPK = 3
DIL_PATTERNS = ((128, 1), (512, 4), (2048, 16))
N_BRANCHES = 4
X_HEADS = 4
PEER_HEADS = 8
PEER_NKEYS = 128
PEER_TOPK = 16
LANES = 128
VMEM_LIMIT = 56 * 1024 * 1024


def _cparams(*sem, flags=None):
    return pltpu.CompilerParams(dimension_semantics=sem, vmem_limit_bytes=VMEM_LIMIT, flags=flags)


def _dot(a, b):
    return jnp.dot(a.astype(BF16), b.astype(BF16), preferred_element_type=F32)


def _dot_nt(a, b, precision=None):
    return lax.dot_general(a, b, (((1,), (1,)), ((), ())), precision=precision,
                           preferred_element_type=F32)


def _silu(x):
    return x * (1.0 / (1.0 + jnp.exp(-x)))


def _norm_matmul_body(x_ref, g_ref, w_ref, o_ref, xn_ref, *, normalize):
    @pl.when(pl.program_id(1) == 0)
    def _():
        x = x_ref[...]
        if normalize:
            ms = jnp.mean(x * x, axis=-1, keepdims=True)
            x = x * lax.rsqrt(ms + NORM_EPS) * g_ref[...]
        xn_ref[...] = x.astype(xn_ref.dtype)

    o_ref[...] = jnp.dot(xn_ref[...], w_ref[...], preferred_element_type=F32).astype(o_ref.dtype)


def norm_matmul(x, gain, w, *, normalize=True, tm=512, tn=512, out_dtype=F32):
    t, k = x.shape
    n = w.shape[1]
    tm, tn = min(tm, t), min(tn, n)
    assert t % tm == 0 and n % tn == 0
    return pl.pallas_call(
        functools.partial(_norm_matmul_body, normalize=normalize),
        grid=(t // tm, n // tn),
        in_specs=[pl.BlockSpec((tm, k), lambda i, j: (i, 0)),
                  pl.BlockSpec((1, k), lambda i, j: (0, 0)),
                  pl.BlockSpec((k, tn), lambda i, j: (0, j))],
        out_specs=pl.BlockSpec((tm, tn), lambda i, j: (i, j)),
        out_shape=jax.ShapeDtypeStruct((t, n), out_dtype),
        scratch_shapes=[pltpu.VMEM((tm, k), BF16)],
        compiler_params=_cparams("parallel", "arbitrary"),
        name="norm_matmul",
    )(x, gain.reshape(1, k), w)


def _ssd_body(x_ref, b_ref, c_ref, z_ref, dtraw_ref, convw_ref, convb_ref, dtbias_ref, alog_ref,
              dskip_ref, gain_ref, o_ref, state_ref, prev_ref):
    seq = x_ref.shape[1]
    nchunks = seq // CHUNK
    rep = N_HEADS // SSD_GROUPS
    gn = SSD_GROUPS * SSD_STATE
    state_ref[...] = jnp.zeros_like(state_ref)
    prev_ref[...] = jnp.zeros_like(prev_ref)
    row = lax.broadcasted_iota(jnp.int32, (CHUNK, CHUNK), 0)
    col = lax.broadcasted_iota(jnp.int32, (CHUNK, CHUNK), 1)
    tril = row >= col
    tril_f = tril.astype(F32)
    conv_row = lax.broadcasted_iota(jnp.int32, (CHUNK, MIX_WIDTH + 2 * gn), 0)
    neg_a = -jnp.exp(alog_ref[...])

    def chunk(c, carry):
        for bb in range(x_ref.shape[0]):
            chunk_of(bb, c)
        return carry

    def chunk_of(bb, c):
        r0 = pl.multiple_of(c * CHUNK, CHUNK)
        rows = pl.ds(r0, CHUNK)
        raw = jnp.concatenate([x_ref[bb, rows, :], b_ref[bb, rows, :], c_ref[bb, rows, :]], axis=1).astype(F32)
        prev = prev_ref[bb]
        conv = raw * convw_ref[SSD_CONV - 1:SSD_CONV, :] + convb_ref[...]
        for k in range(1, SSD_CONV):
            shifted = jnp.where(conv_row < k, pltpu.roll(prev, k, axis=0), pltpu.roll(raw, k, axis=0))
            conv = conv + shifted * convw_ref[SSD_CONV - 1 - k:SSD_CONV - k, :]
        prev_ref[bb] = raw
        xbc = _silu(conv)
        xs, cm = xbc[:, :MIX_WIDTH], xbc[:, MIX_WIDTH + gn:]
        bm_t = xbc[:, MIX_WIDTH:MIX_WIDTH + gn].T
        pre = dtraw_ref[bb, rows, :] + dtbias_ref[...]
        dt = jnp.maximum(pre, 0.0) + jnp.log1p(jnp.exp(-jnp.abs(pre)))
        acum = jnp.dot(tril_f, dt * neg_a, precision=HI, preferred_element_type=F32)
        acum_t = acum.T
        ys = []
        for h in range(N_HEADS):
            g = h // rep
            hs = slice(h * HEAD_DIM, (h + 1) * HEAD_DIM)
            gs = slice(g * SSD_STATE, (g + 1) * SSD_STATE)
            a_col = acum[:, h:h + 1]
            a_row = acum_t[h:h + 1, :]
            a_last = acum[CHUNK - 1:CHUNK, h:h + 1]
            decay = jnp.exp(jnp.where(tril, a_col - a_row, -jnp.inf))
            cc = cm[:, gs]
            bt = bm_t[gs, :]
            x_h = xs[:, hs]
            xdt = x_h * dt[:, h:h + 1]
            scores = _dot(cc, bt) * decay
            prev = state_ref[bb, h]
            y = _dot(scores, xdt) + _dot(cc, prev) * jnp.exp(a_col)
            state_ref[bb, h] = prev * jnp.exp(a_last) + _dot(bt * jnp.exp(a_last - a_row), xdt)
            ys.append(y + x_h * dskip_ref[0:1, hs])
        y = jnp.concatenate(ys, axis=-1)
        y = y * _silu(z_ref[bb, rows, :].astype(F32))
        ms = jnp.mean(y * y, axis=-1, keepdims=True)
        o_ref[bb, rows, :] = (y * lax.rsqrt(ms + NORM_EPS) * gain_ref[...]).astype(o_ref.dtype)

    lax.fori_loop(0, nchunks, chunk, 0)


def ssd_mixer(proj, x_col, z_col, dt_raw, conv_w, conv_b, dt_bias, a_log, d_skip, norm_gain):
    bsz, seq, _ = proj.shape
    width = MIX_WIDTH
    gn = SSD_GROUPS * SSD_STATE
    conv_dim = width + 2 * gn
    lanes8 = lambda v: jnp.pad(v, (0, LANES - N_HEADS)).reshape(1, LANES)
    nb = _SCAN_BATCH if bsz % _SCAN_BATCH == 0 else 1
    wide = lambda blk: pl.BlockSpec((nb, seq, width), lambda b: (b, 0, blk))
    narrow = lambda blk: pl.BlockSpec((nb, seq, LANES), lambda b: (b, 0, blk))
    const = lambda shp: pl.BlockSpec(shp, lambda b: (0,) * len(shp))
    return pl.pallas_call(
        _ssd_body,
        grid=(bsz // nb,),
        in_specs=[wide(x_col // 4), narrow(x_col + 4), narrow(x_col + 5), wide(z_col // 4), narrow(0),
                  const((SSD_CONV, conv_dim)), const((1, conv_dim)), const((1, LANES)), const((1, LANES)),
                  const((1, width)), const((1, width))],
        out_specs=pl.BlockSpec((nb, seq, width), lambda b: (b, 0, 0)),
        out_shape=jax.ShapeDtypeStruct((bsz, seq, width), BF16),
        scratch_shapes=[pltpu.VMEM((nb, N_HEADS, SSD_STATE, HEAD_DIM), F32),
                        pltpu.VMEM((nb, CHUNK, conv_dim), F32)],
        compiler_params=_cparams("parallel"),
        name="ssd_mixer",
    )(proj, proj, proj, proj, dt_raw, conv_w, conv_b.reshape(1, conv_dim), lanes8(dt_bias), lanes8(a_log),
      jnp.repeat(d_skip, HEAD_DIM).reshape(1, width), norm_gain.reshape(1, width))


def _rope_tables(seq):
    half = HEAD_DIM // 2
    inv_freq = ROPE_THETA ** (-jnp.arange(half, dtype=F32) / half)
    ang = jnp.arange(seq, dtype=F32)[:, None] * inv_freq[None, :]
    sin = jnp.sin(ang)
    cos = jnp.tile(jnp.cos(ang), (1, LANES // half))
    sin_signed = jnp.tile(jnp.concatenate([-sin, sin], axis=1), (1, LANES // HEAD_DIM))
    return cos, sin_signed


def _rope_lanes(x, cos, sin_signed):
    width = x.shape[1]
    half = HEAD_DIM // 2
    lane = lax.broadcasted_iota(jnp.int32, x.shape, 1)
    partner = jnp.where(lane % HEAD_DIM < half, pltpu.roll(x, width - half, axis=1), pltpu.roll(x, half, axis=1))
    reps = width // LANES
    if reps > 1:
        cos = jnp.concatenate([cos] * reps, axis=1)
        sin_signed = jnp.concatenate([sin_signed] * reps, axis=1)
    return x * cos + partner * sin_signed


def _ret_body(q_ref, k_ref, v_ref, g_ref, cos_ref, sin_ref, dmat_ref, zeta_ref, xi_ref, cdec_ref, gain_ref,
              o_ref, state_ref):
    seq = q_ref.shape[1]
    nchunks = seq // CHUNK
    state_ref[...] = jnp.zeros_like(state_ref)

    def chunk(c, carry):
        for bb in range(q_ref.shape[0]):
            chunk_of(bb, c)
        return carry

    def chunk_of(bb, c):
        r0 = pl.multiple_of(c * CHUNK, CHUNK)
        rows = pl.ds(r0, CHUNK)
        cos, sin = cos_ref[rows, :], sin_ref[rows, :]
        q = _rope_lanes(q_ref[bb, rows, :].astype(F32), cos, sin).astype(BF16)
        k = _rope_lanes(k_ref[bb, rows, :].astype(F32), cos, sin) * HEAD_DIM ** -0.5
        k_t = k.T
        ys = []
        for h in range(N_HEADS):
            hs = slice(h * HEAD_DIM, (h + 1) * HEAD_DIM)
            qc = q[:, hs]
            kt = k_t[hs, :]
            vc = v_ref[bb, rows, hs]
            inner = _dot(qc, kt) * dmat_ref[h]
            prev = state_ref[bb, h]
            y = _dot(inner, vc) + _dot(qc, prev) * xi_ref[h]
            state_ref[bb, h] = prev * cdec_ref[h] + _dot(kt * zeta_ref[h], vc)
            mu = jnp.mean(y, axis=-1, keepdims=True)
            yc = y - mu
            var = jnp.mean(yc * yc, axis=-1, keepdims=True)
            ys.append(yc * lax.rsqrt(var + NORM_EPS))
        y = jnp.concatenate(ys, axis=-1) * gain_ref[...]
        o_ref[bb, rows, :] = (_silu(g_ref[bb, rows, :].astype(F32)) * y).astype(o_ref.dtype)

    lax.fori_loop(0, nchunks, chunk, 0)


def retention_mixer(proj, cols, rope_tables, norm_gain):
    bsz, seq, _ = proj.shape
    width = MIX_WIDTH
    log_gamma = jnp.log1p(-jnp.exp2(-5.0 - jnp.arange(N_HEADS, dtype=F32)))
    idx = jnp.arange(CHUNK, dtype=F32)
    rel = idx[:, None] - idx[None, :]
    dmat = jnp.where(rel >= 0, jnp.exp(jnp.maximum(rel, 0.0)[None] * log_gamma[:, None, None]), 0.0)
    zeta = jnp.exp((CHUNK - 1.0 - idx)[None, None, :] * log_gamma[:, None, None])
    xi = jnp.exp((idx + 1.0)[None, :, None] * log_gamma[:, None, None])
    cdec = jnp.exp(CHUNK * log_gamma)[:, None, None]
    nb = _SCAN_BATCH if bsz % _SCAN_BATCH == 0 else 1
    col = lambda blk: pl.BlockSpec((nb, seq, width), lambda b: (b, 0, blk))
    const = lambda shp: pl.BlockSpec(shp, lambda b: (0,) * len(shp))
    return pl.pallas_call(
        _ret_body,
        grid=(bsz // nb,),
        in_specs=[col(c) for c in cols] + [const((seq, LANES)), const((seq, LANES)),
                  const((N_HEADS, CHUNK, CHUNK)), const((N_HEADS, 1, CHUNK)),
                  const((N_HEADS, CHUNK, 1)), const((N_HEADS, 1, 1)), const((1, width))],
        out_specs=pl.BlockSpec((nb, seq, width), lambda b: (b, 0, 0)),
        out_shape=jax.ShapeDtypeStruct((bsz, seq, width), BF16),
        scratch_shapes=[pltpu.VMEM((nb, N_HEADS, HEAD_DIM, HEAD_DIM), F32)],
        compiler_params=_cparams("parallel"),
        name="retention_mixer",
    )(proj, proj, proj, proj, *rope_tables, dmat, zeta, xi, cdec, norm_gain.reshape(1, width))


def _softmax_pv(pieces, v_ones, hs):
    m = jnp.max(functools.reduce(jnp.maximum, pieces), axis=-1, keepdims=True)
    p = jnp.concatenate([jnp.exp(sp - m) for sp in pieces], axis=1).astype(BF16)
    o = jnp.dot(p, v_ones, preferred_element_type=F32)
    return (o / pltpu.roll(o, HEAD_DIM, axis=1))[:, hs]


def _values_with_ones(v_ref, hs, vo_ref):
    v = v_ref[0]
    lane = lax.broadcasted_iota(jnp.int32, v.shape, 1)
    vo_ref[...] = jnp.where((lane >= hs.start) & (lane < hs.stop), v, jnp.ones_like(v))


def _split_bf16(x):
    hi = x.astype(BF16)
    return hi, (x - hi.astype(F32)).astype(BF16)


def _rope_qk(q_ref, k_ref, cos_ref, sin_ref, qs_ref, kt_ref):
    cos, sin = cos_ref[...], sin_ref[...]
    qs_ref[...] = _rope_lanes(q_ref[0].astype(F32), cos, sin)
    kt_ref[...] = _rope_lanes(k_ref[0].astype(F32), cos, sin).T.astype(kt_ref.dtype)


def _moba_body(q_ref, k_ref, v_ref, cos_ref, sin_ref, o_ref, qs_ref, kt_ref, vo_ref):
    seq = q_ref.shape[1]
    nb = seq // ATT_BLOCK
    halves = ATT_BLOCK // LANES
    scale = HEAD_DIM ** -0.5
    heads = q_ref.shape[2] // HEAD_DIM
    n_sel = min(MOBA_TOPK, nb - 1)
    row = lax.broadcasted_iota(jnp.int32, (ATT_BLOCK, LANES), 0)
    col = lax.broadcasted_iota(jnp.int32, (ATT_BLOCK, LANES), 1)
    tril = [row >= col + hf * LANES for hf in range(halves)]
    _rope_qk(q_ref, k_ref, cos_ref, sin_ref, qs_ref, kt_ref)

    for h in range(heads):
        hs = slice(h * HEAD_DIM, (h + 1) * HEAD_DIM)
        kt = kt_ref[hs, :]
        kt_bf = kt.astype(BF16)
        _values_with_ones(v_ref, hs, vo_ref)
        reps = [jnp.broadcast_to(jnp.mean(kt[:, n * ATT_BLOCK:(n + 1) * ATT_BLOCK], axis=1, keepdims=True),
                                 (HEAD_DIM, LANES)) for n in range(nb - 1)]
        kr_hi, kr_lo = _split_bf16(jnp.concatenate(reps, axis=1))
        kr4 = jnp.concatenate([kr_hi, kr_lo, kr_hi, kr_lo], axis=0)

        for i in range(nb):
            rows = slice(i * ATT_BLOCK, (i + 1) * ATT_BLOCK)
            n_keys = (i + 1) * ATT_BLOCK
            q = qs_ref[rows, hs]
            sel = None
            if i > n_sel:
                q_hi, q_lo = _split_bf16(q)
                q4 = jnp.concatenate([q_hi, q_hi, q_lo, q_lo], axis=1)
                g = jnp.dot(q4, kr4[:, :i * LANES], preferred_element_type=F32)
                gs = [g[:, n * LANES:(n + 1) * LANES] for n in range(i)]
                cnt = [jnp.full((ATT_BLOCK, LANES), float(i - 1 - a), F32) for a in range(i)]
                for a in range(i):
                    for b in range(a + 1, i):
                        a_wins = jnp.where(gs[a] >= gs[b], 1.0, 0.0)
                        cnt[b] = cnt[b] + a_wins
                        cnt[a] = cnt[a] - a_wins
                sel = [c < n_sel for c in cnt]
            s = jnp.dot((q * scale).astype(BF16), kt_bf[:, :n_keys], preferred_element_type=F32)
            pieces = []
            for k in range(halves * (i + 1)):
                n, hf = divmod(k, halves)
                sp = s[:, k * LANES:(k + 1) * LANES]
                if n == i:
                    sp = jnp.where(tril[hf], sp, NEG_INF)
                elif sel is not None:
                    sp = jnp.where(sel[n], sp, NEG_INF)
                pieces.append(sp)
            o_ref[0, rows, hs] = _softmax_pv(pieces, vo_ref[:n_keys, :], hs).astype(o_ref.dtype)


def _dil_body(q_ref, k_ref, v_ref, cos_ref, sin_ref, lm_ref, o_ref, qs_ref, kt_ref, vo_ref):
    seq = q_ref.shape[1]
    nb = seq // ATT_BLOCK
    halves = ATT_BLOCK // LANES
    scale = HEAD_DIM ** -0.5
    heads = q_ref.shape[2] // HEAD_DIM
    _rope_qk(q_ref, k_ref, cos_ref, sin_ref, qs_ref, kt_ref)

    for h in range(heads):
        hs = slice(h * HEAD_DIM, (h + 1) * HEAD_DIM)
        kt_bf = kt_ref[hs, :]
        _values_with_ones(v_ref, hs, vo_ref)
        for i in range(nb):
            rows = slice(i * ATT_BLOCK, (i + 1) * ATT_BLOCK)
            n_keys = (i + 1) * ATT_BLOCK
            qs = (qs_ref[rows, hs] * scale).astype(BF16)
            s = jnp.dot(qs, kt_bf[:, :n_keys], preferred_element_type=F32)
            pieces = []
            for k in range(halves * (i + 1)):
                n, hf = divmod(k, halves)
                pieces.append(s[:, k * LANES:(k + 1) * LANES] + lm_ref[i - n, :, hf * LANES:(hf + 1) * LANES])
            o_ref[0, rows, hs] = _softmax_pv(pieces, vo_ref[:n_keys, :], hs).astype(o_ref.dtype)


def _dilation_log_multiplicity(seq):
    nb = seq // ATT_BLOCK
    r = jnp.arange(ATT_BLOCK)
    d = (jnp.arange(nb)[:, None, None] * ATT_BLOCK + r[None, :, None] - r[None, None, :])
    mult = jnp.zeros(d.shape, F32)
    for window, dil in DIL_PATTERNS:
        mult = mult + ((d >= 0) & (d <= window) & (d % dil == 0)).astype(F32)
    return jnp.where(mult > 0, jnp.log(jnp.maximum(mult, 1.0)), NEG_INF)


def _attention_mixer(body, name, kt_dtype, proj, cols, rope_tables, extra=()):
    bsz, seq, _ = proj.shape
    pair = lambda lane_blk: pl.BlockSpec((1, seq, LANES), lambda b, p: (b, 0, lane_blk + p))
    const = lambda shp: pl.BlockSpec(shp, lambda b, p: (0,) * len(shp))
    return pl.pallas_call(
        body,
        grid=(bsz, MIX_WIDTH // LANES),
        in_specs=[pair(c) for c in cols] + [const((seq, LANES)), const((seq, LANES))]
                 + [const(e.shape) for e in extra],
        out_specs=pl.BlockSpec((1, seq, LANES), lambda b, p: (b, 0, p)),
        out_shape=jax.ShapeDtypeStruct((bsz, seq, MIX_WIDTH), BF16),
        scratch_shapes=[pltpu.VMEM((seq, LANES), F32), pltpu.VMEM((LANES, seq), kt_dtype),
                        pltpu.VMEM((seq, LANES), BF16)],
        compiler_params=_cparams("parallel", "parallel"),
        name=name,
    )(proj, proj, proj, *rope_tables, *extra)


def moba_mixer(proj, cols, rope_tables):
    return _attention_mixer(_moba_body, "moba_mixer", F32, proj, cols, rope_tables)


def dilated_mixer(proj, cols, rope_tables):
    seq = proj.shape[1]
    return _attention_mixer(_dil_body, "dilated_mixer", BF16, proj, cols, rope_tables,
                            extra=(_dilation_log_multiplicity(seq),))


def _merge_body(h_ref, gates_ref, y0_ref, y1_ref, y2_ref, y3_ref, wb_ref, wo_ref, o_ref):
    d = h_ref.shape[1]
    merged = jnp.zeros(h_ref.shape, F32)
    for i, y_ref in enumerate((y0_ref, y1_ref, y2_ref, y3_ref)):
        gate = 1.0 / (1.0 + jnp.exp(-gates_ref[:, i * d:(i + 1) * d].astype(F32)))
        merged = merged + gate * _dot(y_ref[...], wb_ref[i])
    o_ref[...] = h_ref[...] + _dot(merged, wo_ref[...])


def gated_merge(h, proj, ys, w_branch, w_out, tm=512):
    t, d = h.shape
    width = ys[0].shape[1]
    tm = min(tm, t)
    rowblk = lambda w: pl.BlockSpec((tm, w), lambda i: (i, 0))
    return pl.pallas_call(
        _merge_body,
        grid=(t // tm,),
        in_specs=[rowblk(d), rowblk(N_BRANCHES * d)] + [rowblk(width)] * N_BRANCHES
                 + [pl.BlockSpec((N_BRANCHES, width, d), lambda i: (0, 0, 0)),
                    pl.BlockSpec((d, d), lambda i: (0, 0))],
        out_specs=rowblk(d),
        out_shape=jax.ShapeDtypeStruct((t, d), F32),
        compiler_params=_cparams("parallel"),
        name="gated_merge",
    )(h, proj, *ys, w_branch, w_out)


def _xattn_body(h_ref, gain_ref, wq_ref, k_ref, v_ref, wo_ref, o_ref):
    h = h_ref[0]
    d = h.shape[1]
    dh = d // X_HEADS
    ms = jnp.mean(h * h, axis=-1, keepdims=True)
    hn = h * lax.rsqrt(ms + NORM_EPS) * gain_ref[...]
    q = _dot(hn, wq_ref[...]) * dh ** -0.5
    outs = []
    for a in range(X_HEADS):
        cs = slice(a * dh, (a + 1) * dh)
        s = _dot_nt(q[:, cs].astype(BF16), k_ref[0, :, cs])
        m = jnp.max(s, axis=-1, keepdims=True)
        e = jnp.exp(s - m)
        p = e / jnp.sum(e, axis=-1, keepdims=True)
        outs.append(_dot(p, v_ref[0, :, cs]))
    o = jnp.concatenate(outs, axis=-1)
    o_ref[0] = h + _dot(o, wo_ref[...])


def cross_attention(h, kv, gain, w_q, w_o, tq=512):
    bsz, seq, d = h.shape
    mlen = kv.shape[1]
    tq = min(tq, seq)
    return pl.pallas_call(
        _xattn_body,
        grid=(bsz, seq // tq),
        in_specs=[pl.BlockSpec((1, tq, d), lambda b, i: (b, i, 0)),
                  pl.BlockSpec((1, d), lambda b, i: (0, 0)),
                  pl.BlockSpec((d, d), lambda b, i: (0, 0)),
                  pl.BlockSpec((1, mlen, d), lambda b, i: (b, 0, 0)),
                  pl.BlockSpec((1, mlen, d), lambda b, i: (b, 0, 1)),
                  pl.BlockSpec((d, d), lambda b, i: (0, 0))],
        out_specs=pl.BlockSpec((1, tq, d), lambda b, i: (b, i, 0)),
        out_shape=jax.ShapeDtypeStruct((bsz, seq, d), F32),
        compiler_params=_cparams("parallel", "parallel"),
        name="cross_attention",
    )(h, gain.reshape(1, d), w_q, kv, kv, w_o)


_SUBLANES = 8


def _sorting_network(n):
    pairs, p = [], 1
    while p < n:
        k = p
        while k >= 1:
            for j in range(k % p, n - k, 2 * k):
                for i in range(min(k, n - j - k)):
                    if (i + j) // (2 * p) == (i + j + k) // (2 * p):
                        pairs.append((i + j, i + j + k))
            k //= 2
        p *= 2
    return pairs


def _top_rows(vals, count):
    n_tiles = vals.shape[0] // _SUBLANES
    levels = [vals[_SUBLANES * g:_SUBLANES * (g + 1)] for g in range(n_tiles)]
    size = 1 << (n_tiles - 1).bit_length()
    neg = jnp.full(levels[0].shape, -jnp.inf, F32)
    levels += [neg] * (size - n_tiles)
    for a, b in _sorting_network(size):
        levels[a], levels[b] = jnp.maximum(levels[a], levels[b]), jnp.minimum(levels[a], levels[b])
    levels = levels[:n_tiles]
    tops = []
    for r in range(count):
        m = jnp.max(levels[0], axis=0, keepdims=True)
        tops.append(m)
        left = count - r - 1
        hit = levels[0] == m
        for g in range(min(len(levels), left)):
            below = levels[g + 1] if g + 1 < len(levels) else neg
            levels[g] = jnp.where(hit, below, levels[g])
    return tops


def _count_greater(sorted_rows, x):
    count = jnp.zeros(x.shape, F32)
    for r, row in enumerate(sorted_rows):
        count = jnp.where(row > x, float(r + 1), count)
    return count


_N_RANKS = PEER_TOPK + 1
_RANK_ROWS = 24


_N_CAND = sum(_N_RANKS // (i + 1) for i in range(_N_RANKS))
_CAND_ROWS = -(-_N_CAND // 8) * 8


def _pair_candidates(a1, a2_ref, cand_ref):
    cand_ref[_N_CAND // 8 * 8:, :] = jnp.full((_CAND_ROWS - _N_CAND // 8 * 8, cand_ref.shape[1]), -jnp.inf, F32)
    row = 0
    for i in range(_N_RANKS):
        n_j = _N_RANKS // (i + 1)
        cand_ref[row:row + n_j, :] = a1[i] + a2_ref[0:n_j, :]
        row += n_j
    return cand_ref[...]


def _router_body(h_ref, gain_ref, wq_hi_ref, wq_lo_ref, keys_ref, rank2_ref, e2_ref, cut_ref, e1_ref,
                 a2_ref, cand_ref):
    h = h_ref[...]
    ms = jnp.mean(h * h, axis=-1, keepdims=True)
    x_hi, x_lo = _split_bf16(h * lax.rsqrt(ms + NORM_EPS) * gain_ref[...])
    q = (jnp.dot(x_hi, wq_hi_ref[...], preferred_element_type=F32)
         + jnp.dot(x_hi, wq_lo_ref[...], preferred_element_type=F32)
         + jnp.dot(x_lo, wq_hi_ref[...], preferred_element_type=F32))
    dk = keys_ref.shape[-1] // 2
    for hd in range(PEER_HEADS):
        s = []
        for half in range(2):
            c0 = (hd * 2 + half) * dk
            q_hi, q_lo = _split_bf16(q[:, c0:c0 + dk])
            keys2 = keys_ref[hd, half]
            s.append(_dot_nt(keys2, jnp.concatenate([q_hi, q_hi], axis=1))
                     + _dot_nt(keys2, jnp.concatenate([q_lo, q_lo], axis=1)))
        s1, s2 = s
        a1 = _top_rows(s1, _N_RANKS)
        a2 = _top_rows(s2, _N_RANKS)
        for r in range(_N_RANKS):
            a2_ref[r:r + 1, :] = a2[r]
        cand = _pair_candidates(a1, a2_ref, cand_ref)
        top = _top_rows(cand, _N_RANKS)
        tau = 0.5 * (top[PEER_TOPK - 1] + top[PEER_TOPK])
        z = jnp.sum(jnp.where(cand > tau, jnp.exp(cand - (a1[0] + a2[0])), 0.0), axis=0, keepdims=True)
        rank2_ref[hd] = _count_greater(a2, s2).astype(rank2_ref.dtype)
        e2_ref[hd] = jnp.exp(s2 - a2[0]).astype(e2_ref.dtype)
        cut_ref[hd] = _count_greater(a2[:PEER_TOPK], tau - s1)
        e1_ref[hd] = jnp.exp(s1 - a1[0]) / z


def peer_router(h, gain, w_q, sub_keys, tt=256):
    t, d = h.shape
    tt = min(tt, t)
    nk = sub_keys.shape[2]
    wq_hi, wq_lo = _split_bf16(w_q)
    keys2 = jnp.concatenate(_split_bf16(sub_keys), axis=-1)
    out = [jax.ShapeDtypeStruct((PEER_HEADS, nk, t), dt) for dt in (BF16, BF16, F32, F32)]
    ospec = pl.BlockSpec((PEER_HEADS, nk, tt), lambda i: (0, 0, i))
    return pl.pallas_call(
        _router_body,
        grid=(t // tt,),
        in_specs=[pl.BlockSpec((tt, d), lambda i: (i, 0)),
                  pl.BlockSpec((1, d), lambda i: (0, 0)),
                  pl.BlockSpec(w_q.shape, lambda i: (0, 0)),
                  pl.BlockSpec(w_q.shape, lambda i: (0, 0)),
                  pl.BlockSpec(keys2.shape, lambda i: (0, 0, 0, 0))],
        out_specs=[ospec] * 4,
        out_shape=out,
        scratch_shapes=[pltpu.VMEM((_RANK_ROWS, tt), F32), pltpu.VMEM((_CAND_ROWS, tt), F32)],
        compiler_params=_cparams("parallel"),
        name="peer_router",
    )(h, gain.reshape(1, d), wq_hi, wq_lo, keys2)


def _gelu(x):
    return 0.5 * x * (1.0 + lax.erf(x * (2.0 ** -0.5)))


_ROWS_PER_STEP = 8
_J_CHUNK = 32
_ROW_GROUP = 4
_TOK_COLS = 256
_MM_SPLIT = 4
_BF16_ROWS = 16


def _experts_body(h_ref, gain_ref, u0_ref, un_ref, vt_ref, rank2_ref, e2_ref, cut_ref, e1_ref, o_ref,
                  xn_ref, acc_ref, act_a_ref, act_b_ref, p_a_ref, p_b_ref, cutb_ref, e1b_ref):
    step = pl.program_id(1)
    n_tiles = pl.num_programs(1) - 1
    nk = rank2_ref.shape[1]
    tt = h_ref.shape[0]
    n_cols = tt // _TOK_COLS
    act_refs = (act_a_ref, act_b_ref)
    p_refs = (p_a_ref, p_b_ref)

    @pl.when(step == 0)
    def _():
        h = h_ref[...]
        ms = jnp.mean(h * h, axis=-1, keepdims=True)
        xn_ref[...] = (h * lax.rsqrt(ms + NORM_EPS) * gain_ref[...]).astype(BF16)
        acc_ref[...] = jnp.zeros_like(acc_ref)
        p_refs[1][...] = jnp.zeros(p_refs[1].shape, BF16)
        act_refs[0][...] = _dot_nt(u0_ref[...], xn_ref[...])

    def main_block(act_cur, act_nxt, p_cur, p_prev):
        d_rows = acc_ref.shape[0] // _MM_SPLIT
        e_rows = un_ref.shape[0] // _MM_SPLIT

        def v_piece(k, c):
            rs, cs = slice(k * d_rows, (k + 1) * d_rows), slice(c * _TOK_COLS, (c + 1) * _TOK_COLS)
            acc_ref[rs, cs] += jnp.dot(vt_ref[rs, :], p_prev[:, cs], preferred_element_type=F32)

        def act_piece(k, c):
            rs, cs = slice(k * e_rows, (k + 1) * e_rows), slice(c * _TOK_COLS, (c + 1) * _TOK_COLS)
            act_nxt[rs, cs] = _dot_nt(un_ref[rs, :], xn_ref[cs, :])

        pieces = [(f, k, c) for k in range(_MM_SPLIT) for c in range(n_cols) for f in (v_piece, act_piece)]

        for hd in range(PEER_HEADS):
            cut8 = cut_ref[hd]
            e18 = e1_ref[hd]
            for r in range(_ROWS_PER_STEP):
                cutb_ref[hd, r] = jnp.broadcast_to(cut8[r:r + 1], (_BF16_ROWS, tt)).astype(BF16)
                e1b_ref[hd, r] = jnp.broadcast_to(e18[r:r + 1], (_BF16_ROWS, tt)).astype(BF16)

        reps = _J_CHUNK // _BF16_ROWS
        chunks = [(c, jc, rg) for c in range(n_cols) for jc in range(nk // _J_CHUNK)
                  for rg in range(_ROWS_PER_STEP // _ROW_GROUP)]
        per_chunk = -(-len(pieces) // len(chunks))
        for n, (c, jc, rg) in enumerate(chunks):
            for f, k, cc in pieces[n * per_chunk:(n + 1) * per_chunk]:
                f(k, cc)
            ts = slice(c * _TOK_COLS, (c + 1) * _TOK_COLS)
            js = slice(jc * _J_CHUNK, (jc + 1) * _J_CHUNK)
            gates = [jnp.zeros((_J_CHUNK, _TOK_COLS), BF16) for _ in range(_ROW_GROUP)]
            for hd in range(PEER_HEADS):
                rank2 = rank2_ref[hd, js, ts]
                e2 = e2_ref[hd, js, ts]
                for k in range(_ROW_GROUP):
                    r = rg * _ROW_GROUP + k
                    cut = jnp.concatenate([cutb_ref[hd, r, :, ts]] * reps, axis=0)
                    e1 = jnp.concatenate([e1b_ref[hd, r, :, ts]] * reps, axis=0)
                    gates[k] = gates[k] + jnp.where(rank2 < cut, e2 * e1, jnp.zeros_like(e2))
            for k in range(_ROW_GROUP):
                rows = slice((rg * _ROW_GROUP + k) * nk + jc * _J_CHUNK,
                             (rg * _ROW_GROUP + k) * nk + (jc + 1) * _J_CHUNK)
                p_cur[rows, ts] = _gelu(act_cur[rows, ts]).astype(BF16) * gates[k]

    for parity in range(2):
        @pl.when((step % 2 == parity) & (step < n_tiles))
        def _():
            main_block(act_refs[parity], act_refs[1 - parity], p_refs[parity], p_refs[1 - parity])

    last_p = p_refs[(rank2_ref.shape[1] // _ROWS_PER_STEP - 1) % 2]

    @pl.when(step == n_tiles)
    def _():
        acc = acc_ref[...] + jnp.dot(vt_ref[...], last_p[...], preferred_element_type=F32)
        o_ref[...] = h_ref[...] + acc.T


def peer_experts(h, gain, u, v_t, rank2, e2, cut, e1, tt=512):
    t, d = h.shape
    nk = rank2.shape[1]
    tt = min(tt, t)
    et = _ROWS_PER_STEP * nk
    n_tiles = nk // _ROWS_PER_STEP
    rspec = pl.BlockSpec((PEER_HEADS, nk, tt), lambda i, s: (0, 0, i))
    row_spec = pl.BlockSpec((PEER_HEADS, _ROWS_PER_STEP, tt), lambda i, s: (0, jnp.minimum(s, n_tiles - 1), i))
    return pl.pallas_call(
        _experts_body,
        grid=(t // tt, n_tiles + 1),
        in_specs=[pl.BlockSpec((tt, d), lambda i, s: (i, 0)),
                  pl.BlockSpec((1, d), lambda i, s: (0, 0)),
                  pl.BlockSpec((et, d), lambda i, s: (0, 0)),
                  pl.BlockSpec((et, d), lambda i, s: (jnp.minimum(s + 1, n_tiles - 1), 0)),
                  pl.BlockSpec((None, d, et), lambda i, s: (jnp.maximum(s - 1, 0), 0, 0)),
                  rspec, rspec, row_spec, row_spec],
        out_specs=pl.BlockSpec((tt, d), lambda i, s: (i, 0)),
        out_shape=jax.ShapeDtypeStruct((t, d), F32),
        scratch_shapes=[pltpu.VMEM((tt, d), BF16), pltpu.VMEM((d, tt), F32),
                        pltpu.VMEM((et, tt), F32), pltpu.VMEM((et, tt), F32),
                        pltpu.VMEM((et, tt), BF16), pltpu.VMEM((et, tt), BF16),
                        pltpu.VMEM((PEER_HEADS, _ROWS_PER_STEP, _BF16_ROWS, tt), BF16),
                        pltpu.VMEM((PEER_HEADS, _ROWS_PER_STEP, _BF16_ROWS, tt), BF16)],
        compiler_params=_cparams("parallel", "arbitrary"),
        name="peer_experts",
    )(h, gain.reshape(1, d), u, u, v_t, rank2, e2, cut, e1)


def _transpose_cast_body(x_ref, o_ref):
    o_ref[...] = x_ref[...].T.astype(o_ref.dtype)


def transpose_cast(x3, index, dtype, tile, blk=512):
    _, r, c = x3.shape
    per = tile // blk
    return pl.pallas_call(
        _transpose_cast_body,
        grid=(r // blk, c // blk),
        in_specs=[pl.BlockSpec((None, blk, blk), lambda i, j: (index, i, j))],
        out_specs=pl.BlockSpec((None, blk, blk), lambda i, j: (i // per, j, i % per)),
        out_shape=jax.ShapeDtypeStruct((r // tile, c, tile), dtype),
        compiler_params=_cparams("parallel", "parallel"),
        name="transpose_cast",
    )(x3)


def _final_norm_body(x_ref, g_ref, o_ref):
    x = x_ref[...]
    ms = jnp.mean(x * x, axis=-1, keepdims=True)
    o_ref[...] = x * lax.rsqrt(ms + NORM_EPS) * g_ref[...]


def final_norm(x, gain, tm=1024):
    t, d = x.shape
    tm = min(tm, t)
    return pl.pallas_call(
        _final_norm_body,
        grid=(t // tm,),
        in_specs=[pl.BlockSpec((tm, d), lambda i: (i, 0)), pl.BlockSpec((1, d), lambda i: (0, 0))],
        out_specs=pl.BlockSpec((tm, d), lambda i: (i, 0)),
        out_shape=jax.ShapeDtypeStruct((t, d), F32),
        compiler_params=_cparams("parallel"),
        name="final_norm",
    )(x, gain.reshape(1, d))


_GATE_BLOCKS = N_BRANCHES * 8
_COL_Z = _GATE_BLOCKS
_COL_RET = _COL_Z + 4
_COL_MOBA = _COL_RET + 16
_COL_DIL = _COL_MOBA + 12
_COL_XBC = _COL_DIL + 12
_PROJ_BLOCKS = 84


def _pack_w_in(w_in_l):
    ssd_conv_dim = MIX_WIDTH + 2 * SSD_GROUPS * SSD_STATE
    o_xbc = MIX_WIDTH
    o_dt = o_xbc + ssd_conv_dim
    o_rest = o_dt + N_HEADS
    o_gate = o_rest + 10 * MIX_WIDTH
    d_model = w_in_l.shape[0]
    used = _COL_XBC * LANES + ssd_conv_dim
    pad = jnp.zeros((d_model, _PROJ_BLOCKS * LANES - used), w_in_l.dtype)
    main = jnp.concatenate([w_in_l[:, o_gate:], w_in_l[:, :o_xbc], w_in_l[:, o_rest:o_gate],
                            w_in_l[:, o_xbc:o_dt], pad], axis=1).astype(BF16)
    w_dt = jnp.pad(w_in_l[:, o_dt:o_rest], ((0, 0), (0, LANES - N_HEADS))).astype(BF16)
    return main, w_dt


def kernel(x, mem, mix_norm, w_in, ssd_conv_w, ssd_conv_b, ssd_dt_bias, ssd_a_log, ssd_d, ssd_norm, ret_norm, w_branch, w_out, x_norm, w_xq, w_xkv, w_xo, ffn_norm, w_pq, peer_sub_keys, peer_u, peer_v, final_norm_gain):
    bsz, seq, d = x.shape
    depth = w_in.shape[0]
    t = bsz * seq
    w = MIX_WIDTH
    h = x.reshape(t, d)
    mem2 = mem.reshape(-1, d)
    ones = jnp.ones((d,), F32)

    rope_tables = _rope_tables(seq)

    for layer in range(depth):
        w_main, w_dt = _pack_w_in(w_in[layer])
        proj2 = norm_matmul(h, mix_norm[layer], w_main, tm=1024, tn=1536, out_dtype=BF16)
        dt_raw = norm_matmul(h, mix_norm[layer], w_dt, tm=1024).reshape(bsz, seq, LANES)
        proj = proj2.reshape(bsz, seq, -1)

        y_a = ssd_mixer(proj, _COL_XBC, _COL_Z, dt_raw, ssd_conv_w[layer], ssd_conv_b[layer],
                        ssd_dt_bias[layer], ssd_a_log[layer], ssd_d[layer], ssd_norm[layer])
        y_b = retention_mixer(proj, [_COL_RET // 4 + i for i in range(4)], rope_tables, ret_norm[layer])
        y_c = moba_mixer(proj, [_COL_MOBA + 4 * i for i in range(3)], rope_tables)
        y_d = dilated_mixer(proj, [_COL_DIL + 4 * i for i in range(3)], rope_tables)
        ys = [y.reshape(t, w) for y in (y_a, y_b, y_c, y_d)]
        h = gated_merge(h, proj2, ys, w_branch[layer].astype(BF16), w_out[layer].astype(BF16))
        kv = norm_matmul(mem2, ones, w_xkv[layer].astype(BF16), normalize=False, out_dtype=BF16)
        h = cross_attention(h.reshape(bsz, seq, d), kv.reshape(bsz, -1, 2 * d), x_norm[layer],
                            w_xq[layer].astype(BF16), w_xo[layer].astype(BF16)).reshape(t, d)
        s2, e2, thr, e1 = peer_router(h, ffn_norm[layer], w_pq[layer], peer_sub_keys[layer])
        h = peer_experts(h, ffn_norm[layer], peer_u[layer].astype(BF16),
                         transpose_cast(peer_v, layer, BF16, _ROWS_PER_STEP * PEER_NKEYS), s2, e2, thr, e1)

    return final_norm(h, final_norm_gain).reshape(bsz, seq, d)
```

```python
import functools
import math

import jax
import jax.numpy as jnp
from jax import lax
from jax.experimental import pallas as pl
from jax.experimental.pallas import tpu as pltpu

F32 = jnp.float32
BF16 = jnp.bfloat16
HI = lax.Precision.HIGHEST

NORM_EPS = 1e-6
NEG_INF = -1e30
ROPE_THETA = 10000.0
HEAD_DIM = 64
N_HEADS = 8
MIX_WIDTH = N_HEADS * HEAD_DIM
SSD_GROUPS = 2
SSD_STATE = 64
SSD_CONV = 4
CHUNK = 128
_SCAN_BATCH = 1
ATT_BLOCK = 256
MOBA_TOPK = 3
DIL_PATTERNS = ((128, 1), (512, 4), (2048, 16))
N_BRANCHES = 4
X_HEADS = 4
PEER_HEADS = 8
PEER_NKEYS = 128
PEER_TOPK = 16
LANES = 128
VMEM_LIMIT = 56 * 1024 * 1024


def _cparams(*sem, flags=None):
    return pltpu.CompilerParams(dimension_semantics=sem, vmem_limit_bytes=VMEM_LIMIT, flags=flags)


def _dot(a, b):
    return jnp.dot(a.astype(BF16), b.astype(BF16), preferred_element_type=F32)


def _dot_nt(a, b, precision=None):
    return lax.dot_general(a, b, (((1,), (1,)), ((), ())), precision=precision,
                           preferred_element_type=F32)


def _silu(x):
    return x * (1.0 / (1.0 + jnp.exp(-x)))


def _norm_matmul_body(x_ref, g_ref, w_ref, o_ref, xn_ref, *, normalize):
    @pl.when(pl.program_id(1) == 0)
    def _():
        x = x_ref[...]
        if normalize:
            ms = jnp.mean(x * x, axis=-1, keepdims=True)
            x = x * lax.rsqrt(ms + NORM_EPS) * g_ref[...]
        xn_ref[...] = x.astype(xn_ref.dtype)

    o_ref[...] = jnp.dot(xn_ref[...], w_ref[...], preferred_element_type=F32).astype(o_ref.dtype)


def norm_matmul(x, gain, w, *, normalize=True, tm=512, tn=512, out_dtype=F32):
    t, k = x.shape
    n = w.shape[1]
    tm, tn = min(tm, t), min(tn, n)
    assert t % tm == 0 and n % tn == 0
    return pl.pallas_call(
        functools.partial(_norm_matmul_body, normalize=normalize),
        grid=(t // tm, n // tn),
        in_specs=[pl.BlockSpec((tm, k), lambda i, j: (i, 0)),
                  pl.BlockSpec((1, k), lambda i, j: (0, 0)),
                  pl.BlockSpec((k, tn), lambda i, j: (0, j))],
        out_specs=pl.BlockSpec((tm, tn), lambda i, j: (i, j)),
        out_shape=jax.ShapeDtypeStruct((t, n), out_dtype),
        scratch_shapes=[pltpu.VMEM((tm, k), BF16)],
        compiler_params=_cparams("parallel", "arbitrary"),
        name="norm_matmul",
    )(x, gain.reshape(1, k), w)


def _ssd_body(x_ref, b_ref, c_ref, z_ref, dtraw_ref, convw_ref, convb_ref, dtbias_ref, alog_ref,
              dskip_ref, gain_ref, o_ref, state_ref, prev_ref):
    seq = x_ref.shape[1]
    nchunks = seq // CHUNK
    rep = N_HEADS // SSD_GROUPS
    gn = SSD_GROUPS * SSD_STATE
    state_ref[...] = jnp.zeros_like(state_ref)
    prev_ref[...] = jnp.zeros_like(prev_ref)
    row = lax.broadcasted_iota(jnp.int32, (CHUNK, CHUNK), 0)
    col = lax.broadcasted_iota(jnp.int32, (CHUNK, CHUNK), 1)
    tril = row >= col
    tril_f = tril.astype(F32)
    conv_row = lax.broadcasted_iota(jnp.int32, (CHUNK, MIX_WIDTH + 2 * gn), 0)
    neg_a = -jnp.exp(alog_ref[...])

    def chunk(c, carry):
        for bb in range(x_ref.shape[0]):
            chunk_of(bb, c)
        return carry

    def chunk_of(bb, c):
        r0 = pl.multiple_of(c * CHUNK, CHUNK)
        rows = pl.ds(r0, CHUNK)
        raw = jnp.concatenate([x_ref[bb, rows, :], b_ref[bb, rows, :], c_ref[bb, rows, :]], axis=1).astype(F32)
        prev = prev_ref[bb]
        conv = raw * convw_ref[SSD_CONV - 1:SSD_CONV, :] + convb_ref[...]
        for k in range(1, SSD_CONV):
            shifted = jnp.where(conv_row < k, pltpu.roll(prev, k, axis=0), pltpu.roll(raw, k, axis=0))
            conv = conv + shifted * convw_ref[SSD_CONV - 1 - k:SSD_CONV - k, :]
        prev_ref[bb] = raw
        xbc = _silu(conv)
        xs, cm = xbc[:, :MIX_WIDTH], xbc[:, MIX_WIDTH + gn:]
        bm_t = xbc[:, MIX_WIDTH:MIX_WIDTH + gn].T
        pre = dtraw_ref[bb, rows, :] + dtbias_ref[...]
        dt = jnp.maximum(pre, 0.0) + jnp.log1p(jnp.exp(-jnp.abs(pre)))
        acum = jnp.dot(tril_f, dt * neg_a, precision=HI, preferred_element_type=F32)
        acum_t = acum.T
        ys = []
        for h in range(N_HEADS):
            g = h // rep
            hs = slice(h * HEAD_DIM, (h + 1) * HEAD_DIM)
            gs = slice(g * SSD_STATE, (g + 1) * SSD_STATE)
            a_col = acum[:, h:h + 1]
            a_row = acum_t[h:h + 1, :]
            a_last = acum[CHUNK - 1:CHUNK, h:h + 1]
            decay = jnp.exp(jnp.where(tril, a_col - a_row, -jnp.inf))
            cc = cm[:, gs]
            bt = bm_t[gs, :]
            x_h = xs[:, hs]
            xdt = x_h * dt[:, h:h + 1]
            scores = _dot(cc, bt) * decay
            prev = state_ref[bb, h]
            y = _dot(scores, xdt) + _dot(cc, prev) * jnp.exp(a_col)
            state_ref[bb, h] = prev * jnp.exp(a_last) + _dot(bt * jnp.exp(a_last - a_row), xdt)
            ys.append(y + x_h * dskip_ref[0:1, hs])
        y = jnp.concatenate(ys, axis=-1)
        y = y * _silu(z_ref[bb, rows, :].astype(F32))
        ms = jnp.mean(y * y, axis=-1, keepdims=True)
        o_ref[bb, rows, :] = (y * lax.rsqrt(ms + NORM_EPS) * gain_ref[...]).astype(o_ref.dtype)

    lax.fori_loop(0, nchunks, chunk, 0)


def ssd_mixer(proj, x_col, z_col, dt_raw, conv_w, conv_b, dt_bias, a_log, d_skip, norm_gain):
    bsz, seq, _ = proj.shape
    width = MIX_WIDTH
    gn = SSD_GROUPS * SSD_STATE
    conv_dim = width + 2 * gn
    lanes8 = lambda v: jnp.pad(v, (0, LANES - N_HEADS)).reshape(1, LANES)
    nb = _SCAN_BATCH if bsz % _SCAN_BATCH == 0 else 1
    wide = lambda blk: pl.BlockSpec((nb, seq, width), lambda b: (b, 0, blk))
    narrow = lambda blk: pl.BlockSpec((nb, seq, LANES), lambda b: (b, 0, blk))
    const = lambda shp: pl.BlockSpec(shp, lambda b: (0,) * len(shp))
    return pl.pallas_call(
        _ssd_body,
        grid=(bsz // nb,),
        in_specs=[wide(x_col // 4), narrow(x_col + 4), narrow(x_col + 5), wide(z_col // 4), narrow(0),
                  const((SSD_CONV, conv_dim)), const((1, conv_dim)), const((1, LANES)), const((1, LANES)),
                  const((1, width)), const((1, width))],
        out_specs=pl.BlockSpec((nb, seq, width), lambda b: (b, 0, 0)),
        out_shape=jax.ShapeDtypeStruct((bsz, seq, width), BF16),
        scratch_shapes=[pltpu.VMEM((nb, N_HEADS, SSD_STATE, HEAD_DIM), F32),
                        pltpu.VMEM((nb, CHUNK, conv_dim), F32)],
        compiler_params=_cparams("parallel"),
        name="ssd_mixer",
    )(proj, proj, proj, proj, dt_raw, conv_w, conv_b.reshape(1, conv_dim), lanes8(dt_bias), lanes8(a_log),
      jnp.repeat(d_skip, HEAD_DIM).reshape(1, width), norm_gain.reshape(1, width))


def _rope_tables(seq):
    half = HEAD_DIM // 2
    inv_freq = ROPE_THETA ** (-jnp.arange(half, dtype=F32) / half)
    ang = jnp.arange(seq, dtype=F32)[:, None] * inv_freq[None, :]
    sin = jnp.sin(ang)
    cos = jnp.tile(jnp.cos(ang), (1, LANES // half))
    sin_signed = jnp.tile(jnp.concatenate([-sin, sin], axis=1), (1, LANES // HEAD_DIM))
    return cos, sin_signed


def _rope_lanes(x, cos, sin_signed):
    width = x.shape[1]
    half = HEAD_DIM // 2
    lane = lax.broadcasted_iota(jnp.int32, x.shape, 1)
    partner = jnp.where(lane % HEAD_DIM < half, pltpu.roll(x, width - half, axis=1), pltpu.roll(x, half, axis=1))
    reps = width // LANES
    if reps > 1:
        cos = jnp.concatenate([cos] * reps, axis=1)
        sin_signed = jnp.concatenate([sin_signed] * reps, axis=1)
    return x * cos + partner * sin_signed


def _ret_body(q_ref, k_ref, v_ref, g_ref, cos_ref, sin_ref, dmat_ref, zeta_ref, xi_ref, cdec_ref, gain_ref,
              o_ref, state_ref):
    seq = q_ref.shape[1]
    nchunks = seq // CHUNK
    state_ref[...] = jnp.zeros_like(state_ref)

    def chunk(c, carry):
        for bb in range(q_ref.shape[0]):
            chunk_of(bb, c)
        return carry

    def chunk_of(bb, c):
        r0 = pl.multiple_of(c * CHUNK, CHUNK)
        rows = pl.ds(r0, CHUNK)
        cos, sin = cos_ref[rows, :], sin_ref[rows, :]
        q = _rope_lanes(q_ref[bb, rows, :].astype(F32), cos, sin).astype(BF16)
        k = _rope_lanes(k_ref[bb, rows, :].astype(F32), cos, sin) * HEAD_DIM ** -0.5
        k_t = k.T
        ys = []
        for h in range(N_HEADS):
            hs = slice(h * HEAD_DIM, (h + 1) * HEAD_DIM)
            qc = q[:, hs]
            kt = k_t[hs, :]
            vc = v_ref[bb, rows, hs]
            inner = _dot(qc, kt) * dmat_ref[h]
            prev = state_ref[bb, h]
            y = _dot(inner, vc) + _dot(qc, prev) * xi_ref[h]
            state_ref[bb, h] = prev * cdec_ref[h] + _dot(kt * zeta_ref[h], vc)
            mu = jnp.mean(y, axis=-1, keepdims=True)
            yc = y - mu
            var = jnp.mean(yc * yc, axis=-1, keepdims=True)
            ys.append(yc * lax.rsqrt(var + NORM_EPS))
        y = jnp.concatenate(ys, axis=-1) * gain_ref[...]
        o_ref[bb, rows, :] = (_silu(g_ref[bb, rows, :].astype(F32)) * y).astype(o_ref.dtype)

    lax.fori_loop(0, nchunks, chunk, 0)


def retention_mixer(proj, cols, rope_tables, norm_gain):
    bsz, seq, _ = proj.shape
    width = MIX_WIDTH
    log_gamma = jnp.log1p(-jnp.exp2(-5.0 - jnp.arange(N_HEADS, dtype=F32)))
    idx = jnp.arange(CHUNK, dtype=F32)
    rel = idx[:, None] - idx[None, :]
    dmat = jnp.where(rel >= 0, jnp.exp(jnp.maximum(rel, 0.0)[None] * log_gamma[:, None, None]), 0.0)
    zeta = jnp.exp((CHUNK - 1.0 - idx)[None, None, :] * log_gamma[:, None, None])
    xi = jnp.exp((idx + 1.0)[None, :, None] * log_gamma[:, None, None])
    cdec = jnp.exp(CHUNK * log_gamma)[:, None, None]
    nb = _SCAN_BATCH if bsz % _SCAN_BATCH == 0 else 1
    col = lambda blk: pl.BlockSpec((nb, seq, width), lambda b: (b, 0, blk))
    const = lambda shp: pl.BlockSpec(shp, lambda b: (0,) * len(shp))
    return pl.pallas_call(
        _ret_body,
        grid=(bsz // nb,),
        in_specs=[col(c) for c in cols] + [const((seq, LANES)), const((seq, LANES)),
                  const((N_HEADS, CHUNK, CHUNK)), const((N_HEADS, 1, CHUNK)),
                  const((N_HEADS, CHUNK, 1)), const((N_HEADS, 1, 1)), const((1, width))],
        out_specs=pl.BlockSpec((nb, seq, width), lambda b: (b, 0, 0)),
        out_shape=jax.ShapeDtypeStruct((bsz, seq, width), BF16),
        scratch_shapes=[pltpu.VMEM((nb, N_HEADS, HEAD_DIM, HEAD_DIM), F32)],
        compiler_params=_cparams("parallel"),
        name="retention_mixer",
    )(proj, proj, proj, proj, *rope_tables, dmat, zeta, xi, cdec, norm_gain.reshape(1, width))


def _softmax_pv(pieces, v_ones, hs):
    m = jnp.max(functools.reduce(jnp.maximum, pieces), axis=-1, keepdims=True)
    p = jnp.concatenate([jnp.exp(sp - m) for sp in pieces], axis=1).astype(BF16)
    o = jnp.dot(p, v_ones, preferred_element_type=F32)
    return (o / pltpu.roll(o, HEAD_DIM, axis=1))[:, hs]


def _values_with_ones(v_ref, hs, vo_ref):
    v = v_ref[0]
    lane = lax.broadcasted_iota(jnp.int32, v.shape, 1)
    vo_ref[...] = jnp.where((lane >= hs.start) & (lane < hs.stop), v, jnp.ones_like(v))


def _split_bf16(x):
    hi = x.astype(BF16)
    return hi, (x - hi.astype(F32)).astype(BF16)


def _rope_qk(q_ref, k_ref, cos_ref, sin_ref, qs_ref, kt_ref):
    cos, sin = cos_ref[...], sin_ref[...]
    qs_ref[...] = _rope_lanes(q_ref[0].astype(F32), cos, sin)
    kt_ref[...] = _rope_lanes(k_ref[0].astype(F32), cos, sin).T.astype(kt_ref.dtype)


def _moba_body(q_ref, k_ref, v_ref, cos_ref, sin_ref, o_ref, qs_ref, kt_ref, vo_ref):
    seq = q_ref.shape[1]
    nb = seq // ATT_BLOCK
    halves = ATT_BLOCK // LANES
    scale = HEAD_DIM ** -0.5
    heads = q_ref.shape[2] // HEAD_DIM
    n_sel = min(MOBA_TOPK, nb - 1)
    row = lax.broadcasted_iota(jnp.int32, (ATT_BLOCK, LANES), 0)
    col = lax.broadcasted_iota(jnp.int32, (ATT_BLOCK, LANES), 1)
    tril = [row >= col + hf * LANES for hf in range(halves)]
    _rope_qk(q_ref, k_ref, cos_ref, sin_ref, qs_ref, kt_ref)

    for h in range(heads):
        hs = slice(h * HEAD_DIM, (h + 1) * HEAD_DIM)
        kt = kt_ref[hs, :]
        kt_bf = kt.astype(BF16)
        _values_with_ones(v_ref, hs, vo_ref)
        reps = [jnp.broadcast_to(jnp.mean(kt[:, n * ATT_BLOCK:(n + 1) * ATT_BLOCK], axis=1, keepdims=True),
                                 (HEAD_DIM, LANES)) for n in range(nb - 1)]
        kr_hi, kr_lo = _split_bf16(jnp.concatenate(reps, axis=1))
        kr4 = jnp.concatenate([kr_hi, kr_lo, kr_hi, kr_lo], axis=0)

        for i in range(nb):
            rows = slice(i * ATT_BLOCK, (i + 1) * ATT_BLOCK)
            n_keys = (i + 1) * ATT_BLOCK
            q = qs_ref[rows, hs]
            sel = None
            if i > n_sel:
                q_hi, q_lo = _split_bf16(q)
                q4 = jnp.concatenate([q_hi, q_hi, q_lo, q_lo], axis=1)
                g = jnp.dot(q4, kr4[:, :i * LANES], preferred_element_type=F32)
                gs = [g[:, n * LANES:(n + 1) * LANES] for n in range(i)]
                cnt = [jnp.full((ATT_BLOCK, LANES), float(i - 1 - a), F32) for a in range(i)]
                for a in range(i):
                    for b in range(a + 1, i):
                        a_wins = jnp.where(gs[a] >= gs[b], 1.0, 0.0)
                        cnt[b] = cnt[b] + a_wins
                        cnt[a] = cnt[a] - a_wins
                sel = [c < n_sel for c in cnt]
            s = jnp.dot((q * scale).astype(BF16), kt_bf[:, :n_keys], preferred_element_type=F32)
            pieces = []
            for k in range(halves * (i + 1)):
                n, hf = divmod(k, halves)
                sp = s[:, k * LANES:(k + 1) * LANES]
                if n == i:
                    sp = jnp.where(tril[hf], sp, NEG_INF)
                elif sel is not None:
                    sp = jnp.where(sel[n], sp, NEG_INF)
                pieces.append(sp)
            o_ref[0, rows, hs] = _softmax_pv(pieces, vo_ref[:n_keys, :], hs).astype(o_ref.dtype)


def _dil_body(q_ref, k_ref, v_ref, cos_ref, sin_ref, lm_ref, o_ref, qs_ref, kt_ref, vo_ref):
    seq = q_ref.shape[1]
    nb = seq // ATT_BLOCK
    halves = ATT_BLOCK // LANES
    scale = HEAD_DIM ** -0.5
    heads = q_ref.shape[2] // HEAD_DIM
    _rope_qk(q_ref, k_ref, cos_ref, sin_ref, qs_ref, kt_ref)

    for h in range(heads):
        hs = slice(h * HEAD_DIM, (h + 1) * HEAD_DIM)
        kt_bf = kt_ref[hs, :]
        _values_with_ones(v_ref, hs, vo_ref)
        for i in range(nb):
            rows = slice(i * ATT_BLOCK, (i + 1) * ATT_BLOCK)
            n_keys = (i + 1) * ATT_BLOCK
            qs = (qs_ref[rows, hs] * scale).astype(BF16)
            s = jnp.dot(qs, kt_bf[:, :n_keys], preferred_element_type=F32)
            pieces = []
            for k in range(halves * (i + 1)):
                n, hf = divmod(k, halves)
                pieces.append(s[:, k * LANES:(k + 1) * LANES] + lm_ref[i - n, :, hf * LANES:(hf + 1) * LANES])
            o_ref[0, rows, hs] = _softmax_pv(pieces, vo_ref[:n_keys, :], hs).astype(o_ref.dtype)


def _dilation_log_multiplicity(seq):
    nb = seq // ATT_BLOCK
    r = jnp.arange(ATT_BLOCK)
    d = (jnp.arange(nb)[:, None, None] * ATT_BLOCK + r[None, :, None] - r[None, None, :])
    mult = jnp.zeros(d.shape, F32)
    for window, dil in DIL_PATTERNS:
        mult = mult + ((d >= 0) & (d <= window) & (d % dil == 0)).astype(F32)
    return jnp.where(mult > 0, jnp.log(jnp.maximum(mult, 1.0)), NEG_INF)


def _attention_mixer(body, name, kt_dtype, proj, cols, rope_tables, extra=()):
    bsz, seq, _ = proj.shape
    pair = lambda lane_blk: pl.BlockSpec((1, seq, LANES), lambda b, p: (b, 0, lane_blk + p))
    const = lambda shp: pl.BlockSpec(shp, lambda b, p: (0,) * len(shp))
    return pl.pallas_call(
        body,
        grid=(bsz, MIX_WIDTH // LANES),
        in_specs=[pair(c) for c in cols] + [const((seq, LANES)), const((seq, LANES))]
                 + [const(e.shape) for e in extra],
        out_specs=pl.BlockSpec((1, seq, LANES), lambda b, p: (b, 0, p)),
        out_shape=jax.ShapeDtypeStruct((bsz, seq, MIX_WIDTH), BF16),
        scratch_shapes=[pltpu.VMEM((seq, LANES), F32), pltpu.VMEM((LANES, seq), kt_dtype),
                        pltpu.VMEM((seq, LANES), BF16)],
        compiler_params=_cparams("parallel", "parallel"),
        name=name,
    )(proj, proj, proj, *rope_tables, *extra)


def moba_mixer(proj, cols, rope_tables):
    return _attention_mixer(_moba_body, "moba_mixer", F32, proj, cols, rope_tables)


def dilated_mixer(proj, cols, rope_tables):
    seq = proj.shape[1]
    return _attention_mixer(_dil_body, "dilated_mixer", BF16, proj, cols, rope_tables,
                            extra=(_dilation_log_multiplicity(seq),))


def _merge_xattn_body(h_ref, gates_ref, y0_ref, y1_ref, y2_ref, y3_ref, wb_ref, wout_ref,
                      gain_ref, wq_ref, k_ref, v_ref, wo_ref, o_ref):
    d = h_ref.shape[2]
    dh = d // X_HEADS
    merged = jnp.zeros(h_ref.shape[1:], F32)
    for i, y_ref in enumerate((y0_ref, y1_ref, y2_ref, y3_ref)):
        gate = 1.0 / (1.0 + jnp.exp(-gates_ref[0, :, i * d:(i + 1) * d].astype(F32)))
        merged = merged + gate * _dot(y_ref[0], wb_ref[i])
    h = h_ref[0] + _dot(merged, wout_ref[...])

    ms = jnp.mean(h * h, axis=-1, keepdims=True)
    hn = h * lax.rsqrt(ms + NORM_EPS) * gain_ref[...]
    q = _dot(hn, wq_ref[...]) * dh ** -0.5
    outs = []
    for a in range(X_HEADS):
        cs = slice(a * dh, (a + 1) * dh)
        s = _dot_nt(q[:, cs].astype(BF16), k_ref[0, :, cs])
        m = jnp.max(s, axis=-1, keepdims=True)
        e = jnp.exp(s - m)
        p = e / jnp.sum(e, axis=-1, keepdims=True)
        outs.append(_dot(p, v_ref[0, :, cs]))
    o = jnp.concatenate(outs, axis=-1)
    o_ref[0] = h + _dot(o, wo_ref[...])


def merge_cross_attention(h, proj, ys, w_branch, w_out, kv, gain, w_q, w_o, tq=512):
    bsz, seq, d = h.shape
    mlen = kv.shape[1]
    width = ys[0].shape[2]
    tq = min(tq, seq)
    rows = lambda w: pl.BlockSpec((1, tq, w), lambda b, i: (b, i, 0))
    const = lambda shp: pl.BlockSpec(shp, lambda b, i: (0,) * len(shp))
    return pl.pallas_call(
        _merge_xattn_body,
        grid=(bsz, seq // tq),
        in_specs=[rows(d), rows(N_BRANCHES * d)] + [rows(width)] * N_BRANCHES
                 + [const((N_BRANCHES, width, d)), const((d, d)), const((1, d)), const((d, d)),
                    pl.BlockSpec((1, mlen, d), lambda b, i: (b, 0, 0)),
                    pl.BlockSpec((1, mlen, d), lambda b, i: (b, 0, 1)),
                    const((d, d))],
        out_specs=rows(d),
        out_shape=jax.ShapeDtypeStruct((bsz, seq, d), F32),
        compiler_params=_cparams("parallel", "parallel"),
        name="merge_cross_attention",
    )(h, proj, *ys, w_branch, w_out, gain.reshape(1, d), w_q, kv, kv, w_o)


_SUBLANES = 8


def _sorting_network(n):
    pairs, p = [], 1
    while p < n:
        k = p
        while k >= 1:
            for j in range(k % p, n - k, 2 * k):
                for i in range(min(k, n - j - k)):
                    if (i + j) // (2 * p) == (i + j + k) // (2 * p):
                        pairs.append((i + j, i + j + k))
            k //= 2
        p *= 2
    return pairs


def _top_rows(vals, count):
    n_tiles = vals.shape[0] // _SUBLANES
    levels = [vals[_SUBLANES * g:_SUBLANES * (g + 1)] for g in range(n_tiles)]
    size = 1 << (n_tiles - 1).bit_length()
    neg = jnp.full(levels[0].shape, -jnp.inf, F32)
    levels += [neg] * (size - n_tiles)
    for a, b in _sorting_network(size):
        levels[a], levels[b] = jnp.maximum(levels[a], levels[b]), jnp.minimum(levels[a], levels[b])
    levels = levels[:n_tiles]
    tops = []
    for r in range(count):
        m = jnp.max(levels[0], axis=0, keepdims=True)
        tops.append(m)
        left = count - r - 1
        hit = levels[0] == m
        for g in range(min(len(levels), left)):
            below = levels[g + 1] if g + 1 < len(levels) else neg
            levels[g] = jnp.where(hit, below, levels[g])
    return tops


def _count_greater(sorted_rows, x):
    count = jnp.zeros(x.shape, F32)
    for r, row in enumerate(sorted_rows):
        count = jnp.where(row > x, float(r + 1), count)
    return count


_N_RANKS = PEER_TOPK + 1
_RANK_ROWS = 24


_N_CAND = sum(_N_RANKS // (i + 1) for i in range(_N_RANKS))
_CAND_ROWS = -(-_N_CAND // 8) * 8


def _pair_candidates(a1, a2_ref, cand_ref):
    cand_ref[_N_CAND // 8 * 8:, :] = jnp.full((_CAND_ROWS - _N_CAND // 8 * 8, cand_ref.shape[1]), -jnp.inf, F32)
    row = 0
    for i in range(_N_RANKS):
        n_j = _N_RANKS // (i + 1)
        cand_ref[row:row + n_j, :] = a1[i] + a2_ref[0:n_j, :]
        row += n_j
    return cand_ref[...]


def _router_body(h_ref, gain_ref, wq_hi_ref, wq_lo_ref, keys_ref, rank2_ref, e2_ref, cut_ref, e1_ref,
                 a2_ref, cand_ref):
    h = h_ref[...]
    ms = jnp.mean(h * h, axis=-1, keepdims=True)
    x_hi, x_lo = _split_bf16(h * lax.rsqrt(ms + NORM_EPS) * gain_ref[...])
    q = (jnp.dot(x_hi, wq_hi_ref[...], preferred_element_type=F32)
         + jnp.dot(x_hi, wq_lo_ref[...], preferred_element_type=F32)
         + jnp.dot(x_lo, wq_hi_ref[...], preferred_element_type=F32))
    dk = keys_ref.shape[-1] // 2
    for hd in range(PEER_HEADS):
        s = []
        for half in range(2):
            c0 = (hd * 2 + half) * dk
            q_hi, q_lo = _split_bf16(q[:, c0:c0 + dk])
            keys2 = keys_ref[hd, half]
            s.append(_dot_nt(keys2, jnp.concatenate([q_hi, q_hi], axis=1))
                     + _dot_nt(keys2, jnp.concatenate([q_lo, q_lo], axis=1)))
        s1, s2 = s
        a1 = _top_rows(s1, _N_RANKS)
        a2 = _top_rows(s2, _N_RANKS)
        for r in range(_N_RANKS):
            a2_ref[r:r + 1, :] = a2[r]
        cand = _pair_candidates(a1, a2_ref, cand_ref)
        top = _top_rows(cand, _N_RANKS)
        tau = 0.5 * (top[PEER_TOPK - 1] + top[PEER_TOPK])
        z = jnp.sum(jnp.where(cand > tau, jnp.exp(cand - (a1[0] + a2[0])), 0.0), axis=0, keepdims=True)
        rank2_ref[hd] = _count_greater(a2, s2).astype(rank2_ref.dtype)
        e2_ref[hd] = jnp.exp(s2 - a2[0]).astype(e2_ref.dtype)
        cut_ref[hd] = _count_greater(a2[:PEER_TOPK], tau - s1)
        e1_ref[hd] = jnp.exp(s1 - a1[0]) * (0.5 / z)


def peer_router(h, gain, w_q, sub_keys, tt=256):
    t, d = h.shape
    tt = min(tt, t)
    nk = sub_keys.shape[2]
    wq_hi, wq_lo = _split_bf16(w_q)
    keys2 = jnp.concatenate(_split_bf16(sub_keys), axis=-1)
    out = [jax.ShapeDtypeStruct((PEER_HEADS, nk, t), dt) for dt in (BF16, BF16, F32, F32)]
    ospec = pl.BlockSpec((PEER_HEADS, nk, tt), lambda i: (0, 0, i))
    return pl.pallas_call(
        _router_body,
        grid=(t // tt,),
        in_specs=[pl.BlockSpec((tt, d), lambda i: (i, 0)),
                  pl.BlockSpec((1, d), lambda i: (0, 0)),
                  pl.BlockSpec(w_q.shape, lambda i: (0, 0)),
                  pl.BlockSpec(w_q.shape, lambda i: (0, 0)),
                  pl.BlockSpec(keys2.shape, lambda i: (0, 0, 0, 0))],
        out_specs=[ospec] * 4,
        out_shape=out,
        scratch_shapes=[pltpu.VMEM((_RANK_ROWS, tt), F32), pltpu.VMEM((_CAND_ROWS, tt), F32)],
        compiler_params=_cparams("parallel"),
        name="peer_router",
    )(h, gain.reshape(1, d), wq_hi, wq_lo, keys2)


def _gelu_x2(x):
    return x * (1.0 + lax.erf(x * (2.0 ** -0.5)))


_ROWS_PER_STEP = 8
_J_CHUNK = 32
_ROW_GROUP = 4
_TOK_COLS = 256
_MM_SPLIT = 4
_BF16_ROWS = 16


def _experts_body(h_ref, gain_ref, u0_ref, un_ref, vt_ref, rank2_ref, e2_ref, cut_ref, e1_ref, o_ref,
                  xn_ref, acc_ref, act_a_ref, act_b_ref, p_a_ref, p_b_ref, cutb_ref, e1b_ref):
    step = pl.program_id(1)
    n_tiles = pl.num_programs(1) - 1
    nk = rank2_ref.shape[1]
    tt = h_ref.shape[0]
    n_cols = tt // _TOK_COLS
    act_refs = (act_a_ref, act_b_ref)
    p_refs = (p_a_ref, p_b_ref)

    @pl.when(step == 0)
    def _():
        h = h_ref[...]
        ms = jnp.mean(h * h, axis=-1, keepdims=True)
        xn_ref[...] = (h * lax.rsqrt(ms + NORM_EPS) * gain_ref[...]).astype(BF16)
        acc_ref[...] = jnp.zeros_like(acc_ref)
        p_refs[1][...] = jnp.zeros(p_refs[1].shape, BF16)
        act_refs[0][...] = _dot_nt(u0_ref[...], xn_ref[...])

    def main_block(act_cur, act_nxt, p_cur, p_prev):
        d_rows = acc_ref.shape[0] // _MM_SPLIT
        e_rows = un_ref.shape[0] // _MM_SPLIT

        def v_piece(k, c):
            rs, cs = slice(k * d_rows, (k + 1) * d_rows), slice(c * _TOK_COLS, (c + 1) * _TOK_COLS)
            acc_ref[rs, cs] += jnp.dot(vt_ref[rs, :], p_prev[:, cs], preferred_element_type=F32)

        def act_piece(k, c):
            rs, cs = slice(k * e_rows, (k + 1) * e_rows), slice(c * _TOK_COLS, (c + 1) * _TOK_COLS)
            act_nxt[rs, cs] = _dot_nt(un_ref[rs, :], xn_ref[cs, :])

        pieces = [(f, k, c) for k in range(_MM_SPLIT) for c in range(n_cols) for f in (v_piece, act_piece)]

        for hd in range(PEER_HEADS):
            cut8 = cut_ref[hd]
            e18 = e1_ref[hd]
            for r in range(_ROWS_PER_STEP):
                cutb_ref[hd, r] = jnp.broadcast_to(cut8[r:r + 1], (_BF16_ROWS, tt)).astype(BF16)
                e1b_ref[hd, r] = jnp.broadcast_to(e18[r:r + 1], (_BF16_ROWS, tt)).astype(BF16)

        reps = _J_CHUNK // _BF16_ROWS
        chunks = [(c, jc, rg) for c in range(n_cols) for jc in range(nk // _J_CHUNK)
                  for rg in range(_ROWS_PER_STEP // _ROW_GROUP)]
        per_chunk = -(-len(pieces) // len(chunks))
        for n, (c, jc, rg) in enumerate(chunks):
            for f, k, cc in pieces[n * per_chunk:(n + 1) * per_chunk]:
                f(k, cc)
            ts = slice(c * _TOK_COLS, (c + 1) * _TOK_COLS)
            js = slice(jc * _J_CHUNK, (jc + 1) * _J_CHUNK)
            gates = [jnp.zeros((_J_CHUNK, _TOK_COLS), BF16) for _ in range(_ROW_GROUP)]
            for hd in range(PEER_HEADS):
                rank2 = rank2_ref[hd, js, ts]
                e2 = e2_ref[hd, js, ts]
                for k in range(_ROW_GROUP):
                    r = rg * _ROW_GROUP + k
                    cut = jnp.concatenate([cutb_ref[hd, r, :, ts]] * reps, axis=0)
                    e1 = jnp.concatenate([e1b_ref[hd, r, :, ts]] * reps, axis=0)
                    gates[k] = gates[k] + jnp.where(rank2 < cut, e2 * e1, jnp.zeros_like(e2))
            for k in range(_ROW_GROUP):
                rows = slice((rg * _ROW_GROUP + k) * nk + jc * _J_CHUNK,
                             (rg * _ROW_GROUP + k) * nk + (jc + 1) * _J_CHUNK)
                p_cur[rows, ts] = _gelu_x2(act_cur[rows, ts]).astype(BF16) * gates[k]

    for parity in range(2):
        @pl.when((step % 2 == parity) & (step < n_tiles))
        def _():
            main_block(act_refs[parity], act_refs[1 - parity], p_refs[parity], p_refs[1 - parity])

    last_p = p_refs[(rank2_ref.shape[1] // _ROWS_PER_STEP - 1) % 2]

    @pl.when(step == n_tiles)
    def _():
        acc = acc_ref[...] + jnp.dot(vt_ref[...], last_p[...], preferred_element_type=F32)
        o_ref[...] = h_ref[...] + acc.T


def peer_experts(h, gain, u, v_t, rank2, e2, cut, e1, tt=512):
    t, d = h.shape
    nk = rank2.shape[1]
    tt = min(tt, t)
    et = _ROWS_PER_STEP * nk
    n_tiles = nk // _ROWS_PER_STEP
    rspec = pl.BlockSpec((PEER_HEADS, nk, tt), lambda i, s: (0, 0, i))
    row_spec = pl.BlockSpec((PEER_HEADS, _ROWS_PER_STEP, tt), lambda i, s: (0, jnp.minimum(s, n_tiles - 1), i))
    return pl.pallas_call(
        _experts_body,
        grid=(t // tt, n_tiles + 1),
        in_specs=[pl.BlockSpec((tt, d), lambda i, s: (i, 0)),
                  pl.BlockSpec((1, d), lambda i, s: (0, 0)),
                  pl.BlockSpec((et, d), lambda i, s: (0, 0)),
                  pl.BlockSpec((et, d), lambda i, s: (jnp.minimum(s + 1, n_tiles - 1), 0)),
                  pl.BlockSpec((None, d, et), lambda i, s: (jnp.maximum(s - 1, 0), 0, 0)),
                  rspec, rspec, row_spec, row_spec],
        out_specs=pl.BlockSpec((tt, d), lambda i, s: (i, 0)),
        out_shape=jax.ShapeDtypeStruct((t, d), F32),
        scratch_shapes=[pltpu.VMEM((tt, d), BF16), pltpu.VMEM((d, tt), F32),
                        pltpu.VMEM((et, tt), F32), pltpu.VMEM((et, tt), F32),
                        pltpu.VMEM((et, tt), BF16), pltpu.VMEM((et, tt), BF16),
                        pltpu.VMEM((PEER_HEADS, _ROWS_PER_STEP, _BF16_ROWS, tt), BF16),
                        pltpu.VMEM((PEER_HEADS, _ROWS_PER_STEP, _BF16_ROWS, tt), BF16)],
        compiler_params=_cparams("parallel", "arbitrary"),
        name="peer_experts",
    )(h, gain.reshape(1, d), u, u, v_t, rank2, e2, cut, e1)


def _transpose_cast_body(x_ref, o_ref):
    o_ref[...] = x_ref[...].T.astype(o_ref.dtype)


def transpose_cast(x3, index, dtype, tile, blk=512):
    _, r, c = x3.shape
    per = tile // blk
    return pl.pallas_call(
        _transpose_cast_body,
        grid=(r // blk, c // blk),
        in_specs=[pl.BlockSpec((None, blk, blk), lambda i, j: (index, i, j))],
        out_specs=pl.BlockSpec((None, blk, blk), lambda i, j: (i // per, j, i % per)),
        out_shape=jax.ShapeDtypeStruct((r // tile, c, tile), dtype),
        compiler_params=_cparams("parallel", "parallel"),
        name="transpose_cast",
    )(x3)


def _final_norm_body(x_ref, g_ref, o_ref):
    x = x_ref[...]
    ms = jnp.mean(x * x, axis=-1, keepdims=True)
    o_ref[...] = x * lax.rsqrt(ms + NORM_EPS) * g_ref[...]


def final_norm(x, gain, tm=1024):
    t, d = x.shape
    tm = min(tm, t)
    return pl.pallas_call(
        _final_norm_body,
        grid=(t // tm,),
        in_specs=[pl.BlockSpec((tm, d), lambda i: (i, 0)), pl.BlockSpec((1, d), lambda i: (0, 0))],
        out_specs=pl.BlockSpec((tm, d), lambda i: (i, 0)),
        out_shape=jax.ShapeDtypeStruct((t, d), F32),
        compiler_params=_cparams("parallel"),
        name="final_norm",
    )(x, gain.reshape(1, d))


_GATE_BLOCKS = N_BRANCHES * 8
_COL_Z = _GATE_BLOCKS
_COL_RET = _COL_Z + 4
_COL_MOBA = _COL_RET + 16
_COL_DIL = _COL_MOBA + 12
_COL_XBC = _COL_DIL + 12
_PROJ_BLOCKS = 84


def _pack_w_in(w_in_l):
    ssd_conv_dim = MIX_WIDTH + 2 * SSD_GROUPS * SSD_STATE
    o_xbc = MIX_WIDTH
    o_dt = o_xbc + ssd_conv_dim
    o_rest = o_dt + N_HEADS
    o_gate = o_rest + 10 * MIX_WIDTH
    d_model = w_in_l.shape[0]
    used = _COL_XBC * LANES + ssd_conv_dim
    pad = jnp.zeros((d_model, _PROJ_BLOCKS * LANES - used), w_in_l.dtype)
    main = jnp.concatenate([w_in_l[:, o_gate:], w_in_l[:, :o_xbc], w_in_l[:, o_rest:o_gate],
                            w_in_l[:, o_xbc:o_dt], pad], axis=1).astype(BF16)
    w_dt = jnp.pad(w_in_l[:, o_dt:o_rest], ((0, 0), (0, LANES - N_HEADS))).astype(BF16)
    return main, w_dt


def kernel(x, mem, mix_norm, w_in, ssd_conv_w, ssd_conv_b, ssd_dt_bias, ssd_a_log, ssd_d, ssd_norm, ret_norm, w_branch, w_out, x_norm, w_xq, w_xkv, w_xo, ffn_norm, w_pq, peer_sub_keys, peer_u, peer_v, final_norm_gain):
    bsz, seq, d = x.shape
    depth = w_in.shape[0]
    t = bsz * seq
    h = x.reshape(t, d)
    mem2 = mem.reshape(-1, d)
    ones = jnp.ones((d,), F32)

    rope_tables = _rope_tables(seq)

    for layer in range(depth):
        w_main, w_dt = _pack_w_in(w_in[layer])
        proj2 = norm_matmul(h, mix_norm[layer], w_main, tm=1024, tn=1536, out_dtype=BF16)
        dt_raw = norm_matmul(h, mix_norm[layer], w_dt, tm=1024).reshape(bsz, seq, LANES)
        proj = proj2.reshape(bsz, seq, -1)

        y_a = ssd_mixer(proj, _COL_XBC, _COL_Z, dt_raw, ssd_conv_w[layer], ssd_conv_b[layer],
                        ssd_dt_bias[layer], ssd_a_log[layer], ssd_d[layer], ssd_norm[layer])
        y_b = retention_mixer(proj, [_COL_RET // 4 + i for i in range(4)], rope_tables, ret_norm[layer])
        y_c = moba_mixer(proj, [_COL_MOBA + 4 * i for i in range(3)], rope_tables)
        y_d = dilated_mixer(proj, [_COL_DIL + 4 * i for i in range(3)], rope_tables)
        kv = norm_matmul(mem2, ones, w_xkv[layer].astype(BF16), normalize=False, out_dtype=BF16)
        h = merge_cross_attention(h.reshape(bsz, seq, d), proj, (y_a, y_b, y_c, y_d),
                                  w_branch[layer].astype(BF16), w_out[layer].astype(BF16),
                                  kv.reshape(bsz, -1, 2 * d), x_norm[layer],
                                  w_xq[layer].astype(BF16), w_xo[layer].astype(BF16)).reshape(t, d)
        rank2, e2, cut, e1 = peer_router(h, ffn_norm[layer], w_pq[layer], peer_sub_keys[layer])
        h = peer_experts(h, ffn_norm[layer], peer_u[layer].astype(BF16),
                         transpose_cast(peer_v, layer, BF16, _ROWS_PER_STEP * PEER_NKEYS), rank2, e2, cut, e1)

    return final_norm(h, final_norm_gain).reshape(bsz, seq, d)
```

```python
import functools

import jax
import jax.numpy as jnp
from jax import lax
from jax.experimental import pallas as pl
from jax.experimental.pallas import tpu as pltpu

F32 = jnp.float32
BF16 = jnp.bfloat16
HI = lax.Precision.HIGHEST

NORM_EPS = 1e-6
NEG_INF = -1e30
ROPE_THETA = 10000.0
HEAD_DIM = 64
N_HEADS = 8
MIX_WIDTH = N_HEADS * HEAD_DIM
SSD_GROUPS = 2
SSD_STATE = 64
SSD_CONV = 4
CHUNK = 128
ATT_BLOCK = 256
MOBA_TOPK = 3
DIL_PATTERNS = ((128, 1), (512, 4), (2048, 16))
N_BRANCHES = 4
X_HEADS = 4
PEER_HEADS = 8
PEER_NKEYS = 128
PEER_TOPK = 16
LANES = 128
VMEM_LIMIT = 56 * 1024 * 1024


def _cparams(*sem):
    return pltpu.CompilerParams(dimension_semantics=sem, vmem_limit_bytes=VMEM_LIMIT)


def _dot(a, b):
    return jnp.dot(a.astype(BF16), b.astype(BF16), preferred_element_type=F32)


def _dot_nt(a, b):
    return lax.dot_general(a, b, (((1,), (1,)), ((), ())), preferred_element_type=F32)


def _silu(x):
    return x * (1.0 / (1.0 + jnp.exp(-x)))


def _norm_matmul_body(x_ref, g_ref, w_ref, ws_ref, o_ref, os_ref, xn_ref):
    @pl.when(pl.program_id(1) == 0)
    def _():
        x = x_ref[...]
        ms = jnp.mean(x * x, axis=-1, keepdims=True)
        xn_ref[...] = (x * lax.rsqrt(ms + NORM_EPS) * g_ref[...]).astype(xn_ref.dtype)
        os_ref[...] = jnp.dot(xn_ref[...], ws_ref[...], preferred_element_type=F32)

    o_ref[...] = jnp.dot(xn_ref[...], w_ref[...], preferred_element_type=F32).astype(o_ref.dtype)


def norm_matmul(x, gain, w, w_side, *, tm, tn, out_dtype):
    t, k = x.shape
    n = w.shape[1]
    ns = w_side.shape[1]
    assert t % tm == 0 and n % tn == 0
    return pl.pallas_call(
        _norm_matmul_body,
        grid=(t // tm, n // tn),
        in_specs=[pl.BlockSpec((tm, k), lambda i, j: (i, 0)),
                  pl.BlockSpec((1, k), lambda i, j: (0, 0)),
                  pl.BlockSpec((k, tn), lambda i, j: (0, j)),
                  pl.BlockSpec((k, ns), lambda i, j: (0, 0))],
        out_specs=[pl.BlockSpec((tm, tn), lambda i, j: (i, j)), pl.BlockSpec((tm, ns), lambda i, j: (i, 0))],
        out_shape=[jax.ShapeDtypeStruct((t, n), out_dtype), jax.ShapeDtypeStruct((t, ns), F32)],
        scratch_shapes=[pltpu.VMEM((tm, k), BF16)],
        compiler_params=_cparams("parallel", "arbitrary"),
        name="norm_matmul",
    )(x, gain.reshape(1, k), w, w_side)


def _matmul_body(x_ref, w_ref, o_ref):
    o_ref[...] = _dot(x_ref[...], w_ref[...]).astype(o_ref.dtype)


def matmul(x, w, *, tm, tn, out_dtype):
    t, k = x.shape
    n = w.shape[1]
    assert t % tm == 0 and n % tn == 0
    return pl.pallas_call(
        _matmul_body,
        grid=(t // tm, n // tn),
        in_specs=[pl.BlockSpec((tm, k), lambda i, j: (i, 0)), pl.BlockSpec((k, tn), lambda i, j: (0, j))],
        out_specs=pl.BlockSpec((tm, tn), lambda i, j: (i, j)),
        out_shape=jax.ShapeDtypeStruct((t, n), out_dtype),
        compiler_params=_cparams("parallel", "parallel"),
        name="matmul",
    )(x, w)


def _ssd_body(x_ref, b_ref, c_ref, z_ref, dtraw_ref, convw_ref, convb_ref, dtbias_ref, alog_ref,
              dskip_ref, gain_ref, o_ref, state_ref, prev_ref):
    seq = x_ref.shape[1]
    nchunks = seq // CHUNK
    rep = N_HEADS // SSD_GROUPS
    gn = SSD_GROUPS * SSD_STATE
    state_ref[...] = jnp.zeros_like(state_ref)
    prev_ref[...] = jnp.zeros_like(prev_ref)
    row = lax.broadcasted_iota(jnp.int32, (CHUNK, CHUNK), 0)
    col = lax.broadcasted_iota(jnp.int32, (CHUNK, CHUNK), 1)
    tril = row >= col
    tril_f = tril.astype(F32)
    conv_row = lax.broadcasted_iota(jnp.int32, (CHUNK, MIX_WIDTH + 2 * gn), 0)
    neg_a = -jnp.exp(alog_ref[...])

    def chunk(c, carry):
        for bb in range(x_ref.shape[0]):
            chunk_of(bb, c)
        return carry

    def chunk_of(bb, c):
        r0 = pl.multiple_of(c * CHUNK, CHUNK)
        rows = pl.ds(r0, CHUNK)
        raw = jnp.concatenate([x_ref[bb, rows, :], b_ref[bb, rows, :], c_ref[bb, rows, :]], axis=1).astype(F32)
        prev = prev_ref[bb]
        conv = raw * convw_ref[SSD_CONV - 1:SSD_CONV, :] + convb_ref[...]
        for k in range(1, SSD_CONV):
            shifted = jnp.where(conv_row < k, pltpu.roll(prev, k, axis=0), pltpu.roll(raw, k, axis=0))
            conv = conv + shifted * convw_ref[SSD_CONV - 1 - k:SSD_CONV - k, :]
        prev_ref[bb] = raw
        xbc = _silu(conv)
        xs, cm = xbc[:, :MIX_WIDTH], xbc[:, MIX_WIDTH + gn:]
        bm_t = xbc[:, MIX_WIDTH:MIX_WIDTH + gn].T
        pre = dtraw_ref[bb, rows, :] + dtbias_ref[...]
        dt = jnp.maximum(pre, 0.0) + jnp.log1p(jnp.exp(-jnp.abs(pre)))
        acum = jnp.dot(tril_f, dt * neg_a, precision=HI, preferred_element_type=F32)
        acum_t = acum.T
        ys = []
        for h in range(N_HEADS):
            g = h // rep
            hs = slice(h * HEAD_DIM, (h + 1) * HEAD_DIM)
            gs = slice(g * SSD_STATE, (g + 1) * SSD_STATE)
            a_col = acum[:, h:h + 1]
            a_row = acum_t[h:h + 1, :]
            a_last = acum[CHUNK - 1:CHUNK, h:h + 1]
            decay = jnp.exp(jnp.where(tril, a_col - a_row, -jnp.inf))
            cc = cm[:, gs]
            bt = bm_t[gs, :]
            x_h = xs[:, hs]
            xdt = x_h * dt[:, h:h + 1]
            scores = _dot(cc, bt) * decay
            prev = state_ref[bb, h]
            y = _dot(scores, xdt) + _dot(cc, prev) * jnp.exp(a_col)
            state_ref[bb, h] = prev * jnp.exp(a_last) + _dot(bt * jnp.exp(a_last - a_row), xdt)
            ys.append(y + x_h * dskip_ref[0:1, hs])
        y = jnp.concatenate(ys, axis=-1)
        y = y * _silu(z_ref[bb, rows, :].astype(F32))
        ms = jnp.mean(y * y, axis=-1, keepdims=True)
        o_ref[bb, rows, :] = (y * lax.rsqrt(ms + NORM_EPS) * gain_ref[...]).astype(o_ref.dtype)

    lax.fori_loop(0, nchunks, chunk, 0)


def ssd_mixer(proj, x_col, z_col, dt_raw, conv_w, conv_b, dt_bias, a_log, d_skip, norm_gain):
    bsz, seq, _ = proj.shape
    width = MIX_WIDTH
    gn = SSD_GROUPS * SSD_STATE
    conv_dim = width + 2 * gn
    lanes8 = lambda v: jnp.pad(v, (0, LANES - N_HEADS)).reshape(1, LANES)
    nb = 1
    wide = lambda blk: pl.BlockSpec((nb, seq, width), lambda b: (b, 0, blk))
    narrow = lambda blk: pl.BlockSpec((nb, seq, LANES), lambda b: (b, 0, blk))
    const = lambda shp: pl.BlockSpec(shp, lambda b: (0,) * len(shp))
    return pl.pallas_call(
        _ssd_body,
        grid=(bsz // nb,),
        in_specs=[wide(x_col // 4), narrow(x_col + 4), narrow(x_col + 5), wide(z_col // 4), narrow(0),
                  const((SSD_CONV, conv_dim)), const((1, conv_dim)), const((1, LANES)), const((1, LANES)),
                  const((1, width)), const((1, width))],
        out_specs=pl.BlockSpec((nb, seq, width), lambda b: (b, 0, 0)),
        out_shape=jax.ShapeDtypeStruct((bsz, seq, width), BF16),
        scratch_shapes=[pltpu.VMEM((nb, N_HEADS, SSD_STATE, HEAD_DIM), F32),
                        pltpu.VMEM((nb, CHUNK, conv_dim), F32)],
        compiler_params=_cparams("parallel"),
        name="ssd_mixer",
    )(proj, proj, proj, proj, dt_raw, conv_w, conv_b.reshape(1, conv_dim), lanes8(dt_bias), lanes8(a_log),
      jnp.repeat(d_skip, HEAD_DIM).reshape(1, width), norm_gain.reshape(1, width))


def _rope_tables(seq):
    half = HEAD_DIM // 2
    inv_freq = ROPE_THETA ** (-jnp.arange(half, dtype=F32) / half)
    ang = jnp.arange(seq, dtype=F32)[:, None] * inv_freq[None, :]
    sin = jnp.sin(ang)
    cos = jnp.tile(jnp.cos(ang), (1, LANES // half))
    sin_signed = jnp.tile(jnp.concatenate([-sin, sin], axis=1), (1, LANES // HEAD_DIM))
    return cos, sin_signed


def _rope_lanes(x, cos, sin_signed):
    width = x.shape[1]
    half = HEAD_DIM // 2
    lane = lax.broadcasted_iota(jnp.int32, x.shape, 1)
    partner = jnp.where(lane % HEAD_DIM < half, pltpu.roll(x, width - half, axis=1), pltpu.roll(x, half, axis=1))
    reps = width // LANES
    if reps > 1:
        cos = jnp.concatenate([cos] * reps, axis=1)
        sin_signed = jnp.concatenate([sin_signed] * reps, axis=1)
    return x * cos + partner * sin_signed


def _ret_body(q_ref, k_ref, v_ref, g_ref, cos_ref, sin_ref, dmat_ref, zeta_ref, xi_ref, cdec_ref, gain_ref,
              o_ref, state_ref):
    seq = q_ref.shape[1]
    nchunks = seq // CHUNK
    state_ref[...] = jnp.zeros_like(state_ref)

    def chunk(c, carry):
        for bb in range(q_ref.shape[0]):
            chunk_of(bb, c)
        return carry

    def chunk_of(bb, c):
        r0 = pl.multiple_of(c * CHUNK, CHUNK)
        rows = pl.ds(r0, CHUNK)
        cos, sin = cos_ref[rows, :], sin_ref[rows, :]
        q = _rope_lanes(q_ref[bb, rows, :].astype(F32), cos, sin).astype(BF16)
        k = _rope_lanes(k_ref[bb, rows, :].astype(F32), cos, sin) * HEAD_DIM ** -0.5
        k_t = k.T
        ys = []
        for h in range(N_HEADS):
            hs = slice(h * HEAD_DIM, (h + 1) * HEAD_DIM)
            qc = q[:, hs]
            kt = k_t[hs, :]
            vc = v_ref[bb, rows, hs]
            inner = _dot(qc, kt) * dmat_ref[h]
            prev = state_ref[bb, h]
            y = _dot(inner, vc) + _dot(qc, prev) * xi_ref[h]
            state_ref[bb, h] = prev * cdec_ref[h] + _dot(kt * zeta_ref[h], vc)
            mu = jnp.mean(y, axis=-1, keepdims=True)
            yc = y - mu
            var = jnp.mean(yc * yc, axis=-1, keepdims=True)
            ys.append(yc * lax.rsqrt(var + NORM_EPS))
        y = jnp.concatenate(ys, axis=-1) * gain_ref[...]
        o_ref[bb, rows, :] = (_silu(g_ref[bb, rows, :].astype(F32)) * y).astype(o_ref.dtype)

    lax.fori_loop(0, nchunks, chunk, 0)


def retention_mixer(proj, cols, rope_tables, norm_gain):
    bsz, seq, _ = proj.shape
    width = MIX_WIDTH
    log_gamma = jnp.log1p(-jnp.exp2(-5.0 - jnp.arange(N_HEADS, dtype=F32)))
    idx = jnp.arange(CHUNK, dtype=F32)
    rel = idx[:, None] - idx[None, :]
    dmat = jnp.where(rel >= 0, jnp.exp(jnp.maximum(rel, 0.0)[None] * log_gamma[:, None, None]), 0.0)
    zeta = jnp.exp((CHUNK - 1.0 - idx)[None, None, :] * log_gamma[:, None, None])
    xi = jnp.exp((idx + 1.0)[None, :, None] * log_gamma[:, None, None])
    cdec = jnp.exp(CHUNK * log_gamma)[:, None, None]
    nb = 1
    col = lambda blk: pl.BlockSpec((nb, seq, width), lambda b: (b, 0, blk))
    const = lambda shp: pl.BlockSpec(shp, lambda b: (0,) * len(shp))
    return pl.pallas_call(
        _ret_body,
        grid=(bsz // nb,),
        in_specs=[col(c) for c in cols] + [const((seq, LANES)), const((seq, LANES)),
                  const((N_HEADS, CHUNK, CHUNK)), const((N_HEADS, 1, CHUNK)),
                  const((N_HEADS, CHUNK, 1)), const((N_HEADS, 1, 1)), const((1, width))],
        out_specs=pl.BlockSpec((nb, seq, width), lambda b: (b, 0, 0)),
        out_shape=jax.ShapeDtypeStruct((bsz, seq, width), BF16),
        scratch_shapes=[pltpu.VMEM((nb, N_HEADS, HEAD_DIM, HEAD_DIM), F32)],
        compiler_params=_cparams("parallel"),
        name="retention_mixer",
    )(proj, proj, proj, proj, *rope_tables, dmat, zeta, xi, cdec, norm_gain.reshape(1, width))


def _softmax_pv(pieces, v_ones, hs):
    m = jnp.max(functools.reduce(jnp.maximum, pieces), axis=-1, keepdims=True)
    p = jnp.concatenate([jnp.exp(sp - m) for sp in pieces], axis=1).astype(BF16)
    o = jnp.dot(p, v_ones, preferred_element_type=F32)
    return (o / pltpu.roll(o, HEAD_DIM, axis=1))[:, hs]


def _values_with_ones(v_ref, hs, vo_ref):
    v = v_ref[0]
    lane = lax.broadcasted_iota(jnp.int32, v.shape, 1)
    vo_ref[...] = jnp.where((lane >= hs.start) & (lane < hs.stop), v, jnp.ones_like(v))


def _split_bf16(x):
    hi = x.astype(BF16)
    return hi, (x - hi.astype(F32)).astype(BF16)


def _rope_qk(q_ref, k_ref, cos_ref, sin_ref, qs_ref, kt_ref):
    cos, sin = cos_ref[...], sin_ref[...]
    qs_ref[...] = _rope_lanes(q_ref[0].astype(F32), cos, sin)
    kt_ref[...] = _rope_lanes(k_ref[0].astype(F32), cos, sin).T.astype(kt_ref.dtype)


def _moba_body(q_ref, k_ref, v_ref, cos_ref, sin_ref, o_ref, qs_ref, kt_ref, vo_ref):
    seq = q_ref.shape[1]
    nb = seq // ATT_BLOCK
    halves = ATT_BLOCK // LANES
    scale = HEAD_DIM ** -0.5
    heads = q_ref.shape[2] // HEAD_DIM
    n_sel = min(MOBA_TOPK, nb - 1)
    row = lax.broadcasted_iota(jnp.int32, (ATT_BLOCK, LANES), 0)
    col = lax.broadcasted_iota(jnp.int32, (ATT_BLOCK, LANES), 1)
    tril = [row >= col + hf * LANES for hf in range(halves)]
    _rope_qk(q_ref, k_ref, cos_ref, sin_ref, qs_ref, kt_ref)

    for h in range(heads):
        hs = slice(h * HEAD_DIM, (h + 1) * HEAD_DIM)
        kt = kt_ref[hs, :]
        kt_bf = kt.astype(BF16)
        _values_with_ones(v_ref, hs, vo_ref)
        reps = [jnp.broadcast_to(jnp.mean(kt[:, n * ATT_BLOCK:(n + 1) * ATT_BLOCK], axis=1, keepdims=True),
                                 (HEAD_DIM, LANES)) for n in range(nb - 1)]
        kr_hi, kr_lo = _split_bf16(jnp.concatenate(reps, axis=1))
        kr4 = jnp.concatenate([kr_hi, kr_lo, kr_hi, kr_lo], axis=0)

        for i in range(nb):
            rows = slice(i * ATT_BLOCK, (i + 1) * ATT_BLOCK)
            n_keys = (i + 1) * ATT_BLOCK
            q = qs_ref[rows, hs]
            sel = None
            if i > n_sel:
                q_hi, q_lo = _split_bf16(q)
                q4 = jnp.concatenate([q_hi, q_hi, q_lo, q_lo], axis=1)
                g = jnp.dot(q4, kr4[:, :i * LANES], preferred_element_type=F32)
                gs = [g[:, n * LANES:(n + 1) * LANES] for n in range(i)]
                cnt = [jnp.full((ATT_BLOCK, LANES), float(i - 1 - a), F32) for a in range(i)]
                for a in range(i):
                    for b in range(a + 1, i):
                        a_wins = jnp.where(gs[a] >= gs[b], 1.0, 0.0)
                        cnt[b] = cnt[b] + a_wins
                        cnt[a] = cnt[a] - a_wins
                sel = [c < n_sel for c in cnt]
            s = jnp.dot((q * scale).astype(BF16), kt_bf[:, :n_keys], preferred_element_type=F32)
            pieces = []
            for k in range(halves * (i + 1)):
                n, hf = divmod(k, halves)
                sp = s[:, k * LANES:(k + 1) * LANES]
                if n == i:
                    sp = jnp.where(tril[hf], sp, NEG_INF)
                elif sel is not None:
                    sp = jnp.where(sel[n], sp, NEG_INF)
                pieces.append(sp)
            o_ref[0, rows, hs] = _softmax_pv(pieces, vo_ref[:n_keys, :], hs).astype(o_ref.dtype)


def _dil_body(q_ref, k_ref, v_ref, cos_ref, sin_ref, lm_ref, o_ref, qs_ref, kt_ref, vo_ref):
    seq = q_ref.shape[1]
    nb = seq // ATT_BLOCK
    halves = ATT_BLOCK // LANES
    scale = HEAD_DIM ** -0.5
    heads = q_ref.shape[2] // HEAD_DIM
    _rope_qk(q_ref, k_ref, cos_ref, sin_ref, qs_ref, kt_ref)

    for h in range(heads):
        hs = slice(h * HEAD_DIM, (h + 1) * HEAD_DIM)
        kt_bf = kt_ref[hs, :]
        _values_with_ones(v_ref, hs, vo_ref)
        for i in range(nb):
            rows = slice(i * ATT_BLOCK, (i + 1) * ATT_BLOCK)
            n_keys = (i + 1) * ATT_BLOCK
            qs = (qs_ref[rows, hs] * scale).astype(BF16)
            s = jnp.dot(qs, kt_bf[:, :n_keys], preferred_element_type=F32)
            pieces = []
            for k in range(halves * (i + 1)):
                n, hf = divmod(k, halves)
                pieces.append(s[:, k * LANES:(k + 1) * LANES] + lm_ref[i - n, :, hf * LANES:(hf + 1) * LANES])
            o_ref[0, rows, hs] = _softmax_pv(pieces, vo_ref[:n_keys, :], hs).astype(o_ref.dtype)


def _dilation_log_multiplicity(seq):
    nb = seq // ATT_BLOCK
    r = jnp.arange(ATT_BLOCK)
    d = (jnp.arange(nb)[:, None, None] * ATT_BLOCK + r[None, :, None] - r[None, None, :])
    mult = jnp.zeros(d.shape, F32)
    for window, dil in DIL_PATTERNS:
        mult = mult + ((d >= 0) & (d <= window) & (d % dil == 0)).astype(F32)
    return jnp.where(mult > 0, jnp.log(jnp.maximum(mult, 1.0)), NEG_INF)


def _attention_mixer(body, name, kt_dtype, proj, cols, rope_tables, extra=()):
    bsz, seq, _ = proj.shape
    pair = lambda lane_blk: pl.BlockSpec((1, seq, LANES), lambda b, p: (b, 0, lane_blk + p))
    const = lambda shp: pl.BlockSpec(shp, lambda b, p: (0,) * len(shp))
    return pl.pallas_call(
        body,
        grid=(bsz, MIX_WIDTH // LANES),
        in_specs=[pair(c) for c in cols] + [const((seq, LANES)), const((seq, LANES))]
                 + [const(e.shape) for e in extra],
        out_specs=pl.BlockSpec((1, seq, LANES), lambda b, p: (b, 0, p)),
        out_shape=jax.ShapeDtypeStruct((bsz, seq, MIX_WIDTH), BF16),
        scratch_shapes=[pltpu.VMEM((seq, LANES), F32), pltpu.VMEM((LANES, seq), kt_dtype),
                        pltpu.VMEM((seq, LANES), BF16)],
        compiler_params=_cparams("parallel", "parallel"),
        name=name,
    )(proj, proj, proj, *rope_tables, *extra)


def moba_mixer(proj, cols, rope_tables):
    return _attention_mixer(_moba_body, "moba_mixer", F32, proj, cols, rope_tables)


def dilated_mixer(proj, cols, rope_tables):
    seq = proj.shape[1]
    return _attention_mixer(_dil_body, "dilated_mixer", BF16, proj, cols, rope_tables,
                            extra=(_dilation_log_multiplicity(seq),))


def _merge_xattn_body(h_ref, gates_ref, y0_ref, y1_ref, y2_ref, y3_ref, wb_ref, wout_ref,
                      gain_ref, wq_ref, k_ref, v_ref, wo_ref, o_ref):
    d = h_ref.shape[2]
    dh = d // X_HEADS
    merged = jnp.zeros(h_ref.shape[1:], F32)
    for i, y_ref in enumerate((y0_ref, y1_ref, y2_ref, y3_ref)):
        gate = 1.0 / (1.0 + jnp.exp(-gates_ref[0, :, i * d:(i + 1) * d].astype(F32)))
        merged = merged + gate * _dot(y_ref[0], wb_ref[i])
    h = h_ref[0] + _dot(merged, wout_ref[...])

    ms = jnp.mean(h * h, axis=-1, keepdims=True)
    hn = h * lax.rsqrt(ms + NORM_EPS) * gain_ref[...]
    q = _dot(hn, wq_ref[...]) * dh ** -0.5
    outs = []
    for a in range(X_HEADS):
        cs = slice(a * dh, (a + 1) * dh)
        s = _dot_nt(q[:, cs].astype(BF16), k_ref[0, :, cs])
        m = jnp.max(s, axis=-1, keepdims=True)
        e = jnp.exp(s - m)
        p = e / jnp.sum(e, axis=-1, keepdims=True)
        outs.append(_dot(p, v_ref[0, :, cs]))
    o = jnp.concatenate(outs, axis=-1)
    o_ref[0] = h + _dot(o, wo_ref[...])


def merge_cross_attention(h, proj, ys, w_branch, w_out, kv, gain, w_q, w_o, tq=512):
    bsz, seq, d = h.shape
    mlen = kv.shape[1]
    width = ys[0].shape[2]
    tq = min(tq, seq)
    rows = lambda w: pl.BlockSpec((1, tq, w), lambda b, i: (b, i, 0))
    const = lambda shp: pl.BlockSpec(shp, lambda b, i: (0,) * len(shp))
    return pl.pallas_call(
        _merge_xattn_body,
        grid=(bsz, seq // tq),
        in_specs=[rows(d), rows(N_BRANCHES * d)] + [rows(width)] * N_BRANCHES
                 + [const((N_BRANCHES, width, d)), const((d, d)), const((1, d)), const((d, d)),
                    pl.BlockSpec((1, mlen, d), lambda b, i: (b, 0, 0)),
                    pl.BlockSpec((1, mlen, d), lambda b, i: (b, 0, 1)),
                    const((d, d))],
        out_specs=rows(d),
        out_shape=jax.ShapeDtypeStruct((bsz, seq, d), F32),
        compiler_params=_cparams("parallel", "parallel"),
        name="merge_cross_attention",
    )(h, proj, *ys, w_branch, w_out, gain.reshape(1, d), w_q, kv, kv, w_o)


_SUBLANES = 8


def _sorting_network(n):
    pairs, p = [], 1
    while p < n:
        k = p
        while k >= 1:
            for j in range(k % p, n - k, 2 * k):
                for i in range(min(k, n - j - k)):
                    if (i + j) // (2 * p) == (i + j + k) // (2 * p):
                        pairs.append((i + j, i + j + k))
            k //= 2
        p *= 2
    return pairs


def _top_rows(vals, count):
    n_tiles = vals.shape[0] // _SUBLANES
    levels = [vals[_SUBLANES * g:_SUBLANES * (g + 1)] for g in range(n_tiles)]
    size = 1 << (n_tiles - 1).bit_length()
    neg = jnp.full(levels[0].shape, -jnp.inf, F32)
    levels += [neg] * (size - n_tiles)
    for a, b in _sorting_network(size):
        levels[a], levels[b] = jnp.maximum(levels[a], levels[b]), jnp.minimum(levels[a], levels[b])
    levels = levels[:n_tiles]
    tops = []
    for r in range(count):
        m = jnp.max(levels[0], axis=0, keepdims=True)
        tops.append(m)
        left = count - r - 1
        hit = levels[0] == m
        for g in range(min(len(levels), left)):
            below = levels[g + 1] if g + 1 < len(levels) else neg
            levels[g] = jnp.where(hit, below, levels[g])
    return tops


def _count_greater(sorted_rows, x):
    count = jnp.zeros(x.shape, F32)
    for r, row in enumerate(sorted_rows):
        count = jnp.where(row > x, float(r + 1), count)
    return count


_N_RANKS = PEER_TOPK + 1
_RANK_ROWS = 24


_N_CAND = sum(_N_RANKS // (i + 1) for i in range(_N_RANKS))
_CAND_ROWS = -(-_N_CAND // 8) * 8


def _pair_candidates(a1, a2_ref, cand_ref):
    cand_ref[_N_CAND // 8 * 8:, :] = jnp.full((_CAND_ROWS - _N_CAND // 8 * 8, cand_ref.shape[1]), -jnp.inf, F32)
    row = 0
    for i in range(_N_RANKS):
        n_j = _N_RANKS // (i + 1)
        cand_ref[row:row + n_j, :] = a1[i] + a2_ref[0:n_j, :]
        row += n_j
    return cand_ref[...]


def _router_body(h_ref, gain_ref, wq_hi_ref, wq_lo_ref, keys_ref, rank2_ref, e2_ref, cut_ref, e1_ref,
                 a2_ref, cand_ref):
    h = h_ref[...]
    ms = jnp.mean(h * h, axis=-1, keepdims=True)
    x_hi, x_lo = _split_bf16(h * lax.rsqrt(ms + NORM_EPS) * gain_ref[...])
    q = (jnp.dot(x_hi, wq_hi_ref[...], preferred_element_type=F32)
         + jnp.dot(x_hi, wq_lo_ref[...], preferred_element_type=F32)
         + jnp.dot(x_lo, wq_hi_ref[...], preferred_element_type=F32))
    dk = keys_ref.shape[-1] // 2
    for hd in range(PEER_HEADS):
        s = []
        for half in range(2):
            c0 = (hd * 2 + half) * dk
            q_hi, q_lo = _split_bf16(q[:, c0:c0 + dk])
            keys2 = keys_ref[hd, half]
            s.append(_dot_nt(keys2, jnp.concatenate([q_hi, q_hi], axis=1))
                     + _dot_nt(keys2, jnp.concatenate([q_lo, q_lo], axis=1)))
        s1, s2 = s
        a1 = _top_rows(s1, _N_RANKS)
        a2 = _top_rows(s2, _N_RANKS)
        for r in range(_N_RANKS):
            a2_ref[r:r + 1, :] = a2[r]
        cand = _pair_candidates(a1, a2_ref, cand_ref)
        top = _top_rows(cand, _N_RANKS)
        tau = 0.5 * (top[PEER_TOPK - 1] + top[PEER_TOPK])
        z = jnp.sum(jnp.where(cand > tau, jnp.exp(cand - (a1[0] + a2[0])), 0.0), axis=0, keepdims=True)
        rank2_ref[hd] = _count_greater(a2, s2).astype(rank2_ref.dtype)
        e2_ref[hd] = jnp.exp(s2 - a2[0]).astype(e2_ref.dtype)
        cut_ref[hd] = _count_greater(a2[:PEER_TOPK], tau - s1)
        e1_ref[hd] = jnp.exp(s1 - a1[0]) * (0.5 / z)


def peer_router(h, gain, w_q, sub_keys, tt=256):
    t, d = h.shape
    tt = min(tt, t)
    nk = sub_keys.shape[2]
    wq_hi, wq_lo = _split_bf16(w_q)
    keys2 = jnp.concatenate(_split_bf16(sub_keys), axis=-1)
    out = [jax.ShapeDtypeStruct((PEER_HEADS, nk, t), dt) for dt in (BF16, BF16, F32, F32)]
    ospec = pl.BlockSpec((PEER_HEADS, nk, tt), lambda i: (0, 0, i))
    return pl.pallas_call(
        _router_body,
        grid=(t // tt,),
        in_specs=[pl.BlockSpec((tt, d), lambda i: (i, 0)),
                  pl.BlockSpec((1, d), lambda i: (0, 0)),
                  pl.BlockSpec(w_q.shape, lambda i: (0, 0)),
                  pl.BlockSpec(w_q.shape, lambda i: (0, 0)),
                  pl.BlockSpec(keys2.shape, lambda i: (0, 0, 0, 0))],
        out_specs=[ospec] * 4,
        out_shape=out,
        scratch_shapes=[pltpu.VMEM((_RANK_ROWS, tt), F32), pltpu.VMEM((_CAND_ROWS, tt), F32)],
        compiler_params=_cparams("parallel"),
        name="peer_router",
    )(h, gain.reshape(1, d), wq_hi, wq_lo, keys2)


def _gelu_x2(x):
    return x * (1.0 + lax.erf(x * (2.0 ** -0.5)))


_ROWS_PER_STEP = 8
_J_CHUNK = 32
_ROW_GROUP = 4
_TOK_COLS = 256
_MM_SPLIT = 4
_BF16_ROWS = 16


def _experts_body(h_ref, gain_ref, u0_ref, un_ref, vt_ref, rank2_ref, e2_ref, cut_ref, e1_ref, o_ref,
                  xn_ref, acc_ref, act_a_ref, act_b_ref, p_a_ref, p_b_ref, cutb_ref, e1b_ref):
    step = pl.program_id(1)
    n_tiles = pl.num_programs(1) - 1
    nk = rank2_ref.shape[1]
    tt = h_ref.shape[0]
    n_cols = tt // _TOK_COLS
    act_refs = (act_a_ref, act_b_ref)
    p_refs = (p_a_ref, p_b_ref)

    @pl.when(step == 0)
    def _():
        h = h_ref[...]
        ms = jnp.mean(h * h, axis=-1, keepdims=True)
        xn_ref[...] = (h * lax.rsqrt(ms + NORM_EPS) * gain_ref[...]).astype(BF16)
        acc_ref[...] = jnp.zeros_like(acc_ref)
        p_refs[1][...] = jnp.zeros(p_refs[1].shape, BF16)
        act_refs[0][...] = _dot_nt(u0_ref[...], xn_ref[...])

    def main_block(act_cur, act_nxt, p_cur, p_prev):
        d_rows = acc_ref.shape[0] // _MM_SPLIT
        e_rows = un_ref.shape[0] // _MM_SPLIT

        def v_piece(k, c):
            rs, cs = slice(k * d_rows, (k + 1) * d_rows), slice(c * _TOK_COLS, (c + 1) * _TOK_COLS)
            acc_ref[rs, cs] += jnp.dot(vt_ref[rs, :], p_prev[:, cs], preferred_element_type=F32)

        def act_piece(k, c):
            rs, cs = slice(k * e_rows, (k + 1) * e_rows), slice(c * _TOK_COLS, (c + 1) * _TOK_COLS)
            act_nxt[rs, cs] = _dot_nt(un_ref[rs, :], xn_ref[cs, :])

        pieces = [(f, k, c) for k in range(_MM_SPLIT) for c in range(n_cols) for f in (v_piece, act_piece)]

        for hd in range(PEER_HEADS):
            cut8 = cut_ref[hd]
            e18 = e1_ref[hd]
            for r in range(_ROWS_PER_STEP):
                cutb_ref[hd, r] = jnp.broadcast_to(cut8[r:r + 1], (_BF16_ROWS, tt)).astype(BF16)
                e1b_ref[hd, r] = jnp.broadcast_to(e18[r:r + 1], (_BF16_ROWS, tt)).astype(BF16)

        reps = _J_CHUNK // _BF16_ROWS
        chunks = [(c, jc, rg) for c in range(n_cols) for jc in range(nk // _J_CHUNK)
                  for rg in range(_ROWS_PER_STEP // _ROW_GROUP)]
        per_chunk = -(-len(pieces) // len(chunks))
        for n, (c, jc, rg) in enumerate(chunks):
            for f, k, cc in pieces[n * per_chunk:(n + 1) * per_chunk]:
                f(k, cc)
            ts = slice(c * _TOK_COLS, (c + 1) * _TOK_COLS)
            js = slice(jc * _J_CHUNK, (jc + 1) * _J_CHUNK)
            gates = [jnp.zeros((_J_CHUNK, _TOK_COLS), BF16) for _ in range(_ROW_GROUP)]
            for hd in range(PEER_HEADS):
                rank2 = rank2_ref[hd, js, ts]
                e2 = e2_ref[hd, js, ts]
                for k in range(_ROW_GROUP):
                    r = rg * _ROW_GROUP + k
                    cut = jnp.concatenate([cutb_ref[hd, r, :, ts]] * reps, axis=0)
                    e1 = jnp.concatenate([e1b_ref[hd, r, :, ts]] * reps, axis=0)
                    gates[k] = gates[k] + jnp.where(rank2 < cut, e2 * e1, jnp.zeros_like(e2))
            for k in range(_ROW_GROUP):
                rows = slice((rg * _ROW_GROUP + k) * nk + jc * _J_CHUNK,
                             (rg * _ROW_GROUP + k) * nk + (jc + 1) * _J_CHUNK)
                p_cur[rows, ts] = _gelu_x2(act_cur[rows, ts]).astype(BF16) * gates[k]

    for parity in range(2):
        @pl.when((step % 2 == parity) & (step < n_tiles))
        def _():
            main_block(act_refs[parity], act_refs[1 - parity], p_refs[parity], p_refs[1 - parity])

    last_p = p_refs[(rank2_ref.shape[1] // _ROWS_PER_STEP - 1) % 2]

    @pl.when(step == n_tiles)
    def _():
        acc = acc_ref[...] + jnp.dot(vt_ref[...], last_p[...], preferred_element_type=F32)
        o_ref[...] = h_ref[...] + acc.T


def peer_experts(h, gain, u, v_t, rank2, e2, cut, e1, tt=512):
    t, d = h.shape
    nk = rank2.shape[1]
    tt = min(tt, t)
    et = _ROWS_PER_STEP * nk
    n_tiles = nk // _ROWS_PER_STEP
    rspec = pl.BlockSpec((PEER_HEADS, nk, tt), lambda i, s: (0, 0, i))
    row_spec = pl.BlockSpec((PEER_HEADS, _ROWS_PER_STEP, tt), lambda i, s: (0, jnp.minimum(s, n_tiles - 1), i))
    return pl.pallas_call(
        _experts_body,
        grid=(t // tt, n_tiles + 1),
        in_specs=[pl.BlockSpec((tt, d), lambda i, s: (i, 0)),
                  pl.BlockSpec((1, d), lambda i, s: (0, 0)),
                  pl.BlockSpec((et, d), lambda i, s: (0, 0)),
                  pl.BlockSpec((et, d), lambda i, s: (jnp.minimum(s + 1, n_tiles - 1), 0)),
                  pl.BlockSpec((None, d, et), lambda i, s: (jnp.maximum(s - 1, 0), 0, 0)),
                  rspec, rspec, row_spec, row_spec],
        out_specs=pl.BlockSpec((tt, d), lambda i, s: (i, 0)),
        out_shape=jax.ShapeDtypeStruct((t, d), F32),
        scratch_shapes=[pltpu.VMEM((tt, d), BF16), pltpu.VMEM((d, tt), F32),
                        pltpu.VMEM((et, tt), F32), pltpu.VMEM((et, tt), F32),
                        pltpu.VMEM((et, tt), BF16), pltpu.VMEM((et, tt), BF16),
                        pltpu.VMEM((PEER_HEADS, _ROWS_PER_STEP, _BF16_ROWS, tt), BF16),
                        pltpu.VMEM((PEER_HEADS, _ROWS_PER_STEP, _BF16_ROWS, tt), BF16)],
        compiler_params=_cparams("parallel", "arbitrary"),
        name="peer_experts",
    )(h, gain.reshape(1, d), u, u, v_t, rank2, e2, cut, e1)


def _transpose_cast_body(x_ref, o_ref):
    o_ref[...] = x_ref[...].T.astype(o_ref.dtype)


def transpose_cast(x3, index, dtype, tile, blk=512):
    _, r, c = x3.shape
    per = tile // blk
    return pl.pallas_call(
        _transpose_cast_body,
        grid=(r // blk, c // blk),
        in_specs=[pl.BlockSpec((None, blk, blk), lambda i, j: (index, i, j))],
        out_specs=pl.BlockSpec((None, blk, blk), lambda i, j: (i // per, j, i % per)),
        out_shape=jax.ShapeDtypeStruct((r // tile, c, tile), dtype),
        compiler_params=_cparams("parallel", "parallel"),
        name="transpose_cast",
    )(x3)


def _final_norm_body(x_ref, g_ref, o_ref):
    x = x_ref[...]
    ms = jnp.mean(x * x, axis=-1, keepdims=True)
    o_ref[...] = x * lax.rsqrt(ms + NORM_EPS) * g_ref[...]


def final_norm(x, gain, tm=1024):
    t, d = x.shape
    tm = min(tm, t)
    return pl.pallas_call(
        _final_norm_body,
        grid=(t // tm,),
        in_specs=[pl.BlockSpec((tm, d), lambda i: (i, 0)), pl.BlockSpec((1, d), lambda i: (0, 0))],
        out_specs=pl.BlockSpec((tm, d), lambda i: (i, 0)),
        out_shape=jax.ShapeDtypeStruct((t, d), F32),
        compiler_params=_cparams("parallel"),
        name="final_norm",
    )(x, gain.reshape(1, d))


_GATE_BLOCKS = N_BRANCHES * 8
_COL_Z = _GATE_BLOCKS
_COL_RET = _COL_Z + 4
_COL_MOBA = _COL_RET + 16
_COL_DIL = _COL_MOBA + 12
_COL_XBC = _COL_DIL + 12
_PROJ_BLOCKS = 84


def _pack_w_in(w_in_l):
    ssd_conv_dim = MIX_WIDTH + 2 * SSD_GROUPS * SSD_STATE
    o_xbc = MIX_WIDTH
    o_dt = o_xbc + ssd_conv_dim
    o_rest = o_dt + N_HEADS
    o_gate = o_rest + 10 * MIX_WIDTH
    d_model = w_in_l.shape[0]
    used = _COL_XBC * LANES + ssd_conv_dim
    pad = jnp.zeros((d_model, _PROJ_BLOCKS * LANES - used), w_in_l.dtype)
    main = jnp.concatenate([w_in_l[:, o_gate:], w_in_l[:, :o_xbc], w_in_l[:, o_rest:o_gate],
                            w_in_l[:, o_xbc:o_dt], pad], axis=1).astype(BF16)
    w_dt = jnp.pad(w_in_l[:, o_dt:o_rest], ((0, 0), (0, LANES - N_HEADS))).astype(BF16)
    return main, w_dt


def kernel(x, mem, mix_norm, w_in, ssd_conv_w, ssd_conv_b, ssd_dt_bias, ssd_a_log, ssd_d, ssd_norm, ret_norm, w_branch, w_out, x_norm, w_xq, w_xkv, w_xo, ffn_norm, w_pq, peer_sub_keys, peer_u, peer_v, final_norm_gain):
    bsz, seq, d = x.shape
    depth = w_in.shape[0]
    t = bsz * seq
    h = x.reshape(t, d)
    mem2 = mem.reshape(-1, d)
    rope_tables = _rope_tables(seq)

    for layer in range(depth):
        w_main, w_dt = _pack_w_in(w_in[layer])
        proj, dt_raw = norm_matmul(h, mix_norm[layer], w_main, w_dt, tm=1024, tn=1536, out_dtype=BF16)
        proj = proj.reshape(bsz, seq, -1)
        dt_raw = dt_raw.reshape(bsz, seq, LANES)

        y_a = ssd_mixer(proj, _COL_XBC, _COL_Z, dt_raw, ssd_conv_w[layer], ssd_conv_b[layer],
                        ssd_dt_bias[layer], ssd_a_log[layer], ssd_d[layer], ssd_norm[layer])
        y_b = retention_mixer(proj, [_COL_RET // 4 + i for i in range(4)], rope_tables, ret_norm[layer])
        y_c = moba_mixer(proj, [_COL_MOBA + 4 * i for i in range(3)], rope_tables)
        y_d = dilated_mixer(proj, [_COL_DIL + 4 * i for i in range(3)], rope_tables)
        kv = matmul(mem2, w_xkv[layer].astype(BF16), tm=512, tn=512, out_dtype=BF16)
        h = merge_cross_attention(h.reshape(bsz, seq, d), proj, (y_a, y_b, y_c, y_d),
                                  w_branch[layer].astype(BF16), w_out[layer].astype(BF16),
                                  kv.reshape(bsz, -1, 2 * d), x_norm[layer],
                                  w_xq[layer].astype(BF16), w_xo[layer].astype(BF16)).reshape(t, d)
        rank2, e2, cut, e1 = peer_router(h, ffn_norm[layer], w_pq[layer], peer_sub_keys[layer])
        h = peer_experts(h, ffn_norm[layer], peer_u[layer].astype(BF16),
                         transpose_cast(peer_v, layer, BF16, _ROWS_PER_STEP * PEER_NKEYS), rank2, e2, cut, e1)

    return final_norm(h, final_norm_gain).reshape(bsz, seq, d)
```

```python
import functools

import jax
import jax.numpy as jnp
from jax import lax
from jax.experimental import pallas as pl
from jax.experimental.pallas import tpu as pltpu

F32 = jnp.float32
BF16 = jnp.bfloat16
HI = lax.Precision.HIGHEST

NORM_EPS = 1e-6
NEG_INF = -1e30
ROPE_THETA = 10000.0
HEAD_DIM = 64
N_HEADS = 8
MIX_WIDTH = N_HEADS * HEAD_DIM
SSD_GROUPS = 2
SSD_STATE = 64
SSD_CONV = 4
CHUNK = 128
ATT_BLOCK = 256
MOBA_TOPK = 3
DIL_PATTERNS = ((128, 1), (512, 4), (2048, 16))
N_BRANCHES = 4
X_HEADS = 4
PEER_HEADS = 8
PEER_NKEYS = 128
PEER_TOPK = 16
LANES = 128
VMEM_LIMIT = 56 * 1024 * 1024


def _cparams(*sem):
    return pltpu.CompilerParams(dimension_semantics=sem, vmem_limit_bytes=VMEM_LIMIT)


def _dot(a, b):
    return jnp.dot(a.astype(BF16), b.astype(BF16), preferred_element_type=F32)


def _dot_nt(a, b):
    return lax.dot_general(a, b, (((1,), (1,)), ((), ())), preferred_element_type=F32)


def _silu(x):
    return x * (1.0 / (1.0 + jnp.exp(-x)))


def _norm_matmul_body(x_ref, g_ref, w_ref, ws_ref, o_ref, os_ref, xn_ref):
    @pl.when(pl.program_id(1) == 0)
    def _():
        x = x_ref[...]
        ms = jnp.mean(x * x, axis=-1, keepdims=True)
        xn_ref[...] = (x * lax.rsqrt(ms + NORM_EPS) * g_ref[...]).astype(xn_ref.dtype)
        os_ref[...] = jnp.dot(xn_ref[...], ws_ref[...], preferred_element_type=F32)

    o_ref[...] = jnp.dot(xn_ref[...], w_ref[...], preferred_element_type=F32).astype(o_ref.dtype)


def norm_matmul(x, gain, w, w_side, *, tm, tn, out_dtype):
    t, k = x.shape
    n = w.shape[1]
    ns = w_side.shape[1]
    assert t % tm == 0 and n % tn == 0
    return pl.pallas_call(
        _norm_matmul_body,
        grid=(t // tm, n // tn),
        in_specs=[pl.BlockSpec((tm, k), lambda i, j: (i, 0)),
                  pl.BlockSpec((1, k), lambda i, j: (0, 0)),
                  pl.BlockSpec((k, tn), lambda i, j: (0, j)),
                  pl.BlockSpec((k, ns), lambda i, j: (0, 0))],
        out_specs=[pl.BlockSpec((tm, tn), lambda i, j: (i, j)), pl.BlockSpec((tm, ns), lambda i, j: (i, 0))],
        out_shape=[jax.ShapeDtypeStruct((t, n), out_dtype), jax.ShapeDtypeStruct((t, ns), F32)],
        scratch_shapes=[pltpu.VMEM((tm, k), BF16)],
        compiler_params=_cparams("parallel", "arbitrary"),
        name="norm_matmul",
    )(x, gain.reshape(1, k), w, w_side)


def _matmul_body(x_ref, w_ref, o_ref):
    o_ref[...] = _dot(x_ref[...], w_ref[...]).astype(o_ref.dtype)


def matmul(x, w, *, tm, tn, out_dtype):
    t, k = x.shape
    n = w.shape[1]
    assert t % tm == 0 and n % tn == 0
    return pl.pallas_call(
        _matmul_body,
        grid=(t // tm, n // tn),
        in_specs=[pl.BlockSpec((tm, k), lambda i, j: (i, 0)), pl.BlockSpec((k, tn), lambda i, j: (0, j))],
        out_specs=pl.BlockSpec((tm, tn), lambda i, j: (i, j)),
        out_shape=jax.ShapeDtypeStruct((t, n), out_dtype),
        compiler_params=_cparams("parallel", "parallel"),
        name="matmul",
    )(x, w)


def _ssd_body(x_ref, b_ref, c_ref, z_ref, dtraw_ref, convw_ref, convb_ref, dtbias_ref, alog_ref,
              dskip_ref, gain_ref, o_ref, state_ref, prev_ref):
    seq = x_ref.shape[1]
    nchunks = seq // CHUNK
    rep = N_HEADS // SSD_GROUPS
    gn = SSD_GROUPS * SSD_STATE
    state_ref[...] = jnp.zeros_like(state_ref)
    prev_ref[...] = jnp.zeros_like(prev_ref)
    row = lax.broadcasted_iota(jnp.int32, (CHUNK, CHUNK), 0)
    col = lax.broadcasted_iota(jnp.int32, (CHUNK, CHUNK), 1)
    tril = row >= col
    tril_f = tril.astype(F32)
    conv_row = lax.broadcasted_iota(jnp.int32, (CHUNK, MIX_WIDTH + 2 * gn), 0)
    neg_a = -jnp.exp(alog_ref[...])

    def chunk(c, carry):
        for bb in range(x_ref.shape[0]):
            chunk_of(bb, c)
        return carry

    def chunk_of(bb, c):
        r0 = pl.multiple_of(c * CHUNK, CHUNK)
        rows = pl.ds(r0, CHUNK)
        raw = jnp.concatenate([x_ref[bb, rows, :], b_ref[bb, rows, :], c_ref[bb, rows, :]], axis=1).astype(F32)
        prev = prev_ref[bb]
        conv = raw * convw_ref[SSD_CONV - 1:SSD_CONV, :] + convb_ref[...]
        for k in range(1, SSD_CONV):
            shifted = jnp.where(conv_row < k, pltpu.roll(prev, k, axis=0), pltpu.roll(raw, k, axis=0))
            conv = conv + shifted * convw_ref[SSD_CONV - 1 - k:SSD_CONV - k, :]
        prev_ref[bb] = raw
        xbc = _silu(conv)
        xs, cm = xbc[:, :MIX_WIDTH], xbc[:, MIX_WIDTH + gn:]
        bm_t = xbc[:, MIX_WIDTH:MIX_WIDTH + gn].T
        pre = dtraw_ref[bb, rows, :] + dtbias_ref[...]
        dt = jnp.maximum(pre, 0.0) + jnp.log1p(jnp.exp(-jnp.abs(pre)))
        acum = jnp.dot(tril_f, dt * neg_a, precision=HI, preferred_element_type=F32)
        acum_t = acum.T
        ys = []
        for h in range(N_HEADS):
            g = h // rep
            hs = slice(h * HEAD_DIM, (h + 1) * HEAD_DIM)
            gs = slice(g * SSD_STATE, (g + 1) * SSD_STATE)
            a_col = acum[:, h:h + 1]
            a_row = acum_t[h:h + 1, :]
            a_last = acum[CHUNK - 1:CHUNK, h:h + 1]
            decay = jnp.exp(jnp.where(tril, a_col - a_row, -jnp.inf))
            cc = cm[:, gs]
            bt = bm_t[gs, :]
            x_h = xs[:, hs]
            xdt = x_h * dt[:, h:h + 1]
            scores = _dot(cc, bt) * decay
            prev = state_ref[bb, h]
            y = _dot(scores, xdt) + _dot(cc, prev) * jnp.exp(a_col)
            state_ref[bb, h] = prev * jnp.exp(a_last) + _dot(bt * jnp.exp(a_last - a_row), xdt)
            ys.append(y + x_h * dskip_ref[0:1, hs])
        y = jnp.concatenate(ys, axis=-1)
        y = y * _silu(z_ref[bb, rows, :].astype(F32))
        ms = jnp.mean(y * y, axis=-1, keepdims=True)
        o_ref[bb, rows, :] = (y * lax.rsqrt(ms + NORM_EPS) * gain_ref[...]).astype(o_ref.dtype)

    lax.fori_loop(0, nchunks, chunk, 0)


def ssd_mixer(proj, x_col, z_col, dt_raw, conv_w, conv_b, dt_bias, a_log, d_skip, norm_gain):
    bsz, seq, _ = proj.shape
    width = MIX_WIDTH
    gn = SSD_GROUPS * SSD_STATE
    conv_dim = width + 2 * gn
    lanes8 = lambda v: jnp.pad(v, (0, LANES - N_HEADS)).reshape(1, LANES)
    nb = 1
    wide = lambda blk: pl.BlockSpec((nb, seq, width), lambda b: (b, 0, blk))
    narrow = lambda blk: pl.BlockSpec((nb, seq, LANES), lambda b: (b, 0, blk))
    const = lambda shp: pl.BlockSpec(shp, lambda b: (0,) * len(shp))
    return pl.pallas_call(
        _ssd_body,
        grid=(bsz // nb,),
        in_specs=[wide(x_col // 4), narrow(x_col + 4), narrow(x_col + 5), wide(z_col // 4), narrow(0),
                  const((SSD_CONV, conv_dim)), const((1, conv_dim)), const((1, LANES)), const((1, LANES)),
                  const((1, width)), const((1, width))],
        out_specs=pl.BlockSpec((nb, seq, width), lambda b: (b, 0, 0)),
        out_shape=jax.ShapeDtypeStruct((bsz, seq, width), BF16),
        scratch_shapes=[pltpu.VMEM((nb, N_HEADS, SSD_STATE, HEAD_DIM), F32),
                        pltpu.VMEM((nb, CHUNK, conv_dim), F32)],
        compiler_params=_cparams("parallel"),
        name="ssd_mixer",
    )(proj, proj, proj, proj, dt_raw, conv_w, conv_b.reshape(1, conv_dim), lanes8(dt_bias), lanes8(a_log),
      jnp.repeat(d_skip, HEAD_DIM).reshape(1, width), norm_gain.reshape(1, width))


def _rope_tables(seq):
    half = HEAD_DIM // 2
    inv_freq = ROPE_THETA ** (-jnp.arange(half, dtype=F32) / half)
    ang = jnp.arange(seq, dtype=F32)[:, None] * inv_freq[None, :]
    sin = jnp.sin(ang)
    cos = jnp.tile(jnp.cos(ang), (1, LANES // half))
    sin_signed = jnp.tile(jnp.concatenate([-sin, sin], axis=1), (1, LANES // HEAD_DIM))
    return cos, sin_signed


def _rope_lanes(x, cos, sin_signed):
    width = x.shape[1]
    half = HEAD_DIM // 2
    lane = lax.broadcasted_iota(jnp.int32, x.shape, 1)
    partner = jnp.where(lane % HEAD_DIM < half, pltpu.roll(x, width - half, axis=1), pltpu.roll(x, half, axis=1))
    reps = width // LANES
    if reps > 1:
        cos = jnp.concatenate([cos] * reps, axis=1)
        sin_signed = jnp.concatenate([sin_signed] * reps, axis=1)
    return x * cos + partner * sin_signed


def _ret_body(q_ref, k_ref, v_ref, g_ref, cos_ref, sin_ref, dmat_ref, zeta_ref, xi_ref, cdec_ref, gain_ref,
              o_ref, state_ref):
    seq = q_ref.shape[1]
    nchunks = seq // CHUNK
    state_ref[...] = jnp.zeros_like(state_ref)

    def chunk(c, carry):
        for bb in range(q_ref.shape[0]):
            chunk_of(bb, c)
        return carry

    def chunk_of(bb, c):
        r0 = pl.multiple_of(c * CHUNK, CHUNK)
        rows = pl.ds(r0, CHUNK)
        cos, sin = cos_ref[rows, :], sin_ref[rows, :]
        q = _rope_lanes(q_ref[bb, rows, :].astype(F32), cos, sin).astype(BF16)
        k = _rope_lanes(k_ref[bb, rows, :].astype(F32), cos, sin) * HEAD_DIM ** -0.5
        k_t = k.T
        ys = []
        for h in range(N_HEADS):
            hs = slice(h * HEAD_DIM, (h + 1) * HEAD_DIM)
            qc = q[:, hs]
            kt = k_t[hs, :]
            vc = v_ref[bb, rows, hs]
            inner = _dot(qc, kt) * dmat_ref[h]
            prev = state_ref[bb, h]
            y = _dot(inner, vc) + _dot(qc, prev) * xi_ref[h]
            state_ref[bb, h] = prev * cdec_ref[h] + _dot(kt * zeta_ref[h], vc)
            mu = jnp.mean(y, axis=-1, keepdims=True)
            yc = y - mu
            var = jnp.mean(yc * yc, axis=-1, keepdims=True)
            ys.append(yc * lax.rsqrt(var + NORM_EPS))
        y = jnp.concatenate(ys, axis=-1) * gain_ref[...]
        o_ref[bb, rows, :] = (_silu(g_ref[bb, rows, :].astype(F32)) * y).astype(o_ref.dtype)

    lax.fori_loop(0, nchunks, chunk, 0)


def retention_mixer(proj, cols, rope_tables, norm_gain):
    bsz, seq, _ = proj.shape
    width = MIX_WIDTH
    log_gamma = jnp.log1p(-jnp.exp2(-5.0 - jnp.arange(N_HEADS, dtype=F32)))
    idx = jnp.arange(CHUNK, dtype=F32)
    rel = idx[:, None] - idx[None, :]
    dmat = jnp.where(rel >= 0, jnp.exp(jnp.maximum(rel, 0.0)[None] * log_gamma[:, None, None]), 0.0)
    zeta = jnp.exp((CHUNK - 1.0 - idx)[None, None, :] * log_gamma[:, None, None])
    xi = jnp.exp((idx + 1.0)[None, :, None] * log_gamma[:, None, None])
    cdec = jnp.exp(CHUNK * log_gamma)[:, None, None]
    nb = 1
    col = lambda blk: pl.BlockSpec((nb, seq, width), lambda b: (b, 0, blk))
    const = lambda shp: pl.BlockSpec(shp, lambda b: (0,) * len(shp))
    return pl.pallas_call(
        _ret_body,
        grid=(bsz // nb,),
        in_specs=[col(c) for c in cols] + [const((seq, LANES)), const((seq, LANES)),
                  const((N_HEADS, CHUNK, CHUNK)), const((N_HEADS, 1, CHUNK)),
                  const((N_HEADS, CHUNK, 1)), const((N_HEADS, 1, 1)), const((1, width))],
        out_specs=pl.BlockSpec((nb, seq, width), lambda b: (b, 0, 0)),
        out_shape=jax.ShapeDtypeStruct((bsz, seq, width), BF16),
        scratch_shapes=[pltpu.VMEM((nb, N_HEADS, HEAD_DIM, HEAD_DIM), F32)],
        compiler_params=_cparams("parallel"),
        name="retention_mixer",
    )(proj, proj, proj, proj, *rope_tables, dmat, zeta, xi, cdec, norm_gain.reshape(1, width))


def _softmax_pv(pieces, v_ones, hs):
    m = jnp.max(functools.reduce(jnp.maximum, pieces), axis=-1, keepdims=True)
    p = jnp.concatenate([jnp.exp(sp - m) for sp in pieces], axis=1).astype(BF16)
    o = jnp.dot(p, v_ones, preferred_element_type=F32)
    return (o / pltpu.roll(o, HEAD_DIM, axis=1))[:, hs]


def _values_with_ones(v_ref, hs, vo_ref):
    v = v_ref[0]
    lane = lax.broadcasted_iota(jnp.int32, v.shape, 1)
    vo_ref[...] = jnp.where((lane >= hs.start) & (lane < hs.stop), v, jnp.ones_like(v))


def _split_bf16(x):
    hi = x.astype(BF16)
    return hi, (x - hi.astype(F32)).astype(BF16)


def _rope_qk(q_ref, k_ref, cos_ref, sin_ref, qs_ref, kt_ref):
    cos, sin = cos_ref[...], sin_ref[...]
    qs_ref[...] = _rope_lanes(q_ref[0].astype(F32), cos, sin)
    kt_ref[...] = _rope_lanes(k_ref[0].astype(F32), cos, sin).T.astype(kt_ref.dtype)


def _moba_body(q_ref, k_ref, v_ref, cos_ref, sin_ref, o_ref, qs_ref, kt_ref, vo_ref):
    seq = q_ref.shape[1]
    nb = seq // ATT_BLOCK
    halves = ATT_BLOCK // LANES
    scale = HEAD_DIM ** -0.5
    heads = q_ref.shape[2] // HEAD_DIM
    n_sel = min(MOBA_TOPK, nb - 1)
    row = lax.broadcasted_iota(jnp.int32, (ATT_BLOCK, LANES), 0)
    col = lax.broadcasted_iota(jnp.int32, (ATT_BLOCK, LANES), 1)
    tril = [row >= col + hf * LANES for hf in range(halves)]
    _rope_qk(q_ref, k_ref, cos_ref, sin_ref, qs_ref, kt_ref)

    for h in range(heads):
        hs = slice(h * HEAD_DIM, (h + 1) * HEAD_DIM)
        kt = kt_ref[hs, :]
        kt_bf = kt.astype(BF16)
        _values_with_ones(v_ref, hs, vo_ref)
        reps = [jnp.broadcast_to(jnp.mean(kt[:, n * ATT_BLOCK:(n + 1) * ATT_BLOCK], axis=1, keepdims=True),
                                 (HEAD_DIM, LANES)) for n in range(nb - 1)]
        kr_hi, kr_lo = _split_bf16(jnp.concatenate(reps, axis=1))
        kr4 = jnp.concatenate([kr_hi, kr_lo, kr_hi, kr_lo], axis=0)

        for i in range(nb):
            rows = slice(i * ATT_BLOCK, (i + 1) * ATT_BLOCK)
            n_keys = (i + 1) * ATT_BLOCK
            q = qs_ref[rows, hs]
            sel = None
            if i > n_sel:
                q_hi, q_lo = _split_bf16(q)
                q4 = jnp.concatenate([q_hi, q_hi, q_lo, q_lo], axis=1)
                g = jnp.dot(q4, kr4[:, :i * LANES], preferred_element_type=F32)
                gs = [g[:, n * LANES:(n + 1) * LANES] for n in range(i)]
                cnt = [jnp.full((ATT_BLOCK, LANES), float(i - 1 - a), F32) for a in range(i)]
                for a in range(i):
                    for b in range(a + 1, i):
                        a_wins = jnp.where(gs[a] >= gs[b], 1.0, 0.0)
                        cnt[b] = cnt[b] + a_wins
                        cnt[a] = cnt[a] - a_wins
                sel = [c < n_sel for c in cnt]
            s = jnp.dot((q * scale).astype(BF16), kt_bf[:, :n_keys], preferred_element_type=F32)
            pieces = []
            for k in range(halves * (i + 1)):
                n, hf = divmod(k, halves)
                sp = s[:, k * LANES:(k + 1) * LANES]
                if n == i:
                    sp = jnp.where(tril[hf], sp, NEG_INF)
                elif sel is not None:
                    sp = jnp.where(sel[n], sp, NEG_INF)
                pieces.append(sp)
            o_ref[0, rows, hs] = _softmax_pv(pieces, vo_ref[:n_keys, :], hs).astype(o_ref.dtype)


def _dil_body(q_ref, k_ref, v_ref, cos_ref, sin_ref, lm_ref, o_ref, qs_ref, kt_ref, vo_ref):
    seq = q_ref.shape[1]
    nb = seq // ATT_BLOCK
    halves = ATT_BLOCK // LANES
    scale = HEAD_DIM ** -0.5
    heads = q_ref.shape[2] // HEAD_DIM
    _rope_qk(q_ref, k_ref, cos_ref, sin_ref, qs_ref, kt_ref)

    for h in range(heads):
        hs = slice(h * HEAD_DIM, (h + 1) * HEAD_DIM)
        kt_bf = kt_ref[hs, :]
        _values_with_ones(v_ref, hs, vo_ref)
        for i in range(nb):
            rows = slice(i * ATT_BLOCK, (i + 1) * ATT_BLOCK)
            n_keys = (i + 1) * ATT_BLOCK
            qs = (qs_ref[rows, hs] * scale).astype(BF16)
            s = jnp.dot(qs, kt_bf[:, :n_keys], preferred_element_type=F32)
            pieces = []
            for k in range(halves * (i + 1)):
                n, hf = divmod(k, halves)
                pieces.append(s[:, k * LANES:(k + 1) * LANES] + lm_ref[i - n, :, hf * LANES:(hf + 1) * LANES])
            o_ref[0, rows, hs] = _softmax_pv(pieces, vo_ref[:n_keys, :], hs).astype(o_ref.dtype)


def _dilation_log_multiplicity(seq):
    nb = seq // ATT_BLOCK
    r = jnp.arange(ATT_BLOCK)
    d = (jnp.arange(nb)[:, None, None] * ATT_BLOCK + r[None, :, None] - r[None, None, :])
    mult = jnp.zeros(d.shape, F32)
    for window, dil in DIL_PATTERNS:
        mult = mult + ((d >= 0) & (d <= window) & (d % dil == 0)).astype(F32)
    return jnp.where(mult > 0, jnp.log(jnp.maximum(mult, 1.0)), NEG_INF)


def _attention_mixer(body, name, kt_dtype, proj, cols, rope_tables, extra=()):
    bsz, seq, _ = proj.shape
    pair = lambda lane_blk: pl.BlockSpec((1, seq, LANES), lambda b, p: (b, 0, lane_blk + p))
    const = lambda shp: pl.BlockSpec(shp, lambda b, p: (0,) * len(shp))
    return pl.pallas_call(
        body,
        grid=(bsz, MIX_WIDTH // LANES),
        in_specs=[pair(c) for c in cols] + [const((seq, LANES)), const((seq, LANES))]
                 + [const(e.shape) for e in extra],
        out_specs=pl.BlockSpec((1, seq, LANES), lambda b, p: (b, 0, p)),
        out_shape=jax.ShapeDtypeStruct((bsz, seq, MIX_WIDTH), BF16),
        scratch_shapes=[pltpu.VMEM((seq, LANES), F32), pltpu.VMEM((LANES, seq), kt_dtype),
                        pltpu.VMEM((seq, LANES), BF16)],
        compiler_params=_cparams("parallel", "parallel"),
        name=name,
    )(proj, proj, proj, *rope_tables, *extra)


def moba_mixer(proj, cols, rope_tables):
    return _attention_mixer(_moba_body, "moba_mixer", F32, proj, cols, rope_tables)


def dilated_mixer(proj, cols, rope_tables):
    seq = proj.shape[1]
    return _attention_mixer(_dil_body, "dilated_mixer", BF16, proj, cols, rope_tables,
                            extra=(_dilation_log_multiplicity(seq),))


def _merge_xattn_body(h_ref, gates_ref, y0_ref, y1_ref, y2_ref, y3_ref, wb_ref, wout_ref,
                      gain_ref, wq_ref, k_ref, v_ref, wo_ref, o_ref):
    d = h_ref.shape[2]
    dh = d // X_HEADS
    merged = jnp.zeros(h_ref.shape[1:], F32)
    for i, y_ref in enumerate((y0_ref, y1_ref, y2_ref, y3_ref)):
        gate = 1.0 / (1.0 + jnp.exp(-gates_ref[0, :, i * d:(i + 1) * d].astype(F32)))
        merged = merged + gate * _dot(y_ref[0], wb_ref[i])
    h = h_ref[0] + _dot(merged, wout_ref[...])

    ms = jnp.mean(h * h, axis=-1, keepdims=True)
    hn = h * lax.rsqrt(ms + NORM_EPS) * gain_ref[...]
    q = _dot(hn, wq_ref[...]) * dh ** -0.5
    outs = []
    for a in range(X_HEADS):
        cs = slice(a * dh, (a + 1) * dh)
        s = _dot_nt(q[:, cs].astype(BF16), k_ref[0, :, cs])
        m = jnp.max(s, axis=-1, keepdims=True)
        e = jnp.exp(s - m)
        p = e / jnp.sum(e, axis=-1, keepdims=True)
        outs.append(_dot(p, v_ref[0, :, cs]))
    o = jnp.concatenate(outs, axis=-1)
    o_ref[0] = h + _dot(o, wo_ref[...])


def merge_cross_attention(h, proj, ys, w_branch, w_out, kv, gain, w_q, w_o, tq=512):
    bsz, seq, d = h.shape
    mlen = kv.shape[1]
    width = ys[0].shape[2]
    tq = min(tq, seq)
    rows = lambda w: pl.BlockSpec((1, tq, w), lambda b, i: (b, i, 0))
    const = lambda shp: pl.BlockSpec(shp, lambda b, i: (0,) * len(shp))
    return pl.pallas_call(
        _merge_xattn_body,
        grid=(bsz, seq // tq),
        in_specs=[rows(d), rows(N_BRANCHES * d)] + [rows(width)] * N_BRANCHES
                 + [const((N_BRANCHES, width, d)), const((d, d)), const((1, d)), const((d, d)),
                    pl.BlockSpec((1, mlen, d), lambda b, i: (b, 0, 0)),
                    pl.BlockSpec((1, mlen, d), lambda b, i: (b, 0, 1)),
                    const((d, d))],
        out_specs=rows(d),
        out_shape=jax.ShapeDtypeStruct((bsz, seq, d), F32),
        compiler_params=_cparams("parallel", "parallel"),
        name="merge_cross_attention",
    )(h, proj, *ys, w_branch, w_out, gain.reshape(1, d), w_q, kv, kv, w_o)


_SUBLANES = 8


def _sorting_network(n):
    pairs, p = [], 1
    while p < n:
        k = p
        while k >= 1:
            for j in range(k % p, n - k, 2 * k):
                for i in range(min(k, n - j - k)):
                    if (i + j) // (2 * p) == (i + j + k) // (2 * p):
                        pairs.append((i + j, i + j + k))
            k //= 2
        p *= 2
    return pairs


def _top_rows(vals, count):
    n_tiles = vals.shape[0] // _SUBLANES
    levels = [vals[_SUBLANES * g:_SUBLANES * (g + 1)] for g in range(n_tiles)]
    size = 1 << (n_tiles - 1).bit_length()
    neg = jnp.full(levels[0].shape, -jnp.inf, F32)
    levels += [neg] * (size - n_tiles)
    for a, b in _sorting_network(size):
        levels[a], levels[b] = jnp.maximum(levels[a], levels[b]), jnp.minimum(levels[a], levels[b])
    levels = levels[:n_tiles]
    tops = []
    for r in range(count):
        m = jnp.max(levels[0], axis=0, keepdims=True)
        tops.append(m)
        left = count - r - 1
        hit = levels[0] == m
        for g in range(min(len(levels), left)):
            below = levels[g + 1] if g + 1 < len(levels) else neg
            levels[g] = jnp.where(hit, below, levels[g])
    return tops


def _count_greater(sorted_rows, x):
    count = jnp.zeros(x.shape, F32)
    for r, row in enumerate(sorted_rows):
        count = jnp.where(row > x, float(r + 1), count)
    return count


_N_RANKS = PEER_TOPK + 1
_RANK_ROWS = 24


_N_CAND = sum(_N_RANKS // (i + 1) for i in range(_N_RANKS))
_CAND_ROWS = -(-_N_CAND // 8) * 8


def _pair_candidates(a1, a2_ref, cand_ref):
    cand_ref[_N_CAND // 8 * 8:, :] = jnp.full((_CAND_ROWS - _N_CAND // 8 * 8, cand_ref.shape[1]), -jnp.inf, F32)
    row = 0
    for i in range(_N_RANKS):
        n_j = _N_RANKS // (i + 1)
        cand_ref[row:row + n_j, :] = a1[i] + a2_ref[0:n_j, :]
        row += n_j
    return cand_ref[...]


def _router_body(h_ref, gain_ref, wq_hi_ref, wq_lo_ref, keys_ref, rank2_ref, e2_ref, cut_ref, e1_ref,
                 a2_ref, cand_ref):
    h = h_ref[...]
    ms = jnp.mean(h * h, axis=-1, keepdims=True)
    x_hi, x_lo = _split_bf16(h * lax.rsqrt(ms + NORM_EPS) * gain_ref[...])
    q = (jnp.dot(x_hi, wq_hi_ref[...], preferred_element_type=F32)
         + jnp.dot(x_hi, wq_lo_ref[...], preferred_element_type=F32)
         + jnp.dot(x_lo, wq_hi_ref[...], preferred_element_type=F32))
    dk = keys_ref.shape[-1] // 2
    for hd in range(PEER_HEADS):
        s = []
        for half in range(2):
            c0 = (hd * 2 + half) * dk
            q_hi, q_lo = _split_bf16(q[:, c0:c0 + dk])
            keys2 = keys_ref[hd, half]
            s.append(_dot_nt(keys2, jnp.concatenate([q_hi, q_hi], axis=1))
                     + _dot_nt(keys2, jnp.concatenate([q_lo, q_lo], axis=1)))
        s1, s2 = s
        a1 = _top_rows(s1, _N_RANKS)
        a2 = _top_rows(s2, _N_RANKS)
        for r in range(_N_RANKS):
            a2_ref[r:r + 1, :] = a2[r]
        cand = _pair_candidates(a1, a2_ref, cand_ref)
        top = _top_rows(cand, _N_RANKS)
        tau = 0.5 * (top[PEER_TOPK - 1] + top[PEER_TOPK])
        z = jnp.sum(jnp.where(cand > tau, jnp.exp(cand - (a1[0] + a2[0])), 0.0), axis=0, keepdims=True)
        rank2_ref[hd] = _count_greater(a2, s2).astype(rank2_ref.dtype)
        e2_ref[hd] = jnp.exp(s2 - a2[0]).astype(e2_ref.dtype)
        cut_ref[hd] = _count_greater(a2[:PEER_TOPK], tau - s1)
        e1_ref[hd] = jnp.exp(s1 - a1[0]) * (0.5 / z)


def peer_router(h, gain, w_q, sub_keys, tt=256):
    t, d = h.shape
    tt = min(tt, t)
    nk = sub_keys.shape[2]
    wq_hi, wq_lo = _split_bf16(w_q)
    keys2 = jnp.concatenate(_split_bf16(sub_keys), axis=-1)
    out = [jax.ShapeDtypeStruct((PEER_HEADS, nk, t), dt) for dt in (BF16, BF16, F32, F32)]
    ospec = pl.BlockSpec((PEER_HEADS, nk, tt), lambda i: (0, 0, i))
    return pl.pallas_call(
        _router_body,
        grid=(t // tt,),
        in_specs=[pl.BlockSpec((tt, d), lambda i: (i, 0)),
                  pl.BlockSpec((1, d), lambda i: (0, 0)),
                  pl.BlockSpec(w_q.shape, lambda i: (0, 0)),
                  pl.BlockSpec(w_q.shape, lambda i: (0, 0)),
                  pl.BlockSpec(keys2.shape, lambda i: (0, 0, 0, 0))],
        out_specs=[ospec] * 4,
        out_shape=out,
        scratch_shapes=[pltpu.VMEM((_RANK_ROWS, tt), F32), pltpu.VMEM((_CAND_ROWS, tt), F32)],
        compiler_params=_cparams("parallel"),
        name="peer_router",
    )(h, gain.reshape(1, d), wq_hi, wq_lo, keys2)


def _gelu_x2(x):
    return x * (1.0 + lax.erf(x * (2.0 ** -0.5)))


_ROWS_PER_STEP = 8
_J_CHUNK = 32
_ROW_GROUP = 4
_TOK_COLS = 256
_MM_SPLIT = 4
_BF16_ROWS = 16


def _experts_body(h_ref, gain_ref, u0_ref, un_ref, vt_ref, rank2_ref, e2_ref, cut_ref, e1_ref, o_ref,
                  xn_ref, acc_ref, act_a_ref, act_b_ref, p_a_ref, p_b_ref, cutb_ref, e1b_ref):
    step = pl.program_id(1)
    n_tiles = pl.num_programs(1) - 1
    nk = rank2_ref.shape[1]
    tt = h_ref.shape[0]
    n_cols = tt // _TOK_COLS
    act_refs = (act_a_ref, act_b_ref)
    p_refs = (p_a_ref, p_b_ref)

    @pl.when(step == 0)
    def _():
        h = h_ref[...]
        ms = jnp.mean(h * h, axis=-1, keepdims=True)
        xn_ref[...] = (h * lax.rsqrt(ms + NORM_EPS) * gain_ref[...]).astype(BF16)
        acc_ref[...] = jnp.zeros_like(acc_ref)
        p_refs[1][...] = jnp.zeros(p_refs[1].shape, BF16)
        act_refs[0][...] = _dot_nt(u0_ref[...], xn_ref[...])

    def main_block(act_cur, act_nxt, p_cur, p_prev):
        d_rows = acc_ref.shape[0] // _MM_SPLIT
        e_rows = un_ref.shape[0] // _MM_SPLIT

        def v_piece(k, c):
            rs, cs = slice(k * d_rows, (k + 1) * d_rows), slice(c * _TOK_COLS, (c + 1) * _TOK_COLS)
            acc_ref[rs, cs] += jnp.dot(vt_ref[rs, :], p_prev[:, cs], preferred_element_type=F32)

        def act_piece(k, c):
            rs, cs = slice(k * e_rows, (k + 1) * e_rows), slice(c * _TOK_COLS, (c + 1) * _TOK_COLS)
            act_nxt[rs, cs] = _dot_nt(un_ref[rs, :], xn_ref[cs, :])

        pieces = [(f, k, c) for k in range(_MM_SPLIT) for c in range(n_cols) for f in (v_piece, act_piece)]

        for hd in range(PEER_HEADS):
            cut8 = cut_ref[hd]
            e18 = e1_ref[hd]
            for r in range(_ROWS_PER_STEP):
                cutb_ref[hd, r] = jnp.broadcast_to(cut8[r:r + 1], (_BF16_ROWS, tt)).astype(BF16)
                e1b_ref[hd, r] = jnp.broadcast_to(e18[r:r + 1], (_BF16_ROWS, tt)).astype(BF16)

        reps = _J_CHUNK // _BF16_ROWS
        chunks = [(c, jc, rg) for c in range(n_cols) for jc in range(nk // _J_CHUNK)
                  for rg in range(_ROWS_PER_STEP // _ROW_GROUP)]
        per_chunk = -(-len(pieces) // len(chunks))
        for n, (c, jc, rg) in enumerate(chunks):
            for f, k, cc in pieces[n * per_chunk:(n + 1) * per_chunk]:
                f(k, cc)
            ts = slice(c * _TOK_COLS, (c + 1) * _TOK_COLS)
            js = slice(jc * _J_CHUNK, (jc + 1) * _J_CHUNK)
            gates = [jnp.zeros((_J_CHUNK, _TOK_COLS), BF16) for _ in range(_ROW_GROUP)]
            for hd in range(PEER_HEADS):
                rank2 = rank2_ref[hd, js, ts]
                e2 = e2_ref[hd, js, ts]
                for k in range(_ROW_GROUP):
                    r = rg * _ROW_GROUP + k
                    cut = jnp.concatenate([cutb_ref[hd, r, :, ts]] * reps, axis=0)
                    e1 = jnp.concatenate([e1b_ref[hd, r, :, ts]] * reps, axis=0)
                    gates[k] = gates[k] + jnp.where(rank2 < cut, e2 * e1, jnp.zeros_like(e2))
            for k in range(_ROW_GROUP):
                rows = slice((rg * _ROW_GROUP + k) * nk + jc * _J_CHUNK,
                             (rg * _ROW_GROUP + k) * nk + (jc + 1) * _J_CHUNK)
                p_cur[rows, ts] = _gelu_x2(act_cur[rows, ts]).astype(BF16) * gates[k]

    for parity in range(2):
        @pl.when((step % 2 == parity) & (step < n_tiles))
        def _():
            main_block(act_refs[parity], act_refs[1 - parity], p_refs[parity], p_refs[1 - parity])

    last_p = p_refs[(rank2_ref.shape[1] // _ROWS_PER_STEP - 1) % 2]

    @pl.when(step == n_tiles)
    def _():
        acc = acc_ref[...] + jnp.dot(vt_ref[...], last_p[...], preferred_element_type=F32)
        o_ref[...] = h_ref[...] + acc.T


def peer_experts(h, gain, u, v_t, rank2, e2, cut, e1, tt=512):
    t, d = h.shape
    nk = rank2.shape[1]
    tt = min(tt, t)
    et = _ROWS_PER_STEP * nk
    n_tiles = nk // _ROWS_PER_STEP
    rspec = pl.BlockSpec((PEER_HEADS, nk, tt), lambda i, s: (0, 0, i))
    row_spec = pl.BlockSpec((PEER_HEADS, _ROWS_PER_STEP, tt), lambda i, s: (0, jnp.minimum(s, n_tiles - 1), i))
    return pl.pallas_call(
        _experts_body,
        grid=(t // tt, n_tiles + 1),
        in_specs=[pl.BlockSpec((tt, d), lambda i, s: (i, 0)),
                  pl.BlockSpec((1, d), lambda i, s: (0, 0)),
                  pl.BlockSpec((et, d), lambda i, s: (0, 0)),
                  pl.BlockSpec((et, d), lambda i, s: (jnp.minimum(s + 1, n_tiles - 1), 0)),
                  pl.BlockSpec((None, d, et), lambda i, s: (jnp.maximum(s - 1, 0), 0, 0)),
                  rspec, rspec, row_spec, row_spec],
        out_specs=pl.BlockSpec((tt, d), lambda i, s: (i, 0)),
        out_shape=jax.ShapeDtypeStruct((t, d), F32),
        scratch_shapes=[pltpu.VMEM((tt, d), BF16), pltpu.VMEM((d, tt), F32),
                        pltpu.VMEM((et, tt), F32), pltpu.VMEM((et, tt), F32),
                        pltpu.VMEM((et, tt), BF16), pltpu.VMEM((et, tt), BF16),
                        pltpu.VMEM((PEER_HEADS, _ROWS_PER_STEP, _BF16_ROWS, tt), BF16),
                        pltpu.VMEM((PEER_HEADS, _ROWS_PER_STEP, _BF16_ROWS, tt), BF16)],
        compiler_params=_cparams("parallel", "arbitrary"),
        name="peer_experts",
    )(h, gain.reshape(1, d), u, u, v_t, rank2, e2, cut, e1)


def _transpose_cast_body(x_ref, o_ref):
    o_ref[...] = x_ref[...].T.astype(o_ref.dtype)


def transpose_cast(x3, index, dtype, tile, blk=1024):
    _, r, c = x3.shape
    per = tile // blk
    return pl.pallas_call(
        _transpose_cast_body,
        grid=(r // blk, c // blk),
        in_specs=[pl.BlockSpec((None, blk, blk), lambda i, j: (index, i, j))],
        out_specs=pl.BlockSpec((None, blk, blk), lambda i, j: (i // per, j, i % per)),
        out_shape=jax.ShapeDtypeStruct((r // tile, c, tile), dtype),
        compiler_params=_cparams("parallel", "parallel"),
        name="transpose_cast",
    )(x3)


def _final_norm_body(x_ref, g_ref, o_ref):
    x = x_ref[...]
    ms = jnp.mean(x * x, axis=-1, keepdims=True)
    o_ref[...] = x * lax.rsqrt(ms + NORM_EPS) * g_ref[...]


def final_norm(x, gain, tm=1024):
    t, d = x.shape
    tm = min(tm, t)
    return pl.pallas_call(
        _final_norm_body,
        grid=(t // tm,),
        in_specs=[pl.BlockSpec((tm, d), lambda i: (i, 0)), pl.BlockSpec((1, d), lambda i: (0, 0))],
        out_specs=pl.BlockSpec((tm, d), lambda i: (i, 0)),
        out_shape=jax.ShapeDtypeStruct((t, d), F32),
        compiler_params=_cparams("parallel"),
        name="final_norm",
    )(x, gain.reshape(1, d))


_GATE_BLOCKS = N_BRANCHES * 8
_COL_Z = _GATE_BLOCKS
_COL_RET = _COL_Z + 4
_COL_MOBA = _COL_RET + 16
_COL_DIL = _COL_MOBA + 12
_COL_XBC = _COL_DIL + 12
_PROJ_BLOCKS = 84


def _pack_w_in(w_in_l):
    ssd_conv_dim = MIX_WIDTH + 2 * SSD_GROUPS * SSD_STATE
    o_xbc = MIX_WIDTH
    o_dt = o_xbc + ssd_conv_dim
    o_rest = o_dt + N_HEADS
    o_gate = o_rest + 10 * MIX_WIDTH
    d_model = w_in_l.shape[0]
    used = _COL_XBC * LANES + ssd_conv_dim
    pad = jnp.zeros((d_model, _PROJ_BLOCKS * LANES - used), w_in_l.dtype)
    main = jnp.concatenate([w_in_l[:, o_gate:], w_in_l[:, :o_xbc], w_in_l[:, o_rest:o_gate],
                            w_in_l[:, o_xbc:o_dt], pad], axis=1).astype(BF16)
    w_dt = jnp.pad(w_in_l[:, o_dt:o_rest], ((0, 0), (0, LANES - N_HEADS))).astype(BF16)
    return main, w_dt


def kernel(x, mem, mix_norm, w_in, ssd_conv_w, ssd_conv_b, ssd_dt_bias, ssd_a_log, ssd_d, ssd_norm, ret_norm, w_branch, w_out, x_norm, w_xq, w_xkv, w_xo, ffn_norm, w_pq, peer_sub_keys, peer_u, peer_v, final_norm_gain):
    bsz, seq, d = x.shape
    depth = w_in.shape[0]
    t = bsz * seq
    h = x.reshape(t, d)
    mem2 = mem.reshape(-1, d)
    rope_tables = _rope_tables(seq)

    for layer in range(depth):
        w_main, w_dt = _pack_w_in(w_in[layer])
        proj, dt_raw = norm_matmul(h, mix_norm[layer], w_main, w_dt, tm=2048, tn=1536, out_dtype=BF16)
        proj = proj.reshape(bsz, seq, -1)
        dt_raw = dt_raw.reshape(bsz, seq, LANES)

        y_a = ssd_mixer(proj, _COL_XBC, _COL_Z, dt_raw, ssd_conv_w[layer], ssd_conv_b[layer],
                        ssd_dt_bias[layer], ssd_a_log[layer], ssd_d[layer], ssd_norm[layer])
        y_b = retention_mixer(proj, [_COL_RET // 4 + i for i in range(4)], rope_tables, ret_norm[layer])
        y_c = moba_mixer(proj, [_COL_MOBA + 4 * i for i in range(3)], rope_tables)
        y_d = dilated_mixer(proj, [_COL_DIL + 4 * i for i in range(3)], rope_tables)
        kv = matmul(mem2, w_xkv[layer].astype(BF16), tm=512, tn=512, out_dtype=BF16)
        h = merge_cross_attention(h.reshape(bsz, seq, d), proj, (y_a, y_b, y_c, y_d),
                                  w_branch[layer].astype(BF16), w_out[layer].astype(BF16),
                                  kv.reshape(bsz, -1, 2 * d), x_norm[layer],
                                  w_xq[layer].astype(BF16), w_xo[layer].astype(BF16)).reshape(t, d)
        rank2, e2, cut, e1 = peer_router(h, ffn_norm[layer], w_pq[layer], peer_sub_keys[layer])
        h = peer_experts(h, ffn_norm[layer], peer_u[layer].astype(BF16),
                         transpose_cast(peer_v, layer, BF16, _ROWS_PER_STEP * PEER_NKEYS), rank2, e2, cut, e1)

    return final_norm(h, final_norm_gain).reshape(bsz, seq, d)
```

```python
import functools

import jax
import jax.numpy as jnp
from jax import lax
from jax.experimental import pallas as pl
from jax.experimental.pallas import tpu as pltpu

F32 = jnp.float32
BF16 = jnp.bfloat16
HI = lax.Precision.HIGHEST

NORM_EPS = 1e-6
NEG_INF = -1e30
ROPE_THETA = 10000.0
HEAD_DIM = 64
N_HEADS = 8
MIX_WIDTH = N_HEADS * HEAD_DIM
SSD_GROUPS = 2
SSD_STATE = 64
SSD_CONV = 4
CHUNK = 128
ATT_BLOCK = 256
MOBA_TOPK = 3
DIL_PATTERNS = ((128, 1), (512, 4), (2048, 16))
N_BRANCHES = 4
X_HEADS = 4
PEER_HEADS = 8
PEER_NKEYS = 128
PEER_TOPK = 16
LANES = 128
VMEM_LIMIT = 56 * 1024 * 1024


def _cparams(*sem):
    return pltpu.CompilerParams(dimension_semantics=sem, vmem_limit_bytes=VMEM_LIMIT)


def _dot(a, b):
    return jnp.dot(a.astype(BF16), b.astype(BF16), preferred_element_type=F32)


def _dot_nt(a, b):
    return lax.dot_general(a, b, (((1,), (1,)), ((), ())), preferred_element_type=F32)


def _silu(x):
    return x * (1.0 / (1.0 + jnp.exp(-x)))


def _norm_matmul_body(x_ref, g_ref, w_ref, ws_ref, o_ref, os_ref, xn_ref):
    @pl.when(pl.program_id(1) == 0)
    def _():
        x = x_ref[...]
        ms = jnp.mean(x * x, axis=-1, keepdims=True)
        xn_ref[...] = (x * lax.rsqrt(ms + NORM_EPS) * g_ref[...]).astype(xn_ref.dtype)
        os_ref[...] = jnp.dot(xn_ref[...], ws_ref[...], preferred_element_type=F32)

    o_ref[...] = jnp.dot(xn_ref[...], w_ref[...], preferred_element_type=F32).astype(o_ref.dtype)


def norm_matmul(x, gain, w, w_side, *, tm, tn, out_dtype):
    t, k = x.shape
    n = w.shape[1]
    ns = w_side.shape[1]
    assert t % tm == 0 and n % tn == 0
    return pl.pallas_call(
        _norm_matmul_body,
        grid=(t // tm, n // tn),
        in_specs=[pl.BlockSpec((tm, k), lambda i, j: (i, 0)),
                  pl.BlockSpec((1, k), lambda i, j: (0, 0)),
                  pl.BlockSpec((k, tn), lambda i, j: (0, j)),
                  pl.BlockSpec((k, ns), lambda i, j: (0, 0))],
        out_specs=[pl.BlockSpec((tm, tn), lambda i, j: (i, j)), pl.BlockSpec((tm, ns), lambda i, j: (i, 0))],
        out_shape=[jax.ShapeDtypeStruct((t, n), out_dtype), jax.ShapeDtypeStruct((t, ns), F32)],
        scratch_shapes=[pltpu.VMEM((tm, k), BF16)],
        compiler_params=_cparams("parallel", "arbitrary"),
        name="norm_matmul",
    )(x, gain.reshape(1, k), w, w_side)


def _matmul_body(x_ref, w_ref, o_ref):
    o_ref[...] = _dot(x_ref[...], w_ref[...]).astype(o_ref.dtype)


def matmul(x, w, *, tm, tn, out_dtype):
    t, k = x.shape
    n = w.shape[1]
    assert t % tm == 0 and n % tn == 0
    return pl.pallas_call(
        _matmul_body,
        grid=(t // tm, n // tn),
        in_specs=[pl.BlockSpec((tm, k), lambda i, j: (i, 0)), pl.BlockSpec((k, tn), lambda i, j: (0, j))],
        out_specs=pl.BlockSpec((tm, tn), lambda i, j: (i, j)),
        out_shape=jax.ShapeDtypeStruct((t, n), out_dtype),
        compiler_params=_cparams("parallel", "parallel"),
        name="matmul",
    )(x, w)


def _ssd_body(x_ref, b_ref, c_ref, z_ref, dtraw_ref, convw_ref, convb_ref, dtbias_ref, alog_ref,
              dskip_ref, gain_ref, o_ref, state_ref, prev_ref):
    seq = x_ref.shape[1]
    nchunks = seq // CHUNK
    rep = N_HEADS // SSD_GROUPS
    gn = SSD_GROUPS * SSD_STATE
    state_ref[...] = jnp.zeros_like(state_ref)
    prev_ref[...] = jnp.zeros_like(prev_ref)
    row = lax.broadcasted_iota(jnp.int32, (CHUNK, CHUNK), 0)
    col = lax.broadcasted_iota(jnp.int32, (CHUNK, CHUNK), 1)
    tril = row >= col
    tril_f = tril.astype(F32)
    conv_row = lax.broadcasted_iota(jnp.int32, (CHUNK, MIX_WIDTH + 2 * gn), 0)
    neg_a = -jnp.exp(alog_ref[...])

    def chunk(c, carry):
        for bb in range(x_ref.shape[0]):
            chunk_of(bb, c)
        return carry

    def chunk_of(bb, c):
        r0 = pl.multiple_of(c * CHUNK, CHUNK)
        rows = pl.ds(r0, CHUNK)
        raw = jnp.concatenate([x_ref[bb, rows, :], b_ref[bb, rows, :], c_ref[bb, rows, :]], axis=1).astype(F32)
        prev = prev_ref[bb]
        conv = raw * convw_ref[SSD_CONV - 1:SSD_CONV, :] + convb_ref[...]
        for k in range(1, SSD_CONV):
            shifted = jnp.where(conv_row < k, pltpu.roll(prev, k, axis=0), pltpu.roll(raw, k, axis=0))
            conv = conv + shifted * convw_ref[SSD_CONV - 1 - k:SSD_CONV - k, :]
        prev_ref[bb] = raw
        xbc = _silu(conv)
        xs, cm = xbc[:, :MIX_WIDTH], xbc[:, MIX_WIDTH + gn:]
        bm_t = xbc[:, MIX_WIDTH:MIX_WIDTH + gn].T
        pre = dtraw_ref[bb, rows, :] + dtbias_ref[...]
        dt = jnp.maximum(pre, 0.0) + jnp.log1p(jnp.exp(-jnp.abs(pre)))
        acum = jnp.dot(tril_f, dt * neg_a, precision=HI, preferred_element_type=F32)
        acum_t = acum.T
        ys = []
        for h in range(N_HEADS):
            g = h // rep
            hs = slice(h * HEAD_DIM, (h + 1) * HEAD_DIM)
            gs = slice(g * SSD_STATE, (g + 1) * SSD_STATE)
            a_col = acum[:, h:h + 1]
            a_row = acum_t[h:h + 1, :]
            a_last = acum[CHUNK - 1:CHUNK, h:h + 1]
            decay = jnp.exp(jnp.where(tril, a_col - a_row, -jnp.inf))
            cc = cm[:, gs]
            bt = bm_t[gs, :]
            x_h = xs[:, hs]
            xdt = x_h * dt[:, h:h + 1]
            scores = _dot(cc, bt) * decay
            prev = state_ref[bb, h]
            y = _dot(scores, xdt) + _dot(cc, prev) * jnp.exp(a_col)
            state_ref[bb, h] = prev * jnp.exp(a_last) + _dot(bt * jnp.exp(a_last - a_row), xdt)
            ys.append(y + x_h * dskip_ref[0:1, hs])
        y = jnp.concatenate(ys, axis=-1)
        y = y * _silu(z_ref[bb, rows, :].astype(F32))
        ms = jnp.mean(y * y, axis=-1, keepdims=True)
        o_ref[bb, rows, :] = (y * lax.rsqrt(ms + NORM_EPS) * gain_ref[...]).astype(o_ref.dtype)

    lax.fori_loop(0, nchunks, chunk, 0)


def ssd_mixer(proj, x_col, z_col, dt_raw, conv_w, conv_b, dt_bias, a_log, d_skip, norm_gain):
    bsz, seq, _ = proj.shape
    width = MIX_WIDTH
    gn = SSD_GROUPS * SSD_STATE
    conv_dim = width + 2 * gn
    lanes8 = lambda v: jnp.pad(v, (0, LANES - N_HEADS)).reshape(1, LANES)
    nb = 1
    wide = lambda blk: pl.BlockSpec((nb, seq, width), lambda b: (b, 0, blk))
    narrow = lambda blk: pl.BlockSpec((nb, seq, LANES), lambda b: (b, 0, blk))
    const = lambda shp: pl.BlockSpec(shp, lambda b: (0,) * len(shp))
    return pl.pallas_call(
        _ssd_body,
        grid=(bsz // nb,),
        in_specs=[wide(x_col // 4), narrow(x_col + 4), narrow(x_col + 5), wide(z_col // 4), narrow(0),
                  const((SSD_CONV, conv_dim)), const((1, conv_dim)), const((1, LANES)), const((1, LANES)),
                  const((1, width)), const((1, width))],
        out_specs=pl.BlockSpec((nb, seq, width), lambda b: (b, 0, 0)),
        out_shape=jax.ShapeDtypeStruct((bsz, seq, width), BF16),
        scratch_shapes=[pltpu.VMEM((nb, N_HEADS, SSD_STATE, HEAD_DIM), F32),
                        pltpu.VMEM((nb, CHUNK, conv_dim), F32)],
        compiler_params=_cparams("parallel"),
        name="ssd_mixer",
    )(proj, proj, proj, proj, dt_raw, conv_w, conv_b.reshape(1, conv_dim), lanes8(dt_bias), lanes8(a_log),
      jnp.repeat(d_skip, HEAD_DIM).reshape(1, width), norm_gain.reshape(1, width))


def _rope_tables(seq):
    half = HEAD_DIM // 2
    inv_freq = ROPE_THETA ** (-jnp.arange(half, dtype=F32) / half)
    ang = jnp.arange(seq, dtype=F32)[:, None] * inv_freq[None, :]
    sin = jnp.sin(ang)
    cos = jnp.tile(jnp.cos(ang), (1, LANES // half))
    sin_signed = jnp.tile(jnp.concatenate([-sin, sin], axis=1), (1, LANES // HEAD_DIM))
    return cos, sin_signed


def _rope_lanes(x, cos, sin_signed):
    width = x.shape[1]
    half = HEAD_DIM // 2
    lane = lax.broadcasted_iota(jnp.int32, x.shape, 1)
    partner = jnp.where(lane % HEAD_DIM < half, pltpu.roll(x, width - half, axis=1), pltpu.roll(x, half, axis=1))
    reps = width // LANES
    if reps > 1:
        cos = jnp.concatenate([cos] * reps, axis=1)
        sin_signed = jnp.concatenate([sin_signed] * reps, axis=1)
    return x * cos + partner * sin_signed


def _ret_body(q_ref, k_ref, v_ref, g_ref, cos_ref, sin_ref, dmat_ref, zeta_ref, xi_ref, cdec_ref, gain_ref,
              o_ref, state_ref):
    seq = q_ref.shape[1]
    nchunks = seq // CHUNK
    state_ref[...] = jnp.zeros_like(state_ref)

    def chunk(c, carry):
        for bb in range(q_ref.shape[0]):
            chunk_of(bb, c)
        return carry

    def chunk_of(bb, c):
        r0 = pl.multiple_of(c * CHUNK, CHUNK)
        rows = pl.ds(r0, CHUNK)
        cos, sin = cos_ref[rows, :], sin_ref[rows, :]
        q = _rope_lanes(q_ref[bb, rows, :].astype(F32), cos, sin).astype(BF16)
        k = _rope_lanes(k_ref[bb, rows, :].astype(F32), cos, sin) * HEAD_DIM ** -0.5
        k_t = k.T
        ys = []
        for h in range(N_HEADS):
            hs = slice(h * HEAD_DIM, (h + 1) * HEAD_DIM)
            qc = q[:, hs]
            kt = k_t[hs, :]
            vc = v_ref[bb, rows, hs]
            inner = _dot(qc, kt) * dmat_ref[h]
            prev = state_ref[bb, h]
            y = _dot(inner, vc) + _dot(qc, prev) * xi_ref[h]
            state_ref[bb, h] = prev * cdec_ref[h] + _dot(kt * zeta_ref[h], vc)
            mu = jnp.mean(y, axis=-1, keepdims=True)
            yc = y - mu
            var = jnp.mean(yc * yc, axis=-1, keepdims=True)
            ys.append(yc * lax.rsqrt(var + NORM_EPS))
        y = jnp.concatenate(ys, axis=-1) * gain_ref[...]
        o_ref[bb, rows, :] = (_silu(g_ref[bb, rows, :].astype(F32)) * y).astype(o_ref.dtype)

    lax.fori_loop(0, nchunks, chunk, 0)


def retention_mixer(proj, cols, rope_tables, norm_gain):
    bsz, seq, _ = proj.shape
    width = MIX_WIDTH
    log_gamma = jnp.log1p(-jnp.exp2(-5.0 - jnp.arange(N_HEADS, dtype=F32)))
    idx = jnp.arange(CHUNK, dtype=F32)
    rel = idx[:, None] - idx[None, :]
    dmat = jnp.where(rel >= 0, jnp.exp(jnp.maximum(rel, 0.0)[None] * log_gamma[:, None, None]), 0.0)
    zeta = jnp.exp((CHUNK - 1.0 - idx)[None, None, :] * log_gamma[:, None, None])
    xi = jnp.exp((idx + 1.0)[None, :, None] * log_gamma[:, None, None])
    cdec = jnp.exp(CHUNK * log_gamma)[:, None, None]
    nb = 1
    col = lambda blk: pl.BlockSpec((nb, seq, width), lambda b: (b, 0, blk))
    const = lambda shp: pl.BlockSpec(shp, lambda b: (0,) * len(shp))
    return pl.pallas_call(
        _ret_body,
        grid=(bsz // nb,),
        in_specs=[col(c) for c in cols] + [const((seq, LANES)), const((seq, LANES)),
                  const((N_HEADS, CHUNK, CHUNK)), const((N_HEADS, 1, CHUNK)),
                  const((N_HEADS, CHUNK, 1)), const((N_HEADS, 1, 1)), const((1, width))],
        out_specs=pl.BlockSpec((nb, seq, width), lambda b: (b, 0, 0)),
        out_shape=jax.ShapeDtypeStruct((bsz, seq, width), BF16),
        scratch_shapes=[pltpu.VMEM((nb, N_HEADS, HEAD_DIM, HEAD_DIM), F32)],
        compiler_params=_cparams("parallel"),
        name="retention_mixer",
    )(proj, proj, proj, proj, *rope_tables, dmat, zeta, xi, cdec, norm_gain.reshape(1, width))


def _softmax_pv(pieces, v_ones, hs):
    m = jnp.max(functools.reduce(jnp.maximum, pieces), axis=-1, keepdims=True)
    p = jnp.concatenate([jnp.exp(sp - m) for sp in pieces], axis=1).astype(BF16)
    o = jnp.dot(p, v_ones, preferred_element_type=F32)
    return (o / pltpu.roll(o, HEAD_DIM, axis=1))[:, hs]


def _values_with_ones(v_ref, hs, vo_ref):
    v = v_ref[0]
    lane = lax.broadcasted_iota(jnp.int32, v.shape, 1)
    vo_ref[...] = jnp.where((lane >= hs.start) & (lane < hs.stop), v, jnp.ones_like(v))


def _split_bf16(x):
    hi = x.astype(BF16)
    return hi, (x - hi.astype(F32)).astype(BF16)


def _rope_qk(q_ref, k_ref, cos_ref, sin_ref, qs_ref, kt_ref):
    cos, sin = cos_ref[...], sin_ref[...]
    qs_ref[...] = _rope_lanes(q_ref[0].astype(F32), cos, sin)
    kt_ref[...] = _rope_lanes(k_ref[0].astype(F32), cos, sin).T.astype(kt_ref.dtype)


def _moba_body(q_ref, k_ref, v_ref, cos_ref, sin_ref, o_ref, qs_ref, kt_ref, vo_ref):
    seq = q_ref.shape[1]
    nb = seq // ATT_BLOCK
    halves = ATT_BLOCK // LANES
    scale = HEAD_DIM ** -0.5
    heads = q_ref.shape[2] // HEAD_DIM
    n_sel = min(MOBA_TOPK, nb - 1)
    row = lax.broadcasted_iota(jnp.int32, (ATT_BLOCK, LANES), 0)
    col = lax.broadcasted_iota(jnp.int32, (ATT_BLOCK, LANES), 1)
    tril = [row >= col + hf * LANES for hf in range(halves)]
    _rope_qk(q_ref, k_ref, cos_ref, sin_ref, qs_ref, kt_ref)

    for h in range(heads):
        hs = slice(h * HEAD_DIM, (h + 1) * HEAD_DIM)
        kt = kt_ref[hs, :]
        kt_bf = kt.astype(BF16)
        _values_with_ones(v_ref, hs, vo_ref)
        reps = [jnp.broadcast_to(jnp.mean(kt[:, n * ATT_BLOCK:(n + 1) * ATT_BLOCK], axis=1, keepdims=True),
                                 (HEAD_DIM, LANES)) for n in range(nb - 1)]
        kr_hi, kr_lo = _split_bf16(jnp.concatenate(reps, axis=1))
        kr4 = jnp.concatenate([kr_hi, kr_lo, kr_hi, kr_lo], axis=0)

        for i in range(nb):
            rows = slice(i * ATT_BLOCK, (i + 1) * ATT_BLOCK)
            n_keys = (i + 1) * ATT_BLOCK
            q = qs_ref[rows, hs]
            sel = None
            if i > n_sel:
                q_hi, q_lo = _split_bf16(q)
                q4 = jnp.concatenate([q_hi, q_hi, q_lo, q_lo], axis=1)
                g = jnp.dot(q4, kr4[:, :i * LANES], preferred_element_type=F32)
                gs = [g[:, n * LANES:(n + 1) * LANES] for n in range(i)]
                cnt = [jnp.full((ATT_BLOCK, LANES), float(i - 1 - a), F32) for a in range(i)]
                for a in range(i):
                    for b in range(a + 1, i):
                        a_wins = jnp.where(gs[a] >= gs[b], 1.0, 0.0)
                        cnt[b] = cnt[b] + a_wins
                        cnt[a] = cnt[a] - a_wins
                sel = [c < n_sel for c in cnt]
            s = jnp.dot((q * scale).astype(BF16), kt_bf[:, :n_keys], preferred_element_type=F32)
            pieces = []
            for k in range(halves * (i + 1)):
                n, hf = divmod(k, halves)
                sp = s[:, k * LANES:(k + 1) * LANES]
                if n == i:
                    sp = jnp.where(tril[hf], sp, NEG_INF)
                elif sel is not None:
                    sp = jnp.where(sel[n], sp, NEG_INF)
                pieces.append(sp)
            o_ref[0, rows, hs] = _softmax_pv(pieces, vo_ref[:n_keys, :], hs).astype(o_ref.dtype)


def _dil_body(q_ref, k_ref, v_ref, cos_ref, sin_ref, lm_ref, o_ref, qs_ref, kt_ref, vo_ref):
    seq = q_ref.shape[1]
    nb = seq // ATT_BLOCK
    halves = ATT_BLOCK // LANES
    scale = HEAD_DIM ** -0.5
    heads = q_ref.shape[2] // HEAD_DIM
    _rope_qk(q_ref, k_ref, cos_ref, sin_ref, qs_ref, kt_ref)

    for h in range(heads):
        hs = slice(h * HEAD_DIM, (h + 1) * HEAD_DIM)
        kt_bf = kt_ref[hs, :]
        _values_with_ones(v_ref, hs, vo_ref)
        for i in range(nb):
            rows = slice(i * ATT_BLOCK, (i + 1) * ATT_BLOCK)
            n_keys = (i + 1) * ATT_BLOCK
            qs = (qs_ref[rows, hs] * scale).astype(BF16)
            s = jnp.dot(qs, kt_bf[:, :n_keys], preferred_element_type=F32)
            pieces = []
            for k in range(halves * (i + 1)):
                n, hf = divmod(k, halves)
                pieces.append(s[:, k * LANES:(k + 1) * LANES] + lm_ref[i - n, :, hf * LANES:(hf + 1) * LANES])
            o_ref[0, rows, hs] = _softmax_pv(pieces, vo_ref[:n_keys, :], hs).astype(o_ref.dtype)


def _dilation_log_multiplicity(seq):
    nb = seq // ATT_BLOCK
    r = jnp.arange(ATT_BLOCK)
    d = (jnp.arange(nb)[:, None, None] * ATT_BLOCK + r[None, :, None] - r[None, None, :])
    mult = jnp.zeros(d.shape, F32)
    for window, dil in DIL_PATTERNS:
        mult = mult + ((d >= 0) & (d <= window) & (d % dil == 0)).astype(F32)
    return jnp.where(mult > 0, jnp.log(jnp.maximum(mult, 1.0)), NEG_INF)


def _attention_mixer(body, name, kt_dtype, proj, cols, rope_tables, extra=()):
    bsz, seq, _ = proj.shape
    pair = lambda lane_blk: pl.BlockSpec((1, seq, LANES), lambda b, p: (b, 0, lane_blk + p))
    const = lambda shp: pl.BlockSpec(shp, lambda b, p: (0,) * len(shp))
    return pl.pallas_call(
        body,
        grid=(bsz, MIX_WIDTH // LANES),
        in_specs=[pair(c) for c in cols] + [const((seq, LANES)), const((seq, LANES))]
                 + [const(e.shape) for e in extra],
        out_specs=pl.BlockSpec((1, seq, LANES), lambda b, p: (b, 0, p)),
        out_shape=jax.ShapeDtypeStruct((bsz, seq, MIX_WIDTH), BF16),
        scratch_shapes=[pltpu.VMEM((seq, LANES), F32), pltpu.VMEM((LANES, seq), kt_dtype),
                        pltpu.VMEM((seq, LANES), BF16)],
        compiler_params=_cparams("parallel", "parallel"),
        name=name,
    )(proj, proj, proj, *rope_tables, *extra)


def moba_mixer(proj, cols, rope_tables):
    return _attention_mixer(_moba_body, "moba_mixer", F32, proj, cols, rope_tables)


def dilated_mixer(proj, cols, rope_tables):
    seq = proj.shape[1]
    return _attention_mixer(_dil_body, "dilated_mixer", BF16, proj, cols, rope_tables,
                            extra=(_dilation_log_multiplicity(seq),))


def _merge_xattn_body(h_ref, gates_ref, y0_ref, y1_ref, y2_ref, y3_ref, wb_ref, wout_ref,
                      gain_ref, wq_ref, k_ref, v_ref, wo_ref, o_ref):
    d = h_ref.shape[2]
    dh = d // X_HEADS
    merged = jnp.zeros(h_ref.shape[1:], F32)
    for i, y_ref in enumerate((y0_ref, y1_ref, y2_ref, y3_ref)):
        gate = 1.0 / (1.0 + jnp.exp(-gates_ref[0, :, i * d:(i + 1) * d].astype(F32)))
        merged = merged + gate * _dot(y_ref[0], wb_ref[i])
    h = h_ref[0] + _dot(merged, wout_ref[...])

    ms = jnp.mean(h * h, axis=-1, keepdims=True)
    hn = h * lax.rsqrt(ms + NORM_EPS) * gain_ref[...]
    q = _dot(hn, wq_ref[...]) * dh ** -0.5
    outs = []
    for a in range(X_HEADS):
        cs = slice(a * dh, (a + 1) * dh)
        s = _dot_nt(q[:, cs].astype(BF16), k_ref[0, :, cs])
        m = jnp.max(s, axis=-1, keepdims=True)
        e = jnp.exp(s - m)
        p = e / jnp.sum(e, axis=-1, keepdims=True)
        outs.append(_dot(p, v_ref[0, :, cs]))
    o = jnp.concatenate(outs, axis=-1)
    o_ref[0] = h + _dot(o, wo_ref[...])


def merge_cross_attention(h, proj, ys, w_branch, w_out, kv, gain, w_q, w_o, tq=512):
    bsz, seq, d = h.shape
    mlen = kv.shape[1]
    width = ys[0].shape[2]
    tq = min(tq, seq)
    rows = lambda w: pl.BlockSpec((1, tq, w), lambda b, i: (b, i, 0))
    const = lambda shp: pl.BlockSpec(shp, lambda b, i: (0,) * len(shp))
    return pl.pallas_call(
        _merge_xattn_body,
        grid=(bsz, seq // tq),
        in_specs=[rows(d), rows(N_BRANCHES * d)] + [rows(width)] * N_BRANCHES
                 + [const((N_BRANCHES, width, d)), const((d, d)), const((1, d)), const((d, d)),
                    pl.BlockSpec((1, mlen, d), lambda b, i: (b, 0, 0)),
                    pl.BlockSpec((1, mlen, d), lambda b, i: (b, 0, 1)),
                    const((d, d))],
        out_specs=rows(d),
        out_shape=jax.ShapeDtypeStruct((bsz, seq, d), F32),
        compiler_params=_cparams("parallel", "parallel"),
        name="merge_cross_attention",
    )(h, proj, *ys, w_branch, w_out, gain.reshape(1, d), w_q, kv, kv, w_o)


_SUBLANES = 8


def _sorting_network(n):
    pairs, p = [], 1
    while p < n:
        k = p
        while k >= 1:
            for j in range(k % p, n - k, 2 * k):
                for i in range(min(k, n - j - k)):
                    if (i + j) // (2 * p) == (i + j + k) // (2 * p):
                        pairs.append((i + j, i + j + k))
            k //= 2
        p *= 2
    return pairs


def _top_rows(vals, count):
    n_tiles = vals.shape[0] // _SUBLANES
    levels = [vals[_SUBLANES * g:_SUBLANES * (g + 1)] for g in range(n_tiles)]
    size = 1 << (n_tiles - 1).bit_length()
    neg = jnp.full(levels[0].shape, -jnp.inf, F32)
    levels += [neg] * (size - n_tiles)
    for a, b in _sorting_network(size):
        levels[a], levels[b] = jnp.maximum(levels[a], levels[b]), jnp.minimum(levels[a], levels[b])
    levels = levels[:n_tiles]
    tops = []
    for r in range(count):
        m = jnp.max(levels[0], axis=0, keepdims=True)
        tops.append(m)
        left = count - r - 1
        hit = levels[0] == m
        for g in range(min(len(levels), left)):
            below = levels[g + 1] if g + 1 < len(levels) else neg
            levels[g] = jnp.where(hit, below, levels[g])
    return tops


def _count_greater(sorted_rows, x):
    count = jnp.zeros(x.shape, F32)
    for r, row in enumerate(sorted_rows):
        count = jnp.where(row > x, float(r + 1), count)
    return count


_N_RANKS = PEER_TOPK + 1
_RANK_ROWS = 24


_N_CAND = sum(_N_RANKS // (i + 1) for i in range(_N_RANKS))
_CAND_ROWS = -(-_N_CAND // 8) * 8


def _pair_candidates(a1, a2_ref, cand_ref):
    cand_ref[_N_CAND // 8 * 8:, :] = jnp.full((_CAND_ROWS - _N_CAND // 8 * 8, cand_ref.shape[1]), -jnp.inf, F32)
    row = 0
    for i in range(_N_RANKS):
        n_j = _N_RANKS // (i + 1)
        cand_ref[row:row + n_j, :] = a1[i] + a2_ref[0:n_j, :]
        row += n_j
    return cand_ref[...]


def _router_body(h_ref, gain_ref, wq_hi_ref, wq_lo_ref, keys_ref, rank2_ref, e2_ref, cut_ref, e1_ref,
                 a2_ref, cand_ref):
    h = h_ref[...]
    ms = jnp.mean(h * h, axis=-1, keepdims=True)
    x_hi, x_lo = _split_bf16(h * lax.rsqrt(ms + NORM_EPS) * gain_ref[...])
    q = (jnp.dot(x_hi, wq_hi_ref[...], preferred_element_type=F32)
         + jnp.dot(x_hi, wq_lo_ref[...], preferred_element_type=F32)
         + jnp.dot(x_lo, wq_hi_ref[...], preferred_element_type=F32))
    dk = keys_ref.shape[-1] // 2
    for hd in range(PEER_HEADS):
        s = []
        for half in range(2):
            c0 = (hd * 2 + half) * dk
            q_hi, q_lo = _split_bf16(q[:, c0:c0 + dk])
            keys2 = keys_ref[hd, half]
            s.append(_dot_nt(keys2, jnp.concatenate([q_hi, q_hi], axis=1))
                     + _dot_nt(keys2, jnp.concatenate([q_lo, q_lo], axis=1)))
        s1, s2 = s
        a1 = _top_rows(s1, _N_RANKS)
        a2 = _top_rows(s2, _N_RANKS)
        for r in range(_N_RANKS):
            a2_ref[r:r + 1, :] = a2[r]
        cand = _pair_candidates(a1, a2_ref, cand_ref)
        top = _top_rows(cand, _N_RANKS)
        tau = 0.5 * (top[PEER_TOPK - 1] + top[PEER_TOPK])
        z = jnp.sum(jnp.where(cand > tau, jnp.exp(cand - (a1[0] + a2[0])), 0.0), axis=0, keepdims=True)
        rank2_ref[hd] = _count_greater(a2, s2).astype(rank2_ref.dtype)
        e2_ref[hd] = jnp.exp(s2 - a2[0]).astype(e2_ref.dtype)
        cut_ref[hd] = _count_greater(a2[:PEER_TOPK], tau - s1)
        e1_ref[hd] = jnp.exp(s1 - a1[0]) * (0.5 / z)


def peer_router(h, gain, w_q, sub_keys, tt=256):
    t, d = h.shape
    tt = min(tt, t)
    nk = sub_keys.shape[2]
    wq_hi, wq_lo = _split_bf16(w_q)
    keys2 = jnp.concatenate(_split_bf16(sub_keys), axis=-1)
    out = [jax.ShapeDtypeStruct((PEER_HEADS, nk, t), dt) for dt in (BF16, BF16, F32, F32)]
    ospec = pl.BlockSpec((PEER_HEADS, nk, tt), lambda i: (0, 0, i))
    return pl.pallas_call(
        _router_body,
        grid=(t // tt,),
        in_specs=[pl.BlockSpec((tt, d), lambda i: (i, 0)),
                  pl.BlockSpec((1, d), lambda i: (0, 0)),
                  pl.BlockSpec(w_q.shape, lambda i: (0, 0)),
                  pl.BlockSpec(w_q.shape, lambda i: (0, 0)),
                  pl.BlockSpec(keys2.shape, lambda i: (0, 0, 0, 0))],
        out_specs=[ospec] * 4,
        out_shape=out,
        scratch_shapes=[pltpu.VMEM((_RANK_ROWS, tt), F32), pltpu.VMEM((_CAND_ROWS, tt), F32)],
        compiler_params=_cparams("parallel"),
        name="peer_router",
    )(h, gain.reshape(1, d), wq_hi, wq_lo, keys2)


def _gelu_x2(x):
    return x * (1.0 + lax.erf(x * (2.0 ** -0.5)))


_ROWS_PER_STEP = 8
_J_CHUNK = 32
_ROW_GROUP = 4
_TOK_COLS = 256
_MM_SPLIT = 4
_BF16_ROWS = 16


def _experts_body(h_ref, gain_ref, u0_ref, un_ref, vt_ref, rank2_ref, e2_ref, cut_ref, e1_ref, o_ref,
                  xn_ref, acc_ref, act_a_ref, act_b_ref, p_a_ref, p_b_ref, cutb_ref, e1b_ref):
    step = pl.program_id(1)
    n_tiles = pl.num_programs(1) - 1
    nk = rank2_ref.shape[1]
    tt = h_ref.shape[0]
    n_cols = tt // _TOK_COLS
    act_refs = (act_a_ref, act_b_ref)
    p_refs = (p_a_ref, p_b_ref)

    @pl.when(step == 0)
    def _():
        h = h_ref[...]
        ms = jnp.mean(h * h, axis=-1, keepdims=True)
        xn_ref[...] = (h * lax.rsqrt(ms + NORM_EPS) * gain_ref[...]).astype(BF16)
        acc_ref[...] = jnp.zeros_like(acc_ref)
        p_refs[1][...] = jnp.zeros(p_refs[1].shape, BF16)
        act_refs[0][...] = _dot_nt(u0_ref[...], xn_ref[...])

    def main_block(act_cur, act_nxt, p_cur, p_prev):
        d_rows = acc_ref.shape[0] // _MM_SPLIT
        e_rows = un_ref.shape[0] // _MM_SPLIT

        def v_piece(k):
            rs = slice(k * d_rows, (k + 1) * d_rows)
            acc_ref[rs, :] += jnp.dot(vt_ref[rs, :], p_prev[...], preferred_element_type=F32)

        def act_piece(k):
            rs = slice(k * e_rows, (k + 1) * e_rows)
            act_nxt[rs, :] = _dot_nt(un_ref[rs, :], xn_ref[...])

        pieces = [(f, k) for k in range(_MM_SPLIT) for f in (v_piece, act_piece)]

        for hd in range(PEER_HEADS):
            cut8 = cut_ref[hd]
            e18 = e1_ref[hd]
            for r in range(_ROWS_PER_STEP):
                cutb_ref[hd, r] = jnp.broadcast_to(cut8[r:r + 1], (_BF16_ROWS, tt)).astype(BF16)
                e1b_ref[hd, r] = jnp.broadcast_to(e18[r:r + 1], (_BF16_ROWS, tt)).astype(BF16)

        reps = _J_CHUNK // _BF16_ROWS
        chunks = [(c, jc, rg) for c in range(n_cols) for jc in range(nk // _J_CHUNK)
                  for rg in range(_ROWS_PER_STEP // _ROW_GROUP)]
        every = len(chunks) // len(pieces)
        for n, (c, jc, rg) in enumerate(chunks):
            if n % every == 0:
                f, k = pieces[n // every]
                f(k)
            ts = slice(c * _TOK_COLS, (c + 1) * _TOK_COLS)
            js = slice(jc * _J_CHUNK, (jc + 1) * _J_CHUNK)
            gates = [jnp.zeros((_J_CHUNK, _TOK_COLS), BF16) for _ in range(_ROW_GROUP)]
            for hd in range(PEER_HEADS):
                rank2 = rank2_ref[hd, js, ts]
                e2 = e2_ref[hd, js, ts]
                for k in range(_ROW_GROUP):
                    r = rg * _ROW_GROUP + k
                    cut = jnp.concatenate([cutb_ref[hd, r, :, ts]] * reps, axis=0)
                    e1 = jnp.concatenate([e1b_ref[hd, r, :, ts]] * reps, axis=0)
                    gates[k] = gates[k] + jnp.where(rank2 < cut, e2 * e1, jnp.zeros_like(e2))
            for k in range(_ROW_GROUP):
                rows = slice((rg * _ROW_GROUP + k) * nk + jc * _J_CHUNK,
                             (rg * _ROW_GROUP + k) * nk + (jc + 1) * _J_CHUNK)
                p_cur[rows, ts] = _gelu_x2(act_cur[rows, ts]).astype(BF16) * gates[k]

    for parity in range(2):
        @pl.when((step % 2 == parity) & (step < n_tiles))
        def _():
            main_block(act_refs[parity], act_refs[1 - parity], p_refs[parity], p_refs[1 - parity])

    last_p = p_refs[(rank2_ref.shape[1] // _ROWS_PER_STEP - 1) % 2]

    @pl.when(step == n_tiles)
    def _():
        acc = acc_ref[...] + jnp.dot(vt_ref[...], last_p[...], preferred_element_type=F32)
        o_ref[...] = h_ref[...] + acc.T


def peer_experts(h, gain, u, v_t, rank2, e2, cut, e1, tt=512):
    t, d = h.shape
    nk = rank2.shape[1]
    tt = min(tt, t)
    et = _ROWS_PER_STEP * nk
    n_tiles = nk // _ROWS_PER_STEP
    rspec = pl.BlockSpec((PEER_HEADS, nk, tt), lambda i, s: (0, 0, i))
    row_spec = pl.BlockSpec((PEER_HEADS, _ROWS_PER_STEP, tt), lambda i, s: (0, jnp.minimum(s, n_tiles - 1), i))
    return pl.pallas_call(
        _experts_body,
        grid=(t // tt, n_tiles + 1),
        in_specs=[pl.BlockSpec((tt, d), lambda i, s: (i, 0)),
                  pl.BlockSpec((1, d), lambda i, s: (0, 0)),
                  pl.BlockSpec((et, d), lambda i, s: (0, 0)),
                  pl.BlockSpec((et, d), lambda i, s: (jnp.minimum(s + 1, n_tiles - 1), 0)),
                  pl.BlockSpec((None, d, et), lambda i, s: (jnp.maximum(s - 1, 0), 0, 0)),
                  rspec, rspec, row_spec, row_spec],
        out_specs=pl.BlockSpec((tt, d), lambda i, s: (i, 0)),
        out_shape=jax.ShapeDtypeStruct((t, d), F32),
        scratch_shapes=[pltpu.VMEM((tt, d), BF16), pltpu.VMEM((d, tt), F32),
                        pltpu.VMEM((et, tt), F32), pltpu.VMEM((et, tt), F32),
                        pltpu.VMEM((et, tt), BF16), pltpu.VMEM((et, tt), BF16),
                        pltpu.VMEM((PEER_HEADS, _ROWS_PER_STEP, _BF16_ROWS, tt), BF16),
                        pltpu.VMEM((PEER_HEADS, _ROWS_PER_STEP, _BF16_ROWS, tt), BF16)],
        compiler_params=_cparams("parallel", "arbitrary"),
        name="peer_experts",
    )(h, gain.reshape(1, d), u, u, v_t, rank2, e2, cut, e1)


def _transpose_cast_body(x_ref, o_ref):
    o_ref[...] = x_ref[...].T.astype(o_ref.dtype)


def transpose_cast(x3, index, dtype, tile, blk=1024):
    _, r, c = x3.shape
    per = tile // blk
    return pl.pallas_call(
        _transpose_cast_body,
        grid=(r // blk, c // blk),
        in_specs=[pl.BlockSpec((None, blk, blk), lambda i, j: (index, i, j))],
        out_specs=pl.BlockSpec((None, blk, blk), lambda i, j: (i // per, j, i % per)),
        out_shape=jax.ShapeDtypeStruct((r // tile, c, tile), dtype),
        compiler_params=_cparams("parallel", "parallel"),
        name="transpose_cast",
    )(x3)


def _final_norm_body(x_ref, g_ref, o_ref):
    x = x_ref[...]
    ms = jnp.mean(x * x, axis=-1, keepdims=True)
    o_ref[...] = x * lax.rsqrt(ms + NORM_EPS) * g_ref[...]


def final_norm(x, gain, tm=1024):
    t, d = x.shape
    tm = min(tm, t)
    return pl.pallas_call(
        _final_norm_body,
        grid=(t // tm,),
        in_specs=[pl.BlockSpec((tm, d), lambda i: (i, 0)), pl.BlockSpec((1, d), lambda i: (0, 0))],
        out_specs=pl.BlockSpec((tm, d), lambda i: (i, 0)),
        out_shape=jax.ShapeDtypeStruct((t, d), F32),
        compiler_params=_cparams("parallel"),
        name="final_norm",
    )(x, gain.reshape(1, d))


_GATE_BLOCKS = N_BRANCHES * 8
_COL_Z = _GATE_BLOCKS
_COL_RET = _COL_Z + 4
_COL_MOBA = _COL_RET + 16
_COL_DIL = _COL_MOBA + 12
_COL_XBC = _COL_DIL + 12
_PROJ_BLOCKS = 84


def _pack_w_in(w_in_l):
    ssd_conv_dim = MIX_WIDTH + 2 * SSD_GROUPS * SSD_STATE
    o_xbc = MIX_WIDTH
    o_dt = o_xbc + ssd_conv_dim
    o_rest = o_dt + N_HEADS
    o_gate = o_rest + 10 * MIX_WIDTH
    d_model = w_in_l.shape[0]
    used = _COL_XBC * LANES + ssd_conv_dim
    pad = jnp.zeros((d_model, _PROJ_BLOCKS * LANES - used), w_in_l.dtype)
    main = jnp.concatenate([w_in_l[:, o_gate:], w_in_l[:, :o_xbc], w_in_l[:, o_rest:o_gate],
                            w_in_l[:, o_xbc:o_dt], pad], axis=1).astype(BF16)
    w_dt = jnp.pad(w_in_l[:, o_dt:o_rest], ((0, 0), (0, LANES - N_HEADS))).astype(BF16)
    return main, w_dt


def kernel(x, mem, mix_norm, w_in, ssd_conv_w, ssd_conv_b, ssd_dt_bias, ssd_a_log, ssd_d, ssd_norm, ret_norm, w_branch, w_out, x_norm, w_xq, w_xkv, w_xo, ffn_norm, w_pq, peer_sub_keys, peer_u, peer_v, final_norm_gain):
    bsz, seq, d = x.shape
    depth = w_in.shape[0]
    t = bsz * seq
    h = x.reshape(t, d)
    mem2 = mem.reshape(-1, d)
    rope_tables = _rope_tables(seq)

    for layer in range(depth):
        w_main, w_dt = _pack_w_in(w_in[layer])
        proj, dt_raw = norm_matmul(h, mix_norm[layer], w_main, w_dt, tm=2048, tn=1536, out_dtype=BF16)
        proj = proj.reshape(bsz, seq, -1)
        dt_raw = dt_raw.reshape(bsz, seq, LANES)

        y_a = ssd_mixer(proj, _COL_XBC, _COL_Z, dt_raw, ssd_conv_w[layer], ssd_conv_b[layer],
                        ssd_dt_bias[layer], ssd_a_log[layer], ssd_d[layer], ssd_norm[layer])
        y_b = retention_mixer(proj, [_COL_RET // 4 + i for i in range(4)], rope_tables, ret_norm[layer])
        y_c = moba_mixer(proj, [_COL_MOBA + 4 * i for i in range(3)], rope_tables)
        y_d = dilated_mixer(proj, [_COL_DIL + 4 * i for i in range(3)], rope_tables)
        kv = matmul(mem2, w_xkv[layer].astype(BF16), tm=512, tn=512, out_dtype=BF16)
        h = merge_cross_attention(h.reshape(bsz, seq, d), proj, (y_a, y_b, y_c, y_d),
                                  w_branch[layer].astype(BF16), w_out[layer].astype(BF16),
                                  kv.reshape(bsz, -1, 2 * d), x_norm[layer],
                                  w_xq[layer].astype(BF16), w_xo[layer].astype(BF16)).reshape(t, d)
        rank2, e2, cut, e1 = peer_router(h, ffn_norm[layer], w_pq[layer], peer_sub_keys[layer])
        h = peer_experts(h, ffn_norm[layer], peer_u[layer].astype(BF16),
                         transpose_cast(peer_v, layer, BF16, _ROWS_PER_STEP * PEER_NKEYS), rank2, e2, cut, e1)

    return final_norm(h, final_norm_gain).reshape(bsz, seq, d)
```

```python
import functools

import jax
import jax.numpy as jnp
from jax import lax
from jax.experimental import pallas as pl
from jax.experimental.pallas import tpu as pltpu

F32 = jnp.float32
BF16 = jnp.bfloat16
HI = lax.Precision.HIGHEST

NORM_EPS = 1e-6
NEG_INF = -1e30
ROPE_THETA = 10000.0
HEAD_DIM = 64
N_HEADS = 8
MIX_WIDTH = N_HEADS * HEAD_DIM
SSD_GROUPS = 2
SSD_STATE = 64
SSD_CONV = 4
CHUNK = 128
ATT_BLOCK = 256
MOBA_TOPK = 3
DIL_PATTERNS = ((128, 1), (512, 4), (2048, 16))
N_BRANCHES = 4
X_HEADS = 4
PEER_HEADS = 8
PEER_NKEYS = 128
PEER_TOPK = 16
LANES = 128
VMEM_LIMIT = 56 * 1024 * 1024


def _cparams(*sem):
    return pltpu.CompilerParams(dimension_semantics=sem, vmem_limit_bytes=VMEM_LIMIT)


def _dot(a, b):
    return jnp.dot(a.astype(BF16), b.astype(BF16), preferred_element_type=F32)


def _dot_nt(a, b):
    return lax.dot_general(a, b, (((1,), (1,)), ((), ())), preferred_element_type=F32)


def _silu(x):
    return x * (1.0 / (1.0 + jnp.exp(-x)))


def _norm_matmul_body(x_ref, g_ref, w_ref, ws_ref, o_ref, os_ref, xn_ref):
    @pl.when(pl.program_id(1) == 0)
    def _():
        x = x_ref[...]
        ms = jnp.mean(x * x, axis=-1, keepdims=True)
        xn_ref[...] = (x * lax.rsqrt(ms + NORM_EPS) * g_ref[...]).astype(xn_ref.dtype)
        os_ref[...] = jnp.dot(xn_ref[...], ws_ref[...], preferred_element_type=F32)

    o_ref[...] = jnp.dot(xn_ref[...], w_ref[...], preferred_element_type=F32).astype(o_ref.dtype)


def norm_matmul(x, gain, w, w_side, *, tm, tn, out_dtype):
    t, k = x.shape
    n = w.shape[1]
    ns = w_side.shape[1]
    assert t % tm == 0 and n % tn == 0
    return pl.pallas_call(
        _norm_matmul_body,
        grid=(t // tm, n // tn),
        in_specs=[pl.BlockSpec((tm, k), lambda i, j: (i, 0)),
                  pl.BlockSpec((1, k), lambda i, j: (0, 0)),
                  pl.BlockSpec((k, tn), lambda i, j: (0, j)),
                  pl.BlockSpec((k, ns), lambda i, j: (0, 0))],
        out_specs=[pl.BlockSpec((tm, tn), lambda i, j: (i, j)), pl.BlockSpec((tm, ns), lambda i, j: (i, 0))],
        out_shape=[jax.ShapeDtypeStruct((t, n), out_dtype), jax.ShapeDtypeStruct((t, ns), F32)],
        scratch_shapes=[pltpu.VMEM((tm, k), BF16)],
        compiler_params=_cparams("parallel", "arbitrary"),
        name="norm_matmul",
    )(x, gain.reshape(1, k), w, w_side)


def _matmul_body(x_ref, w_ref, o_ref):
    o_ref[...] = _dot(x_ref[...], w_ref[...]).astype(o_ref.dtype)


def matmul(x, w, *, tm, tn, out_dtype):
    t, k = x.shape
    n = w.shape[1]
    assert t % tm == 0 and n % tn == 0
    return pl.pallas_call(
        _matmul_body,
        grid=(t // tm, n // tn),
        in_specs=[pl.BlockSpec((tm, k), lambda i, j: (i, 0)), pl.BlockSpec((k, tn), lambda i, j: (0, j))],
        out_specs=pl.BlockSpec((tm, tn), lambda i, j: (i, j)),
        out_shape=jax.ShapeDtypeStruct((t, n), out_dtype),
        compiler_params=_cparams("parallel", "parallel"),
        name="matmul",
    )(x, w)


def _ssd_body(x_ref, b_ref, c_ref, z_ref, dtraw_ref, convw_ref, convb_ref, dtbias_ref, alog_ref,
              dskip_ref, gain_ref, o_ref, state_ref, prev_ref):
    seq = x_ref.shape[1]
    nchunks = seq // CHUNK
    rep = N_HEADS // SSD_GROUPS
    gn = SSD_GROUPS * SSD_STATE
    state_ref[...] = jnp.zeros_like(state_ref)
    prev_ref[...] = jnp.zeros_like(prev_ref)
    row = lax.broadcasted_iota(jnp.int32, (CHUNK, CHUNK), 0)
    col = lax.broadcasted_iota(jnp.int32, (CHUNK, CHUNK), 1)
    tril = row >= col
    tril_f = tril.astype(F32)
    conv_row = lax.broadcasted_iota(jnp.int32, (CHUNK, MIX_WIDTH + 2 * gn), 0)
    neg_a = -jnp.exp(alog_ref[...])

    def chunk(c, carry):
        for bb in range(x_ref.shape[0]):
            chunk_of(bb, c)
        return carry

    def chunk_of(bb, c):
        r0 = pl.multiple_of(c * CHUNK, CHUNK)
        rows = pl.ds(r0, CHUNK)
        raw = jnp.concatenate([x_ref[bb, rows, :], b_ref[bb, rows, :], c_ref[bb, rows, :]], axis=1).astype(F32)
        prev = prev_ref[bb]
        conv = raw * convw_ref[SSD_CONV - 1:SSD_CONV, :] + convb_ref[...]
        for k in range(1, SSD_CONV):
            shifted = jnp.where(conv_row < k, pltpu.roll(prev, k, axis=0), pltpu.roll(raw, k, axis=0))
            conv = conv + shifted * convw_ref[SSD_CONV - 1 - k:SSD_CONV - k, :]
        prev_ref[bb] = raw
        xbc = _silu(conv)
        xs, cm = xbc[:, :MIX_WIDTH], xbc[:, MIX_WIDTH + gn:]
        bm_t = xbc[:, MIX_WIDTH:MIX_WIDTH + gn].T
        pre = dtraw_ref[bb, rows, :] + dtbias_ref[...]
        dt = jnp.maximum(pre, 0.0) + jnp.log1p(jnp.exp(-jnp.abs(pre)))
        acum = jnp.dot(tril_f, dt * neg_a, precision=HI, preferred_element_type=F32)
        acum_t = acum.T
        ys = []
        for h in range(N_HEADS):
            g = h // rep
            hs = slice(h * HEAD_DIM, (h + 1) * HEAD_DIM)
            gs = slice(g * SSD_STATE, (g + 1) * SSD_STATE)
            a_col = acum[:, h:h + 1]
            a_row = acum_t[h:h + 1, :]
            a_last = acum[CHUNK - 1:CHUNK, h:h + 1]
            decay = jnp.exp(jnp.where(tril, a_col - a_row, -jnp.inf))
            cc = cm[:, gs]
            bt = bm_t[gs, :]
            x_h = xs[:, hs]
            xdt = x_h * dt[:, h:h + 1]
            scores = _dot(cc, bt) * decay
            prev = state_ref[bb, h]
            y = _dot(scores, xdt) + _dot(cc, prev) * jnp.exp(a_col)
            state_ref[bb, h] = prev * jnp.exp(a_last) + _dot(bt * jnp.exp(a_last - a_row), xdt)
            ys.append(y + x_h * dskip_ref[0:1, hs])
        y = jnp.concatenate(ys, axis=-1)
        y = y * _silu(z_ref[bb, rows, :].astype(F32))
        ms = jnp.mean(y * y, axis=-1, keepdims=True)
        o_ref[bb, rows, :] = (y * lax.rsqrt(ms + NORM_EPS) * gain_ref[...]).astype(o_ref.dtype)

    lax.fori_loop(0, nchunks, chunk, 0)


def ssd_mixer(proj, x_col, z_col, dt_raw, conv_w, conv_b, dt_bias, a_log, d_skip, norm_gain):
    bsz, seq, _ = proj.shape
    width = MIX_WIDTH
    gn = SSD_GROUPS * SSD_STATE
    conv_dim = width + 2 * gn
    lanes8 = lambda v: jnp.pad(v, (0, LANES - N_HEADS)).reshape(1, LANES)
    nb = 1
    wide = lambda blk: pl.BlockSpec((nb, seq, width), lambda b: (b, 0, blk))
    narrow = lambda blk: pl.BlockSpec((nb, seq, LANES), lambda b: (b, 0, blk))
    const = lambda shp: pl.BlockSpec(shp, lambda b: (0,) * len(shp))
    return pl.pallas_call(
        _ssd_body,
        grid=(bsz // nb,),
        in_specs=[wide(x_col // 4), narrow(x_col + 4), narrow(x_col + 5), wide(z_col // 4), narrow(0),
                  const((SSD_CONV, conv_dim)), const((1, conv_dim)), const((1, LANES)), const((1, LANES)),
                  const((1, width)), const((1, width))],
        out_specs=pl.BlockSpec((nb, seq, width), lambda b: (b, 0, 0)),
        out_shape=jax.ShapeDtypeStruct((bsz, seq, width), BF16),
        scratch_shapes=[pltpu.VMEM((nb, N_HEADS, SSD_STATE, HEAD_DIM), F32),
                        pltpu.VMEM((nb, CHUNK, conv_dim), F32)],
        compiler_params=_cparams("parallel"),
        name="ssd_mixer",
    )(proj, proj, proj, proj, dt_raw, conv_w, conv_b.reshape(1, conv_dim), lanes8(dt_bias), lanes8(a_log),
      jnp.repeat(d_skip, HEAD_DIM).reshape(1, width), norm_gain.reshape(1, width))


def _rope_tables(seq):
    half = HEAD_DIM // 2
    inv_freq = ROPE_THETA ** (-jnp.arange(half, dtype=F32) / half)
    ang = jnp.arange(seq, dtype=F32)[:, None] * inv_freq[None, :]
    sin = jnp.sin(ang)
    cos = jnp.tile(jnp.cos(ang), (1, LANES // half))
    sin_signed = jnp.tile(jnp.concatenate([-sin, sin], axis=1), (1, LANES // HEAD_DIM))
    return cos, sin_signed


def _rope_lanes(x, cos, sin_signed):
    width = x.shape[1]
    half = HEAD_DIM // 2
    lane = lax.broadcasted_iota(jnp.int32, x.shape, 1)
    partner = jnp.where(lane % HEAD_DIM < half, pltpu.roll(x, width - half, axis=1), pltpu.roll(x, half, axis=1))
    reps = width // LANES
    if reps > 1:
        cos = jnp.concatenate([cos] * reps, axis=1)
        sin_signed = jnp.concatenate([sin_signed] * reps, axis=1)
    return x * cos + partner * sin_signed


def _ret_body(q_ref, k_ref, v_ref, g_ref, cos_ref, sin_ref, dmat_ref, zeta_ref, xi_ref, cdec_ref, gain_ref,
              o_ref, state_ref):
    seq = q_ref.shape[1]
    nchunks = seq // CHUNK
    state_ref[...] = jnp.zeros_like(state_ref)

    def chunk(c, carry):
        for bb in range(q_ref.shape[0]):
            chunk_of(bb, c)
        return carry

    def chunk_of(bb, c):
        r0 = pl.multiple_of(c * CHUNK, CHUNK)
        rows = pl.ds(r0, CHUNK)
        cos, sin = cos_ref[rows, :], sin_ref[rows, :]
        q = _rope_lanes(q_ref[bb, rows, :].astype(F32), cos, sin).astype(BF16)
        k = _rope_lanes(k_ref[bb, rows, :].astype(F32), cos, sin) * HEAD_DIM ** -0.5
        k_t = k.T
        ys = []
        for h in range(N_HEADS):
            hs = slice(h * HEAD_DIM, (h + 1) * HEAD_DIM)
            qc = q[:, hs]
            kt = k_t[hs, :]
            vc = v_ref[bb, rows, hs]
            inner = _dot(qc, kt) * dmat_ref[h]
            prev = state_ref[bb, h]
            y = _dot(inner, vc) + _dot(qc, prev) * xi_ref[h]
            state_ref[bb, h] = prev * cdec_ref[h] + _dot(kt * zeta_ref[h], vc)
            mu = jnp.mean(y, axis=-1, keepdims=True)
            yc = y - mu
            var = jnp.mean(yc * yc, axis=-1, keepdims=True)
            ys.append(yc * lax.rsqrt(var + NORM_EPS))
        y = jnp.concatenate(ys, axis=-1) * gain_ref[...]
        o_ref[bb, rows, :] = (_silu(g_ref[bb, rows, :].astype(F32)) * y).astype(o_ref.dtype)

    lax.fori_loop(0, nchunks, chunk, 0)


def retention_mixer(proj, cols, rope_tables, norm_gain):
    bsz, seq, _ = proj.shape
    width = MIX_WIDTH
    log_gamma = jnp.log1p(-jnp.exp2(-5.0 - jnp.arange(N_HEADS, dtype=F32)))
    idx = jnp.arange(CHUNK, dtype=F32)
    rel = idx[:, None] - idx[None, :]
    dmat = jnp.where(rel >= 0, jnp.exp(jnp.maximum(rel, 0.0)[None] * log_gamma[:, None, None]), 0.0)
    zeta = jnp.exp((CHUNK - 1.0 - idx)[None, None, :] * log_gamma[:, None, None])
    xi = jnp.exp((idx + 1.0)[None, :, None] * log_gamma[:, None, None])
    cdec = jnp.exp(CHUNK * log_gamma)[:, None, None]
    nb = 1
    col = lambda blk: pl.BlockSpec((nb, seq, width), lambda b: (b, 0, blk))
    const = lambda shp: pl.BlockSpec(shp, lambda b: (0,) * len(shp))
    return pl.pallas_call(
        _ret_body,
        grid=(bsz // nb,),
        in_specs=[col(c) for c in cols] + [const((seq, LANES)), const((seq, LANES)),
                  const((N_HEADS, CHUNK, CHUNK)), const((N_HEADS, 1, CHUNK)),
                  const((N_HEADS, CHUNK, 1)), const((N_HEADS, 1, 1)), const((1, width))],
        out_specs=pl.BlockSpec((nb, seq, width), lambda b: (b, 0, 0)),
        out_shape=jax.ShapeDtypeStruct((bsz, seq, width), BF16),
        scratch_shapes=[pltpu.VMEM((nb, N_HEADS, HEAD_DIM, HEAD_DIM), F32)],
        compiler_params=_cparams("parallel"),
        name="retention_mixer",
    )(proj, proj, proj, proj, *rope_tables, dmat, zeta, xi, cdec, norm_gain.reshape(1, width))


def _softmax_pv(pieces, v_ones, hs):
    m = jnp.max(functools.reduce(jnp.maximum, pieces), axis=-1, keepdims=True)
    p = jnp.concatenate([jnp.exp(sp - m) for sp in pieces], axis=1).astype(BF16)
    o = jnp.dot(p, v_ones, preferred_element_type=F32)
    return (o / pltpu.roll(o, HEAD_DIM, axis=1))[:, hs]


def _values_with_ones(v_ref, hs, vo_ref):
    v = v_ref[0]
    lane = lax.broadcasted_iota(jnp.int32, v.shape, 1)
    vo_ref[...] = jnp.where((lane >= hs.start) & (lane < hs.stop), v, jnp.ones_like(v))


def _split_bf16(x):
    hi = x.astype(BF16)
    return hi, (x - hi.astype(F32)).astype(BF16)


def _rope_qk(q_ref, k_ref, cos_ref, sin_ref, qs_ref, kt_ref):
    cos, sin = cos_ref[...], sin_ref[...]
    qs_ref[...] = _rope_lanes(q_ref[0].astype(F32), cos, sin)
    kt_ref[...] = _rope_lanes(k_ref[0].astype(F32), cos, sin).T.astype(kt_ref.dtype)


def _moba_body(q_ref, k_ref, v_ref, cos_ref, sin_ref, o_ref, qs_ref, kt_ref, vo_ref):
    seq = q_ref.shape[1]
    nb = seq // ATT_BLOCK
    halves = ATT_BLOCK // LANES
    scale = HEAD_DIM ** -0.5
    heads = q_ref.shape[2] // HEAD_DIM
    n_sel = min(MOBA_TOPK, nb - 1)
    row = lax.broadcasted_iota(jnp.int32, (ATT_BLOCK, LANES), 0)
    col = lax.broadcasted_iota(jnp.int32, (ATT_BLOCK, LANES), 1)
    tril = [row >= col + hf * LANES for hf in range(halves)]
    _rope_qk(q_ref, k_ref, cos_ref, sin_ref, qs_ref, kt_ref)

    for h in range(heads):
        hs = slice(h * HEAD_DIM, (h + 1) * HEAD_DIM)
        kt = kt_ref[hs, :]
        kt_bf = kt.astype(BF16)
        _values_with_ones(v_ref, hs, vo_ref)
        reps = [jnp.broadcast_to(jnp.mean(kt[:, n * ATT_BLOCK:(n + 1) * ATT_BLOCK], axis=1, keepdims=True),
                                 (HEAD_DIM, LANES)) for n in range(nb - 1)]
        kr_hi, kr_lo = _split_bf16(jnp.concatenate(reps, axis=1))
        kr4 = jnp.concatenate([kr_hi, kr_lo, kr_hi, kr_lo], axis=0)

        for i in range(nb):
            rows = slice(i * ATT_BLOCK, (i + 1) * ATT_BLOCK)
            n_keys = (i + 1) * ATT_BLOCK
            q = qs_ref[rows, hs]
            sel = None
            if i > n_sel:
                q_hi, q_lo = _split_bf16(q)
                q4 = jnp.concatenate([q_hi, q_hi, q_lo, q_lo], axis=1)
                g = jnp.dot(q4, kr4[:, :i * LANES], preferred_element_type=F32)
                gs = [g[:, n * LANES:(n + 1) * LANES] for n in range(i)]
                cnt = [jnp.full((ATT_BLOCK, LANES), float(i - 1 - a), F32) for a in range(i)]
                for a in range(i):
                    for b in range(a + 1, i):
                        a_wins = jnp.where(gs[a] >= gs[b], 1.0, 0.0)
                        cnt[b] = cnt[b] + a_wins
                        cnt[a] = cnt[a] - a_wins
                sel = [c < n_sel for c in cnt]
            s = jnp.dot((q * scale).astype(BF16), kt_bf[:, :n_keys], preferred_element_type=F32)
            pieces = []
            for k in range(halves * (i + 1)):
                n, hf = divmod(k, halves)
                sp = s[:, k * LANES:(k + 1) * LANES]
                if n == i:
                    sp = jnp.where(tril[hf], sp, NEG_INF)
                elif sel is not None:
                    sp = jnp.where(sel[n], sp, NEG_INF)
                pieces.append(sp)
            o_ref[0, rows, hs] = _softmax_pv(pieces, vo_ref[:n_keys, :], hs).astype(o_ref.dtype)


def _dil_body(q_ref, k_ref, v_ref, cos_ref, sin_ref, lm_ref, o_ref, qs_ref, kt_ref, vo_ref):
    seq = q_ref.shape[1]
    nb = seq // ATT_BLOCK
    halves = ATT_BLOCK // LANES
    scale = HEAD_DIM ** -0.5
    heads = q_ref.shape[2] // HEAD_DIM
    _rope_qk(q_ref, k_ref, cos_ref, sin_ref, qs_ref, kt_ref)

    for h in range(heads):
        hs = slice(h * HEAD_DIM, (h + 1) * HEAD_DIM)
        kt_bf = kt_ref[hs, :]
        _values_with_ones(v_ref, hs, vo_ref)
        for i in range(nb):
            rows = slice(i * ATT_BLOCK, (i + 1) * ATT_BLOCK)
            n_keys = (i + 1) * ATT_BLOCK
            qs = (qs_ref[rows, hs] * scale).astype(BF16)
            s = jnp.dot(qs, kt_bf[:, :n_keys], preferred_element_type=F32)
            pieces = []
            for k in range(halves * (i + 1)):
                n, hf = divmod(k, halves)
                pieces.append(s[:, k * LANES:(k + 1) * LANES] + lm_ref[i - n, :, hf * LANES:(hf + 1) * LANES])
            o_ref[0, rows, hs] = _softmax_pv(pieces, vo_ref[:n_keys, :], hs).astype(o_ref.dtype)


def _dilation_log_multiplicity(seq):
    nb = seq // ATT_BLOCK
    r = jnp.arange(ATT_BLOCK)
    d = (jnp.arange(nb)[:, None, None] * ATT_BLOCK + r[None, :, None] - r[None, None, :])
    mult = jnp.zeros(d.shape, F32)
    for window, dil in DIL_PATTERNS:
        mult = mult + ((d >= 0) & (d <= window) & (d % dil == 0)).astype(F32)
    return jnp.where(mult > 0, jnp.log(jnp.maximum(mult, 1.0)), NEG_INF)


def _attention_mixer(body, name, kt_dtype, proj, cols, rope_tables, extra=()):
    bsz, seq, _ = proj.shape
    pair = lambda lane_blk: pl.BlockSpec((1, seq, LANES), lambda b, p: (b, 0, lane_blk + p))
    const = lambda shp: pl.BlockSpec(shp, lambda b, p: (0,) * len(shp))
    return pl.pallas_call(
        body,
        grid=(bsz, MIX_WIDTH // LANES),
        in_specs=[pair(c) for c in cols] + [const((seq, LANES)), const((seq, LANES))]
                 + [const(e.shape) for e in extra],
        out_specs=pl.BlockSpec((1, seq, LANES), lambda b, p: (b, 0, p)),
        out_shape=jax.ShapeDtypeStruct((bsz, seq, MIX_WIDTH), BF16),
        scratch_shapes=[pltpu.VMEM((seq, LANES), F32), pltpu.VMEM((LANES, seq), kt_dtype),
                        pltpu.VMEM((seq, LANES), BF16)],
        compiler_params=_cparams("parallel", "parallel"),
        name=name,
    )(proj, proj, proj, *rope_tables, *extra)


def moba_mixer(proj, cols, rope_tables):
    return _attention_mixer(_moba_body, "moba_mixer", F32, proj, cols, rope_tables)


def dilated_mixer(proj, cols, rope_tables):
    seq = proj.shape[1]
    return _attention_mixer(_dil_body, "dilated_mixer", BF16, proj, cols, rope_tables,
                            extra=(_dilation_log_multiplicity(seq),))


def _merge_xattn_body(h_ref, gates_ref, y0_ref, y1_ref, y2_ref, y3_ref, wb_ref, wout_ref,
                      gain_ref, wq_ref, k_ref, v_ref, wo_ref, o_ref):
    d = h_ref.shape[2]
    dh = d // X_HEADS
    merged = jnp.zeros(h_ref.shape[1:], F32)
    for i, y_ref in enumerate((y0_ref, y1_ref, y2_ref, y3_ref)):
        gate = 1.0 / (1.0 + jnp.exp(-gates_ref[0, :, i * d:(i + 1) * d].astype(F32)))
        merged = merged + gate * _dot(y_ref[0], wb_ref[i])
    h = h_ref[0] + _dot(merged, wout_ref[...])

    ms = jnp.mean(h * h, axis=-1, keepdims=True)
    hn = h * lax.rsqrt(ms + NORM_EPS) * gain_ref[...]
    q = _dot(hn, wq_ref[...]) * dh ** -0.5
    outs = []
    for a in range(X_HEADS):
        cs = slice(a * dh, (a + 1) * dh)
        s = _dot_nt(q[:, cs].astype(BF16), k_ref[0, :, cs])
        m = jnp.max(s, axis=-1, keepdims=True)
        e = jnp.exp(s - m)
        p = e / jnp.sum(e, axis=-1, keepdims=True)
        outs.append(_dot(p, v_ref[0, :, cs]))
    o = jnp.concatenate(outs, axis=-1)
    o_ref[0] = h + _dot(o, wo_ref[...])


def merge_cross_attention(h, proj, ys, w_branch, w_out, kv, gain, w_q, w_o, tq=512):
    bsz, seq, d = h.shape
    mlen = kv.shape[1]
    width = ys[0].shape[2]
    tq = min(tq, seq)
    rows = lambda w: pl.BlockSpec((1, tq, w), lambda b, i: (b, i, 0))
    const = lambda shp: pl.BlockSpec(shp, lambda b, i: (0,) * len(shp))
    return pl.pallas_call(
        _merge_xattn_body,
        grid=(bsz, seq // tq),
        in_specs=[rows(d), rows(N_BRANCHES * d)] + [rows(width)] * N_BRANCHES
                 + [const((N_BRANCHES, width, d)), const((d, d)), const((1, d)), const((d, d)),
                    pl.BlockSpec((1, mlen, d), lambda b, i: (b, 0, 0)),
                    pl.BlockSpec((1, mlen, d), lambda b, i: (b, 0, 1)),
                    const((d, d))],
        out_specs=rows(d),
        out_shape=jax.ShapeDtypeStruct((bsz, seq, d), F32),
        compiler_params=_cparams("parallel", "parallel"),
        name="merge_cross_attention",
    )(h, proj, *ys, w_branch, w_out, gain.reshape(1, d), w_q, kv, kv, w_o)


_SUBLANES = 8


def _sorting_network(n):
    pairs, p = [], 1
    while p < n:
        k = p
        while k >= 1:
            for j in range(k % p, n - k, 2 * k):
                for i in range(min(k, n - j - k)):
                    if (i + j) // (2 * p) == (i + j + k) // (2 * p):
                        pairs.append((i + j, i + j + k))
            k //= 2
        p *= 2
    return pairs


def _top_rows(vals, count):
    n_tiles = vals.shape[0] // _SUBLANES
    levels = [vals[_SUBLANES * g:_SUBLANES * (g + 1)] for g in range(n_tiles)]
    size = 1 << (n_tiles - 1).bit_length()
    neg = jnp.full(levels[0].shape, -jnp.inf, F32)
    levels += [neg] * (size - n_tiles)
    for a, b in _sorting_network(size):
        levels[a], levels[b] = jnp.maximum(levels[a], levels[b]), jnp.minimum(levels[a], levels[b])
    levels = levels[:n_tiles]
    tops = []
    for r in range(count):
        m = jnp.max(levels[0], axis=0, keepdims=True)
        tops.append(m)
        left = count - r - 1
        hit = levels[0] == m
        for g in range(min(len(levels), left)):
            below = levels[g + 1] if g + 1 < len(levels) else neg
            levels[g] = jnp.where(hit, below, levels[g])
    return tops


def _count_greater(sorted_rows, x):
    count = jnp.zeros(x.shape, F32)
    for r, row in enumerate(sorted_rows):
        count = jnp.where(row > x, float(r + 1), count)
    return count


_N_RANKS = PEER_TOPK + 1
_RANK_ROWS = 24


_N_CAND = sum(_N_RANKS // (i + 1) for i in range(_N_RANKS))
_CAND_ROWS = -(-_N_CAND // 8) * 8


def _pair_candidates(a1, a2_ref, cand_ref):
    cand_ref[_N_CAND // 8 * 8:, :] = jnp.full((_CAND_ROWS - _N_CAND // 8 * 8, cand_ref.shape[1]), -jnp.inf, F32)
    row = 0
    for i in range(_N_RANKS):
        n_j = _N_RANKS // (i + 1)
        cand_ref[row:row + n_j, :] = a1[i] + a2_ref[0:n_j, :]
        row += n_j
    return cand_ref[...]


def _router_body(h_ref, gain_ref, wq_hi_ref, wq_lo_ref, keys_ref, rank2_ref, e2_ref, cut_ref, e1_ref,
                 a2_ref, cand_ref):
    h = h_ref[...]
    ms = jnp.mean(h * h, axis=-1, keepdims=True)
    x_hi, x_lo = _split_bf16(h * lax.rsqrt(ms + NORM_EPS) * gain_ref[...])
    q = (jnp.dot(x_hi, wq_hi_ref[...], preferred_element_type=F32)
         + jnp.dot(x_hi, wq_lo_ref[...], preferred_element_type=F32)
         + jnp.dot(x_lo, wq_hi_ref[...], preferred_element_type=F32))
    dk = keys_ref.shape[-1] // 2
    for hd in range(PEER_HEADS):
        s = []
        for half in range(2):
            c0 = (hd * 2 + half) * dk
            q_hi, q_lo = _split_bf16(q[:, c0:c0 + dk])
            keys2 = keys_ref[hd, half]
            s.append(_dot_nt(keys2, jnp.concatenate([q_hi, q_hi], axis=1))
                     + _dot_nt(keys2, jnp.concatenate([q_lo, q_lo], axis=1)))
        s1, s2 = s
        a1 = _top_rows(s1, _N_RANKS)
        a2 = _top_rows(s2, _N_RANKS)
        for r in range(_N_RANKS):
            a2_ref[r:r + 1, :] = a2[r]
        cand = _pair_candidates(a1, a2_ref, cand_ref)
        top = _top_rows(cand, _N_RANKS)
        tau = 0.5 * (top[PEER_TOPK - 1] + top[PEER_TOPK])
        z = jnp.sum(jnp.where(cand > tau, jnp.exp(cand - (a1[0] + a2[0])), 0.0), axis=0, keepdims=True)
        rank2_ref[hd] = _count_greater(a2[:PEER_TOPK], s2).astype(rank2_ref.dtype)
        e2_ref[hd] = jnp.exp(s2 - a2[0]).astype(e2_ref.dtype)
        cut_ref[hd] = _count_greater(a2[:PEER_TOPK], tau - s1)
        e1_ref[hd] = jnp.exp(s1 - a1[0]) * (0.5 / z)


def peer_router(h, gain, w_q, sub_keys, tt=256):
    t, d = h.shape
    tt = min(tt, t)
    nk = sub_keys.shape[2]
    wq_hi, wq_lo = _split_bf16(w_q)
    keys2 = jnp.concatenate(_split_bf16(sub_keys), axis=-1)
    out = [jax.ShapeDtypeStruct((PEER_HEADS, nk, t), dt) for dt in (BF16, BF16, F32, F32)]
    ospec = pl.BlockSpec((PEER_HEADS, nk, tt), lambda i: (0, 0, i))
    return pl.pallas_call(
        _router_body,
        grid=(t // tt,),
        in_specs=[pl.BlockSpec((tt, d), lambda i: (i, 0)),
                  pl.BlockSpec((1, d), lambda i: (0, 0)),
                  pl.BlockSpec(w_q.shape, lambda i: (0, 0)),
                  pl.BlockSpec(w_q.shape, lambda i: (0, 0)),
                  pl.BlockSpec(keys2.shape, lambda i: (0, 0, 0, 0))],
        out_specs=[ospec] * 4,
        out_shape=out,
        scratch_shapes=[pltpu.VMEM((_RANK_ROWS, tt), F32), pltpu.VMEM((_CAND_ROWS, tt), F32)],
        compiler_params=_cparams("parallel"),
        name="peer_router",
    )(h, gain.reshape(1, d), wq_hi, wq_lo, keys2)


def _gelu_x2(x):
    return x * (1.0 + lax.erf(x * (2.0 ** -0.5)))


_ROWS_PER_STEP = 8
_J_CHUNK = 64
_ROW_GROUP = 2
_TOK_COLS = 256
_MM_SPLIT = 4
_BF16_ROWS = 16


def _experts_body(h_ref, gain_ref, u0_ref, un_ref, vt_ref, rank2_ref, e2_ref, cut_ref, e1_ref, o_ref,
                  xn_ref, acc_ref, act_a_ref, act_b_ref, p_a_ref, p_b_ref, cutb_ref, e1b_ref):
    step = pl.program_id(1)
    n_tiles = pl.num_programs(1) - 1
    nk = rank2_ref.shape[1]
    tt = h_ref.shape[0]
    n_cols = tt // _TOK_COLS
    act_refs = (act_a_ref, act_b_ref)
    p_refs = (p_a_ref, p_b_ref)

    @pl.when(step == 0)
    def _():
        h = h_ref[...]
        ms = jnp.mean(h * h, axis=-1, keepdims=True)
        xn_ref[...] = (h * lax.rsqrt(ms + NORM_EPS) * gain_ref[...]).astype(BF16)
        acc_ref[...] = jnp.zeros_like(acc_ref)
        p_refs[1][...] = jnp.zeros(p_refs[1].shape, BF16)
        act_refs[0][...] = _dot_nt(u0_ref[...], xn_ref[...])

    def main_block(act_cur, act_nxt, p_cur, p_prev):
        d_rows = acc_ref.shape[0] // _MM_SPLIT
        e_rows = un_ref.shape[0] // _MM_SPLIT

        def v_piece(k, c):
            rs, cs = slice(k * d_rows, (k + 1) * d_rows), slice(c * _TOK_COLS, (c + 1) * _TOK_COLS)
            acc_ref[rs, cs] += jnp.dot(vt_ref[rs, :], p_prev[:, cs], preferred_element_type=F32)

        def act_piece(k, c):
            rs, cs = slice(k * e_rows, (k + 1) * e_rows), slice(c * _TOK_COLS, (c + 1) * _TOK_COLS)
            act_nxt[rs, cs] = _dot_nt(un_ref[rs, :], xn_ref[cs, :])

        pieces = [(f, k, c) for k in range(_MM_SPLIT) for c in range(n_cols) for f in (v_piece, act_piece)]

        for hd in range(PEER_HEADS):
            cut8 = cut_ref[hd]
            e18 = e1_ref[hd]
            for r in range(_ROWS_PER_STEP):
                cutb_ref[hd, r] = jnp.broadcast_to(cut8[r:r + 1], (_BF16_ROWS, tt)).astype(BF16)
                e1b_ref[hd, r] = jnp.broadcast_to(e18[r:r + 1], (_BF16_ROWS, tt)).astype(BF16)

        reps = _J_CHUNK // _BF16_ROWS
        chunks = [(c, jc, rg) for c in range(n_cols) for jc in range(nk // _J_CHUNK)
                  for rg in range(_ROWS_PER_STEP // _ROW_GROUP)]
        per_chunk = -(-len(pieces) // len(chunks))
        for n, (c, jc, rg) in enumerate(chunks):
            for f, k, cc in pieces[n * per_chunk:(n + 1) * per_chunk]:
                f(k, cc)
            ts = slice(c * _TOK_COLS, (c + 1) * _TOK_COLS)
            js = slice(jc * _J_CHUNK, (jc + 1) * _J_CHUNK)
            gates = [jnp.zeros((_J_CHUNK, _TOK_COLS), BF16) for _ in range(_ROW_GROUP)]
            for hd in range(PEER_HEADS):
                rank2 = rank2_ref[hd, js, ts]
                e2 = e2_ref[hd, js, ts]
                for k in range(_ROW_GROUP):
                    r = rg * _ROW_GROUP + k
                    cut = jnp.concatenate([cutb_ref[hd, r, :, ts]] * reps, axis=0)
                    e1 = jnp.concatenate([e1b_ref[hd, r, :, ts]] * reps, axis=0)
                    gates[k] = gates[k] + jnp.where(rank2 < cut, e2 * e1, jnp.zeros_like(e2))
            for k in range(_ROW_GROUP):
                rows = slice((rg * _ROW_GROUP + k) * nk + jc * _J_CHUNK,
                             (rg * _ROW_GROUP + k) * nk + (jc + 1) * _J_CHUNK)
                p_cur[rows, ts] = _gelu_x2(act_cur[rows, ts]).astype(BF16) * gates[k]

    for parity in range(2):
        @pl.when((step % 2 == parity) & (step < n_tiles))
        def _():
            main_block(act_refs[parity], act_refs[1 - parity], p_refs[parity], p_refs[1 - parity])

    last_p = p_refs[(rank2_ref.shape[1] // _ROWS_PER_STEP - 1) % 2]

    @pl.when(step == n_tiles)
    def _():
        acc = acc_ref[...] + jnp.dot(vt_ref[...], last_p[...], preferred_element_type=F32)
        o_ref[...] = h_ref[...] + acc.T


def peer_experts(h, gain, u, v_t, rank2, e2, cut, e1, tt=512):
    t, d = h.shape
    nk = rank2.shape[1]
    tt = min(tt, t)
    et = _ROWS_PER_STEP * nk
    n_tiles = nk // _ROWS_PER_STEP
    rspec = pl.BlockSpec((PEER_HEADS, nk, tt), lambda i, s: (0, 0, i))
    row_spec = pl.BlockSpec((PEER_HEADS, _ROWS_PER_STEP, tt), lambda i, s: (0, jnp.minimum(s, n_tiles - 1), i))
    return pl.pallas_call(
        _experts_body,
        grid=(t // tt, n_tiles + 1),
        in_specs=[pl.BlockSpec((tt, d), lambda i, s: (i, 0)),
                  pl.BlockSpec((1, d), lambda i, s: (0, 0)),
                  pl.BlockSpec((et, d), lambda i, s: (0, 0)),
                  pl.BlockSpec((et, d), lambda i, s: (jnp.minimum(s + 1, n_tiles - 1), 0)),
                  pl.BlockSpec((None, d, et), lambda i, s: (jnp.maximum(s - 1, 0), 0, 0)),
                  rspec, rspec, row_spec, row_spec],
        out_specs=pl.BlockSpec((tt, d), lambda i, s: (i, 0)),
        out_shape=jax.ShapeDtypeStruct((t, d), F32),
        scratch_shapes=[pltpu.VMEM((tt, d), BF16), pltpu.VMEM((d, tt), F32),
                        pltpu.VMEM((et, tt), F32), pltpu.VMEM((et, tt), F32),
                        pltpu.VMEM((et, tt), BF16), pltpu.VMEM((et, tt), BF16),
                        pltpu.VMEM((PEER_HEADS, _ROWS_PER_STEP, _BF16_ROWS, tt), BF16),
                        pltpu.VMEM((PEER_HEADS, _ROWS_PER_STEP, _BF16_ROWS, tt), BF16)],
        compiler_params=_cparams("parallel", "arbitrary"),
        name="peer_experts",
    )(h, gain.reshape(1, d), u, u, v_t, rank2, e2, cut, e1)


def _transpose_cast_body(x_ref, o_ref):
    o_ref[...] = x_ref[...].T.astype(o_ref.dtype)


def transpose_cast(x3, index, dtype, tile, blk=1024):
    _, r, c = x3.shape
    per = tile // blk
    return pl.pallas_call(
        _transpose_cast_body,
        grid=(r // blk, c // blk),
        in_specs=[pl.BlockSpec((None, blk, blk), lambda i, j: (index, i, j))],
        out_specs=pl.BlockSpec((None, blk, blk), lambda i, j: (i // per, j, i % per)),
        out_shape=jax.ShapeDtypeStruct((r // tile, c, tile), dtype),
        compiler_params=_cparams("parallel", "parallel"),
        name="transpose_cast",
    )(x3)


def _final_norm_body(x_ref, g_ref, o_ref):
    x = x_ref[...]
    ms = jnp.mean(x * x, axis=-1, keepdims=True)
    o_ref[...] = x * lax.rsqrt(ms + NORM_EPS) * g_ref[...]


def final_norm(x, gain, tm=1024):
    t, d = x.shape
    tm = min(tm, t)
    return pl.pallas_call(
        _final_norm_body,
        grid=(t // tm,),
        in_specs=[pl.BlockSpec((tm, d), lambda i: (i, 0)), pl.BlockSpec((1, d), lambda i: (0, 0))],
        out_specs=pl.BlockSpec((tm, d), lambda i: (i, 0)),
        out_shape=jax.ShapeDtypeStruct((t, d), F32),
        compiler_params=_cparams("parallel"),
        name="final_norm",
    )(x, gain.reshape(1, d))


_GATE_BLOCKS = N_BRANCHES * 8
_COL_Z = _GATE_BLOCKS
_COL_RET = _COL_Z + 4
_COL_MOBA = _COL_RET + 16
_COL_DIL = _COL_MOBA + 12
_COL_XBC = _COL_DIL + 12
_PROJ_BLOCKS = 84


def _pack_w_in(w_in_l):
    ssd_conv_dim = MIX_WIDTH + 2 * SSD_GROUPS * SSD_STATE
    o_xbc = MIX_WIDTH
    o_dt = o_xbc + ssd_conv_dim
    o_rest = o_dt + N_HEADS
    o_gate = o_rest + 10 * MIX_WIDTH
    d_model = w_in_l.shape[0]
    used = _COL_XBC * LANES + ssd_conv_dim
    pad = jnp.zeros((d_model, _PROJ_BLOCKS * LANES - used), w_in_l.dtype)
    main = jnp.concatenate([w_in_l[:, o_gate:], w_in_l[:, :o_xbc], w_in_l[:, o_rest:o_gate],
                            w_in_l[:, o_xbc:o_dt], pad], axis=1).astype(BF16)
    w_dt = jnp.pad(w_in_l[:, o_dt:o_rest], ((0, 0), (0, LANES - N_HEADS))).astype(BF16)
    return main, w_dt


def kernel(x, mem, mix_norm, w_in, ssd_conv_w, ssd_conv_b, ssd_dt_bias, ssd_a_log, ssd_d, ssd_norm, ret_norm, w_branch, w_out, x_norm, w_xq, w_xkv, w_xo, ffn_norm, w_pq, peer_sub_keys, peer_u, peer_v, final_norm_gain):
    bsz, seq, d = x.shape
    depth = w_in.shape[0]
    t = bsz * seq
    h = x.reshape(t, d)
    mem2 = mem.reshape(-1, d)
    rope_tables = _rope_tables(seq)

    for layer in range(depth):
        w_main, w_dt = _pack_w_in(w_in[layer])
        proj, dt_raw = norm_matmul(h, mix_norm[layer], w_main, w_dt, tm=2048, tn=1536, out_dtype=BF16)
        proj = proj.reshape(bsz, seq, -1)
        dt_raw = dt_raw.reshape(bsz, seq, LANES)

        y_a = ssd_mixer(proj, _COL_XBC, _COL_Z, dt_raw, ssd_conv_w[layer], ssd_conv_b[layer],
                        ssd_dt_bias[layer], ssd_a_log[layer], ssd_d[layer], ssd_norm[layer])
        y_b = retention_mixer(proj, [_COL_RET // 4 + i for i in range(4)], rope_tables, ret_norm[layer])
        y_c = moba_mixer(proj, [_COL_MOBA + 4 * i for i in range(3)], rope_tables)
        y_d = dilated_mixer(proj, [_COL_DIL + 4 * i for i in range(3)], rope_tables)
        kv = matmul(mem2, w_xkv[layer].astype(BF16), tm=512, tn=512, out_dtype=BF16)
        h = merge_cross_attention(h.reshape(bsz, seq, d), proj, (y_a, y_b, y_c, y_d),
                                  w_branch[layer].astype(BF16), w_out[layer].astype(BF16),
                                  kv.reshape(bsz, -1, 2 * d), x_norm[layer],
                                  w_xq[layer].astype(BF16), w_xo[layer].astype(BF16)).reshape(t, d)
        rank2, e2, cut, e1 = peer_router(h, ffn_norm[layer], w_pq[layer], peer_sub_keys[layer])
        h = peer_experts(h, ffn_norm[layer], peer_u[layer].astype(BF16),
                         transpose_cast(peer_v, layer, BF16, _ROWS_PER_STEP * PEER_NKEYS), rank2, e2, cut, e1)

    return final_norm(h, final_norm_gain).reshape(bsz, seq, d)
```

```python
import functools

import jax
import jax.numpy as jnp
from jax import lax
from jax.experimental import pallas as pl
from jax.experimental.pallas import tpu as pltpu

F32 = jnp.float32
BF16 = jnp.bfloat16
HI = lax.Precision.HIGHEST

NORM_EPS = 1e-6
NEG_INF = -1e30
ROPE_THETA = 10000.0
HEAD_DIM = 64
N_HEADS = 8
MIX_WIDTH = N_HEADS * HEAD_DIM
SSD_GROUPS = 2
SSD_STATE = 64
SSD_CONV = 4
CHUNK = 128
SSD_CHUNK = 256
ATT_BLOCK = 256
MOBA_TOPK = 3
DIL_PATTERNS = ((128, 1), (512, 4), (2048, 16))
N_BRANCHES = 4
X_HEADS = 4
PEER_HEADS = 8
PEER_NKEYS = 128
PEER_TOPK = 16
LANES = 128
VMEM_LIMIT = 56 * 1024 * 1024


def _cparams(*sem):
    return pltpu.CompilerParams(dimension_semantics=sem, vmem_limit_bytes=VMEM_LIMIT)


def _dot(a, b):
    return jnp.dot(a.astype(BF16), b.astype(BF16), preferred_element_type=F32)


def _dot_nt(a, b):
    return lax.dot_general(a, b, (((1,), (1,)), ((), ())), preferred_element_type=F32)


def _silu(x):
    return x * (1.0 / (1.0 + jnp.exp(-x)))


def _norm_matmul_body(x_ref, g_ref, w_ref, ws_ref, o_ref, os_ref, xn_ref):
    @pl.when(pl.program_id(1) == 0)
    def _():
        x = x_ref[...]
        ms = jnp.mean(x * x, axis=-1, keepdims=True)
        xn_ref[...] = (x * lax.rsqrt(ms + NORM_EPS) * g_ref[...]).astype(xn_ref.dtype)
        os_ref[...] = jnp.dot(xn_ref[...], ws_ref[...], preferred_element_type=F32)

    o_ref[...] = jnp.dot(xn_ref[...], w_ref[...], preferred_element_type=F32).astype(o_ref.dtype)


def norm_matmul(x, gain, w, w_side, *, tm, tn, out_dtype):
    t, k = x.shape
    n = w.shape[1]
    ns = w_side.shape[1]
    assert t % tm == 0 and n % tn == 0
    return pl.pallas_call(
        _norm_matmul_body,
        grid=(t // tm, n // tn),
        in_specs=[pl.BlockSpec((tm, k), lambda i, j: (i, 0)),
                  pl.BlockSpec((1, k), lambda i, j: (0, 0)),
                  pl.BlockSpec((k, tn), lambda i, j: (0, j)),
                  pl.BlockSpec((k, ns), lambda i, j: (0, 0))],
        out_specs=[pl.BlockSpec((tm, tn), lambda i, j: (i, j)), pl.BlockSpec((tm, ns), lambda i, j: (i, 0))],
        out_shape=[jax.ShapeDtypeStruct((t, n), out_dtype), jax.ShapeDtypeStruct((t, ns), F32)],
        scratch_shapes=[pltpu.VMEM((tm, k), BF16)],
        compiler_params=_cparams("parallel", "arbitrary"),
        name="norm_matmul",
    )(x, gain.reshape(1, k), w, w_side)


def _matmul_body(x_ref, w_ref, o_ref):
    o_ref[...] = _dot(x_ref[...], w_ref[...]).astype(o_ref.dtype)


def matmul(x, w, *, tm, tn, out_dtype):
    t, k = x.shape
    n = w.shape[1]
    assert t % tm == 0 and n % tn == 0
    return pl.pallas_call(
        _matmul_body,
        grid=(t // tm, n // tn),
        in_specs=[pl.BlockSpec((tm, k), lambda i, j: (i, 0)), pl.BlockSpec((k, tn), lambda i, j: (0, j))],
        out_specs=pl.BlockSpec((tm, tn), lambda i, j: (i, j)),
        out_shape=jax.ShapeDtypeStruct((t, n), out_dtype),
        compiler_params=_cparams("parallel", "parallel"),
        name="matmul",
    )(x, w)


def _ssd_body(x_ref, b_ref, c_ref, z_ref, dtraw_ref, convw_ref, convb_ref, dtbias_ref, alog_ref,
              dskip_ref, gain_ref, o_ref, state_ref, prev_ref):
    seq = x_ref.shape[1]
    CHUNK = SSD_CHUNK
    nchunks = seq // CHUNK
    rep = N_HEADS // SSD_GROUPS
    gn = SSD_GROUPS * SSD_STATE
    state_ref[...] = jnp.zeros_like(state_ref)
    prev_ref[...] = jnp.zeros_like(prev_ref)
    row = lax.broadcasted_iota(jnp.int32, (CHUNK, CHUNK), 0)
    col = lax.broadcasted_iota(jnp.int32, (CHUNK, CHUNK), 1)
    tril = row >= col
    tril_f = tril.astype(F32)
    conv_row = lax.broadcasted_iota(jnp.int32, (CHUNK, MIX_WIDTH + 2 * gn), 0)
    neg_a = -jnp.exp(alog_ref[...])

    def chunk(c, carry):
        for bb in range(x_ref.shape[0]):
            chunk_of(bb, c)
        return carry

    def chunk_of(bb, c):
        r0 = pl.multiple_of(c * CHUNK, CHUNK)
        rows = pl.ds(r0, CHUNK)
        raw = jnp.concatenate([x_ref[bb, rows, :], b_ref[bb, rows, :], c_ref[bb, rows, :]], axis=1).astype(F32)
        prev = prev_ref[bb]
        conv = raw * convw_ref[SSD_CONV - 1:SSD_CONV, :] + convb_ref[...]
        for k in range(1, SSD_CONV):
            shifted = jnp.where(conv_row < k, pltpu.roll(prev, k, axis=0), pltpu.roll(raw, k, axis=0))
            conv = conv + shifted * convw_ref[SSD_CONV - 1 - k:SSD_CONV - k, :]
        prev_ref[bb] = raw
        xbc = _silu(conv)
        xs, cm = xbc[:, :MIX_WIDTH], xbc[:, MIX_WIDTH + gn:]
        bm_t = xbc[:, MIX_WIDTH:MIX_WIDTH + gn].T
        pre = dtraw_ref[bb, rows, :] + dtbias_ref[...]
        dt = jnp.maximum(pre, 0.0) + jnp.log1p(jnp.exp(-jnp.abs(pre)))
        acum = jnp.dot(tril_f, dt * neg_a, precision=HI, preferred_element_type=F32)
        acum_t = acum.T
        ys = []
        for h in range(N_HEADS):
            g = h // rep
            hs = slice(h * HEAD_DIM, (h + 1) * HEAD_DIM)
            gs = slice(g * SSD_STATE, (g + 1) * SSD_STATE)
            a_col = acum[:, h:h + 1]
            a_row = acum_t[h:h + 1, :]
            a_last = acum[CHUNK - 1:CHUNK, h:h + 1]
            decay = jnp.exp(jnp.where(tril, a_col - a_row, -jnp.inf))
            cc = cm[:, gs]
            bt = bm_t[gs, :]
            x_h = xs[:, hs]
            xdt = x_h * dt[:, h:h + 1]
            scores = _dot(cc, bt) * decay
            prev = state_ref[bb, h]
            y = _dot(scores, xdt) + _dot(cc, prev) * jnp.exp(a_col)
            state_ref[bb, h] = prev * jnp.exp(a_last) + _dot(bt * jnp.exp(a_last - a_row), xdt)
            ys.append(y + x_h * dskip_ref[0:1, hs])
        y = jnp.concatenate(ys, axis=-1)
        y = y * _silu(z_ref[bb, rows, :].astype(F32))
        ms = jnp.mean(y * y, axis=-1, keepdims=True)
        o_ref[bb, rows, :] = (y * lax.rsqrt(ms + NORM_EPS) * gain_ref[...]).astype(o_ref.dtype)

    lax.fori_loop(0, nchunks, chunk, 0)


def ssd_mixer(proj, x_col, z_col, dt_raw, conv_w, conv_b, dt_bias, a_log, d_skip, norm_gain):
    bsz, seq, _ = proj.shape
    width = MIX_WIDTH
    gn = SSD_GROUPS * SSD_STATE
    conv_dim = width + 2 * gn
    lanes8 = lambda v: jnp.pad(v, (0, LANES - N_HEADS)).reshape(1, LANES)
    nb = 1
    wide = lambda blk: pl.BlockSpec((nb, seq, width), lambda b: (b, 0, blk))
    narrow = lambda blk: pl.BlockSpec((nb, seq, LANES), lambda b: (b, 0, blk))
    const = lambda shp: pl.BlockSpec(shp, lambda b: (0,) * len(shp))
    return pl.pallas_call(
        _ssd_body,
        grid=(bsz // nb,),
        in_specs=[wide(x_col // 4), narrow(x_col + 4), narrow(x_col + 5), wide(z_col // 4), narrow(0),
                  const((SSD_CONV, conv_dim)), const((1, conv_dim)), const((1, LANES)), const((1, LANES)),
                  const((1, width)), const((1, width))],
        out_specs=pl.BlockSpec((nb, seq, width), lambda b: (b, 0, 0)),
        out_shape=jax.ShapeDtypeStruct((bsz, seq, width), BF16),
        scratch_shapes=[pltpu.VMEM((nb, N_HEADS, SSD_STATE, HEAD_DIM), F32),
                        pltpu.VMEM((nb, SSD_CHUNK, conv_dim), F32)],
        compiler_params=_cparams("parallel"),
        name="ssd_mixer",
    )(proj, proj, proj, proj, dt_raw, conv_w, conv_b.reshape(1, conv_dim), lanes8(dt_bias), lanes8(a_log),
      jnp.repeat(d_skip, HEAD_DIM).reshape(1, width), norm_gain.reshape(1, width))


def _rope_tables(seq):
    half = HEAD_DIM // 2
    inv_freq = ROPE_THETA ** (-jnp.arange(half, dtype=F32) / half)
    ang = jnp.arange(seq, dtype=F32)[:, None] * inv_freq[None, :]
    sin = jnp.sin(ang)
    cos = jnp.tile(jnp.cos(ang), (1, LANES // half))
    sin_signed = jnp.tile(jnp.concatenate([-sin, sin], axis=1), (1, LANES // HEAD_DIM))
    return cos, sin_signed


def _rope_lanes(x, cos, sin_signed):
    width = x.shape[1]
    half = HEAD_DIM // 2
    lane = lax.broadcasted_iota(jnp.int32, x.shape, 1)
    partner = jnp.where(lane % HEAD_DIM < half, pltpu.roll(x, width - half, axis=1), pltpu.roll(x, half, axis=1))
    reps = width // LANES
    if reps > 1:
        cos = jnp.concatenate([cos] * reps, axis=1)
        sin_signed = jnp.concatenate([sin_signed] * reps, axis=1)
    return x * cos + partner * sin_signed


def _ret_body(q_ref, k_ref, v_ref, g_ref, cos_ref, sin_ref, dmat_ref, zeta_ref, xi_ref, cdec_ref, gain_ref,
              o_ref, state_ref):
    seq = q_ref.shape[1]
    nchunks = seq // CHUNK
    state_ref[...] = jnp.zeros_like(state_ref)

    def chunk(c, carry):
        for bb in range(q_ref.shape[0]):
            chunk_of(bb, c)
        return carry

    def chunk_of(bb, c):
        r0 = pl.multiple_of(c * CHUNK, CHUNK)
        rows = pl.ds(r0, CHUNK)
        cos, sin = cos_ref[rows, :], sin_ref[rows, :]
        q = _rope_lanes(q_ref[bb, rows, :].astype(F32), cos, sin).astype(BF16)
        k = _rope_lanes(k_ref[bb, rows, :].astype(F32), cos, sin) * HEAD_DIM ** -0.5
        k_t = k.T
        ys = []
        for h in range(N_HEADS):
            hs = slice(h * HEAD_DIM, (h + 1) * HEAD_DIM)
            qc = q[:, hs]
            kt = k_t[hs, :]
            vc = v_ref[bb, rows, hs]
            inner = _dot(qc, kt) * dmat_ref[h]
            prev = state_ref[bb, h]
            y = _dot(inner, vc) + _dot(qc, prev) * xi_ref[h]
            state_ref[bb, h] = prev * cdec_ref[h] + _dot(kt * zeta_ref[h], vc)
            mu = jnp.mean(y, axis=-1, keepdims=True)
            yc = y - mu
            var = jnp.mean(yc * yc, axis=-1, keepdims=True)
            ys.append(yc * lax.rsqrt(var + NORM_EPS))
        y = jnp.concatenate(ys, axis=-1) * gain_ref[...]
        o_ref[bb, rows, :] = (_silu(g_ref[bb, rows, :].astype(F32)) * y).astype(o_ref.dtype)

    lax.fori_loop(0, nchunks, chunk, 0)


def retention_mixer(proj, cols, rope_tables, norm_gain):
    bsz, seq, _ = proj.shape
    width = MIX_WIDTH
    log_gamma = jnp.log1p(-jnp.exp2(-5.0 - jnp.arange(N_HEADS, dtype=F32)))
    idx = jnp.arange(CHUNK, dtype=F32)
    rel = idx[:, None] - idx[None, :]
    dmat = jnp.where(rel >= 0, jnp.exp(jnp.maximum(rel, 0.0)[None] * log_gamma[:, None, None]), 0.0)
    zeta = jnp.exp((CHUNK - 1.0 - idx)[None, None, :] * log_gamma[:, None, None])
    xi = jnp.exp((idx + 1.0)[None, :, None] * log_gamma[:, None, None])
    cdec = jnp.exp(CHUNK * log_gamma)[:, None, None]
    nb = 1
    col = lambda blk: pl.BlockSpec((nb, seq, width), lambda b: (b, 0, blk))
    const = lambda shp: pl.BlockSpec(shp, lambda b: (0,) * len(shp))
    return pl.pallas_call(
        _ret_body,
        grid=(bsz // nb,),
        in_specs=[col(c) for c in cols] + [const((seq, LANES)), const((seq, LANES)),
                  const((N_HEADS, CHUNK, CHUNK)), const((N_HEADS, 1, CHUNK)),
                  const((N_HEADS, CHUNK, 1)), const((N_HEADS, 1, 1)), const((1, width))],
        out_specs=pl.BlockSpec((nb, seq, width), lambda b: (b, 0, 0)),
        out_shape=jax.ShapeDtypeStruct((bsz, seq, width), BF16),
        scratch_shapes=[pltpu.VMEM((nb, N_HEADS, HEAD_DIM, HEAD_DIM), F32)],
        compiler_params=_cparams("parallel"),
        name="retention_mixer",
    )(proj, proj, proj, proj, *rope_tables, dmat, zeta, xi, cdec, norm_gain.reshape(1, width))


def _softmax_pv(pieces, v_ones, hs):
    m = jnp.max(functools.reduce(jnp.maximum, pieces), axis=-1, keepdims=True)
    p = jnp.concatenate([jnp.exp(sp - m) for sp in pieces], axis=1).astype(BF16)
    o = jnp.dot(p, v_ones, preferred_element_type=F32)
    return (o / pltpu.roll(o, HEAD_DIM, axis=1))[:, hs]


def _values_with_ones(v_ref, hs, vo_ref):
    v = v_ref[0]
    lane = lax.broadcasted_iota(jnp.int32, v.shape, 1)
    vo_ref[...] = jnp.where((lane >= hs.start) & (lane < hs.stop), v, jnp.ones_like(v))


def _split_bf16(x):
    hi = x.astype(BF16)
    return hi, (x - hi.astype(F32)).astype(BF16)


def _rope_qk(q_ref, k_ref, cos_ref, sin_ref, qs_ref, kt_ref):
    cos, sin = cos_ref[...], sin_ref[...]
    qs_ref[...] = _rope_lanes(q_ref[0].astype(F32), cos, sin)
    kt_ref[...] = _rope_lanes(k_ref[0].astype(F32), cos, sin).T.astype(kt_ref.dtype)


def _moba_body(q_ref, k_ref, v_ref, cos_ref, sin_ref, o_ref, qs_ref, kt_ref, vo_ref):
    seq = q_ref.shape[1]
    nb = seq // ATT_BLOCK
    halves = ATT_BLOCK // LANES
    scale = HEAD_DIM ** -0.5
    heads = q_ref.shape[2] // HEAD_DIM
    n_sel = min(MOBA_TOPK, nb - 1)
    row = lax.broadcasted_iota(jnp.int32, (ATT_BLOCK, LANES), 0)
    col = lax.broadcasted_iota(jnp.int32, (ATT_BLOCK, LANES), 1)
    tril = [row >= col + hf * LANES for hf in range(halves)]
    _rope_qk(q_ref, k_ref, cos_ref, sin_ref, qs_ref, kt_ref)

    for h in range(heads):
        hs = slice(h * HEAD_DIM, (h + 1) * HEAD_DIM)
        kt = kt_ref[hs, :]
        kt_bf = kt.astype(BF16)
        _values_with_ones(v_ref, hs, vo_ref)
        reps = [jnp.broadcast_to(jnp.mean(kt[:, n * ATT_BLOCK:(n + 1) * ATT_BLOCK], axis=1, keepdims=True),
                                 (HEAD_DIM, LANES)) for n in range(nb - 1)]
        kr_hi, kr_lo = _split_bf16(jnp.concatenate(reps, axis=1))
        kr4 = jnp.concatenate([kr_hi, kr_lo, kr_hi, kr_lo], axis=0)

        for i in range(nb):
            rows = slice(i * ATT_BLOCK, (i + 1) * ATT_BLOCK)
            n_keys = (i + 1) * ATT_BLOCK
            q = qs_ref[rows, hs]
            sel = None
            if i > n_sel:
                q_hi, q_lo = _split_bf16(q)
                q4 = jnp.concatenate([q_hi, q_hi, q_lo, q_lo], axis=1)
                g = jnp.dot(q4, kr4[:, :i * LANES], preferred_element_type=F32)
                gs = [g[:, n * LANES:(n + 1) * LANES] for n in range(i)]
                cnt = [jnp.full((ATT_BLOCK, LANES), float(i - 1 - a), F32) for a in range(i)]
                for a in range(i):
                    for b in range(a + 1, i):
                        a_wins = jnp.where(gs[a] >= gs[b], 1.0, 0.0)
                        cnt[b] = cnt[b] + a_wins
                        cnt[a] = cnt[a] - a_wins
                sel = [c < n_sel for c in cnt]
            s = jnp.dot((q * scale).astype(BF16), kt_bf[:, :n_keys], preferred_element_type=F32)
            pieces = []
            for k in range(halves * (i + 1)):
                n, hf = divmod(k, halves)
                sp = s[:, k * LANES:(k + 1) * LANES]
                if n == i:
                    sp = jnp.where(tril[hf], sp, NEG_INF)
                elif sel is not None:
                    sp = jnp.where(sel[n], sp, NEG_INF)
                pieces.append(sp)
            o_ref[0, rows, hs] = _softmax_pv(pieces, vo_ref[:n_keys, :], hs).astype(o_ref.dtype)


def _dil_body(q_ref, k_ref, v_ref, cos_ref, sin_ref, lm_ref, o_ref, qs_ref, kt_ref, vo_ref):
    seq = q_ref.shape[1]
    nb = seq // ATT_BLOCK
    halves = ATT_BLOCK // LANES
    scale = HEAD_DIM ** -0.5
    heads = q_ref.shape[2] // HEAD_DIM
    _rope_qk(q_ref, k_ref, cos_ref, sin_ref, qs_ref, kt_ref)

    for h in range(heads):
        hs = slice(h * HEAD_DIM, (h + 1) * HEAD_DIM)
        kt_bf = kt_ref[hs, :]
        _values_with_ones(v_ref, hs, vo_ref)
        for i in range(nb):
            rows = slice(i * ATT_BLOCK, (i + 1) * ATT_BLOCK)
            n_keys = (i + 1) * ATT_BLOCK
            qs = (qs_ref[rows, hs] * scale).astype(BF16)
            s = jnp.dot(qs, kt_bf[:, :n_keys], preferred_element_type=F32)
            pieces = []
            for k in range(halves * (i + 1)):
                n, hf = divmod(k, halves)
                pieces.append(s[:, k * LANES:(k + 1) * LANES] + lm_ref[i - n, :, hf * LANES:(hf + 1) * LANES])
            o_ref[0, rows, hs] = _softmax_pv(pieces, vo_ref[:n_keys, :], hs).astype(o_ref.dtype)


def _dilation_log_multiplicity(seq):
    nb = seq // ATT_BLOCK
    r = jnp.arange(ATT_BLOCK)
    d = (jnp.arange(nb)[:, None, None] * ATT_BLOCK + r[None, :, None] - r[None, None, :])
    mult = jnp.zeros(d.shape, F32)
    for window, dil in DIL_PATTERNS:
        mult = mult + ((d >= 0) & (d <= window) & (d % dil == 0)).astype(F32)
    return jnp.where(mult > 0, jnp.log(jnp.maximum(mult, 1.0)), NEG_INF)


def _attention_mixer(body, name, kt_dtype, proj, cols, rope_tables, extra=()):
    bsz, seq, _ = proj.shape
    pair = lambda lane_blk: pl.BlockSpec((1, seq, LANES), lambda b, p: (b, 0, lane_blk + p))
    const = lambda shp: pl.BlockSpec(shp, lambda b, p: (0,) * len(shp))
    return pl.pallas_call(
        body,
        grid=(bsz, MIX_WIDTH // LANES),
        in_specs=[pair(c) for c in cols] + [const((seq, LANES)), const((seq, LANES))]
                 + [const(e.shape) for e in extra],
        out_specs=pl.BlockSpec((1, seq, LANES), lambda b, p: (b, 0, p)),
        out_shape=jax.ShapeDtypeStruct((bsz, seq, MIX_WIDTH), BF16),
        scratch_shapes=[pltpu.VMEM((seq, LANES), F32), pltpu.VMEM((LANES, seq), kt_dtype),
                        pltpu.VMEM((seq, LANES), BF16)],
        compiler_params=_cparams("parallel", "parallel"),
        name=name,
    )(proj, proj, proj, *rope_tables, *extra)


def moba_mixer(proj, cols, rope_tables):
    return _attention_mixer(_moba_body, "moba_mixer", F32, proj, cols, rope_tables)


def dilated_mixer(proj, cols, rope_tables):
    seq = proj.shape[1]
    return _attention_mixer(_dil_body, "dilated_mixer", BF16, proj, cols, rope_tables,
                            extra=(_dilation_log_multiplicity(seq),))


def _merge_xattn_body(h_ref, gates_ref, y0_ref, y1_ref, y2_ref, y3_ref, wb_ref, wout_ref,
                      gain_ref, wq_ref, k_ref, v_ref, wo_ref, o_ref):
    d = h_ref.shape[2]
    dh = d // X_HEADS
    merged = jnp.zeros(h_ref.shape[1:], F32)
    for i, y_ref in enumerate((y0_ref, y1_ref, y2_ref, y3_ref)):
        gate = 1.0 / (1.0 + jnp.exp(-gates_ref[0, :, i * d:(i + 1) * d].astype(F32)))
        merged = merged + gate * _dot(y_ref[0], wb_ref[i])
    h = h_ref[0] + _dot(merged, wout_ref[...])

    ms = jnp.mean(h * h, axis=-1, keepdims=True)
    hn = h * lax.rsqrt(ms + NORM_EPS) * gain_ref[...]
    q = _dot(hn, wq_ref[...]) * dh ** -0.5
    outs = []
    for a in range(X_HEADS):
        cs = slice(a * dh, (a + 1) * dh)
        s = _dot_nt(q[:, cs].astype(BF16), k_ref[0, :, cs])
        m = jnp.max(s, axis=-1, keepdims=True)
        e = jnp.exp(s - m)
        p = e / jnp.sum(e, axis=-1, keepdims=True)
        outs.append(_dot(p, v_ref[0, :, cs]))
    o = jnp.concatenate(outs, axis=-1)
    o_ref[0] = h + _dot(o, wo_ref[...])


def merge_cross_attention(h, proj, ys, w_branch, w_out, kv, gain, w_q, w_o, tq=512):
    bsz, seq, d = h.shape
    mlen = kv.shape[1]
    width = ys[0].shape[2]
    tq = min(tq, seq)
    rows = lambda w: pl.BlockSpec((1, tq, w), lambda b, i: (b, i, 0))
    const = lambda shp: pl.BlockSpec(shp, lambda b, i: (0,) * len(shp))
    return pl.pallas_call(
        _merge_xattn_body,
        grid=(bsz, seq // tq),
        in_specs=[rows(d), rows(N_BRANCHES * d)] + [rows(width)] * N_BRANCHES
                 + [const((N_BRANCHES, width, d)), const((d, d)), const((1, d)), const((d, d)),
                    pl.BlockSpec((1, mlen, d), lambda b, i: (b, 0, 0)),
                    pl.BlockSpec((1, mlen, d), lambda b, i: (b, 0, 1)),
                    const((d, d))],
        out_specs=rows(d),
        out_shape=jax.ShapeDtypeStruct((bsz, seq, d), F32),
        compiler_params=_cparams("parallel", "parallel"),
        name="merge_cross_attention",
    )(h, proj, *ys, w_branch, w_out, gain.reshape(1, d), w_q, kv, kv, w_o)


_SUBLANES = 8


def _sorting_network(n):
    pairs, p = [], 1
    while p < n:
        k = p
        while k >= 1:
            for j in range(k % p, n - k, 2 * k):
                for i in range(min(k, n - j - k)):
                    if (i + j) // (2 * p) == (i + j + k) // (2 * p):
                        pairs.append((i + j, i + j + k))
            k //= 2
        p *= 2
    return pairs


def _top_rows(vals, count):
    n_tiles = vals.shape[0] // _SUBLANES
    levels = [vals[_SUBLANES * g:_SUBLANES * (g + 1)] for g in range(n_tiles)]
    size = 1 << (n_tiles - 1).bit_length()
    neg = jnp.full(levels[0].shape, -jnp.inf, F32)
    levels += [neg] * (size - n_tiles)
    for a, b in _sorting_network(size):
        levels[a], levels[b] = jnp.maximum(levels[a], levels[b]), jnp.minimum(levels[a], levels[b])
    levels = levels[:n_tiles]
    tops = []
    for r in range(count):
        m = jnp.max(levels[0], axis=0, keepdims=True)
        tops.append(m)
        left = count - r - 1
        hit = levels[0] == m
        for g in range(min(len(levels), left)):
            below = levels[g + 1] if g + 1 < len(levels) else neg
            levels[g] = jnp.where(hit, below, levels[g])
    return tops


def _count_greater(sorted_rows, x):
    count = jnp.zeros(x.shape, F32)
    for r, row in enumerate(sorted_rows):
        count = jnp.where(row > x, float(r + 1), count)
    return count


_N_RANKS = PEER_TOPK + 1
_RANK_ROWS = 24


_N_CAND = sum(_N_RANKS // (i + 1) for i in range(_N_RANKS))
_CAND_ROWS = -(-_N_CAND // 8) * 8


def _pair_candidates(a1, a2_ref, cand_ref):
    cand_ref[_N_CAND // 8 * 8:, :] = jnp.full((_CAND_ROWS - _N_CAND // 8 * 8, cand_ref.shape[1]), -jnp.inf, F32)
    row = 0
    for i in range(_N_RANKS):
        n_j = _N_RANKS // (i + 1)
        cand_ref[row:row + n_j, :] = a1[i] + a2_ref[0:n_j, :]
        row += n_j
    return cand_ref[...]


def _router_body(h_ref, gain_ref, wq_hi_ref, wq_lo_ref, keys_ref, rank2_ref, e2_ref, cut_ref, e1_ref,
                 a2_ref, cand_ref):
    h = h_ref[...]
    ms = jnp.mean(h * h, axis=-1, keepdims=True)
    x_hi, x_lo = _split_bf16(h * lax.rsqrt(ms + NORM_EPS) * gain_ref[...])
    q = (jnp.dot(x_hi, wq_hi_ref[...], preferred_element_type=F32)
         + jnp.dot(x_hi, wq_lo_ref[...], preferred_element_type=F32)
         + jnp.dot(x_lo, wq_hi_ref[...], preferred_element_type=F32))
    dk = keys_ref.shape[-1] // 2
    for hd in range(PEER_HEADS):
        s = []
        for half in range(2):
            c0 = (hd * 2 + half) * dk
            q_hi, q_lo = _split_bf16(q[:, c0:c0 + dk])
            keys2 = keys_ref[hd, half]
            s.append(_dot_nt(keys2, jnp.concatenate([q_hi, q_hi], axis=1))
                     + _dot_nt(keys2, jnp.concatenate([q_lo, q_lo], axis=1)))
        s1, s2 = s
        a1 = _top_rows(s1, _N_RANKS)
        a2 = _top_rows(s2, _N_RANKS)
        for r in range(_N_RANKS):
            a2_ref[r:r + 1, :] = a2[r]
        cand = _pair_candidates(a1, a2_ref, cand_ref)
        top = _top_rows(cand, _N_RANKS)
        tau = 0.5 * (top[PEER_TOPK - 1] + top[PEER_TOPK])
        z = jnp.sum(jnp.where(cand > tau, jnp.exp(cand - (a1[0] + a2[0])), 0.0), axis=0, keepdims=True)
        rank2_ref[hd] = _count_greater(a2, s2).astype(rank2_ref.dtype)
        e2_ref[hd] = jnp.exp(s2 - a2[0]).astype(e2_ref.dtype)
        cut_ref[hd] = _count_greater(a2[:PEER_TOPK], tau - s1)
        e1_ref[hd] = jnp.exp(s1 - a1[0]) * (0.5 / z)


def peer_router(h, gain, w_q, sub_keys, tt=256):
    t, d = h.shape
    tt = min(tt, t)
    nk = sub_keys.shape[2]
    wq_hi, wq_lo = _split_bf16(w_q)
    keys2 = jnp.concatenate(_split_bf16(sub_keys), axis=-1)
    out = [jax.ShapeDtypeStruct((PEER_HEADS, nk, t), dt) for dt in (BF16, BF16, F32, F32)]
    ospec = pl.BlockSpec((PEER_HEADS, nk, tt), lambda i: (0, 0, i))
    return pl.pallas_call(
        _router_body,
        grid=(t // tt,),
        in_specs=[pl.BlockSpec((tt, d), lambda i: (i, 0)),
                  pl.BlockSpec((1, d), lambda i: (0, 0)),
                  pl.BlockSpec(w_q.shape, lambda i: (0, 0)),
                  pl.BlockSpec(w_q.shape, lambda i: (0, 0)),
                  pl.BlockSpec(keys2.shape, lambda i: (0, 0, 0, 0))],
        out_specs=[ospec] * 4,
        out_shape=out,
        scratch_shapes=[pltpu.VMEM((_RANK_ROWS, tt), F32), pltpu.VMEM((_CAND_ROWS, tt), F32)],
        compiler_params=_cparams("parallel"),
        name="peer_router",
    )(h, gain.reshape(1, d), wq_hi, wq_lo, keys2)


def _gelu_x2(x):
    return x * (1.0 + lax.erf(x * (2.0 ** -0.5)))


_ROWS_PER_STEP = 8
_J_CHUNK = 32
_ROW_GROUP = 4
_TOK_COLS = 256
_MM_SPLIT = 4
_BF16_ROWS = 16


def _experts_body(h_ref, gain_ref, u0_ref, un_ref, vt_ref, rank2_ref, e2_ref, cut_ref, e1_ref, o_ref,
                  xn_ref, acc_ref, act_a_ref, act_b_ref, p_a_ref, p_b_ref, cutb_ref, e1b_ref):
    step = pl.program_id(1)
    n_tiles = pl.num_programs(1) - 1
    nk = rank2_ref.shape[1]
    tt = h_ref.shape[0]
    n_cols = tt // _TOK_COLS
    act_refs = (act_a_ref, act_b_ref)
    p_refs = (p_a_ref, p_b_ref)

    @pl.when(step == 0)
    def _():
        h = h_ref[...]
        ms = jnp.mean(h * h, axis=-1, keepdims=True)
        xn_ref[...] = (h * lax.rsqrt(ms + NORM_EPS) * gain_ref[...]).astype(BF16)
        acc_ref[...] = jnp.zeros_like(acc_ref)
        p_refs[1][...] = jnp.zeros(p_refs[1].shape, BF16)
        act_refs[0][...] = _dot_nt(u0_ref[...], xn_ref[...])

    def main_block(act_cur, act_nxt, p_cur, p_prev):
        d_rows = acc_ref.shape[0] // _MM_SPLIT
        e_rows = un_ref.shape[0] // _MM_SPLIT

        def v_piece(k, c):
            rs, cs = slice(k * d_rows, (k + 1) * d_rows), slice(c * _TOK_COLS, (c + 1) * _TOK_COLS)
            acc_ref[rs, cs] += jnp.dot(vt_ref[rs, :], p_prev[:, cs], preferred_element_type=F32)

        def act_piece(k, c):
            rs, cs = slice(k * e_rows, (k + 1) * e_rows), slice(c * _TOK_COLS, (c + 1) * _TOK_COLS)
            act_nxt[rs, cs] = _dot_nt(un_ref[rs, :], xn_ref[cs, :])

        pieces = [(f, k, c) for k in range(_MM_SPLIT) for c in range(n_cols) for f in (v_piece, act_piece)]

        for hd in range(PEER_HEADS):
            cut8 = cut_ref[hd]
            e18 = e1_ref[hd]
            for r in range(_ROWS_PER_STEP):
                cutb_ref[hd, r] = jnp.broadcast_to(cut8[r:r + 1], (_BF16_ROWS, tt)).astype(BF16)
                e1b_ref[hd, r] = jnp.broadcast_to(e18[r:r + 1], (_BF16_ROWS, tt)).astype(BF16)

        reps = _J_CHUNK // _BF16_ROWS
        chunks = [(c, jc, rg) for c in range(n_cols) for jc in range(nk // _J_CHUNK)
                  for rg in range(_ROWS_PER_STEP // _ROW_GROUP)]
        per_chunk = -(-len(pieces) // len(chunks))
        for n, (c, jc, rg) in enumerate(chunks):
            for f, k, cc in pieces[n * per_chunk:(n + 1) * per_chunk]:
                f(k, cc)
            ts = slice(c * _TOK_COLS, (c + 1) * _TOK_COLS)
            js = slice(jc * _J_CHUNK, (jc + 1) * _J_CHUNK)
            gates = [jnp.zeros((_J_CHUNK, _TOK_COLS), BF16) for _ in range(_ROW_GROUP)]
            for hd in range(PEER_HEADS):
                rank2 = rank2_ref[hd, js, ts]
                e2 = e2_ref[hd, js, ts]
                for k in range(_ROW_GROUP):
                    r = rg * _ROW_GROUP + k
                    cut = jnp.concatenate([cutb_ref[hd, r, :, ts]] * reps, axis=0)
                    e1 = jnp.concatenate([e1b_ref[hd, r, :, ts]] * reps, axis=0)
                    gates[k] = gates[k] + jnp.where(rank2 < cut, e2 * e1, jnp.zeros_like(e2))
            for k in range(_ROW_GROUP):
                rows = slice((rg * _ROW_GROUP + k) * nk + jc * _J_CHUNK,
                             (rg * _ROW_GROUP + k) * nk + (jc + 1) * _J_CHUNK)
                p_cur[rows, ts] = _gelu_x2(act_cur[rows, ts]).astype(BF16) * gates[k]

    for parity in range(2):
        @pl.when((step % 2 == parity) & (step < n_tiles))
        def _():
            main_block(act_refs[parity], act_refs[1 - parity], p_refs[parity], p_refs[1 - parity])

    last_p = p_refs[(rank2_ref.shape[1] // _ROWS_PER_STEP - 1) % 2]

    @pl.when(step == n_tiles)
    def _():
        acc = acc_ref[...] + jnp.dot(vt_ref[...], last_p[...], preferred_element_type=F32)
        o_ref[...] = h_ref[...] + acc.T


def peer_experts(h, gain, u, v_t, rank2, e2, cut, e1, tt=512):
    t, d = h.shape
    nk = rank2.shape[1]
    tt = min(tt, t)
    et = _ROWS_PER_STEP * nk
    n_tiles = nk // _ROWS_PER_STEP
    rspec = pl.BlockSpec((PEER_HEADS, nk, tt), lambda i, s: (0, 0, i))
    row_spec = pl.BlockSpec((PEER_HEADS, _ROWS_PER_STEP, tt), lambda i, s: (0, jnp.minimum(s, n_tiles - 1), i))
    return pl.pallas_call(
        _experts_body,
        grid=(t // tt, n_tiles + 1),
        in_specs=[pl.BlockSpec((tt, d), lambda i, s: (i, 0)),
                  pl.BlockSpec((1, d), lambda i, s: (0, 0)),
                  pl.BlockSpec((et, d), lambda i, s: (0, 0)),
                  pl.BlockSpec((et, d), lambda i, s: (jnp.minimum(s + 1, n_tiles - 1), 0)),
                  pl.BlockSpec((None, d, et), lambda i, s: (jnp.maximum(s - 1, 0), 0, 0)),
                  rspec, rspec, row_spec, row_spec],
        out_specs=pl.BlockSpec((tt, d), lambda i, s: (i, 0)),
        out_shape=jax.ShapeDtypeStruct((t, d), F32),
        scratch_shapes=[pltpu.VMEM((tt, d), BF16), pltpu.VMEM((d, tt), F32),
                        pltpu.VMEM((et, tt), F32), pltpu.VMEM((et, tt), F32),
                        pltpu.VMEM((et, tt), BF16), pltpu.VMEM((et, tt), BF16),
                        pltpu.VMEM((PEER_HEADS, _ROWS_PER_STEP, _BF16_ROWS, tt), BF16),
                        pltpu.VMEM((PEER_HEADS, _ROWS_PER_STEP, _BF16_ROWS, tt), BF16)],
        compiler_params=_cparams("parallel", "arbitrary"),
        name="peer_experts",
    )(h, gain.reshape(1, d), u, u, v_t, rank2, e2, cut, e1)


def _transpose_cast_body(x_ref, o_ref):
    o_ref[...] = x_ref[...].T.astype(o_ref.dtype)


def transpose_cast(x3, index, dtype, tile, blk=1024):
    _, r, c = x3.shape
    per = tile // blk
    return pl.pallas_call(
        _transpose_cast_body,
        grid=(r // blk, c // blk),
        in_specs=[pl.BlockSpec((None, blk, blk), lambda i, j: (index, i, j))],
        out_specs=pl.BlockSpec((None, blk, blk), lambda i, j: (i // per, j, i % per)),
        out_shape=jax.ShapeDtypeStruct((r // tile, c, tile), dtype),
        compiler_params=_cparams("parallel", "parallel"),
        name="transpose_cast",
    )(x3)


def _final_norm_body(x_ref, g_ref, o_ref):
    x = x_ref[...]
    ms = jnp.mean(x * x, axis=-1, keepdims=True)
    o_ref[...] = x * lax.rsqrt(ms + NORM_EPS) * g_ref[...]


def final_norm(x, gain, tm=1024):
    t, d = x.shape
    tm = min(tm, t)
    return pl.pallas_call(
        _final_norm_body,
        grid=(t // tm,),
        in_specs=[pl.BlockSpec((tm, d), lambda i: (i, 0)), pl.BlockSpec((1, d), lambda i: (0, 0))],
        out_specs=pl.BlockSpec((tm, d), lambda i: (i, 0)),
        out_shape=jax.ShapeDtypeStruct((t, d), F32),
        compiler_params=_cparams("parallel"),
        name="final_norm",
    )(x, gain.reshape(1, d))


_GATE_BLOCKS = N_BRANCHES * 8
_COL_Z = _GATE_BLOCKS
_COL_RET = _COL_Z + 4
_COL_MOBA = _COL_RET + 16
_COL_DIL = _COL_MOBA + 12
_COL_XBC = _COL_DIL + 12
_PROJ_BLOCKS = 84


def _pack_w_in(w_in_l):
    ssd_conv_dim = MIX_WIDTH + 2 * SSD_GROUPS * SSD_STATE
    o_xbc = MIX_WIDTH
    o_dt = o_xbc + ssd_conv_dim
    o_rest = o_dt + N_HEADS
    o_gate = o_rest + 10 * MIX_WIDTH
    d_model = w_in_l.shape[0]
    used = _COL_XBC * LANES + ssd_conv_dim
    pad = jnp.zeros((d_model, _PROJ_BLOCKS * LANES - used), w_in_l.dtype)
    main = jnp.concatenate([w_in_l[:, o_gate:], w_in_l[:, :o_xbc], w_in_l[:, o_rest:o_gate],
                            w_in_l[:, o_xbc:o_dt], pad], axis=1).astype(BF16)
    w_dt = jnp.pad(w_in_l[:, o_dt:o_rest], ((0, 0), (0, LANES - N_HEADS))).astype(BF16)
    return main, w_dt


def kernel(x, mem, mix_norm, w_in, ssd_conv_w, ssd_conv_b, ssd_dt_bias, ssd_a_log, ssd_d, ssd_norm, ret_norm, w_branch, w_out, x_norm, w_xq, w_xkv, w_xo, ffn_norm, w_pq, peer_sub_keys, peer_u, peer_v, final_norm_gain):
    bsz, seq, d = x.shape
    depth = w_in.shape[0]
    t = bsz * seq
    h = x.reshape(t, d)
    mem2 = mem.reshape(-1, d)
    rope_tables = _rope_tables(seq)

    for layer in range(depth):
        w_main, w_dt = _pack_w_in(w_in[layer])
        proj, dt_raw = norm_matmul(h, mix_norm[layer], w_main, w_dt, tm=2048, tn=1536, out_dtype=BF16)
        proj = proj.reshape(bsz, seq, -1)
        dt_raw = dt_raw.reshape(bsz, seq, LANES)

        y_a = ssd_mixer(proj, _COL_XBC, _COL_Z, dt_raw, ssd_conv_w[layer], ssd_conv_b[layer],
                        ssd_dt_bias[layer], ssd_a_log[layer], ssd_d[layer], ssd_norm[layer])
        y_b = retention_mixer(proj, [_COL_RET // 4 + i for i in range(4)], rope_tables, ret_norm[layer])
        y_c = moba_mixer(proj, [_COL_MOBA + 4 * i for i in range(3)], rope_tables)
        y_d = dilated_mixer(proj, [_COL_DIL + 4 * i for i in range(3)], rope_tables)
        kv = matmul(mem2, w_xkv[layer].astype(BF16), tm=512, tn=512, out_dtype=BF16)
        h = merge_cross_attention(h.reshape(bsz, seq, d), proj, (y_a, y_b, y_c, y_d),
                                  w_branch[layer].astype(BF16), w_out[layer].astype(BF16),
                                  kv.reshape(bsz, -1, 2 * d), x_norm[layer],
                                  w_xq[layer].astype(BF16), w_xo[layer].astype(BF16)).reshape(t, d)
        rank2, e2, cut, e1 = peer_router(h, ffn_norm[layer], w_pq[layer], peer_sub_keys[layer])
        h = peer_experts(h, ffn_norm[layer], peer_u[layer].astype(BF16),
                         transpose_cast(peer_v, layer, BF16, _ROWS_PER_STEP * PEER_NKEYS), rank2, e2, cut, e1)

    return final_norm(h, final_norm_gain).reshape(bsz, seq, d)
```

```python
import functools

import jax
import jax.numpy as jnp
from jax import lax
from jax.experimental import pallas as pl
from jax.experimental.pallas import tpu as pltpu

F32 = jnp.float32
BF16 = jnp.bfloat16
HI = lax.Precision.HIGHEST

NORM_EPS = 1e-6
NEG_INF = -1e30
ROPE_THETA = 10000.0
HEAD_DIM = 64
N_HEADS = 8
MIX_WIDTH = N_HEADS * HEAD_DIM
SSD_GROUPS = 2
SSD_STATE = 64
SSD_CONV = 4
CHUNK = 128
SSD_CHUNK = 256
ATT_BLOCK = 256
MOBA_TOPK = 3
DIL_PATTERNS = ((128, 1), (512, 4), (2048, 16))
N_BRANCHES = 4
X_HEADS = 4
PEER_HEADS = 8
PEER_NKEYS = 128
PEER_TOPK = 16
LANES = 128
VMEM_LIMIT = 56 * 1024 * 1024


def _cparams(*sem):
    return pltpu.CompilerParams(dimension_semantics=sem, vmem_limit_bytes=VMEM_LIMIT)


def _dot(a, b):
    return jnp.dot(a.astype(BF16), b.astype(BF16), preferred_element_type=F32)


def _dot_nt(a, b):
    return lax.dot_general(a, b, (((1,), (1,)), ((), ())), preferred_element_type=F32)


def _silu(x):
    return x * (1.0 / (1.0 + jnp.exp(-x)))


def _norm_matmul_body(x_ref, g_ref, w_ref, ws_ref, o_ref, os_ref, xn_ref):
    @pl.when(pl.program_id(1) == 0)
    def _():
        x = x_ref[...]
        ms = jnp.mean(x * x, axis=-1, keepdims=True)
        xn_ref[...] = (x * lax.rsqrt(ms + NORM_EPS) * g_ref[...]).astype(xn_ref.dtype)
        os_ref[...] = jnp.dot(xn_ref[...], ws_ref[...], preferred_element_type=F32)

    o_ref[...] = jnp.dot(xn_ref[...], w_ref[...], preferred_element_type=F32).astype(o_ref.dtype)


def norm_matmul(x, gain, w, w_side, *, tm, tn, out_dtype):
    t, k = x.shape
    n = w.shape[1]
    ns = w_side.shape[1]
    assert t % tm == 0 and n % tn == 0
    return pl.pallas_call(
        _norm_matmul_body,
        grid=(t // tm, n // tn),
        in_specs=[pl.BlockSpec((tm, k), lambda i, j: (i, 0)),
                  pl.BlockSpec((1, k), lambda i, j: (0, 0)),
                  pl.BlockSpec((k, tn), lambda i, j: (0, j)),
                  pl.BlockSpec((k, ns), lambda i, j: (0, 0))],
        out_specs=[pl.BlockSpec((tm, tn), lambda i, j: (i, j)), pl.BlockSpec((tm, ns), lambda i, j: (i, 0))],
        out_shape=[jax.ShapeDtypeStruct((t, n), out_dtype), jax.ShapeDtypeStruct((t, ns), F32)],
        scratch_shapes=[pltpu.VMEM((tm, k), BF16)],
        compiler_params=_cparams("parallel", "arbitrary"),
        name="norm_matmul",
    )(x, gain.reshape(1, k), w, w_side)


def _matmul_body(x_ref, w_ref, o_ref):
    o_ref[...] = _dot(x_ref[...], w_ref[...]).astype(o_ref.dtype)


def matmul(x, w, *, tm, tn, out_dtype):
    t, k = x.shape
    n = w.shape[1]
    assert t % tm == 0 and n % tn == 0
    return pl.pallas_call(
        _matmul_body,
        grid=(t // tm, n // tn),
        in_specs=[pl.BlockSpec((tm, k), lambda i, j: (i, 0)), pl.BlockSpec((k, tn), lambda i, j: (0, j))],
        out_specs=pl.BlockSpec((tm, tn), lambda i, j: (i, j)),
        out_shape=jax.ShapeDtypeStruct((t, n), out_dtype),
        compiler_params=_cparams("parallel", "parallel"),
        name="matmul",
    )(x, w)


def _ssd_body(x_ref, b_ref, c_ref, z_ref, dtraw_ref, convw_ref, convb_ref, dtbias_ref, alog_ref,
              dskip_ref, gain_ref, o_ref, state_ref, prev_ref):
    seq = x_ref.shape[1]
    CHUNK = SSD_CHUNK
    nchunks = seq // CHUNK
    rep = N_HEADS // SSD_GROUPS
    gn = SSD_GROUPS * SSD_STATE
    state_ref[...] = jnp.zeros_like(state_ref)
    prev_ref[...] = jnp.zeros_like(prev_ref)
    row = lax.broadcasted_iota(jnp.int32, (CHUNK, CHUNK), 0)
    col = lax.broadcasted_iota(jnp.int32, (CHUNK, CHUNK), 1)
    tril = row >= col
    tril_f = tril.astype(F32)
    conv_row = lax.broadcasted_iota(jnp.int32, (CHUNK, MIX_WIDTH + 2 * gn), 0)
    neg_a = -jnp.exp(alog_ref[...])

    def chunk(c, carry):
        for bb in range(x_ref.shape[0]):
            chunk_of(bb, c)
        return carry

    def chunk_of(bb, c):
        r0 = pl.multiple_of(c * CHUNK, CHUNK)
        rows = pl.ds(r0, CHUNK)
        raw = jnp.concatenate([x_ref[bb, rows, :], b_ref[bb, rows, :], c_ref[bb, rows, :]], axis=1).astype(F32)
        prev = prev_ref[bb]
        conv = raw * convw_ref[SSD_CONV - 1:SSD_CONV, :] + convb_ref[...]
        for k in range(1, SSD_CONV):
            shifted = jnp.where(conv_row < k, pltpu.roll(prev, k, axis=0), pltpu.roll(raw, k, axis=0))
            conv = conv + shifted * convw_ref[SSD_CONV - 1 - k:SSD_CONV - k, :]
        prev_ref[bb] = raw
        xbc = _silu(conv)
        xs, cm = xbc[:, :MIX_WIDTH], xbc[:, MIX_WIDTH + gn:]
        bm_t = xbc[:, MIX_WIDTH:MIX_WIDTH + gn].T
        pre = dtraw_ref[bb, rows, :] + dtbias_ref[...]
        dt = jnp.maximum(pre, 0.0) + jnp.log1p(jnp.exp(-jnp.abs(pre)))
        acum = jnp.dot(tril_f, dt * neg_a, precision=HI, preferred_element_type=F32)
        acum_t = acum.T
        ys = []
        for h in range(N_HEADS):
            g = h // rep
            hs = slice(h * HEAD_DIM, (h + 1) * HEAD_DIM)
            gs = slice(g * SSD_STATE, (g + 1) * SSD_STATE)
            a_col = acum[:, h:h + 1]
            a_row = acum_t[h:h + 1, :]
            a_last = acum[CHUNK - 1:CHUNK, h:h + 1]
            decay = jnp.exp(jnp.where(tril, a_col - a_row, -jnp.inf))
            cc = cm[:, gs]
            bt = bm_t[gs, :]
            x_h = xs[:, hs]
            xdt = x_h * dt[:, h:h + 1]
            scores = _dot(cc, bt) * decay
            prev = state_ref[bb, h]
            y = _dot(scores, xdt) + _dot(cc, prev) * jnp.exp(a_col)
            state_ref[bb, h] = prev * jnp.exp(a_last) + _dot(bt * jnp.exp(a_last - a_row), xdt)
            ys.append(y + x_h * dskip_ref[0:1, hs])
        y = jnp.concatenate(ys, axis=-1)
        y = y * _silu(z_ref[bb, rows, :].astype(F32))
        ms = jnp.mean(y * y, axis=-1, keepdims=True)
        o_ref[bb, rows, :] = (y * lax.rsqrt(ms + NORM_EPS) * gain_ref[...]).astype(o_ref.dtype)

    lax.fori_loop(0, nchunks, chunk, 0)


def ssd_mixer(proj, x_col, z_col, dt_raw, conv_w, conv_b, dt_bias, a_log, d_skip, norm_gain):
    bsz, seq, _ = proj.shape
    width = MIX_WIDTH
    gn = SSD_GROUPS * SSD_STATE
    conv_dim = width + 2 * gn
    lanes8 = lambda v: jnp.pad(v, (0, LANES - N_HEADS)).reshape(1, LANES)
    nb = 1
    wide = lambda blk: pl.BlockSpec((nb, seq, width), lambda b: (b, 0, blk))
    narrow = lambda blk: pl.BlockSpec((nb, seq, LANES), lambda b: (b, 0, blk))
    const = lambda shp: pl.BlockSpec(shp, lambda b: (0,) * len(shp))
    return pl.pallas_call(
        _ssd_body,
        grid=(bsz // nb,),
        in_specs=[wide(x_col // 4), narrow(x_col + 4), narrow(x_col + 5), wide(z_col // 4), narrow(0),
                  const((SSD_CONV, conv_dim)), const((1, conv_dim)), const((1, LANES)), const((1, LANES)),
                  const((1, width)), const((1, width))],
        out_specs=pl.BlockSpec((nb, seq, width), lambda b: (b, 0, 0)),
        out_shape=jax.ShapeDtypeStruct((bsz, seq, width), BF16),
        scratch_shapes=[pltpu.VMEM((nb, N_HEADS, SSD_STATE, HEAD_DIM), F32),
                        pltpu.VMEM((nb, SSD_CHUNK, conv_dim), F32)],
        compiler_params=_cparams("parallel"),
        name="ssd_mixer",
    )(proj, proj, proj, proj, dt_raw, conv_w, conv_b.reshape(1, conv_dim), lanes8(dt_bias), lanes8(a_log),
      jnp.repeat(d_skip, HEAD_DIM).reshape(1, width), norm_gain.reshape(1, width))


def _rope_tables(seq):
    half = HEAD_DIM // 2
    inv_freq = ROPE_THETA ** (-jnp.arange(half, dtype=F32) / half)
    ang = jnp.arange(seq, dtype=F32)[:, None] * inv_freq[None, :]
    sin = jnp.sin(ang)
    cos = jnp.tile(jnp.cos(ang), (1, LANES // half))
    sin_signed = jnp.tile(jnp.concatenate([-sin, sin], axis=1), (1, LANES // HEAD_DIM))
    return cos, sin_signed


def _rope_lanes(x, cos, sin_signed):
    width = x.shape[1]
    half = HEAD_DIM // 2
    lane = lax.broadcasted_iota(jnp.int32, x.shape, 1)
    partner = jnp.where(lane % HEAD_DIM < half, pltpu.roll(x, width - half, axis=1), pltpu.roll(x, half, axis=1))
    reps = width // LANES
    if reps > 1:
        cos = jnp.concatenate([cos] * reps, axis=1)
        sin_signed = jnp.concatenate([sin_signed] * reps, axis=1)
    return x * cos + partner * sin_signed


def _rope_mxu(x, cos, sin_signed, perm):
    reps = x.shape[1] // LANES
    cos = jnp.concatenate([cos] * reps, axis=1)
    sin_signed = jnp.concatenate([sin_signed] * reps, axis=1)
    partner = jnp.dot(x, perm, preferred_element_type=F32)
    return x.astype(F32) * cos + partner * sin_signed


def _ret_body(q_ref, k_ref, v_ref, g_ref, cos_ref, sin_ref, perm_ref, dmat_ref, zeta_ref, xi_ref, cdec_ref,
              gain_ref, o_ref, state_ref):
    seq = q_ref.shape[1]
    nchunks = seq // CHUNK
    state_ref[...] = jnp.zeros_like(state_ref)

    def chunk(c, carry):
        for bb in range(q_ref.shape[0]):
            chunk_of(bb, c)
        return carry

    def chunk_of(bb, c):
        r0 = pl.multiple_of(c * CHUNK, CHUNK)
        rows = pl.ds(r0, CHUNK)
        cos, sin = cos_ref[rows, :], sin_ref[rows, :]
        q = _rope_mxu(q_ref[bb, rows, :], cos, sin, perm_ref[...]).astype(BF16)
        k = _rope_mxu(k_ref[bb, rows, :], cos, sin, perm_ref[...]) * HEAD_DIM ** -0.5
        k_t = k.T
        ys = []
        for h in range(N_HEADS):
            hs = slice(h * HEAD_DIM, (h + 1) * HEAD_DIM)
            qc = q[:, hs]
            kt = k_t[hs, :]
            vc = v_ref[bb, rows, hs]
            inner = _dot(qc, kt) * dmat_ref[h]
            prev = state_ref[bb, h]
            y = _dot(inner, vc) + _dot(qc, prev) * xi_ref[h]
            state_ref[bb, h] = prev * cdec_ref[h] + _dot(kt * zeta_ref[h], vc)
            mu = jnp.mean(y, axis=-1, keepdims=True)
            yc = y - mu
            var = jnp.mean(yc * yc, axis=-1, keepdims=True)
            ys.append(yc * lax.rsqrt(var + NORM_EPS))
        y = jnp.concatenate(ys, axis=-1) * gain_ref[...]
        o_ref[bb, rows, :] = (_silu(g_ref[bb, rows, :].astype(F32)) * y).astype(o_ref.dtype)

    lax.fori_loop(0, nchunks, chunk, 0)


def retention_mixer(proj, cols, rope_tables, norm_gain):
    bsz, seq, _ = proj.shape
    width = MIX_WIDTH
    log_gamma = jnp.log1p(-jnp.exp2(-5.0 - jnp.arange(N_HEADS, dtype=F32)))
    idx = jnp.arange(CHUNK, dtype=F32)
    rel = idx[:, None] - idx[None, :]
    dmat = jnp.where(rel >= 0, jnp.exp(jnp.maximum(rel, 0.0)[None] * log_gamma[:, None, None]), 0.0)
    zeta = jnp.exp((CHUNK - 1.0 - idx)[None, None, :] * log_gamma[:, None, None])
    xi = jnp.exp((idx + 1.0)[None, :, None] * log_gamma[:, None, None])
    cdec = jnp.exp(CHUNK * log_gamma)[:, None, None]
    lane = jnp.arange(width)
    half = HEAD_DIM // 2
    partner_of = jnp.where(lane % HEAD_DIM < half, lane + half, lane - half)
    perm = (lane[:, None] == partner_of[None, :]).astype(BF16)
    nb = 1
    col = lambda blk: pl.BlockSpec((nb, seq, width), lambda b: (b, 0, blk))
    const = lambda shp: pl.BlockSpec(shp, lambda b: (0,) * len(shp))
    return pl.pallas_call(
        _ret_body,
        grid=(bsz // nb,),
        in_specs=[col(c) for c in cols] + [const((seq, LANES)), const((seq, LANES)), const((width, width)),
                  const((N_HEADS, CHUNK, CHUNK)), const((N_HEADS, 1, CHUNK)),
                  const((N_HEADS, CHUNK, 1)), const((N_HEADS, 1, 1)), const((1, width))],
        out_specs=pl.BlockSpec((nb, seq, width), lambda b: (b, 0, 0)),
        out_shape=jax.ShapeDtypeStruct((bsz, seq, width), BF16),
        scratch_shapes=[pltpu.VMEM((nb, N_HEADS, HEAD_DIM, HEAD_DIM), F32)],
        compiler_params=_cparams("parallel"),
        name="retention_mixer",
    )(proj, proj, proj, proj, *rope_tables, perm, dmat, zeta, xi, cdec, norm_gain.reshape(1, width))


def _softmax_pv(pieces, v_ones, hs):
    m = jnp.max(functools.reduce(jnp.maximum, pieces), axis=-1, keepdims=True)
    p = jnp.concatenate([jnp.exp(sp - m) for sp in pieces], axis=1).astype(BF16)
    o = jnp.dot(p, v_ones, preferred_element_type=F32)
    return (o / pltpu.roll(o, HEAD_DIM, axis=1))[:, hs]


def _values_with_ones(v_ref, hs, vo_ref):
    v = v_ref[0]
    lane = lax.broadcasted_iota(jnp.int32, v.shape, 1)
    vo_ref[...] = jnp.where((lane >= hs.start) & (lane < hs.stop), v, jnp.ones_like(v))


def _split_bf16(x):
    hi = x.astype(BF16)
    return hi, (x - hi.astype(F32)).astype(BF16)


def _rope_qk(q_ref, k_ref, cos_ref, sin_ref, qs_ref, kt_ref):
    cos, sin = cos_ref[...], sin_ref[...]
    qs_ref[...] = _rope_lanes(q_ref[0].astype(F32), cos, sin)
    kt_ref[...] = _rope_lanes(k_ref[0].astype(F32), cos, sin).T.astype(kt_ref.dtype)


def _moba_body(q_ref, k_ref, v_ref, cos_ref, sin_ref, o_ref, qs_ref, kt_ref, vo_ref):
    seq = q_ref.shape[1]
    nb = seq // ATT_BLOCK
    halves = ATT_BLOCK // LANES
    scale = HEAD_DIM ** -0.5
    heads = q_ref.shape[2] // HEAD_DIM
    n_sel = min(MOBA_TOPK, nb - 1)
    row = lax.broadcasted_iota(jnp.int32, (ATT_BLOCK, LANES), 0)
    col = lax.broadcasted_iota(jnp.int32, (ATT_BLOCK, LANES), 1)
    tril = [row >= col + hf * LANES for hf in range(halves)]
    _rope_qk(q_ref, k_ref, cos_ref, sin_ref, qs_ref, kt_ref)

    for h in range(heads):
        hs = slice(h * HEAD_DIM, (h + 1) * HEAD_DIM)
        kt = kt_ref[hs, :]
        kt_bf = kt.astype(BF16)
        _values_with_ones(v_ref, hs, vo_ref)
        reps = [jnp.broadcast_to(jnp.mean(kt[:, n * ATT_BLOCK:(n + 1) * ATT_BLOCK], axis=1, keepdims=True),
                                 (HEAD_DIM, LANES)) for n in range(nb - 1)]
        kr_hi, kr_lo = _split_bf16(jnp.concatenate(reps, axis=1))
        kr4 = jnp.concatenate([kr_hi, kr_lo, kr_hi, kr_lo], axis=0)

        for i in range(nb):
            rows = slice(i * ATT_BLOCK, (i + 1) * ATT_BLOCK)
            n_keys = (i + 1) * ATT_BLOCK
            q = qs_ref[rows, hs]
            sel = None
            if i > n_sel:
                q_hi, q_lo = _split_bf16(q)
                q4 = jnp.concatenate([q_hi, q_hi, q_lo, q_lo], axis=1)
                g = jnp.dot(q4, kr4[:, :i * LANES], preferred_element_type=F32)
                gs = [g[:, n * LANES:(n + 1) * LANES] for n in range(i)]
                cnt = [jnp.full((ATT_BLOCK, LANES), float(i - 1 - a), F32) for a in range(i)]
                for a in range(i):
                    for b in range(a + 1, i):
                        a_wins = jnp.where(gs[a] >= gs[b], 1.0, 0.0)
                        cnt[b] = cnt[b] + a_wins
                        cnt[a] = cnt[a] - a_wins
                sel = [c < n_sel for c in cnt]
            s = jnp.dot((q * scale).astype(BF16), kt_bf[:, :n_keys], preferred_element_type=F32)
            pieces = []
            for k in range(halves * (i + 1)):
                n, hf = divmod(k, halves)
                sp = s[:, k * LANES:(k + 1) * LANES]
                if n == i:
                    sp = jnp.where(tril[hf], sp, NEG_INF)
                elif sel is not None:
                    sp = jnp.where(sel[n], sp, NEG_INF)
                pieces.append(sp)
            o_ref[0, rows, hs] = _softmax_pv(pieces, vo_ref[:n_keys, :], hs).astype(o_ref.dtype)


def _dil_body(q_ref, k_ref, v_ref, cos_ref, sin_ref, lm_ref, o_ref, qs_ref, kt_ref, vo_ref):
    seq = q_ref.shape[1]
    nb = seq // ATT_BLOCK
    halves = ATT_BLOCK // LANES
    scale = HEAD_DIM ** -0.5
    heads = q_ref.shape[2] // HEAD_DIM
    _rope_qk(q_ref, k_ref, cos_ref, sin_ref, qs_ref, kt_ref)

    for h in range(heads):
        hs = slice(h * HEAD_DIM, (h + 1) * HEAD_DIM)
        kt_bf = kt_ref[hs, :]
        _values_with_ones(v_ref, hs, vo_ref)
        for i in range(nb):
            rows = slice(i * ATT_BLOCK, (i + 1) * ATT_BLOCK)
            n_keys = (i + 1) * ATT_BLOCK
            qs = (qs_ref[rows, hs] * scale).astype(BF16)
            s = jnp.dot(qs, kt_bf[:, :n_keys], preferred_element_type=F32)
            pieces = []
            for k in range(halves * (i + 1)):
                n, hf = divmod(k, halves)
                pieces.append(s[:, k * LANES:(k + 1) * LANES] + lm_ref[i - n, :, hf * LANES:(hf + 1) * LANES])
            o_ref[0, rows, hs] = _softmax_pv(pieces, vo_ref[:n_keys, :], hs).astype(o_ref.dtype)


def _dilation_log_multiplicity(seq):
    nb = seq // ATT_BLOCK
    r = jnp.arange(ATT_BLOCK)
    d = (jnp.arange(nb)[:, None, None] * ATT_BLOCK + r[None, :, None] - r[None, None, :])
    mult = jnp.zeros(d.shape, F32)
    for window, dil in DIL_PATTERNS:
        mult = mult + ((d >= 0) & (d <= window) & (d % dil == 0)).astype(F32)
    return jnp.where(mult > 0, jnp.log(jnp.maximum(mult, 1.0)), NEG_INF)


def _attention_mixer(body, name, kt_dtype, proj, cols, rope_tables, extra=()):
    bsz, seq, _ = proj.shape
    pair = lambda lane_blk: pl.BlockSpec((1, seq, LANES), lambda b, p: (b, 0, lane_blk + p))
    const = lambda shp: pl.BlockSpec(shp, lambda b, p: (0,) * len(shp))
    return pl.pallas_call(
        body,
        grid=(bsz, MIX_WIDTH // LANES),
        in_specs=[pair(c) for c in cols] + [const((seq, LANES)), const((seq, LANES))]
                 + [const(e.shape) for e in extra],
        out_specs=pl.BlockSpec((1, seq, LANES), lambda b, p: (b, 0, p)),
        out_shape=jax.ShapeDtypeStruct((bsz, seq, MIX_WIDTH), BF16),
        scratch_shapes=[pltpu.VMEM((seq, LANES), F32), pltpu.VMEM((LANES, seq), kt_dtype),
                        pltpu.VMEM((seq, LANES), BF16)],
        compiler_params=_cparams("parallel", "parallel"),
        name=name,
    )(proj, proj, proj, *rope_tables, *extra)


def moba_mixer(proj, cols, rope_tables):
    return _attention_mixer(_moba_body, "moba_mixer", F32, proj, cols, rope_tables)


def dilated_mixer(proj, cols, rope_tables):
    seq = proj.shape[1]
    return _attention_mixer(_dil_body, "dilated_mixer", BF16, proj, cols, rope_tables,
                            extra=(_dilation_log_multiplicity(seq),))


def _merge_xattn_body(h_ref, gates_ref, y0_ref, y1_ref, y2_ref, y3_ref, wb_ref, wout_ref,
                      gain_ref, wq_ref, k_ref, v_ref, wo_ref, o_ref):
    d = h_ref.shape[2]
    dh = d // X_HEADS
    merged = jnp.zeros(h_ref.shape[1:], F32)
    for i, y_ref in enumerate((y0_ref, y1_ref, y2_ref, y3_ref)):
        gate = 1.0 / (1.0 + jnp.exp(-gates_ref[0, :, i * d:(i + 1) * d].astype(F32)))
        merged = merged + gate * _dot(y_ref[0], wb_ref[i])
    h = h_ref[0] + _dot(merged, wout_ref[...])

    ms = jnp.mean(h * h, axis=-1, keepdims=True)
    hn = h * lax.rsqrt(ms + NORM_EPS) * gain_ref[...]
    q = _dot(hn, wq_ref[...]) * dh ** -0.5
    outs = []
    for a in range(X_HEADS):
        cs = slice(a * dh, (a + 1) * dh)
        s = _dot_nt(q[:, cs].astype(BF16), k_ref[0, :, cs])
        m = jnp.max(s, axis=-1, keepdims=True)
        e = jnp.exp(s - m)
        p = e / jnp.sum(e, axis=-1, keepdims=True)
        outs.append(_dot(p, v_ref[0, :, cs]))
    o = jnp.concatenate(outs, axis=-1)
    o_ref[0] = h + _dot(o, wo_ref[...])


def merge_cross_attention(h, proj, ys, w_branch, w_out, kv, gain, w_q, w_o, tq=512):
    bsz, seq, d = h.shape
    mlen = kv.shape[1]
    width = ys[0].shape[2]
    tq = min(tq, seq)
    rows = lambda w: pl.BlockSpec((1, tq, w), lambda b, i: (b, i, 0))
    const = lambda shp: pl.BlockSpec(shp, lambda b, i: (0,) * len(shp))
    return pl.pallas_call(
        _merge_xattn_body,
        grid=(bsz, seq // tq),
        in_specs=[rows(d), rows(N_BRANCHES * d)] + [rows(width)] * N_BRANCHES
                 + [const((N_BRANCHES, width, d)), const((d, d)), const((1, d)), const((d, d)),
                    pl.BlockSpec((1, mlen, d), lambda b, i: (b, 0, 0)),
                    pl.BlockSpec((1, mlen, d), lambda b, i: (b, 0, 1)),
                    const((d, d))],
        out_specs=rows(d),
        out_shape=jax.ShapeDtypeStruct((bsz, seq, d), F32),
        compiler_params=_cparams("parallel", "parallel"),
        name="merge_cross_attention",
    )(h, proj, *ys, w_branch, w_out, gain.reshape(1, d), w_q, kv, kv, w_o)


_SUBLANES = 8


def _sorting_network(n):
    pairs, p = [], 1
    while p < n:
        k = p
        while k >= 1:
            for j in range(k % p, n - k, 2 * k):
                for i in range(min(k, n - j - k)):
                    if (i + j) // (2 * p) == (i + j + k) // (2 * p):
                        pairs.append((i + j, i + j + k))
            k //= 2
        p *= 2
    return pairs


def _top_rows(vals, count):
    n_tiles = vals.shape[0] // _SUBLANES
    levels = [vals[_SUBLANES * g:_SUBLANES * (g + 1)] for g in range(n_tiles)]
    size = 1 << (n_tiles - 1).bit_length()
    neg = jnp.full(levels[0].shape, -jnp.inf, F32)
    levels += [neg] * (size - n_tiles)
    for a, b in _sorting_network(size):
        levels[a], levels[b] = jnp.maximum(levels[a], levels[b]), jnp.minimum(levels[a], levels[b])
    levels = levels[:n_tiles]
    tops = []
    for r in range(count):
        m = jnp.max(levels[0], axis=0, keepdims=True)
        tops.append(m)
        left = count - r - 1
        hit = levels[0] == m
        for g in range(min(len(levels), left)):
            below = levels[g + 1] if g + 1 < len(levels) else neg
            levels[g] = jnp.where(hit, below, levels[g])
    return tops


def _count_greater(sorted_rows, x):
    count = jnp.zeros(x.shape, F32)
    for r, row in enumerate(sorted_rows):
        count = jnp.where(row > x, float(r + 1), count)
    return count


_N_RANKS = PEER_TOPK + 1
_RANK_ROWS = 24


_N_CAND = sum(_N_RANKS // (i + 1) for i in range(_N_RANKS))
_CAND_ROWS = -(-_N_CAND // 8) * 8


def _pair_candidates(a1, a2_ref, cand_ref):
    cand_ref[_N_CAND // 8 * 8:, :] = jnp.full((_CAND_ROWS - _N_CAND // 8 * 8, cand_ref.shape[1]), -jnp.inf, F32)
    row = 0
    for i in range(_N_RANKS):
        n_j = _N_RANKS // (i + 1)
        cand_ref[row:row + n_j, :] = a1[i] + a2_ref[0:n_j, :]
        row += n_j
    return cand_ref[...]


def _router_body(h_ref, gain_ref, wq_hi_ref, wq_lo_ref, keys_ref, rank2_ref, e2_ref, cut_ref, e1_ref,
                 a2_ref, cand_ref):
    h = h_ref[...]
    ms = jnp.mean(h * h, axis=-1, keepdims=True)
    x_hi, x_lo = _split_bf16(h * lax.rsqrt(ms + NORM_EPS) * gain_ref[...])
    q = (jnp.dot(x_hi, wq_hi_ref[...], preferred_element_type=F32)
         + jnp.dot(x_hi, wq_lo_ref[...], preferred_element_type=F32)
         + jnp.dot(x_lo, wq_hi_ref[...], preferred_element_type=F32))
    dk = keys_ref.shape[-1] // 2
    for hd in range(PEER_HEADS):
        s = []
        for half in range(2):
            c0 = (hd * 2 + half) * dk
            q_hi, q_lo = _split_bf16(q[:, c0:c0 + dk])
            keys2 = keys_ref[hd, half]
            s.append(_dot_nt(keys2, jnp.concatenate([q_hi, q_hi], axis=1))
                     + _dot_nt(keys2, jnp.concatenate([q_lo, q_lo], axis=1)))
        s1, s2 = s
        a1 = _top_rows(s1, _N_RANKS)
        a2 = _top_rows(s2, _N_RANKS)
        for r in range(_N_RANKS):
            a2_ref[r:r + 1, :] = a2[r]
        cand = _pair_candidates(a1, a2_ref, cand_ref)
        top = _top_rows(cand, _N_RANKS)
        tau = 0.5 * (top[PEER_TOPK - 1] + top[PEER_TOPK])
        z = jnp.sum(jnp.where(cand > tau, jnp.exp(cand - (a1[0] + a2[0])), 0.0), axis=0, keepdims=True)
        rank2_ref[hd] = _count_greater(a2, s2).astype(rank2_ref.dtype)
        e2_ref[hd] = jnp.exp(s2 - a2[0]).astype(e2_ref.dtype)
        cut_ref[hd] = _count_greater(a2[:PEER_TOPK], tau - s1)
        e1_ref[hd] = jnp.exp(s1 - a1[0]) * (0.5 / z)


def peer_router(h, gain, w_q, sub_keys, tt=256):
    t, d = h.shape
    tt = min(tt, t)
    nk = sub_keys.shape[2]
    wq_hi, wq_lo = _split_bf16(w_q)
    keys2 = jnp.concatenate(_split_bf16(sub_keys), axis=-1)
    out = [jax.ShapeDtypeStruct((PEER_HEADS, nk, t), dt) for dt in (BF16, BF16, F32, F32)]
    ospec = pl.BlockSpec((PEER_HEADS, nk, tt), lambda i: (0, 0, i))
    return pl.pallas_call(
        _router_body,
        grid=(t // tt,),
        in_specs=[pl.BlockSpec((tt, d), lambda i: (i, 0)),
                  pl.BlockSpec((1, d), lambda i: (0, 0)),
                  pl.BlockSpec(w_q.shape, lambda i: (0, 0)),
                  pl.BlockSpec(w_q.shape, lambda i: (0, 0)),
                  pl.BlockSpec(keys2.shape, lambda i: (0, 0, 0, 0))],
        out_specs=[ospec] * 4,
        out_shape=out,
        scratch_shapes=[pltpu.VMEM((_RANK_ROWS, tt), F32), pltpu.VMEM((_CAND_ROWS, tt), F32)],
        compiler_params=_cparams("parallel"),
        name="peer_router",
    )(h, gain.reshape(1, d), wq_hi, wq_lo, keys2)


def _gelu_x2(x):
    return x * (1.0 + lax.erf(x * (2.0 ** -0.5)))


_ROWS_PER_STEP = 8
_J_CHUNK = 32
_ROW_GROUP = 4
_TOK_COLS = 256
_MM_SPLIT = 4
_BF16_ROWS = 16


def _experts_body(h_ref, gain_ref, u0_ref, un_ref, vt_ref, rank2_ref, e2_ref, cut_ref, e1_ref, o_ref,
                  xn_ref, acc_ref, act_a_ref, act_b_ref, p_a_ref, p_b_ref, cutb_ref, e1b_ref):
    step = pl.program_id(1)
    n_tiles = pl.num_programs(1) - 1
    nk = rank2_ref.shape[1]
    tt = h_ref.shape[0]
    n_cols = tt // _TOK_COLS
    act_refs = (act_a_ref, act_b_ref)
    p_refs = (p_a_ref, p_b_ref)

    @pl.when(step == 0)
    def _():
        h = h_ref[...]
        ms = jnp.mean(h * h, axis=-1, keepdims=True)
        xn_ref[...] = (h * lax.rsqrt(ms + NORM_EPS) * gain_ref[...]).astype(BF16)
        acc_ref[...] = jnp.zeros_like(acc_ref)
        p_refs[1][...] = jnp.zeros(p_refs[1].shape, BF16)
        act_refs[0][...] = _dot_nt(u0_ref[...], xn_ref[...])

    def main_block(act_cur, act_nxt, p_cur, p_prev):
        d_rows = acc_ref.shape[0] // _MM_SPLIT
        e_rows = un_ref.shape[0] // _MM_SPLIT

        def v_piece(k, c):
            rs, cs = slice(k * d_rows, (k + 1) * d_rows), slice(c * _TOK_COLS, (c + 1) * _TOK_COLS)
            acc_ref[rs, cs] += jnp.dot(vt_ref[rs, :], p_prev[:, cs], preferred_element_type=F32)

        def act_piece(k, c):
            rs, cs = slice(k * e_rows, (k + 1) * e_rows), slice(c * _TOK_COLS, (c + 1) * _TOK_COLS)
            act_nxt[rs, cs] = _dot_nt(un_ref[rs, :], xn_ref[cs, :])

        pieces = [(f, k, c) for k in range(_MM_SPLIT) for c in range(n_cols) for f in (v_piece, act_piece)]

        for hd in range(PEER_HEADS):
            cut8 = cut_ref[hd]
            e18 = e1_ref[hd]
            for r in range(_ROWS_PER_STEP):
                cutb_ref[hd, r] = jnp.broadcast_to(cut8[r:r + 1], (_BF16_ROWS, tt)).astype(BF16)
                e1b_ref[hd, r] = jnp.broadcast_to(e18[r:r + 1], (_BF16_ROWS, tt)).astype(BF16)

        reps = _J_CHUNK // _BF16_ROWS
        chunks = [(c, jc, rg) for c in range(n_cols) for jc in range(nk // _J_CHUNK)
                  for rg in range(_ROWS_PER_STEP // _ROW_GROUP)]
        per_chunk = -(-len(pieces) // len(chunks))
        for n, (c, jc, rg) in enumerate(chunks):
            for f, k, cc in pieces[n * per_chunk:(n + 1) * per_chunk]:
                f(k, cc)
            ts = slice(c * _TOK_COLS, (c + 1) * _TOK_COLS)
            js = slice(jc * _J_CHUNK, (jc + 1) * _J_CHUNK)
            gates = [jnp.zeros((_J_CHUNK, _TOK_COLS), BF16) for _ in range(_ROW_GROUP)]
            for hd in range(PEER_HEADS):
                rank2 = rank2_ref[hd, js, ts]
                e2 = e2_ref[hd, js, ts]
                for k in range(_ROW_GROUP):
                    r = rg * _ROW_GROUP + k
                    cut = jnp.concatenate([cutb_ref[hd, r, :, ts]] * reps, axis=0)
                    e1 = jnp.concatenate([e1b_ref[hd, r, :, ts]] * reps, axis=0)
                    gates[k] = gates[k] + jnp.where(rank2 < cut, e2 * e1, jnp.zeros_like(e2))
            for k in range(_ROW_GROUP):
                rows = slice((rg * _ROW_GROUP + k) * nk + jc * _J_CHUNK,
                             (rg * _ROW_GROUP + k) * nk + (jc + 1) * _J_CHUNK)
                p_cur[rows, ts] = _gelu_x2(act_cur[rows, ts]).astype(BF16) * gates[k]

    for parity in range(2):
        @pl.when((step % 2 == parity) & (step < n_tiles))
        def _():
            main_block(act_refs[parity], act_refs[1 - parity], p_refs[parity], p_refs[1 - parity])

    last_p = p_refs[(rank2_ref.shape[1] // _ROWS_PER_STEP - 1) % 2]

    @pl.when(step == n_tiles)
    def _():
        acc = acc_ref[...] + jnp.dot(vt_ref[...], last_p[...], preferred_element_type=F32)
        o_ref[...] = h_ref[...] + acc.T


def peer_experts(h, gain, u, v_t, rank2, e2, cut, e1, tt=512):
    t, d = h.shape
    nk = rank2.shape[1]
    tt = min(tt, t)
    et = _ROWS_PER_STEP * nk
    n_tiles = nk // _ROWS_PER_STEP
    rspec = pl.BlockSpec((PEER_HEADS, nk, tt), lambda i, s: (0, 0, i))
    row_spec = pl.BlockSpec((PEER_HEADS, _ROWS_PER_STEP, tt), lambda i, s: (0, jnp.minimum(s, n_tiles - 1), i))
    return pl.pallas_call(
        _experts_body,
        grid=(t // tt, n_tiles + 1),
        in_specs=[pl.BlockSpec((tt, d), lambda i, s: (i, 0)),
                  pl.BlockSpec((1, d), lambda i, s: (0, 0)),
                  pl.BlockSpec((et, d), lambda i, s: (0, 0)),
                  pl.BlockSpec((et, d), lambda i, s: (jnp.minimum(s + 1, n_tiles - 1), 0)),
                  pl.BlockSpec((None, d, et), lambda i, s: (jnp.maximum(s - 1, 0), 0, 0)),
                  rspec, rspec, row_spec, row_spec],
        out_specs=pl.BlockSpec((tt, d), lambda i, s: (i, 0)),
        out_shape=jax.ShapeDtypeStruct((t, d), F32),
        scratch_shapes=[pltpu.VMEM((tt, d), BF16), pltpu.VMEM((d, tt), F32),
                        pltpu.VMEM((et, tt), F32), pltpu.VMEM((et, tt), F32),
                        pltpu.VMEM((et, tt), BF16), pltpu.VMEM((et, tt), BF16),
                        pltpu.VMEM((PEER_HEADS, _ROWS_PER_STEP, _BF16_ROWS, tt), BF16),
                        pltpu.VMEM((PEER_HEADS, _ROWS_PER_STEP, _BF16_ROWS, tt), BF16)],
        compiler_params=_cparams("parallel", "arbitrary"),
        name="peer_experts",
    )(h, gain.reshape(1, d), u, u, v_t, rank2, e2, cut, e1)


def _transpose_cast_body(x_ref, o_ref):
    o_ref[...] = x_ref[...].T.astype(o_ref.dtype)


def transpose_cast(x3, index, dtype, tile, blk=1024):
    _, r, c = x3.shape
    per = tile // blk
    return pl.pallas_call(
        _transpose_cast_body,
        grid=(r // blk, c // blk),
        in_specs=[pl.BlockSpec((None, blk, blk), lambda i, j: (index, i, j))],
        out_specs=pl.BlockSpec((None, blk, blk), lambda i, j: (i // per, j, i % per)),
        out_shape=jax.ShapeDtypeStruct((r // tile, c, tile), dtype),
        compiler_params=_cparams("parallel", "parallel"),
        name="transpose_cast",
    )(x3)


def _final_norm_body(x_ref, g_ref, o_ref):
    x = x_ref[...]
    ms = jnp.mean(x * x, axis=-1, keepdims=True)
    o_ref[...] = x * lax.rsqrt(ms + NORM_EPS) * g_ref[...]


def final_norm(x, gain, tm=1024):
    t, d = x.shape
    tm = min(tm, t)
    return pl.pallas_call(
        _final_norm_body,
        grid=(t // tm,),
        in_specs=[pl.BlockSpec((tm, d), lambda i: (i, 0)), pl.BlockSpec((1, d), lambda i: (0, 0))],
        out_specs=pl.BlockSpec((tm, d), lambda i: (i, 0)),
        out_shape=jax.ShapeDtypeStruct((t, d), F32),
        compiler_params=_cparams("parallel"),
        name="final_norm",
    )(x, gain.reshape(1, d))


_GATE_BLOCKS = N_BRANCHES * 8
_COL_Z = _GATE_BLOCKS
_COL_RET = _COL_Z + 4
_COL_MOBA = _COL_RET + 16
_COL_DIL = _COL_MOBA + 12
_COL_XBC = _COL_DIL + 12
_PROJ_BLOCKS = 84


def _pack_w_in(w_in_l):
    ssd_conv_dim = MIX_WIDTH + 2 * SSD_GROUPS * SSD_STATE
    o_xbc = MIX_WIDTH
    o_dt = o_xbc + ssd_conv_dim
    o_rest = o_dt + N_HEADS
    o_gate = o_rest + 10 * MIX_WIDTH
    d_model = w_in_l.shape[0]
    used = _COL_XBC * LANES + ssd_conv_dim
    pad = jnp.zeros((d_model, _PROJ_BLOCKS * LANES - used), w_in_l.dtype)
    main = jnp.concatenate([w_in_l[:, o_gate:], w_in_l[:, :o_xbc], w_in_l[:, o_rest:o_gate],
                            w_in_l[:, o_xbc:o_dt], pad], axis=1).astype(BF16)
    w_dt = jnp.pad(w_in_l[:, o_dt:o_rest], ((0, 0), (0, LANES - N_HEADS))).astype(BF16)
    return main, w_dt


def kernel(x, mem, mix_norm, w_in, ssd_conv_w, ssd_conv_b, ssd_dt_bias, ssd_a_log, ssd_d, ssd_norm, ret_norm, w_branch, w_out, x_norm, w_xq, w_xkv, w_xo, ffn_norm, w_pq, peer_sub_keys, peer_u, peer_v, final_norm_gain):
    bsz, seq, d = x.shape
    depth = w_in.shape[0]
    t = bsz * seq
    h = x.reshape(t, d)
    mem2 = mem.reshape(-1, d)
    rope_tables = _rope_tables(seq)

    for layer in range(depth):
        w_main, w_dt = _pack_w_in(w_in[layer])
        proj, dt_raw = norm_matmul(h, mix_norm[layer], w_main, w_dt, tm=2048, tn=1536, out_dtype=BF16)
        proj = proj.reshape(bsz, seq, -1)
        dt_raw = dt_raw.reshape(bsz, seq, LANES)

        y_a = ssd_mixer(proj, _COL_XBC, _COL_Z, dt_raw, ssd_conv_w[layer], ssd_conv_b[layer],
                        ssd_dt_bias[layer], ssd_a_log[layer], ssd_d[layer], ssd_norm[layer])
        y_b = retention_mixer(proj, [_COL_RET // 4 + i for i in range(4)], rope_tables, ret_norm[layer])
        y_c = moba_mixer(proj, [_COL_MOBA + 4 * i for i in range(3)], rope_tables)
        y_d = dilated_mixer(proj, [_COL_DIL + 4 * i for i in range(3)], rope_tables)
        kv = matmul(mem2, w_xkv[layer].astype(BF16), tm=512, tn=512, out_dtype=BF16)
        h = merge_cross_attention(h.reshape(bsz, seq, d), proj, (y_a, y_b, y_c, y_d),
                                  w_branch[layer].astype(BF16), w_out[layer].astype(BF16),
                                  kv.reshape(bsz, -1, 2 * d), x_norm[layer],
                                  w_xq[layer].astype(BF16), w_xo[layer].astype(BF16)).reshape(t, d)
        rank2, e2, cut, e1 = peer_router(h, ffn_norm[layer], w_pq[layer], peer_sub_keys[layer])
        h = peer_experts(h, ffn_norm[layer], peer_u[layer].astype(BF16),
                         transpose_cast(peer_v, layer, BF16, _ROWS_PER_STEP * PEER_NKEYS), rank2, e2, cut, e1)

    return final_norm(h, final_norm_gain).reshape(bsz, seq, d)
```

```python
import functools

import jax
import jax.numpy as jnp
from jax import lax
from jax.experimental import pallas as pl
from jax.experimental.pallas import tpu as pltpu

F32 = jnp.float32
BF16 = jnp.bfloat16
HI = lax.Precision.HIGHEST

NORM_EPS = 1e-6
NEG_INF = -1e30
ROPE_THETA = 10000.0
HEAD_DIM = 64
N_HEADS = 8
MIX_WIDTH = N_HEADS * HEAD_DIM
SSD_GROUPS = 2
SSD_STATE = 64
SSD_CONV = 4
CHUNK = 128
SSD_CHUNK = 256
ATT_BLOCK = 256
MOBA_TOPK = 3
DIL_PATTERNS = ((128, 1), (512, 4), (2048, 16))
N_BRANCHES = 4
X_HEADS = 4
PEER_HEADS = 8
PEER_NKEYS = 128
PEER_TOPK = 16
LANES = 128
VMEM_LIMIT = 56 * 1024 * 1024


def _cparams(*sem):
    return pltpu.CompilerParams(dimension_semantics=sem, vmem_limit_bytes=VMEM_LIMIT)


def _dot(a, b):
    return jnp.dot(a.astype(BF16), b.astype(BF16), preferred_element_type=F32)


def _dot_nt(a, b):
    return lax.dot_general(a, b, (((1,), (1,)), ((), ())), preferred_element_type=F32)


def _silu(x):
    return x * (1.0 / (1.0 + jnp.exp(-x)))


def _norm_matmul_body(x_ref, g_ref, w_ref, ws_ref, o_ref, os_ref, xn_ref):
    @pl.when(pl.program_id(1) == 0)
    def _():
        x = x_ref[...]
        ms = jnp.mean(x * x, axis=-1, keepdims=True)
        xn_ref[...] = (x * lax.rsqrt(ms + NORM_EPS) * g_ref[...]).astype(xn_ref.dtype)
        os_ref[...] = jnp.dot(xn_ref[...], ws_ref[...], preferred_element_type=F32)

    o_ref[...] = jnp.dot(xn_ref[...], w_ref[...], preferred_element_type=F32).astype(o_ref.dtype)


def norm_matmul(x, gain, w, w_side, *, tm, tn, out_dtype):
    t, k = x.shape
    n = w.shape[1]
    ns = w_side.shape[1]
    assert t % tm == 0 and n % tn == 0
    return pl.pallas_call(
        _norm_matmul_body,
        grid=(t // tm, n // tn),
        in_specs=[pl.BlockSpec((tm, k), lambda i, j: (i, 0)),
                  pl.BlockSpec((1, k), lambda i, j: (0, 0)),
                  pl.BlockSpec((k, tn), lambda i, j: (0, j)),
                  pl.BlockSpec((k, ns), lambda i, j: (0, 0))],
        out_specs=[pl.BlockSpec((tm, tn), lambda i, j: (i, j)), pl.BlockSpec((tm, ns), lambda i, j: (i, 0))],
        out_shape=[jax.ShapeDtypeStruct((t, n), out_dtype), jax.ShapeDtypeStruct((t, ns), F32)],
        scratch_shapes=[pltpu.VMEM((tm, k), BF16)],
        compiler_params=_cparams("parallel", "arbitrary"),
        name="norm_matmul",
    )(x, gain.reshape(1, k), w, w_side)


def _matmul_body(x_ref, w_ref, o_ref):
    o_ref[...] = _dot(x_ref[...], w_ref[...]).astype(o_ref.dtype)


def matmul(x, w, *, tm, tn, out_dtype):
    t, k = x.shape
    n = w.shape[1]
    assert t % tm == 0 and n % tn == 0
    return pl.pallas_call(
        _matmul_body,
        grid=(t // tm, n // tn),
        in_specs=[pl.BlockSpec((tm, k), lambda i, j: (i, 0)), pl.BlockSpec((k, tn), lambda i, j: (0, j))],
        out_specs=pl.BlockSpec((tm, tn), lambda i, j: (i, j)),
        out_shape=jax.ShapeDtypeStruct((t, n), out_dtype),
        compiler_params=_cparams("parallel", "parallel"),
        name="matmul",
    )(x, w)


def _ssd_body(x_ref, b_ref, c_ref, z_ref, dtraw_ref, convw_ref, convb_ref, dtbias_ref, alog_ref,
              dskip_ref, gain_ref, o_ref, state_ref, prev_ref):
    seq = x_ref.shape[1]
    CHUNK = SSD_CHUNK
    nchunks = seq // CHUNK
    rep = N_HEADS // SSD_GROUPS
    gn = SSD_GROUPS * SSD_STATE
    state_ref[...] = jnp.zeros_like(state_ref)
    prev_ref[...] = jnp.zeros_like(prev_ref)
    row = lax.broadcasted_iota(jnp.int32, (CHUNK, CHUNK), 0)
    col = lax.broadcasted_iota(jnp.int32, (CHUNK, CHUNK), 1)
    tril = row >= col
    tril_f = tril.astype(F32)
    conv_row = lax.broadcasted_iota(jnp.int32, (CHUNK, MIX_WIDTH + 2 * gn), 0)
    neg_a = -jnp.exp(alog_ref[...])

    def chunk(c, carry):
        for bb in range(x_ref.shape[0]):
            chunk_of(bb, c)
        return carry

    def chunk_of(bb, c):
        r0 = pl.multiple_of(c * CHUNK, CHUNK)
        rows = pl.ds(r0, CHUNK)
        raw = jnp.concatenate([x_ref[bb, rows, :], b_ref[bb, rows, :], c_ref[bb, rows, :]], axis=1).astype(F32)
        prev = prev_ref[bb]
        conv = raw * convw_ref[SSD_CONV - 1:SSD_CONV, :] + convb_ref[...]
        for k in range(1, SSD_CONV):
            shifted = jnp.where(conv_row < k, pltpu.roll(prev, k, axis=0), pltpu.roll(raw, k, axis=0))
            conv = conv + shifted * convw_ref[SSD_CONV - 1 - k:SSD_CONV - k, :]
        prev_ref[bb] = raw
        xbc = _silu(conv)
        xs, cm = xbc[:, :MIX_WIDTH], xbc[:, MIX_WIDTH + gn:]
        bm_t = xbc[:, MIX_WIDTH:MIX_WIDTH + gn].T
        pre = dtraw_ref[bb, rows, :] + dtbias_ref[...]
        dt = jnp.maximum(pre, 0.0) + jnp.log1p(jnp.exp(-jnp.abs(pre)))
        acum = jnp.dot(tril_f, dt * neg_a, precision=HI, preferred_element_type=F32)
        acum_t = acum.T
        ys = []
        for h in range(N_HEADS):
            g = h // rep
            hs = slice(h * HEAD_DIM, (h + 1) * HEAD_DIM)
            gs = slice(g * SSD_STATE, (g + 1) * SSD_STATE)
            a_col = acum[:, h:h + 1]
            a_row = acum_t[h:h + 1, :]
            a_last = acum[CHUNK - 1:CHUNK, h:h + 1]
            decay = jnp.exp(jnp.where(tril, a_col - a_row, -jnp.inf))
            cc = cm[:, gs]
            bt = bm_t[gs, :]
            x_h = xs[:, hs]
            xdt = x_h * dt[:, h:h + 1]
            scores = _dot(cc, bt) * decay
            prev = state_ref[bb, h]
            y = _dot(scores, xdt) + _dot(cc, prev) * jnp.exp(a_col)
            state_ref[bb, h] = prev * jnp.exp(a_last) + _dot(bt * jnp.exp(a_last - a_row), xdt)
            ys.append(y + x_h * dskip_ref[0:1, hs])
        y = jnp.concatenate(ys, axis=-1)
        y = y * _silu(z_ref[bb, rows, :].astype(F32))
        ms = jnp.mean(y * y, axis=-1, keepdims=True)
        o_ref[bb, rows, :] = (y * lax.rsqrt(ms + NORM_EPS) * gain_ref[...]).astype(o_ref.dtype)

    lax.fori_loop(0, nchunks, chunk, 0)


def ssd_mixer(proj, x_col, z_col, dt_raw, conv_w, conv_b, dt_bias, a_log, d_skip, norm_gain):
    bsz, seq, _ = proj.shape
    width = MIX_WIDTH
    gn = SSD_GROUPS * SSD_STATE
    conv_dim = width + 2 * gn
    lanes8 = lambda v: jnp.pad(v, (0, LANES - N_HEADS)).reshape(1, LANES)
    nb = 1
    wide = lambda blk: pl.BlockSpec((nb, seq, width), lambda b: (b, 0, blk))
    narrow = lambda blk: pl.BlockSpec((nb, seq, LANES), lambda b: (b, 0, blk))
    const = lambda shp: pl.BlockSpec(shp, lambda b: (0,) * len(shp))
    return pl.pallas_call(
        _ssd_body,
        grid=(bsz // nb,),
        in_specs=[wide(x_col // 4), narrow(x_col + 4), narrow(x_col + 5), wide(z_col // 4), narrow(0),
                  const((SSD_CONV, conv_dim)), const((1, conv_dim)), const((1, LANES)), const((1, LANES)),
                  const((1, width)), const((1, width))],
        out_specs=pl.BlockSpec((nb, seq, width), lambda b: (b, 0, 0)),
        out_shape=jax.ShapeDtypeStruct((bsz, seq, width), BF16),
        scratch_shapes=[pltpu.VMEM((nb, N_HEADS, SSD_STATE, HEAD_DIM), F32),
                        pltpu.VMEM((nb, SSD_CHUNK, conv_dim), F32)],
        compiler_params=_cparams("parallel"),
        name="ssd_mixer",
    )(proj, proj, proj, proj, dt_raw, conv_w, conv_b.reshape(1, conv_dim), lanes8(dt_bias), lanes8(a_log),
      jnp.repeat(d_skip, HEAD_DIM).reshape(1, width), norm_gain.reshape(1, width))


def _rope_tables(seq):
    half = HEAD_DIM // 2
    inv_freq = ROPE_THETA ** (-jnp.arange(half, dtype=F32) / half)
    ang = jnp.arange(seq, dtype=F32)[:, None] * inv_freq[None, :]
    sin = jnp.sin(ang)
    cos = jnp.tile(jnp.cos(ang), (1, LANES // half))
    sin_signed = jnp.tile(jnp.concatenate([-sin, sin], axis=1), (1, LANES // HEAD_DIM))
    return cos, sin_signed


def _rope_lanes(x, cos, sin_signed):
    width = x.shape[1]
    half = HEAD_DIM // 2
    lane = lax.broadcasted_iota(jnp.int32, x.shape, 1)
    partner = jnp.where(lane % HEAD_DIM < half, pltpu.roll(x, width - half, axis=1), pltpu.roll(x, half, axis=1))
    reps = width // LANES
    if reps > 1:
        cos = jnp.concatenate([cos] * reps, axis=1)
        sin_signed = jnp.concatenate([sin_signed] * reps, axis=1)
    return x * cos + partner * sin_signed


def _ret_body(q_ref, k_ref, v_ref, g_ref, cos_ref, sin_ref, dmat_ref, zeta_ref, xi_ref, cdec_ref, gain_ref,
              o_ref, state_ref):
    seq = q_ref.shape[1]
    nchunks = seq // CHUNK
    state_ref[...] = jnp.zeros_like(state_ref)
    width = q_ref.shape[2]
    lane_k = lax.broadcasted_iota(jnp.int32, (width, width), 0) // HEAD_DIM
    lane_l = lax.broadcasted_iota(jnp.int32, (width, width), 1) // HEAD_DIM
    head_avg = jnp.where(lane_k == lane_l, 1.0 / HEAD_DIM, 0.0).astype(BF16)

    def chunk(c, carry):
        for bb in range(q_ref.shape[0]):
            chunk_of(bb, c)
        return carry

    def chunk_of(bb, c):
        r0 = pl.multiple_of(c * CHUNK, CHUNK)
        rows = pl.ds(r0, CHUNK)
        cos, sin = cos_ref[rows, :], sin_ref[rows, :]
        q = _rope_lanes(q_ref[bb, rows, :].astype(F32), cos, sin).astype(BF16)
        k = _rope_lanes(k_ref[bb, rows, :].astype(F32), cos, sin) * HEAD_DIM ** -0.5
        k_t = k.T
        ys = []
        for h in range(N_HEADS):
            hs = slice(h * HEAD_DIM, (h + 1) * HEAD_DIM)
            qc = q[:, hs]
            kt = k_t[hs, :]
            vc = v_ref[bb, rows, hs]
            inner = _dot(qc, kt) * dmat_ref[h]
            prev = state_ref[bb, h]
            y = _dot(inner, vc) + _dot(qc, prev) * xi_ref[h]
            state_ref[bb, h] = prev * cdec_ref[h] + _dot(kt * zeta_ref[h], vc)
            ys.append(y)
        y = jnp.concatenate(ys, axis=-1)
        mu = sum(jnp.dot(part, head_avg, preferred_element_type=F32) for part in _split_bf16(y))
        yc = y - mu
        var = sum(jnp.dot(part, head_avg, preferred_element_type=F32) for part in _split_bf16(yc * yc))
        y = yc * lax.rsqrt(var + NORM_EPS) * gain_ref[...]
        o_ref[bb, rows, :] = (_silu(g_ref[bb, rows, :].astype(F32)) * y).astype(o_ref.dtype)

    lax.fori_loop(0, nchunks, chunk, 0)


def retention_mixer(proj, cols, rope_tables, norm_gain):
    bsz, seq, _ = proj.shape
    width = MIX_WIDTH
    log_gamma = jnp.log1p(-jnp.exp2(-5.0 - jnp.arange(N_HEADS, dtype=F32)))
    idx = jnp.arange(CHUNK, dtype=F32)
    rel = idx[:, None] - idx[None, :]
    dmat = jnp.where(rel >= 0, jnp.exp(jnp.maximum(rel, 0.0)[None] * log_gamma[:, None, None]), 0.0)
    zeta = jnp.exp((CHUNK - 1.0 - idx)[None, None, :] * log_gamma[:, None, None])
    xi = jnp.exp((idx + 1.0)[None, :, None] * log_gamma[:, None, None])
    cdec = jnp.exp(CHUNK * log_gamma)[:, None, None]
    nb = 1
    col = lambda blk: pl.BlockSpec((nb, seq, width), lambda b: (b, 0, blk))
    const = lambda shp: pl.BlockSpec(shp, lambda b: (0,) * len(shp))
    return pl.pallas_call(
        _ret_body,
        grid=(bsz // nb,),
        in_specs=[col(c) for c in cols] + [const((seq, LANES)), const((seq, LANES)),
                  const((N_HEADS, CHUNK, CHUNK)), const((N_HEADS, 1, CHUNK)),
                  const((N_HEADS, CHUNK, 1)), const((N_HEADS, 1, 1)), const((1, width))],
        out_specs=pl.BlockSpec((nb, seq, width), lambda b: (b, 0, 0)),
        out_shape=jax.ShapeDtypeStruct((bsz, seq, width), BF16),
        scratch_shapes=[pltpu.VMEM((nb, N_HEADS, HEAD_DIM, HEAD_DIM), F32)],
        compiler_params=_cparams("parallel"),
        name="retention_mixer",
    )(proj, proj, proj, proj, *rope_tables, dmat, zeta, xi, cdec, norm_gain.reshape(1, width))


def _softmax_pv(pieces, v_ones, hs):
    m = jnp.max(functools.reduce(jnp.maximum, pieces), axis=-1, keepdims=True)
    p = jnp.concatenate([jnp.exp(sp - m) for sp in pieces], axis=1).astype(BF16)
    o = jnp.dot(p, v_ones, preferred_element_type=F32)
    return (o / pltpu.roll(o, HEAD_DIM, axis=1))[:, hs]


def _values_with_ones(v_ref, hs, vo_ref):
    v = v_ref[0]
    lane = lax.broadcasted_iota(jnp.int32, v.shape, 1)
    vo_ref[...] = jnp.where((lane >= hs.start) & (lane < hs.stop), v, jnp.ones_like(v))


def _split_bf16(x):
    hi = x.astype(BF16)
    return hi, (x - hi.astype(F32)).astype(BF16)


def _rope_qk(q_ref, k_ref, cos_ref, sin_ref, qs_ref, kt_ref):
    cos, sin = cos_ref[...], sin_ref[...]
    qs_ref[...] = _rope_lanes(q_ref[0].astype(F32), cos, sin)
    kt_ref[...] = _rope_lanes(k_ref[0].astype(F32), cos, sin).T.astype(kt_ref.dtype)


def _moba_body(q_ref, k_ref, v_ref, cos_ref, sin_ref, o_ref, qs_ref, kt_ref, vo_ref):
    seq = q_ref.shape[1]
    nb = seq // ATT_BLOCK
    halves = ATT_BLOCK // LANES
    scale = HEAD_DIM ** -0.5
    heads = q_ref.shape[2] // HEAD_DIM
    n_sel = min(MOBA_TOPK, nb - 1)
    row = lax.broadcasted_iota(jnp.int32, (ATT_BLOCK, LANES), 0)
    col = lax.broadcasted_iota(jnp.int32, (ATT_BLOCK, LANES), 1)
    tril = [row >= col + hf * LANES for hf in range(halves)]
    _rope_qk(q_ref, k_ref, cos_ref, sin_ref, qs_ref, kt_ref)

    for h in range(heads):
        hs = slice(h * HEAD_DIM, (h + 1) * HEAD_DIM)
        kt = kt_ref[hs, :]
        kt_bf = kt.astype(BF16)
        _values_with_ones(v_ref, hs, vo_ref)
        reps = [jnp.broadcast_to(jnp.mean(kt[:, n * ATT_BLOCK:(n + 1) * ATT_BLOCK], axis=1, keepdims=True),
                                 (HEAD_DIM, LANES)) for n in range(nb - 1)]
        kr_hi, kr_lo = _split_bf16(jnp.concatenate(reps, axis=1))
        kr4 = jnp.concatenate([kr_hi, kr_lo, kr_hi, kr_lo], axis=0)

        for i in range(nb):
            rows = slice(i * ATT_BLOCK, (i + 1) * ATT_BLOCK)
            n_keys = (i + 1) * ATT_BLOCK
            q = qs_ref[rows, hs]
            sel = None
            if i > n_sel:
                q_hi, q_lo = _split_bf16(q)
                q4 = jnp.concatenate([q_hi, q_hi, q_lo, q_lo], axis=1)
                g = jnp.dot(q4, kr4[:, :i * LANES], preferred_element_type=F32)
                gs = [g[:, n * LANES:(n + 1) * LANES] for n in range(i)]
                cnt = [jnp.full((ATT_BLOCK, LANES), float(i - 1 - a), F32) for a in range(i)]
                for a in range(i):
                    for b in range(a + 1, i):
                        a_wins = jnp.where(gs[a] >= gs[b], 1.0, 0.0)
                        cnt[b] = cnt[b] + a_wins
                        cnt[a] = cnt[a] - a_wins
                sel = [c < n_sel for c in cnt]
            s = jnp.dot((q * scale).astype(BF16), kt_bf[:, :n_keys], preferred_element_type=F32)
            pieces = []
            for k in range(halves * (i + 1)):
                n, hf = divmod(k, halves)
                sp = s[:, k * LANES:(k + 1) * LANES]
                if n == i:
                    sp = jnp.where(tril[hf], sp, NEG_INF)
                elif sel is not None:
                    sp = jnp.where(sel[n], sp, NEG_INF)
                pieces.append(sp)
            o_ref[0, rows, hs] = _softmax_pv(pieces, vo_ref[:n_keys, :], hs).astype(o_ref.dtype)


def _dil_body(q_ref, k_ref, v_ref, cos_ref, sin_ref, lm_ref, o_ref, qs_ref, kt_ref, vo_ref):
    seq = q_ref.shape[1]
    nb = seq // ATT_BLOCK
    halves = ATT_BLOCK // LANES
    scale = HEAD_DIM ** -0.5
    heads = q_ref.shape[2] // HEAD_DIM
    _rope_qk(q_ref, k_ref, cos_ref, sin_ref, qs_ref, kt_ref)

    for h in range(heads):
        hs = slice(h * HEAD_DIM, (h + 1) * HEAD_DIM)
        kt_bf = kt_ref[hs, :]
        _values_with_ones(v_ref, hs, vo_ref)
        for i in range(nb):
            rows = slice(i * ATT_BLOCK, (i + 1) * ATT_BLOCK)
            n_keys = (i + 1) * ATT_BLOCK
            qs = (qs_ref[rows, hs] * scale).astype(BF16)
            s = jnp.dot(qs, kt_bf[:, :n_keys], preferred_element_type=F32)
            pieces = []
            for k in range(halves * (i + 1)):
                n, hf = divmod(k, halves)
                pieces.append(s[:, k * LANES:(k + 1) * LANES] + lm_ref[i - n, :, hf * LANES:(hf + 1) * LANES])
            o_ref[0, rows, hs] = _softmax_pv(pieces, vo_ref[:n_keys, :], hs).astype(o_ref.dtype)


def _dilation_log_multiplicity(seq):
    nb = seq // ATT_BLOCK
    r = jnp.arange(ATT_BLOCK)
    d = (jnp.arange(nb)[:, None, None] * ATT_BLOCK + r[None, :, None] - r[None, None, :])
    mult = jnp.zeros(d.shape, F32)
    for window, dil in DIL_PATTERNS:
        mult = mult + ((d >= 0) & (d <= window) & (d % dil == 0)).astype(F32)
    return jnp.where(mult > 0, jnp.log(jnp.maximum(mult, 1.0)), NEG_INF)


def _attention_mixer(body, name, kt_dtype, proj, cols, rope_tables, extra=()):
    bsz, seq, _ = proj.shape
    pair = lambda lane_blk: pl.BlockSpec((1, seq, LANES), lambda b, p: (b, 0, lane_blk + p))
    const = lambda shp: pl.BlockSpec(shp, lambda b, p: (0,) * len(shp))
    return pl.pallas_call(
        body,
        grid=(bsz, MIX_WIDTH // LANES),
        in_specs=[pair(c) for c in cols] + [const((seq, LANES)), const((seq, LANES))]
                 + [const(e.shape) for e in extra],
        out_specs=pl.BlockSpec((1, seq, LANES), lambda b, p: (b, 0, p)),
        out_shape=jax.ShapeDtypeStruct((bsz, seq, MIX_WIDTH), BF16),
        scratch_shapes=[pltpu.VMEM((seq, LANES), F32), pltpu.VMEM((LANES, seq), kt_dtype),
                        pltpu.VMEM((seq, LANES), BF16)],
        compiler_params=_cparams("parallel", "parallel"),
        name=name,
    )(proj, proj, proj, *rope_tables, *extra)


def moba_mixer(proj, cols, rope_tables):
    return _attention_mixer(_moba_body, "moba_mixer", F32, proj, cols, rope_tables)


def dilated_mixer(proj, cols, rope_tables):
    seq = proj.shape[1]
    return _attention_mixer(_dil_body, "dilated_mixer", BF16, proj, cols, rope_tables,
                            extra=(_dilation_log_multiplicity(seq),))


def _merge_xattn_body(h_ref, gates_ref, y0_ref, y1_ref, y2_ref, y3_ref, wb_ref, wout_ref,
                      gain_ref, wq_ref, k_ref, v_ref, wo_ref, o_ref):
    d = h_ref.shape[2]
    dh = d // X_HEADS
    merged = jnp.zeros(h_ref.shape[1:], F32)
    for i, y_ref in enumerate((y0_ref, y1_ref, y2_ref, y3_ref)):
        gate = 1.0 / (1.0 + jnp.exp(-gates_ref[0, :, i * d:(i + 1) * d].astype(F32)))
        merged = merged + gate * _dot(y_ref[0], wb_ref[i])
    h = h_ref[0] + _dot(merged, wout_ref[...])

    ms = jnp.mean(h * h, axis=-1, keepdims=True)
    hn = h * lax.rsqrt(ms + NORM_EPS) * gain_ref[...]
    q = _dot(hn, wq_ref[...]) * dh ** -0.5
    outs = []
    for a in range(X_HEADS):
        cs = slice(a * dh, (a + 1) * dh)
        s = _dot_nt(q[:, cs].astype(BF16), k_ref[0, :, cs])
        m = jnp.max(s, axis=-1, keepdims=True)
        e = jnp.exp(s - m)
        p = e / jnp.sum(e, axis=-1, keepdims=True)
        outs.append(_dot(p, v_ref[0, :, cs]))
    o = jnp.concatenate(outs, axis=-1)
    o_ref[0] = h + _dot(o, wo_ref[...])


def merge_cross_attention(h, proj, ys, w_branch, w_out, kv, gain, w_q, w_o, tq=512):
    bsz, seq, d = h.shape
    mlen = kv.shape[1]
    width = ys[0].shape[2]
    tq = min(tq, seq)
    rows = lambda w: pl.BlockSpec((1, tq, w), lambda b, i: (b, i, 0))
    const = lambda shp: pl.BlockSpec(shp, lambda b, i: (0,) * len(shp))
    return pl.pallas_call(
        _merge_xattn_body,
        grid=(bsz, seq // tq),
        in_specs=[rows(d), rows(N_BRANCHES * d)] + [rows(width)] * N_BRANCHES
                 + [const((N_BRANCHES, width, d)), const((d, d)), const((1, d)), const((d, d)),
                    pl.BlockSpec((1, mlen, d), lambda b, i: (b, 0, 0)),
                    pl.BlockSpec((1, mlen, d), lambda b, i: (b, 0, 1)),
                    const((d, d))],
        out_specs=rows(d),
        out_shape=jax.ShapeDtypeStruct((bsz, seq, d), F32),
        compiler_params=_cparams("parallel", "parallel"),
        name="merge_cross_attention",
    )(h, proj, *ys, w_branch, w_out, gain.reshape(1, d), w_q, kv, kv, w_o)


_SUBLANES = 8


def _sorting_network(n):
    pairs, p = [], 1
    while p < n:
        k = p
        while k >= 1:
            for j in range(k % p, n - k, 2 * k):
                for i in range(min(k, n - j - k)):
                    if (i + j) // (2 * p) == (i + j + k) // (2 * p):
                        pairs.append((i + j, i + j + k))
            k //= 2
        p *= 2
    return pairs


def _top_rows(vals, count):
    n_tiles = vals.shape[0] // _SUBLANES
    levels = [vals[_SUBLANES * g:_SUBLANES * (g + 1)] for g in range(n_tiles)]
    size = 1 << (n_tiles - 1).bit_length()
    neg = jnp.full(levels[0].shape, -jnp.inf, F32)
    levels += [neg] * (size - n_tiles)
    for a, b in _sorting_network(size):
        levels[a], levels[b] = jnp.maximum(levels[a], levels[b]), jnp.minimum(levels[a], levels[b])
    levels = levels[:n_tiles]
    tops = []
    for r in range(count):
        m = jnp.max(levels[0], axis=0, keepdims=True)
        tops.append(m)
        left = count - r - 1
        hit = levels[0] == m
        for g in range(min(len(levels), left)):
            below = levels[g + 1] if g + 1 < len(levels) else neg
            levels[g] = jnp.where(hit, below, levels[g])
    return tops


def _count_greater(sorted_rows, x):
    count = jnp.zeros(x.shape, F32)
    for r, row in enumerate(sorted_rows):
        count = jnp.where(row > x, float(r + 1), count)
    return count


_N_RANKS = PEER_TOPK + 1
_RANK_ROWS = 24


_N_CAND = sum(_N_RANKS // (i + 1) for i in range(_N_RANKS))
_CAND_ROWS = -(-_N_CAND // 8) * 8


def _pair_candidates(a1, a2_ref, cand_ref):
    cand_ref[_N_CAND // 8 * 8:, :] = jnp.full((_CAND_ROWS - _N_CAND // 8 * 8, cand_ref.shape[1]), -jnp.inf, F32)
    row = 0
    for i in range(_N_RANKS):
        n_j = _N_RANKS // (i + 1)
        cand_ref[row:row + n_j, :] = a1[i] + a2_ref[0:n_j, :]
        row += n_j
    return cand_ref[...]


def _router_body(h_ref, gain_ref, wq_hi_ref, wq_lo_ref, keys_ref, rank2_ref, e2_ref, cut_ref, e1_ref,
                 a2_ref, cand_ref):
    h = h_ref[...]
    ms = jnp.mean(h * h, axis=-1, keepdims=True)
    x_hi, x_lo = _split_bf16(h * lax.rsqrt(ms + NORM_EPS) * gain_ref[...])
    q = (jnp.dot(x_hi, wq_hi_ref[...], preferred_element_type=F32)
         + jnp.dot(x_hi, wq_lo_ref[...], preferred_element_type=F32)
         + jnp.dot(x_lo, wq_hi_ref[...], preferred_element_type=F32))
    dk = keys_ref.shape[-1] // 2
    for hd in range(PEER_HEADS):
        s = []
        for half in range(2):
            c0 = (hd * 2 + half) * dk
            q_hi, q_lo = _split_bf16(q[:, c0:c0 + dk])
            keys2 = keys_ref[hd, half]
            s.append(_dot_nt(keys2, jnp.concatenate([q_hi, q_hi], axis=1))
                     + _dot_nt(keys2, jnp.concatenate([q_lo, q_lo], axis=1)))
        s1, s2 = s
        a1 = _top_rows(s1, _N_RANKS)
        a2 = _top_rows(s2, _N_RANKS)
        for r in range(_N_RANKS):
            a2_ref[r:r + 1, :] = a2[r]
        cand = _pair_candidates(a1, a2_ref, cand_ref)
        top = _top_rows(cand, _N_RANKS)
        tau = 0.5 * (top[PEER_TOPK - 1] + top[PEER_TOPK])
        z = jnp.sum(jnp.where(cand > tau, jnp.exp(cand - (a1[0] + a2[0])), 0.0), axis=0, keepdims=True)
        rank2_ref[hd] = _count_greater(a2, s2).astype(rank2_ref.dtype)
        e2_ref[hd] = jnp.exp(s2 - a2[0]).astype(e2_ref.dtype)
        cut_ref[hd] = _count_greater(a2[:PEER_TOPK], tau - s1)
        e1_ref[hd] = jnp.exp(s1 - a1[0]) * (0.5 / z)


def peer_router(h, gain, w_q, sub_keys, tt=256):
    t, d = h.shape
    tt = min(tt, t)
    nk = sub_keys.shape[2]
    wq_hi, wq_lo = _split_bf16(w_q)
    keys2 = jnp.concatenate(_split_bf16(sub_keys), axis=-1)
    out = [jax.ShapeDtypeStruct((PEER_HEADS, nk, t), dt) for dt in (BF16, BF16, F32, F32)]
    ospec = pl.BlockSpec((PEER_HEADS, nk, tt), lambda i: (0, 0, i))
    return pl.pallas_call(
        _router_body,
        grid=(t // tt,),
        in_specs=[pl.BlockSpec((tt, d), lambda i: (i, 0)),
                  pl.BlockSpec((1, d), lambda i: (0, 0)),
                  pl.BlockSpec(w_q.shape, lambda i: (0, 0)),
                  pl.BlockSpec(w_q.shape, lambda i: (0, 0)),
                  pl.BlockSpec(keys2.shape, lambda i: (0, 0, 0, 0))],
        out_specs=[ospec] * 4,
        out_shape=out,
        scratch_shapes=[pltpu.VMEM((_RANK_ROWS, tt), F32), pltpu.VMEM((_CAND_ROWS, tt), F32)],
        compiler_params=_cparams("parallel"),
        name="peer_router",
    )(h, gain.reshape(1, d), wq_hi, wq_lo, keys2)


def _gelu_x2(x):
    return x * (1.0 + lax.erf(x * (2.0 ** -0.5)))


_ROWS_PER_STEP = 8
_J_CHUNK = 32
_ROW_GROUP = 4
_TOK_COLS = 256
_MM_SPLIT = 4
_BF16_ROWS = 16


def _experts_body(h_ref, gain_ref, u0_ref, un_ref, vt_ref, rank2_ref, e2_ref, cut_ref, e1_ref, o_ref,
                  xn_ref, acc_ref, act_a_ref, act_b_ref, p_a_ref, p_b_ref, cutb_ref, e1b_ref):
    step = pl.program_id(1)
    n_tiles = pl.num_programs(1) - 1
    nk = rank2_ref.shape[1]
    tt = h_ref.shape[0]
    n_cols = tt // _TOK_COLS
    act_refs = (act_a_ref, act_b_ref)
    p_refs = (p_a_ref, p_b_ref)

    @pl.when(step == 0)
    def _():
        h = h_ref[...]
        ms = jnp.mean(h * h, axis=-1, keepdims=True)
        xn_ref[...] = (h * lax.rsqrt(ms + NORM_EPS) * gain_ref[...]).astype(BF16)
        acc_ref[...] = jnp.zeros_like(acc_ref)
        p_refs[1][...] = jnp.zeros(p_refs[1].shape, BF16)
        act_refs[0][...] = _dot_nt(u0_ref[...], xn_ref[...])

    def main_block(act_cur, act_nxt, p_cur, p_prev):
        d_rows = acc_ref.shape[0] // _MM_SPLIT
        e_rows = un_ref.shape[0] // _MM_SPLIT

        def v_piece(k, c):
            rs, cs = slice(k * d_rows, (k + 1) * d_rows), slice(c * _TOK_COLS, (c + 1) * _TOK_COLS)
            acc_ref[rs, cs] += jnp.dot(vt_ref[rs, :], p_prev[:, cs], preferred_element_type=F32)

        def act_piece(k, c):
            rs, cs = slice(k * e_rows, (k + 1) * e_rows), slice(c * _TOK_COLS, (c + 1) * _TOK_COLS)
            act_nxt[rs, cs] = _dot_nt(un_ref[rs, :], xn_ref[cs, :])

        pieces = [(f, k, c) for k in range(_MM_SPLIT) for c in range(n_cols) for f in (v_piece, act_piece)]

        for hd in range(PEER_HEADS):
            cut8 = cut_ref[hd]
            e18 = e1_ref[hd]
            for r in range(_ROWS_PER_STEP):
                cutb_ref[hd, r] = jnp.broadcast_to(cut8[r:r + 1], (_BF16_ROWS, tt)).astype(BF16)
                e1b_ref[hd, r] = jnp.broadcast_to(e18[r:r + 1], (_BF16_ROWS, tt)).astype(BF16)

        reps = _J_CHUNK // _BF16_ROWS
        chunks = [(c, jc, rg) for c in range(n_cols) for jc in range(nk // _J_CHUNK)
                  for rg in range(_ROWS_PER_STEP // _ROW_GROUP)]
        per_chunk = -(-len(pieces) // len(chunks))
        for n, (c, jc, rg) in enumerate(chunks):
            for f, k, cc in pieces[n * per_chunk:(n + 1) * per_chunk]:
                f(k, cc)
            ts = slice(c * _TOK_COLS, (c + 1) * _TOK_COLS)
            js = slice(jc * _J_CHUNK, (jc + 1) * _J_CHUNK)
            gates = [jnp.zeros((_J_CHUNK, _TOK_COLS), BF16) for _ in range(_ROW_GROUP)]
            for hd in range(PEER_HEADS):
                rank2 = rank2_ref[hd, js, ts]
                e2 = e2_ref[hd, js, ts]
                for k in range(_ROW_GROUP):
                    r = rg * _ROW_GROUP + k
                    cut = jnp.concatenate([cutb_ref[hd, r, :, ts]] * reps, axis=0)
                    e1 = jnp.concatenate([e1b_ref[hd, r, :, ts]] * reps, axis=0)
                    gates[k] = gates[k] + jnp.where(rank2 < cut, e2 * e1, jnp.zeros_like(e2))
            for k in range(_ROW_GROUP):
                rows = slice((rg * _ROW_GROUP + k) * nk + jc * _J_CHUNK,
                             (rg * _ROW_GROUP + k) * nk + (jc + 1) * _J_CHUNK)
                p_cur[rows, ts] = _gelu_x2(act_cur[rows, ts]).astype(BF16) * gates[k]

    for parity in range(2):
        @pl.when((step % 2 == parity) & (step < n_tiles))
        def _():
            main_block(act_refs[parity], act_refs[1 - parity], p_refs[parity], p_refs[1 - parity])

    last_p = p_refs[(rank2_ref.shape[1] // _ROWS_PER_STEP - 1) % 2]

    @pl.when(step == n_tiles)
    def _():
        acc = acc_ref[...] + jnp.dot(vt_ref[...], last_p[...], preferred_element_type=F32)
        o_ref[...] = h_ref[...] + acc.T


def peer_experts(h, gain, u, v_t, rank2, e2, cut, e1, tt=512):
    t, d = h.shape
    nk = rank2.shape[1]
    tt = min(tt, t)
    et = _ROWS_PER_STEP * nk
    n_tiles = nk // _ROWS_PER_STEP
    rspec = pl.BlockSpec((PEER_HEADS, nk, tt), lambda i, s: (0, 0, i))
    row_spec = pl.BlockSpec((PEER_HEADS, _ROWS_PER_STEP, tt), lambda i, s: (0, jnp.minimum(s, n_tiles - 1), i))
    return pl.pallas_call(
        _experts_body,
        grid=(t // tt, n_tiles + 1),
        in_specs=[pl.BlockSpec((tt, d), lambda i, s: (i, 0)),
                  pl.BlockSpec((1, d), lambda i, s: (0, 0)),
                  pl.BlockSpec((et, d), lambda i, s: (0, 0)),
                  pl.BlockSpec((et, d), lambda i, s: (jnp.minimum(s + 1, n_tiles - 1), 0)),
                  pl.BlockSpec((None, d, et), lambda i, s: (jnp.maximum(s - 1, 0), 0, 0)),
                  rspec, rspec, row_spec, row_spec],
        out_specs=pl.BlockSpec((tt, d), lambda i, s: (i, 0)),
        out_shape=jax.ShapeDtypeStruct((t, d), F32),
        scratch_shapes=[pltpu.VMEM((tt, d), BF16), pltpu.VMEM((d, tt), F32),
                        pltpu.VMEM((et, tt), F32), pltpu.VMEM((et, tt), F32),
                        pltpu.VMEM((et, tt), BF16), pltpu.VMEM((et, tt), BF16),
                        pltpu.VMEM((PEER_HEADS, _ROWS_PER_STEP, _BF16_ROWS, tt), BF16),
                        pltpu.VMEM((PEER_HEADS, _ROWS_PER_STEP, _BF16_ROWS, tt), BF16)],
        compiler_params=_cparams("parallel", "arbitrary"),
        name="peer_experts",
    )(h, gain.reshape(1, d), u, u, v_t, rank2, e2, cut, e1)


def _transpose_cast_body(x_ref, o_ref):
    o_ref[...] = x_ref[...].T.astype(o_ref.dtype)


def transpose_cast(x3, index, dtype, tile, blk=1024):
    _, r, c = x3.shape
    per = tile // blk
    return pl.pallas_call(
        _transpose_cast_body,
        grid=(r // blk, c // blk),
        in_specs=[pl.BlockSpec((None, blk, blk), lambda i, j: (index, i, j))],
        out_specs=pl.BlockSpec((None, blk, blk), lambda i, j: (i // per, j, i % per)),
        out_shape=jax.ShapeDtypeStruct((r // tile, c, tile), dtype),
        compiler_params=_cparams("parallel", "parallel"),
        name="transpose_cast",
    )(x3)


def _final_norm_body(x_ref, g_ref, o_ref):
    x = x_ref[...]
    ms = jnp.mean(x * x, axis=-1, keepdims=True)
    o_ref[...] = x * lax.rsqrt(ms + NORM_EPS) * g_ref[...]


def final_norm(x, gain, tm=1024):
    t, d = x.shape
    tm = min(tm, t)
    return pl.pallas_call(
        _final_norm_body,
        grid=(t // tm,),
        in_specs=[pl.BlockSpec((tm, d), lambda i: (i, 0)), pl.BlockSpec((1, d), lambda i: (0, 0))],
        out_specs=pl.BlockSpec((tm, d), lambda i: (i, 0)),
        out_shape=jax.ShapeDtypeStruct((t, d), F32),
        compiler_params=_cparams("parallel"),
        name="final_norm",
    )(x, gain.reshape(1, d))


_GATE_BLOCKS = N_BRANCHES * 8
_COL_Z = _GATE_BLOCKS
_COL_RET = _COL_Z + 4
_COL_MOBA = _COL_RET + 16
_COL_DIL = _COL_MOBA + 12
_COL_XBC = _COL_DIL + 12
_PROJ_BLOCKS = 84


def _pack_w_in(w_in_l):
    ssd_conv_dim = MIX_WIDTH + 2 * SSD_GROUPS * SSD_STATE
    o_xbc = MIX_WIDTH
    o_dt = o_xbc + ssd_conv_dim
    o_rest = o_dt + N_HEADS
    o_gate = o_rest + 10 * MIX_WIDTH
    d_model = w_in_l.shape[0]
    used = _COL_XBC * LANES + ssd_conv_dim
    pad = jnp.zeros((d_model, _PROJ_BLOCKS * LANES - used), w_in_l.dtype)
    main = jnp.concatenate([w_in_l[:, o_gate:], w_in_l[:, :o_xbc], w_in_l[:, o_rest:o_gate],
                            w_in_l[:, o_xbc:o_dt], pad], axis=1).astype(BF16)
    w_dt = jnp.pad(w_in_l[:, o_dt:o_rest], ((0, 0), (0, LANES - N_HEADS))).astype(BF16)
    return main, w_dt


def kernel(x, mem, mix_norm, w_in, ssd_conv_w, ssd_conv_b, ssd_dt_bias, ssd_a_log, ssd_d, ssd_norm, ret_norm, w_branch, w_out, x_norm, w_xq, w_xkv, w_xo, ffn_norm, w_pq, peer_sub_keys, peer_u, peer_v, final_norm_gain):
    bsz, seq, d = x.shape
    depth = w_in.shape[0]
    t = bsz * seq
    h = x.reshape(t, d)
    mem2 = mem.reshape(-1, d)
    rope_tables = _rope_tables(seq)

    for layer in range(depth):
        w_main, w_dt = _pack_w_in(w_in[layer])
        proj, dt_raw = norm_matmul(h, mix_norm[layer], w_main, w_dt, tm=2048, tn=1536, out_dtype=BF16)
        proj = proj.reshape(bsz, seq, -1)
        dt_raw = dt_raw.reshape(bsz, seq, LANES)

        y_a = ssd_mixer(proj, _COL_XBC, _COL_Z, dt_raw, ssd_conv_w[layer], ssd_conv_b[layer],
                        ssd_dt_bias[layer], ssd_a_log[layer], ssd_d[layer], ssd_norm[layer])
        y_b = retention_mixer(proj, [_COL_RET // 4 + i for i in range(4)], rope_tables, ret_norm[layer])
        y_c = moba_mixer(proj, [_COL_MOBA + 4 * i for i in range(3)], rope_tables)
        y_d = dilated_mixer(proj, [_COL_DIL + 4 * i for i in range(3)], rope_tables)
        kv = matmul(mem2, w_xkv[layer].astype(BF16), tm=512, tn=512, out_dtype=BF16)
        h = merge_cross_attention(h.reshape(bsz, seq, d), proj, (y_a, y_b, y_c, y_d),
                                  w_branch[layer].astype(BF16), w_out[layer].astype(BF16),
                                  kv.reshape(bsz, -1, 2 * d), x_norm[layer],
                                  w_xq[layer].astype(BF16), w_xo[layer].astype(BF16)).reshape(t, d)
        rank2, e2, cut, e1 = peer_router(h, ffn_norm[layer], w_pq[layer], peer_sub_keys[layer])
        h = peer_experts(h, ffn_norm[layer], peer_u[layer].astype(BF16),
                         transpose_cast(peer_v, layer, BF16, _ROWS_PER_STEP * PEER_NKEYS), rank2, e2, cut, e1)

    return final_norm(h, final_norm_gain).reshape(bsz, seq, d)
```
